```python
import math
import jax, jax.numpy as jnp
from jax import lax
import numpy as np

D_MODEL = 2048
BATCH = 1
SEQ = 8192
DEPTH = 1
DEC_BATCH = 16
DEC_SEQ = 64
PAST_LEN = 4096

CHUNK = 64
SSD_EXPAND = 2
SSD_INNER = SSD_EXPAND * D_MODEL
SSD_HEADDIM = 64
SSD_HEADS = SSD_INNER // SSD_HEADDIM
SSD_GROUPS = 8
SSD_HPG = SSD_HEADS // SSD_GROUPS
SSD_STATE = 128
CONV_W = 4
CONV_DIM = SSD_INNER + 2 * SSD_GROUPS * SSD_STATE
ATT_HEADS = 16
ATT_HEAD_DIM = 128
ATT_WIDTH = ATT_HEADS * ATT_HEAD_DIM
BAND_CHUNKS = 8
BAND_PAST = BAND_CHUNKS * CHUNK
BAND_LEN = BAND_PAST + CHUNK
REL_CLIP = 128
N_REL = 2 * REL_CLIP + 1
N_BRANCH = 2
D_FF = -(-(8 * D_MODEL) // (3 * 256)) * 256
LN_EPS = 1e-5
RMS_EPS = 1e-5
ALPHA = (2.0 * DEPTH) ** 0.25
BETA = (8.0 * DEPTH) ** -0.25
_SEG = (SSD_INNER, CONV_DIM, SSD_HEADS, ATT_WIDTH, ATT_WIDTH, ATT_WIDTH, N_BRANCH * D_MODEL)
IN_COLS = sum(_SEG)
SPLITS = tuple(int(v) for v in np.cumsum(_SEG)[:-1])

kernel_name = 'hybrid_ssd_bandattn_streaming_encoder'


def layer_norm(x, g, b):
    xf = x.astype(jnp.float32)
    mu = jnp.mean(xf, axis=-1, keepdims=True)
    var = jnp.mean(jnp.square(xf - mu), axis=-1, keepdims=True)
    y = (xf - mu) * lax.rsqrt(var + LN_EPS) * g.astype(jnp.float32) + b.astype(jnp.float32)
    return y.astype(x.dtype)


def causal_dwconv(u, hist, w, b):
    L = u.shape[1]
    up = jnp.concatenate([hist.astype(u.dtype), u], axis=1)
    y = b
    for tap in range(CONV_W):
        y = y + up[:, tap:tap + L] * w[tap]
    return y, up[:, L:]


def ssd_scan(x, dt, a, bm, cm, h0, q):
    b, l = x.shape[:2]
    c = l // q
    G, R, P, N = SSD_GROUPS, SSD_HPG, SSD_HEADDIM, SSD_STATE
    x = x.reshape(b, c, q, G, R, P)
    dt = dt.reshape(b, c, q, G, R)
    bm = bm.reshape(b, c, q, G, N)
    cm = cm.reshape(b, c, q, G, N)
    cs = jnp.cumsum(dt * a.reshape(G, R), axis=2)
    diff = cs[:, :, :, None] - cs[:, :, None, :]
    causal = jnp.tril(jnp.ones((q, q), dtype=bool))[:, :, None, None]
    decay_ij = jnp.exp(jnp.where(causal, diff, -jnp.inf))
    cb = jnp.einsum('bcign,bcjgn->bcijg', cm, bm)
    m = cb[..., None] * decay_ij * dt[:, :, None]
    y_diag = jnp.einsum('bcijgr,bcjgrp->bcigrp', m, x)
    decay_out = jnp.exp(cs[:, :, -1:] - cs)
    chunk_states = jnp.einsum('bcjgn,bcjgr,bcjgrp->bcgrpn', bm, decay_out * dt, x)
    chunk_decay = jnp.exp(cs[:, :, -1])

    def step(h, inp):
        s, d = inp
        return h * d[..., None, None] + s, h

    h_final, h_prev = lax.scan(step, h0.reshape(b, G, R, P, N),
                               (jnp.swapaxes(chunk_states, 0, 1), jnp.swapaxes(chunk_decay, 0, 1)))
    h_prev = jnp.swapaxes(h_prev, 0, 1)
    y_off = jnp.einsum('bcign,bcgrpn,bcigr->bcigrp', cm, h_prev, jnp.exp(cs))
    y = (y_diag + y_off).reshape(b, l, SSD_HEADS, P)
    return y, h_final.reshape(b, SSD_HEADS, P, N)


def ssd_branch(z, xbc, dt_raw, conv_hist, h0, conv_w, conv_b, dt_bias, a_log, d_skip, norm_w):
    bsz, L = z.shape[:2]
    xbc, conv_new = causal_dwconv(xbc, conv_hist, conv_w, conv_b)
    xbc = jax.nn.silu(xbc).astype(jnp.float32)
    xs, bm, cm = jnp.split(xbc, [SSD_INNER, SSD_INNER + SSD_GROUPS * SSD_STATE], axis=-1)
    xs = xs.reshape(bsz, L, SSD_HEADS, SSD_HEADDIM)
    bm = bm.reshape(bsz, L, SSD_GROUPS, SSD_STATE)
    cm = cm.reshape(bsz, L, SSD_GROUPS, SSD_STATE)
    dt = jax.nn.softplus(dt_raw.astype(jnp.float32) + dt_bias.astype(jnp.float32))
    a = -jnp.exp(a_log.astype(jnp.float32))
    y, h_new = ssd_scan(xs, dt, a, bm, cm, h0.astype(jnp.float32), min(CHUNK, L))
    y = (y + d_skip.astype(jnp.float32)[:, None] * xs).reshape(bsz, L, SSD_INNER)
    g = (y * jax.nn.silu(z.astype(jnp.float32))).reshape(bsz, L, SSD_GROUPS, SSD_INNER // SSD_GROUPS)
    g = g * lax.rsqrt(jnp.mean(jnp.square(g), axis=-1, keepdims=True) + RMS_EPS)
    out = g.reshape(bsz, L, SSD_INNER) * norm_w.astype(jnp.float32)
    return out.astype(z.dtype), conv_new, h_new.astype(h0.dtype)


def rel_position_bias(rel_bias, rel):
    return rel_bias[:, jnp.clip(rel, -REL_CLIP, REL_CLIP) + REL_CLIP].astype(jnp.float32)


def band_softmax_attention(q, kb, vb, bias, mask):
    s = jnp.einsum('bnqhd,bnkhd->bnhqk', q, kb).astype(jnp.float32) * (ATT_HEAD_DIM ** -0.5) + bias
    if mask is not None:
        s = jnp.where(mask[None, :, None], s, -jnp.inf)
    p = jax.nn.softmax(s, axis=-1).astype(vb.dtype)
    return jnp.einsum('bnhqk,bnkhd->bnqhd', p, vb)


def prompt_band_attention(q, k, v, rel_bias):
    b, S = q.shape[:2]
    nc = S // CHUNK
    pad = jnp.zeros((b, BAND_PAST, ATT_HEADS, ATT_HEAD_DIM), k.dtype)
    kp = jnp.concatenate([pad, k], axis=1)
    vp = jnp.concatenate([pad, v], axis=1)
    idx = (jnp.arange(nc) * CHUNK)[:, None] + jnp.arange(BAND_LEN)[None, :]
    kb = kp[:, idx]
    vb = vp[:, idx]
    mask = (idx >= BAND_PAST)[:, None, :]
    rel = jnp.arange(CHUNK)[:, None] - jnp.arange(BAND_LEN)[None, :] + BAND_PAST
    out = band_softmax_attention(q.reshape(b, nc, CHUNK, ATT_HEADS, ATT_HEAD_DIM), kb, vb,
                                 rel_position_bias(rel_bias, rel), mask)
    return out.reshape(b, S, ATT_HEADS, ATT_HEAD_DIM)


def sample_band_attention(q, k, v, k_hist, v_hist, rel_bias):
    L = q.shape[1]
    pc = k_hist.shape[1]
    kb = jnp.concatenate([k_hist.astype(k.dtype), k], axis=1)[:, None]
    vb = jnp.concatenate([v_hist.astype(v.dtype), v], axis=1)[:, None]
    rel = jnp.arange(L)[:, None] - jnp.arange(pc + L)[None, :] + pc
    out = band_softmax_attention(q[:, None], kb, vb, rel_position_bias(rel_bias, rel), None)
    return out[:, 0]


def trunk_layer(x, conv_hist, ssm_h0, k_hist, v_hist, w_in, conv_w, conv_b, dt_bias, a_log, d_skip,
                ssd_norm_w, rel_bias, w_ssd_out, w_att_out, w_o, ln1_g, ln1_b, w_gate_up, w_down,
                ln2_g, ln2_b):
    bsz, L = x.shape[:2]
    proj = x @ w_in
    z, xbc, dt_raw, q, k, v, gates = jnp.split(proj, SPLITS, axis=-1)
    ssd_y, conv_new, h_new = ssd_branch(z, xbc, dt_raw, conv_hist, ssm_h0, conv_w, conv_b,
                                        dt_bias, a_log, d_skip, ssd_norm_w)
    q = q.reshape(bsz, L, ATT_HEADS, ATT_HEAD_DIM)
    k = k.reshape(bsz, L, ATT_HEADS, ATT_HEAD_DIM)
    v = v.reshape(bsz, L, ATT_HEADS, ATT_HEAD_DIM)
    if k_hist is None:
        att = prompt_band_attention(q, k, v, rel_bias)
        keep = min(BAND_PAST, L)
        k_rows, v_rows = k[:, L - keep:], v[:, L - keep:]
    else:
        att = sample_band_attention(q, k, v, k_hist, v_hist, rel_bias)
        k_rows, v_rows = k, v
    att = att.reshape(bsz, L, ATT_WIDTH)
    g_ssd, g_att = jnp.split(jax.nn.sigmoid(gates), N_BRANCH, axis=-1)
    mixed = g_ssd * (ssd_y @ w_ssd_out) + g_att * (att @ w_att_out)
    h = layer_norm(ALPHA * x + mixed @ w_o, ln1_g, ln1_b)
    gate, up = jnp.split(h @ w_gate_up, 2, axis=-1)
    y = layer_norm(ALPHA * h + (jax.nn.silu(gate) * up) @ w_down, ln2_g, ln2_b)
    return y, conv_new, h_new, k_rows, v_rows


def setup_inputs(seed: int = 0) -> dict:
    key = jax.random.key(seed)
    ks = jax.random.split(key, 24)
    f32 = jnp.float32
    kv_rows = min(BAND_PAST, PAST_LEN)
    dt0 = jnp.exp(jax.random.uniform(ks[9], (DEPTH, SSD_HEADS), f32, math.log(1e-3), math.log(1e-1)))
    return {
        'x_prompt': jax.random.normal(ks[0], (BATCH, SEQ, D_MODEL), f32),
        'x_sample': jax.random.normal(ks[1], (DEC_BATCH, DEC_SEQ, D_MODEL), f32),
        'cache_k': jax.random.normal(ks[2], (DEPTH, DEC_BATCH, kv_rows, ATT_HEADS, ATT_HEAD_DIM), f32),
        'cache_v': jax.random.normal(ks[3], (DEPTH, DEC_BATCH, kv_rows, ATT_HEADS, ATT_HEAD_DIM), f32),
        'state_conv': jax.random.normal(ks[4], (DEPTH, DEC_BATCH, CONV_W - 1, CONV_DIM), f32),
        'state_ssm': 0.5 * jax.random.normal(ks[5], (DEPTH, DEC_BATCH, SSD_HEADS, SSD_HEADDIM, SSD_STATE), f32),
        'w_in': jax.random.normal(ks[6], (DEPTH, D_MODEL, IN_COLS), f32) * D_MODEL ** -0.5,
        'conv_w': jax.random.normal(ks[7], (DEPTH, CONV_W, CONV_DIM), f32) * CONV_W ** -0.5,
        'conv_b': 0.01 * jax.random.normal(ks[8], (DEPTH, CONV_DIM), f32),
        'dt_bias': dt0 + jnp.log(-jnp.expm1(-dt0)),
        'a_log': jnp.log(jax.random.uniform(ks[10], (DEPTH, SSD_HEADS), f32, 1.0, 16.0)),
        'd_skip': 1.0 + 0.1 * jax.random.normal(ks[11], (DEPTH, SSD_HEADS), f32),
        'ssd_norm_w': 1.0 + 0.05 * jax.random.normal(ks[12], (DEPTH, SSD_INNER), f32),
        'rel_bias': 0.5 * jax.random.normal(ks[13], (DEPTH, ATT_HEADS, N_REL), f32),
        'w_ssd_out': jax.random.normal(ks[14], (DEPTH, SSD_INNER, D_MODEL), f32) * SSD_INNER ** -0.5,
        'w_att_out': jax.random.normal(ks[15], (DEPTH, ATT_WIDTH, D_MODEL), f32) * ATT_WIDTH ** -0.5,
        'w_o': jax.random.normal(ks[16], (DEPTH, D_MODEL, D_MODEL), f32) * (BETA * D_MODEL ** -0.5),
        'ln1_g': 1.0 + 0.05 * jax.random.normal(ks[17], (DEPTH, D_MODEL), f32),
        'ln1_b': 0.01 * jax.random.normal(ks[18], (DEPTH, D_MODEL), f32),
        'w_gate_up': jax.random.normal(ks[19], (DEPTH, D_MODEL, 2 * D_FF), f32) * D_MODEL ** -0.5,
        'w_down': jax.random.normal(ks[20], (DEPTH, D_FF, D_MODEL), f32) * (BETA * D_FF ** -0.5),
        'ln2_g': 1.0 + 0.05 * jax.random.normal(ks[21], (DEPTH, D_MODEL), f32),
        'ln2_b': 0.01 * jax.random.normal(ks[22], (DEPTH, D_MODEL), f32),
    }


def reference(x_prompt, x_sample, cache_k, cache_v, state_conv, state_ssm, w_in, conv_w, conv_b,
              dt_bias, a_log, d_skip, ssd_norm_w, rel_bias, w_ssd_out, w_att_out, w_o, ln1_g, ln1_b,
              w_gate_up, w_down, ln2_g, ln2_b):
    yp, ys = x_prompt, x_sample
    conv_p, ssm_p, k_p, v_p = [], [], [], []
    conv_s, ssm_s, k_s, v_s = [], [], [], []
    for layer in range(DEPTH):
        params = (w_in[layer], conv_w[layer], conv_b[layer], dt_bias[layer], a_log[layer], d_skip[layer],
                  ssd_norm_w[layer], rel_bias[layer], w_ssd_out[layer], w_att_out[layer], w_o[layer],
                  ln1_g[layer], ln1_b[layer], w_gate_up[layer], w_down[layer], ln2_g[layer], ln2_b[layer])
        zero_conv = jnp.zeros((yp.shape[0], CONV_W - 1, CONV_DIM), yp.dtype)
        zero_ssm = jnp.zeros((yp.shape[0], SSD_HEADS, SSD_HEADDIM, SSD_STATE), yp.dtype)
        yp, c, h, k, v = trunk_layer(yp, zero_conv, zero_ssm, None, None, *params)
        conv_p.append(c); ssm_p.append(h); k_p.append(k); v_p.append(v)
        ys, c, h, k, v = trunk_layer(ys, state_conv[layer], state_ssm[layer], cache_k[layer],
                                     cache_v[layer], *params)
        conv_s.append(c); ssm_s.append(h); k_s.append(k); v_s.append(v)
    return (yp, ys, jnp.stack(conv_p), jnp.stack(ssm_p), jnp.stack(k_p), jnp.stack(v_p),
            jnp.stack(conv_s), jnp.stack(ssm_s), jnp.stack(k_s), jnp.stack(v_s))
```

```python
import functools

import jax
import jax.numpy as jnp
from jax import lax
from jax.experimental import pallas as pl
from jax.experimental.pallas import tpu as pltpu

F32 = jnp.float32
BF16 = jnp.bfloat16

CHUNK = 64
SSD_HEADDIM = 64
SSD_GROUPS = 8
SSD_STATE = 128
CONV_W = 4
ATT_HEAD_DIM = 128
BAND_CHUNKS = 8
BAND_PAST = BAND_CHUNKS * CHUNK
REL_CLIP = 128
LN_EPS = 1e-5
RMS_EPS = 1e-5
MASK_NEG = -1e30

LANES = 128
VMEM_LIMIT_BYTES = 56 * 1024 * 1024

ATT_TILE = 512
ATT_WIN = 5 * LANES


def _cparams(n_axes):
    return pltpu.CompilerParams(
        dimension_semantics=("arbitrary",) * n_axes,
        vmem_limit_bytes=VMEM_LIMIT_BYTES,
    )


def _dot(a, b):
    return jnp.dot(a, b, preferred_element_type=F32)


def _dot_nt(a, b):
    return lax.dot_general(a, b, (((1,), (1,)), ((), ())), preferred_element_type=F32)


def _dot_tn(a, b):
    return lax.dot_general(a, b, (((0,), (0,)), ((), ())), preferred_element_type=F32)


def _mm_kernel(x_ref, w_ref, *o_refs):
    acc = _dot(x_ref[...], w_ref[...])
    for o_ref in o_refs:
        o_ref[...] = acc.astype(o_ref.dtype)


def _matmul(x, w, out_dtypes, tm, tn, name):
    m, k = x.shape
    n = w.shape[1]
    assert m % tm == 0 and n % tn == 0
    out_shape = [jax.ShapeDtypeStruct((m, n), dt) for dt in out_dtypes]
    return pl.pallas_call(
        _mm_kernel,
        grid=(n // tn, m // tm),
        in_specs=[pl.BlockSpec((tm, k), lambda j, i: (i, 0)),
                  pl.BlockSpec((k, tn), lambda j, i: (0, j))],
        out_specs=[pl.BlockSpec((tm, tn), lambda j, i: (i, j)) for _ in out_dtypes],
        out_shape=out_shape,
        compiler_params=_cparams(2),
        name=name,
    )(x, w)


def _softplus(x):
    return jnp.maximum(x, 0.0) + jnp.log1p(jnp.exp(-jnp.abs(x)))


def _ssd_kernel(z_ref, xbc_ref, dt_ref, hist_ref, h0_ref, cw_ref, cb_ref, dtb_ref, alog_ref,
                dsk_ref, nw_ref, e_ref,
                y_ref, convo_ref, ho_ref,
                xpad, act, ht, cst, ex, ysc, *, n_prompt_chunks, inner, n_groups, d_state):
    c = pl.program_id(0)
    is_sample = c >= n_prompt_chunks
    gw = inner // n_groups
    conv_dim = inner + 2 * n_groups * d_state

    @pl.when(c == 0)
    def _():
        xpad[0:8, :] = jnp.zeros((8, conv_dim), F32)
        ht[...] = jnp.zeros_like(ht)

    @pl.when(is_sample)
    def _():
        xpad[5:8, :] = hist_ref[0]
        ht[...] = h0_ref[0].T

    xpad[8:8 + CHUNK, :] = xbc_ref[...]
    convo_ref[0] = xbc_ref[CHUNK - (CONV_W - 1):CHUNK, :]
    lane_blk = 4 * LANES
    for j in range(conv_dim // lane_blk):
        sl = slice(j * lane_blk, (j + 1) * lane_blk)
        a = cb_ref[:, sl]
        for tap in range(CONV_W):
            a = a + xpad[5 + tap:5 + tap + CHUNK, sl] * cw_ref[tap:tap + 1, sl]
        act[:, sl] = a * jax.nn.sigmoid(a)
    xpad[5:8, :] = xpad[5 + CHUNK:8 + CHUNK, :]

    dt = _softplus(dt_ref[...] + dtb_ref[...])
    a_neg = -jnp.exp(alog_ref[...])
    da = dt * a_neg
    ri = lax.broadcasted_iota(jnp.int32, (CHUNK, CHUNK), 0)
    ci = lax.broadcasted_iota(jnp.int32, (CHUNK, CHUNK), 1)
    tri = (ri >= ci).astype(F32)
    cs = jnp.dot(tri, da, precision=lax.Precision.HIGHEST, preferred_element_type=F32)
    ecs = jnp.exp(cs)
    cs_last = cs[CHUNK - 1:CHUNK, :]
    dout = jnp.exp(cs_last - cs) * dt
    cd = jnp.exp(cs_last)
    cd_hi = cd.astype(BF16)
    cd_lo = (cd - cd_hi.astype(F32)).astype(BF16)
    lhs = jnp.concatenate(
        [dt.astype(BF16), ecs.astype(BF16), dout.astype(BF16),
         jnp.broadcast_to(cd_hi, (16, LANES)), jnp.broadcast_to(cd_lo, (16, LANES))], axis=0)
    ex[...] = _dot(lhs, e_ref[...])

    cst[...] = jnp.concatenate([cs, jnp.zeros((LANES - CHUNK, LANES), F32)], axis=0).T
    lane = lax.broadcasted_iota(jnp.int32, (CHUNK, LANES), 1)
    lo_half = lane < CHUNK
    n_pairs = inner // LANES
    ev = cst[pl.ds(0, n_pairs, stride=2), :]
    od = cst[pl.ds(1, n_pairs, stride=2), :]
    lane_p = lax.broadcasted_iota(jnp.int32, (n_pairs, LANES), 1)
    cst2 = jnp.where(lane_p < CHUNK, ev, pltpu.roll(od, CHUNK, 1))
    row = lax.broadcasted_iota(jnp.int32, (CHUNK, LANES), 0)
    causal2 = row >= jnp.where(lo_half, lane, lane - CHUNK)

    pairs_per_group = gw // LANES
    for g in range(n_groups):
        gs = slice(g * gw, (g + 1) * gw)
        bm_g = act[:, inner + g * d_state: inner + (g + 1) * d_state].astype(BF16)
        cm_g = act[:, inner + (n_groups + g) * d_state: inner + (n_groups + g + 1) * d_state].astype(BF16)
        cb2 = _dot_nt(cm_g, jnp.concatenate([bm_g, bm_g], axis=0))
        h_g = ht[:, gs]
        yoff = _dot(cm_g, h_g.astype(BF16))
        for p in range(pairs_per_group):
            k = g * pairs_per_group + p
            ls = slice(k * LANES, (k + 1) * LANES)
            csc = jnp.where(lo_half,
                            jnp.broadcast_to(cs[:, 2 * k:2 * k + 1], (CHUNK, LANES)),
                            jnp.broadcast_to(cs[:, 2 * k + 1:2 * k + 2], (CHUNK, LANES)))
            diff = csc - cst2[k:k + 1, :]
            dec = jnp.exp(jnp.where(causal2, diff, MASK_NEG))
            m2 = (cb2 * dec).astype(BF16)
            xs_p = act[:, ls]
            xdt = xs_p * ex[0:CHUNK, ls]
            x2 = jnp.concatenate([jnp.where(lo_half, xdt, 0.0), jnp.where(lo_half, 0.0, xdt)],
                                 axis=0).astype(BF16)
            yd = _dot(m2, x2)
            ysc[:, ls] = (yd + yoff[:, p * LANES:(p + 1) * LANES] * ex[CHUNK:2 * CHUNK, ls]
                          + dsk_ref[:, ls] * xs_p)
        xw = (act[:, gs] * ex[2 * CHUNK:3 * CHUNK, gs]).astype(BF16)
        st = _dot_tn(bm_g, xw)
        cdx = ex[3 * CHUNK:3 * CHUNK + 1, gs] + ex[3 * CHUNK + 16:3 * CHUNK + 17, gs]
        ht[:, gs] = h_g * cdx + st

    for g in range(n_groups):
        gs = slice(g * gw, (g + 1) * gw)
        zg = z_ref[:, gs].astype(F32)
        v = ysc[:, gs] * (zg * jax.nn.sigmoid(zg))
        ms = jnp.mean(v * v, axis=-1, keepdims=True)
        y_ref[:, gs] = (v * lax.rsqrt(ms + RMS_EPS) * nw_ref[:, gs]).astype(y_ref.dtype)

    @pl.when(jnp.logical_or(c == n_prompt_chunks - 1, is_sample))
    def _():
        ho_ref[0] = ht[...].T


def _ssd(za, xbc, dtr, hist, h0, conv_w, conv_b, dtb, alog, dskx, normw, emat, *,
         n_prompt_chunks, n_sample, inner, n_groups, d_state):
    n_chunks = n_prompt_chunks + n_sample
    conv_dim = xbc.shape[1]
    hp = inner
    npc = n_prompt_chunks

    def stream(c):
        return jnp.maximum(c - npc, 0)

    def out_stream(c):
        return jnp.maximum(c - npc + 1, 0)

    kern = functools.partial(_ssd_kernel, n_prompt_chunks=npc, inner=inner, n_groups=n_groups,
                             d_state=d_state)
    const = lambda c: (0, 0)
    return pl.pallas_call(
        kern,
        grid=(n_chunks,),
        in_specs=[
            pl.BlockSpec((CHUNK, inner), lambda c: (c, 0)),
            pl.BlockSpec((CHUNK, conv_dim), lambda c: (c, 0)),
            pl.BlockSpec((CHUNK, LANES), lambda c: (c, 0)),
            pl.BlockSpec((1, CONV_W - 1, conv_dim), lambda c: (stream(c), 0, 0)),
            pl.BlockSpec((1, hp, d_state), lambda c: (stream(c), 0, 0)),
            pl.BlockSpec((CONV_W, conv_dim), const),
            pl.BlockSpec((1, conv_dim), const),
            pl.BlockSpec((1, LANES), const),
            pl.BlockSpec((1, LANES), const),
            pl.BlockSpec((1, inner), const),
            pl.BlockSpec((1, inner), const),
            pl.BlockSpec((LANES, inner), const),
        ],
        out_specs=[
            pl.BlockSpec((CHUNK, inner), lambda c: (c, 0)),
            pl.BlockSpec((1, CONV_W - 1, conv_dim), lambda c: (out_stream(c), 0, 0)),
            pl.BlockSpec((1, hp, d_state), lambda c: (out_stream(c), 0, 0)),
        ],
        out_shape=[
            jax.ShapeDtypeStruct((n_chunks * CHUNK, inner), BF16),
            jax.ShapeDtypeStruct((n_sample + 1, CONV_W - 1, conv_dim), F32),
            jax.ShapeDtypeStruct((n_sample + 1, hp, d_state), F32),
        ],
        scratch_shapes=[
            pltpu.VMEM((8 + CHUNK, conv_dim), F32),
            pltpu.VMEM((CHUNK, conv_dim), F32),
            pltpu.VMEM((d_state, inner), F32),
            pltpu.VMEM((LANES, LANES), F32),
            pltpu.VMEM((3 * CHUNK + 32, inner), F32),
            pltpu.VMEM((CHUNK, inner), F32),
        ],
        compiler_params=_cparams(1),
        name="ssd_scan",
    )(za, xbc, dtr, hist, h0, conv_w, conv_b, dtb, alog, dskx, normw, emat)


def _band_bias_tables(ext_ref, bias_scr, n_heads):
    col = lax.broadcasted_iota(jnp.int32, (CHUNK, ATT_WIN), 1)
    band = BAND_PAST + CHUNK
    for h in range(n_heads):
        e = jnp.broadcast_to(ext_ref[h:h + 1, :], (CHUNK, ATT_WIN))
        even = pltpu.roll(e, ATT_WIN - (CHUNK - 1), 1, stride=1, stride_axis=0)
        odd = pltpu.roll(e, 1, 1, stride=1, stride_axis=0)
        bias_scr[0, h] = jnp.where(col < band, even, MASK_NEG)
        bias_scr[1, h] = jnp.where(col >= CHUNK, odd, MASK_NEG)


def _attn_prompt_kernel(q_ref, kp_ref, kc_ref, vp_ref, vc_ref, ext_ref, o_ref,
                        bias_scr, s_scr, p_scr, *, n_heads):
    i = pl.program_id(0)
    d = ATT_HEAD_DIM
    scale = d ** -0.5
    half = ATT_TILE

    @pl.when(i == 0)
    def _():
        _band_bias_tables(ext_ref, bias_scr, n_heads)
        p_scr[...] = jnp.zeros_like(p_scr)

    past_mask = jnp.where(i == 0, MASK_NEG, 0.0).astype(F32)

    for h in range(n_heads):
        hs = slice(h * d, (h + 1) * d)
        q_h = q_ref[:, hs]
        s_scr[:, 0:half] = _dot_nt(q_h, kp_ref[:, hs]) * scale + past_mask
        s_scr[:, half:2 * half] = _dot_nt(q_h, kc_ref[:, hs]) * scale
        for t in range(ATT_TILE // CHUNK):
            rs = slice(t * CHUNK, (t + 1) * CHUNK)
            c0 = LANES * (t // 2)
            ws = slice(c0, c0 + ATT_WIN)
            s = s_scr[rs, ws] + bias_scr[t % 2, h]
            m = jnp.max(s, axis=-1, keepdims=True)
            p = jnp.exp(s - m)
            r = 1.0 / jnp.sum(p, axis=-1, keepdims=True)
            p_scr[rs, ws] = (p * r).astype(BF16)
        o_ref[:, hs] = (_dot(p_scr[:, 0:half], vp_ref[:, hs])
                        + _dot(p_scr[:, half:2 * half], vc_ref[:, hs])).astype(o_ref.dtype)


def _attn_prompt(qa, kv, ext, *, q_col_blk, n_tiles, n_heads):
    width = n_heads * ATT_HEAD_DIM
    prev = lambda i: jnp.maximum(i - 1, 0)
    kern = functools.partial(_attn_prompt_kernel, n_heads=n_heads)
    return pl.pallas_call(
        kern,
        grid=(n_tiles,),
        in_specs=[
            pl.BlockSpec((ATT_TILE, width), lambda i: (i, q_col_blk)),
            pl.BlockSpec((ATT_TILE, width), lambda i: (prev(i), 0)),
            pl.BlockSpec((ATT_TILE, width), lambda i: (i, 0)),
            pl.BlockSpec((ATT_TILE, width), lambda i: (prev(i), 1)),
            pl.BlockSpec((ATT_TILE, width), lambda i: (i, 1)),
            pl.BlockSpec((n_heads, ATT_WIN), lambda i: (0, 0)),
        ],
        out_specs=pl.BlockSpec((ATT_TILE, width), lambda i: (i, 0)),
        out_shape=jax.ShapeDtypeStruct((n_tiles * ATT_TILE, width), BF16),
        scratch_shapes=[
            pltpu.VMEM((2, n_heads, CHUNK, ATT_WIN), F32),
            pltpu.VMEM((ATT_TILE, 2 * ATT_TILE), F32),
            pltpu.VMEM((ATT_TILE, 2 * ATT_TILE), BF16),
        ],
        compiler_params=_cparams(1),
        name="attn_prompt",
    )(qa, kv, kv, kv, kv, ext)


def _attn_sample_kernel(q_ref, kn_ref, vn_ref, ck_ref, cv_ref, ext_ref, o_ref, bias_scr, *, n_heads):
    d = ATT_HEAD_DIM
    scale = d ** -0.5

    @pl.when(pl.program_id(0) == 0)
    def _():
        _band_bias_tables(ext_ref, bias_scr, n_heads)

    for h in range(n_heads):
        hs = slice(h * d, (h + 1) * d)
        q_h = q_ref[:, hs]
        kc_h = ck_ref[0, :, hs].astype(BF16)
        vc_h = cv_ref[0, :, hs].astype(BF16)
        s1 = _dot_nt(q_h, kc_h) * scale + bias_scr[0, h, :, 0:BAND_PAST]
        s2 = _dot_nt(q_h, kn_ref[:, hs]) * scale + bias_scr[0, h, :, BAND_PAST:BAND_PAST + CHUNK]
        m = jnp.maximum(jnp.max(s1, axis=-1, keepdims=True), jnp.max(s2, axis=-1, keepdims=True))
        p1 = jnp.exp(s1 - m)
        p2 = jnp.exp(s2 - m)
        r = 1.0 / (jnp.sum(p1, axis=-1, keepdims=True) + jnp.sum(p2, axis=-1, keepdims=True))
        o = _dot((p1 * r).astype(BF16), vc_h) + _dot((p2 * r).astype(BF16), vn_ref[:, hs])
        o_ref[:, hs] = o.astype(o_ref.dtype)


def _attn_sample(qa, kv, cache_k, cache_v, ext, *, q_col_blk, row_blk0, n_sample, n_heads):
    width = n_heads * ATT_HEAD_DIM
    kern = functools.partial(_attn_sample_kernel, n_heads=n_heads)
    return pl.pallas_call(
        kern,
        grid=(n_sample,),
        in_specs=[
            pl.BlockSpec((CHUNK, width), lambda s: (row_blk0 + s, q_col_blk)),
            pl.BlockSpec((CHUNK, width), lambda s: (row_blk0 + s, 0)),
            pl.BlockSpec((CHUNK, width), lambda s: (row_blk0 + s, 1)),
            pl.BlockSpec((1, BAND_PAST, width), lambda s: (s, 0, 0)),
            pl.BlockSpec((1, BAND_PAST, width), lambda s: (s, 0, 0)),
            pl.BlockSpec((n_heads, ATT_WIN), lambda s: (0, 0)),
        ],
        out_specs=pl.BlockSpec((CHUNK, width), lambda s: (s, 0)),
        out_shape=jax.ShapeDtypeStruct((n_sample * CHUNK, width), BF16),
        scratch_shapes=[pltpu.VMEM((2, n_heads, CHUNK, ATT_WIN), F32)],
        compiler_params=_cparams(1),
        name="attn_sample",
    )(qa, kv, kv, cache_k, cache_v, ext)


def _mix_kernel(s_ref, ap_ref, as_ref, w1_ref, w2_ref, g1_ref, g2_ref, o_ref, *, n_prompt_tiles):
    i = pl.program_id(1)
    a1 = _dot(s_ref[...], w1_ref[...])
    att = jnp.where(i < n_prompt_tiles, ap_ref[...], as_ref[...])
    a2 = _dot(att, w2_ref[...])
    g1 = jax.nn.sigmoid(g1_ref[...].astype(F32))
    g2 = jax.nn.sigmoid(g2_ref[...].astype(F32))
    o_ref[...] = (g1 * a1 + g2 * a2).astype(o_ref.dtype)


def _mix(ssd_y, att_p, att_s, w1, w2, za, *, gate_col0, tm, tn):
    m, k1 = ssd_y.shape
    k2 = att_p.shape[1]
    n = w1.shape[1]
    npt = att_p.shape[0] // tm
    g1_blk = gate_col0 // tn
    g2_blk = (gate_col0 + n) // tn
    kern = functools.partial(_mix_kernel, n_prompt_tiles=npt)
    return pl.pallas_call(
        kern,
        grid=(n // tn, m // tm),
        in_specs=[
            pl.BlockSpec((tm, k1), lambda j, i: (i, 0)),
            pl.BlockSpec((tm, k2), lambda j, i: (jnp.minimum(i, npt - 1), 0)),
            pl.BlockSpec((tm, k2), lambda j, i: (jnp.maximum(i - npt, 0), 0)),
            pl.BlockSpec((k1, tn), lambda j, i: (0, j)),
            pl.BlockSpec((k2, tn), lambda j, i: (0, j)),
            pl.BlockSpec((tm, tn), lambda j, i: (i, g1_blk + j)),
            pl.BlockSpec((tm, tn), lambda j, i: (i, g2_blk + j)),
        ],
        out_specs=pl.BlockSpec((tm, tn), lambda j, i: (i, j)),
        out_shape=jax.ShapeDtypeStruct((m, n), BF16),
        compiler_params=_cparams(2),
        name="branch_mix",
    )(ssd_y, att_p, att_s, w1, w2, za, za)


def _layer_norm(t, g, b):
    mu = jnp.mean(t, axis=-1, keepdims=True)
    tc = t - mu
    var = jnp.mean(tc * tc, axis=-1, keepdims=True)
    return tc * lax.rsqrt(var + LN_EPS) * g + b


def _proj_ln_kernel(a_ref, w_ref, r_ref, g_ref, b_ref, *rest, alpha, n_k):
    o_refs, acc_ref = rest[:-1], rest[-1]
    kk = pl.program_id(1)
    part = _dot(a_ref[...], w_ref[...])

    @pl.when(kk == 0)
    def _():
        acc_ref[...] = alpha * r_ref[...] + part

    @pl.when(kk > 0)
    def _():
        acc_ref[...] += part

    @pl.when(kk == n_k - 1)
    def _():
        y = _layer_norm(acc_ref[...], g_ref[...], b_ref[...])
        for o_ref in o_refs:
            o_ref[...] = y.astype(o_ref.dtype)


def _proj_ln(a, w, resid, gamma, beta, out_dtypes, *, alpha, tm, n_k, name):
    m, k = a.shape
    n = w.shape[1]
    tk = k // n_k
    assert tk * n_k == k and m % tm == 0
    kern = functools.partial(_proj_ln_kernel, alpha=alpha, n_k=n_k)
    return pl.pallas_call(
        kern,
        grid=(m // tm, n_k),
        in_specs=[
            pl.BlockSpec((tm, tk), lambda i, kk: (i, kk)),
            pl.BlockSpec((tk, n), lambda i, kk: (kk, 0)),
            pl.BlockSpec((tm, n), lambda i, kk: (i, 0)),
            pl.BlockSpec((1, n), lambda i, kk: (0, 0)),
            pl.BlockSpec((1, n), lambda i, kk: (0, 0)),
        ],
        out_specs=[pl.BlockSpec((tm, n), lambda i, kk: (i, 0)) for _ in out_dtypes],
        out_shape=[jax.ShapeDtypeStruct((m, n), dt) for dt in out_dtypes],
        scratch_shapes=[pltpu.VMEM((tm, n), F32)],
        compiler_params=_cparams(2),
        name=name,
    )(a, w, resid, gamma, beta)


def _glu_kernel(h_ref, wg_ref, wu_ref, o_ref):
    h = h_ref[...]
    g = _dot(h, wg_ref[...])
    u = _dot(h, wu_ref[...])
    o_ref[...] = (g * jax.nn.sigmoid(g) * u).astype(o_ref.dtype)


def _glu(h, w_gate_up, *, tm, tf):
    m, k = h.shape
    d_ff = w_gate_up.shape[1] // 2
    nf = d_ff // tf
    assert nf * tf == d_ff
    return pl.pallas_call(
        _glu_kernel,
        grid=(nf, m // tm),
        in_specs=[
            pl.BlockSpec((tm, k), lambda j, i: (i, 0)),
            pl.BlockSpec((k, tf), lambda j, i: (0, j)),
            pl.BlockSpec((k, tf), lambda j, i: (0, nf + j)),
        ],
        out_specs=pl.BlockSpec((tm, tf), lambda j, i: (i, j)),
        out_shape=jax.ShapeDtypeStruct((m, d_ff), BF16),
        compiler_params=_cparams(2),
        name="ffn_glu",
    )(h, w_gate_up, w_gate_up)


def kernel(x_prompt, x_sample, cache_k, cache_v, state_conv, state_ssm, w_in, conv_w, conv_b, dt_bias,
           a_log, d_skip, ssd_norm_w, rel_bias, w_ssd_out, w_att_out, w_o, ln1_g, ln1_b, w_gate_up,
           w_down, ln2_g, ln2_b):
    depth = w_in.shape[0]
    assert depth == 1
    batch, seq, d_model = x_prompt.shape
    n_sample, dec_seq, _ = x_sample.shape
    assert batch == 1 and dec_seq == CHUNK and seq % ATT_TILE == 0
    n_ssd_heads = dt_bias.shape[1]
    inner = n_ssd_heads * SSD_HEADDIM
    conv_dim = conv_w.shape[2]
    n_groups = SSD_GROUPS
    d_state = SSD_STATE
    assert conv_dim == inner + 2 * n_groups * d_state
    n_heads = rel_bias.shape[1]
    width = n_heads * ATT_HEAD_DIM
    assert cache_k.shape[2] == BAND_PAST
    alpha = (2.0 * depth) ** 0.25

    m_p = batch * seq
    m_s = n_sample * dec_seq
    m_all = m_p + m_s

    x_all = jnp.concatenate([x_prompt.reshape(m_p, d_model), x_sample.reshape(m_s, d_model)], axis=0)
    xb = x_all.astype(BF16)

    w = w_in[0]
    o_xbc = inner
    o_dt = o_xbc + conv_dim
    o_q = o_dt + n_ssd_heads
    o_k = o_q + width
    o_v = o_k + width
    o_g = o_v + width
    w_a = jnp.concatenate([w[:, :inner], w[:, o_g:], w[:, o_q:o_k]], axis=1).astype(BF16)
    w_kv = w[:, o_k:o_g].astype(BF16)
    w_xbc = w[:, o_xbc:o_dt].astype(BF16)
    w_dt = jnp.pad(w[:, o_dt:o_q], ((0, 0), (0, LANES - n_ssd_heads))).astype(BF16)

    (za,) = _matmul(xb, w_a, [BF16], 1024, 1024, "in_proj_zgq")
    kv_b, kv_f = _matmul(xb, w_kv, [BF16, F32], 1024, 1024, "in_proj_kv")
    (xbc,) = _matmul(xb, w_xbc, [F32], 1024, 1024, "in_proj_xbc")
    (dtr,) = _matmul(xb, w_dt, [F32], 1024, LANES, "in_proj_dt")

    pad_h = (0, LANES - n_ssd_heads)
    dtb = jnp.pad(dt_bias[0], pad_h).reshape(1, LANES)
    alog = jnp.pad(a_log[0], pad_h).reshape(1, LANES)
    dskx = jnp.repeat(d_skip[0], SSD_HEADDIM).reshape(1, inner)
    emat = (jnp.arange(LANES)[:, None] == (jnp.arange(inner)[None, :] // SSD_HEADDIM)).astype(BF16)
    ssd_y, conv_o, h_o = _ssd(
        za, xbc, dtr, state_conv[0], state_ssm[0].reshape(n_sample, inner, d_state),
        conv_w[0], conv_b[0].reshape(1, conv_dim), dtb, alog, dskx, ssd_norm_w[0].reshape(1, inner), emat,
        n_prompt_chunks=m_p // CHUNK, n_sample=n_sample, inner=inner, n_groups=n_groups, d_state=d_state)

    rb = rel_bias[0]
    band = BAND_PAST + CHUNK
    n_edge = band - REL_CLIP
    n_rev = ATT_WIN - n_edge
    rev = rb[:, ::-1][:, 1:1 + n_rev]
    ext = jnp.concatenate([jnp.broadcast_to(rb[:, 2 * REL_CLIP:], (n_heads, n_edge)), rev], axis=1)
    ext = jnp.pad(ext, ((0, 0), (0, ATT_WIN - ext.shape[1])))
    q_col_blk = (inner + 2 * d_model) // width
    att_p = _attn_prompt(za, kv_b, ext, q_col_blk=q_col_blk, n_tiles=m_p // ATT_TILE, n_heads=n_heads)
    att_s = _attn_sample(za, kv_b, cache_k[0].reshape(n_sample, BAND_PAST, width),
                         cache_v[0].reshape(n_sample, BAND_PAST, width), ext,
                         q_col_blk=q_col_blk, row_blk0=m_p // CHUNK, n_sample=n_sample, n_heads=n_heads)

    mixed = _mix(ssd_y, att_p, att_s, w_ssd_out[0].astype(BF16), w_att_out[0].astype(BF16), za,
                 gate_col0=inner, tm=512, tn=1024)
    h_f, h_b = _proj_ln(mixed, w_o[0].astype(BF16), x_all, ln1_g[0].reshape(1, d_model),
                        ln1_b[0].reshape(1, d_model), [F32, BF16], alpha=alpha, tm=512, n_k=1, name="wo_ln")
    act = _glu(h_b, w_gate_up[0].astype(BF16), tm=512, tf=1408)
    (y,) = _proj_ln(act, w_down[0].astype(BF16), h_f, ln2_g[0].reshape(1, d_model),
                    ln2_b[0].reshape(1, d_model), [F32], alpha=alpha, tm=512, n_k=2, name="down_ln")

    y_p = y[:m_p].reshape(batch, seq, d_model)
    y_s = y[m_p:].reshape(n_sample, dec_seq, d_model)
    conv_p = conv_o[:1].reshape(1, batch, CONV_W - 1, conv_dim)
    conv_s = conv_o[1:].reshape(1, n_sample, CONV_W - 1, conv_dim)
    ssm_p = h_o[:1].reshape(1, batch, n_ssd_heads, SSD_HEADDIM, d_state)
    ssm_s = h_o[1:].reshape(1, n_sample, n_ssd_heads, SSD_HEADDIM, d_state)
    keep = min(BAND_PAST, seq)
    k_p = kv_f[m_p - keep:m_p, :width].reshape(1, batch, keep, n_heads, ATT_HEAD_DIM)
    v_p = kv_f[m_p - keep:m_p, width:].reshape(1, batch, keep, n_heads, ATT_HEAD_DIM)
    k_s = kv_f[m_p:, :width].reshape(1, n_sample, dec_seq, n_heads, ATT_HEAD_DIM)
    v_s = kv_f[m_p:, width:].reshape(1, n_sample, dec_seq, n_heads, ATT_HEAD_DIM)
    return (y_p, y_s, conv_p, ssm_p, k_p, v_p, conv_s, ssm_s, k_s, v_s)
```

```python
import functools

import jax
import jax.numpy as jnp
from jax import lax
from jax.experimental import pallas as pl
from jax.experimental.pallas import tpu as pltpu

F32 = jnp.float32
BF16 = jnp.bfloat16

CHUNK = 64
SSD_HEADDIM = 64
SSD_GROUPS = 8
SSD_STATE = 128
CONV_W = 4
ATT_HEAD_DIM = 128
BAND_CHUNKS = 8
BAND_PAST = BAND_CHUNKS * CHUNK
REL_CLIP = 128
LN_EPS = 1e-5
RMS_EPS = 1e-5
MASK_NEG = -1e30

LANES = 128
SUBLANES = 8
VMEM_LIMIT_BYTES = 56 * 1024 * 1024

ATT_TILE = 512
ATT_WIN = 5 * LANES


def _cparams(n_axes):
    return pltpu.CompilerParams(
        dimension_semantics=("arbitrary",) * n_axes,
        vmem_limit_bytes=VMEM_LIMIT_BYTES,
    )


def _dot(a, b):
    return jnp.dot(a, b, preferred_element_type=F32)


def _dot_nt(a, b):
    return lax.dot_general(a, b, (((1,), (1,)), ((), ())), preferred_element_type=F32)


def _dot_tn(a, b):
    return lax.dot_general(a, b, (((0,), (0,)), ((), ())), preferred_element_type=F32)


def _xcast_dt_kernel(xp_ref, xs_ref, wd_ref, xb_ref, dt_ref, *, n_prompt_tiles, n_dt):
    i = pl.program_id(0)
    x = jnp.where(i < n_prompt_tiles, xp_ref[...], xs_ref[...]).astype(BF16)
    xb_ref[...] = x
    acc = _dot(x, wd_ref[...].astype(BF16))
    lane = lax.broadcasted_iota(jnp.int32, acc.shape, 1)
    dt_ref[...] = jnp.where(lane < n_dt, acc, 0.0)


def _xcast_dt(xp, xs, w2d, *, dt_col, n_dt, tm):
    m_p, k = xp.shape
    m_s = xs.shape[0]
    npt = m_p // tm
    n_tiles = npt + m_s // tm
    assert dt_col % LANES == 0 and n_dt <= LANES
    kern = functools.partial(_xcast_dt_kernel, n_prompt_tiles=npt, n_dt=n_dt)
    return pl.pallas_call(
        kern,
        grid=(n_tiles,),
        in_specs=[
            pl.BlockSpec((tm, k), lambda i: (jnp.minimum(i, npt - 1), 0)),
            pl.BlockSpec((tm, k), lambda i: (jnp.maximum(i - npt, 0), 0)),
            pl.BlockSpec((k, LANES), lambda i: (0, dt_col // LANES)),
        ],
        out_specs=[pl.BlockSpec((tm, k), lambda i: (i, 0)),
                   pl.BlockSpec((tm, LANES), lambda i: (i, 0))],
        out_shape=[jax.ShapeDtypeStruct((m_p + m_s, k), BF16),
                   jax.ShapeDtypeStruct((m_p + m_s, LANES), F32)],
        compiler_params=_cparams(1),
        name="xcast_dt",
    )(xp, xs, w2d)


def _cast_weight_tile(wa_ref, wb_ref, wsc, shift):
    k, tn = wsc.shape
    rows = 256
    if shift == 0:
        def body(r, c):
            rs = pl.ds(pl.multiple_of(r * rows, rows), rows)
            wsc[rs, :] = wa_ref[rs, :].astype(BF16)
            return c
    else:
        lane = lax.broadcasted_iota(jnp.int32, (rows, LANES), 1)

        def body(r, c):
            rs = pl.ds(pl.multiple_of(r * rows, rows), rows)
            ar = pltpu.roll(wa_ref[rs, :], tn - shift, 1)
            br = pltpu.roll(wb_ref[rs, :], LANES - shift, 1)
            wsc[rs, 0:tn - LANES] = ar[:, 0:tn - LANES].astype(BF16)
            wsc[rs, tn - LANES:tn] = jnp.where(lane < LANES - shift, ar[:, tn - LANES:tn], br).astype(BF16)
            return c
    lax.fori_loop(0, k // rows, body, 0)


def _mm_w32_kernel(*refs, shift, n_out, head_rows):
    x_ref, wa_ref = refs[0], refs[1]
    pos = 2
    wb_ref = None
    if shift:
        wb_ref = refs[pos]
        pos += 1
    o_refs = refs[pos:pos + n_out]
    pos += n_out
    hp_ref = hs_ref = None
    if head_rows is not None:
        hp_ref, hs_ref = refs[pos], refs[pos + 1]
        pos += 2
    wsc = refs[pos]
    i = pl.program_id(1)

    @pl.when(i == 0)
    def _():
        _cast_weight_tile(wa_ref, wb_ref, wsc, shift)

    acc = _dot(x_ref[...], wsc[...])
    for o_ref in o_refs:
        o_ref[...] = acc.astype(o_ref.dtype)

    if head_rows is not None:
        tile_p, row0_p, tile_s = head_rows

        @pl.when(i == tile_p)
        def _():
            hp_ref[...] = acc[row0_p:, :]

        @pl.when(i == tile_s)
        def _():
            hs_ref[...] = acc


def _matmul_w32(x, w2d, col_blk, n_col_tiles, out_dtypes, *, tm, tn, shift=0, head_out=None, name):
    m, k = x.shape
    assert m % tm == 0 and tn % LANES == 0
    lpt = tn // LANES
    in_specs = [pl.BlockSpec((tm, k), lambda j, i: (i, 0)),
                pl.BlockSpec((k, tn), lambda j, i: (0, col_blk(j)))]
    args = [x, w2d]
    if shift:
        in_specs.append(pl.BlockSpec((k, LANES), lambda j, i: (0, (col_blk(j) + 1) * lpt)))
        args.append(w2d)
    n = n_col_tiles * tn
    out_specs = [pl.BlockSpec((tm, tn), lambda j, i: (i, j)) for _ in out_dtypes]
    out_shape = [jax.ShapeDtypeStruct((m, n), dt) for dt in out_dtypes]
    head_rows = None
    if head_out is not None:
        n_keep, m_s = head_out
        assert m_s == tm and n_keep <= tm
        tile_s = m // tm - 1
        tile_p = tile_s - 1
        head_rows = (tile_p, tm - n_keep, tile_s)
        out_specs += [pl.BlockSpec((n_keep, tn), lambda j, i: (0, j)),
                      pl.BlockSpec((m_s, tn), lambda j, i: (0, j))]
        out_shape += [jax.ShapeDtypeStruct((n_keep, n), F32), jax.ShapeDtypeStruct((m_s, n), F32)]
    kern = functools.partial(_mm_w32_kernel, shift=shift, n_out=len(out_dtypes), head_rows=head_rows)
    return pl.pallas_call(
        kern,
        grid=(n_col_tiles, m // tm),
        in_specs=in_specs,
        out_specs=out_specs,
        out_shape=out_shape,
        scratch_shapes=[pltpu.VMEM((k, tn), BF16)],
        compiler_params=_cparams(2),
        name=name,
    )(*args)


def _softplus(x):
    return jnp.maximum(x, 0.0) + jnp.log1p(jnp.exp(-jnp.abs(x)))


def _ssd_kernel(z_ref, xbc_ref, dt_ref, hist_ref, h0_ref, cw_ref, cb_ref, dtb_ref, alog_ref,
                dsk_ref, nw_ref, e_ref,
                y_ref, convp_ref, convs_ref, hop_ref, hos_ref,
                xpad, act, ht, cst, ex, ysc, *, n_prompt_chunks, inner, n_groups, d_state):
    c = pl.program_id(0)
    is_sample = c >= n_prompt_chunks
    gw = inner // n_groups
    conv_dim = inner + 2 * n_groups * d_state

    @pl.when(c == 0)
    def _():
        xpad[0:8, :] = jnp.zeros((8, conv_dim), F32)
        ht[...] = jnp.zeros_like(ht)

    @pl.when(is_sample)
    def _():
        xpad[5:8, :] = hist_ref[0]
        ht[...] = h0_ref[0].T

    xpad[8:8 + CHUNK, :] = xbc_ref[...]
    tail = xbc_ref[CHUNK - (CONV_W - 1):CHUNK, :]

    @pl.when(jnp.logical_not(is_sample))
    def _():
        convp_ref[0] = tail

    @pl.when(is_sample)
    def _():
        convs_ref[0] = tail

    lane_blk = 4 * LANES
    for j in range(conv_dim // lane_blk):
        sl = slice(j * lane_blk, (j + 1) * lane_blk)
        a = cb_ref[:, sl]
        for tap in range(CONV_W):
            a = a + xpad[5 + tap:5 + tap + CHUNK, sl] * cw_ref[tap:tap + 1, sl]
        act[:, sl] = a * jax.nn.sigmoid(a)
    xpad[5:8, :] = xpad[5 + CHUNK:8 + CHUNK, :]

    dt = _softplus(dt_ref[...] + dtb_ref[...])
    a_neg = -jnp.exp(alog_ref[...])
    da = dt * a_neg
    ri = lax.broadcasted_iota(jnp.int32, (CHUNK, CHUNK), 0)
    ci = lax.broadcasted_iota(jnp.int32, (CHUNK, CHUNK), 1)
    tri = (ri >= ci).astype(F32)
    cs = jnp.dot(tri, da, precision=lax.Precision.HIGHEST, preferred_element_type=F32)
    ecs = jnp.exp(cs)
    cs_last = cs[CHUNK - 1:CHUNK, :]
    dout = jnp.exp(cs_last - cs) * dt
    cd = jnp.exp(cs_last)
    cd_hi = cd.astype(BF16)
    cd_lo = (cd - cd_hi.astype(F32)).astype(BF16)
    lhs = jnp.concatenate(
        [dt.astype(BF16), ecs.astype(BF16), dout.astype(BF16),
         jnp.broadcast_to(cd_hi, (16, LANES)), jnp.broadcast_to(cd_lo, (16, LANES))], axis=0)
    ex[...] = _dot(lhs, e_ref[...])

    cst[...] = jnp.concatenate([cs, jnp.zeros((LANES - CHUNK, LANES), F32)], axis=0).T
    lane = lax.broadcasted_iota(jnp.int32, (CHUNK, LANES), 1)
    lo_half = lane < CHUNK
    n_pairs = inner // LANES
    ev = cst[pl.ds(0, n_pairs, stride=2), :]
    od = cst[pl.ds(1, n_pairs, stride=2), :]
    lane_p = lax.broadcasted_iota(jnp.int32, (n_pairs, LANES), 1)
    cst2 = jnp.where(lane_p < CHUNK, ev, pltpu.roll(od, CHUNK, 1))
    row = lax.broadcasted_iota(jnp.int32, (CHUNK, LANES), 0)
    causal2 = row >= jnp.where(lo_half, lane, lane - CHUNK)

    pairs_per_group = gw // LANES
    for g in range(n_groups):
        gs = slice(g * gw, (g + 1) * gw)
        bm_g = act[:, inner + g * d_state: inner + (g + 1) * d_state].astype(BF16)
        cm_g = act[:, inner + (n_groups + g) * d_state: inner + (n_groups + g + 1) * d_state].astype(BF16)
        cb2 = _dot_nt(cm_g, jnp.concatenate([bm_g, bm_g], axis=0))
        h_g = ht[:, gs]
        yoff = _dot(cm_g, h_g.astype(BF16))
        for p in range(pairs_per_group):
            k = g * pairs_per_group + p
            ls = slice(k * LANES, (k + 1) * LANES)
            csc = jnp.where(lo_half,
                            jnp.broadcast_to(cs[:, 2 * k:2 * k + 1], (CHUNK, LANES)),
                            jnp.broadcast_to(cs[:, 2 * k + 1:2 * k + 2], (CHUNK, LANES)))
            diff = csc - cst2[k:k + 1, :]
            dec = jnp.exp(jnp.where(causal2, diff, MASK_NEG))
            m2 = (cb2 * dec).astype(BF16)
            xs_p = act[:, ls]
            xdt = xs_p * ex[0:CHUNK, ls]
            x2 = jnp.concatenate([jnp.where(lo_half, xdt, 0.0), jnp.where(lo_half, 0.0, xdt)],
                                 axis=0).astype(BF16)
            yd = _dot(m2, x2)
            ysc[:, ls] = (yd + yoff[:, p * LANES:(p + 1) * LANES] * ex[CHUNK:2 * CHUNK, ls]
                          + dsk_ref[:, ls] * xs_p)
        xw = (act[:, gs] * ex[2 * CHUNK:3 * CHUNK, gs]).astype(BF16)
        st = _dot_tn(bm_g, xw)
        cdx = ex[3 * CHUNK:3 * CHUNK + 1, gs] + ex[3 * CHUNK + 16:3 * CHUNK + 17, gs]
        ht[:, gs] = h_g * cdx + st

    for g in range(n_groups):
        gs = slice(g * gw, (g + 1) * gw)
        zg = z_ref[:, gs].astype(F32)
        v = ysc[:, gs] * (zg * jax.nn.sigmoid(zg))
        ms = jnp.mean(v * v, axis=-1, keepdims=True)
        y_ref[:, gs] = (v * lax.rsqrt(ms + RMS_EPS) * nw_ref[:, gs]).astype(y_ref.dtype)

    @pl.when(c == n_prompt_chunks - 1)
    def _():
        hop_ref[0] = ht[...].T

    @pl.when(is_sample)
    def _():
        hos_ref[0] = ht[...].T


def _ssd(z, xbc, dtr, hist, h0, conv_w, conv_b, dtb, alog, dskx, normw, emat, *,
         n_prompt_chunks, n_sample, inner, n_groups, d_state):
    n_chunks = n_prompt_chunks + n_sample
    conv_dim = xbc.shape[1]
    hp = inner
    npc = n_prompt_chunks

    def stream(c):
        return jnp.maximum(c - npc, 0)

    kern = functools.partial(_ssd_kernel, n_prompt_chunks=npc, inner=inner, n_groups=n_groups,
                             d_state=d_state)
    const = lambda c: (0, 0)
    return pl.pallas_call(
        kern,
        grid=(n_chunks,),
        in_specs=[
            pl.BlockSpec((CHUNK, inner), lambda c: (c, 0)),
            pl.BlockSpec((CHUNK, conv_dim), lambda c: (c, 0)),
            pl.BlockSpec((CHUNK, LANES), lambda c: (c, 0)),
            pl.BlockSpec((1, CONV_W - 1, conv_dim), lambda c: (stream(c), 0, 0)),
            pl.BlockSpec((1, hp, d_state), lambda c: (stream(c), 0, 0)),
            pl.BlockSpec((CONV_W, conv_dim), const),
            pl.BlockSpec((1, conv_dim), const),
            pl.BlockSpec((1, LANES), const),
            pl.BlockSpec((1, LANES), const),
            pl.BlockSpec((1, inner), const),
            pl.BlockSpec((1, inner), const),
            pl.BlockSpec((LANES, inner), const),
        ],
        out_specs=[
            pl.BlockSpec((CHUNK, inner), lambda c: (c, 0)),
            pl.BlockSpec((1, CONV_W - 1, conv_dim), lambda c: (0, 0, 0)),
            pl.BlockSpec((1, CONV_W - 1, conv_dim), lambda c: (stream(c), 0, 0)),
            pl.BlockSpec((1, hp, d_state), lambda c: (0, 0, 0)),
            pl.BlockSpec((1, hp, d_state), lambda c: (stream(c), 0, 0)),
        ],
        out_shape=[
            jax.ShapeDtypeStruct((n_chunks * CHUNK, inner), BF16),
            jax.ShapeDtypeStruct((1, CONV_W - 1, conv_dim), F32),
            jax.ShapeDtypeStruct((n_sample, CONV_W - 1, conv_dim), F32),
            jax.ShapeDtypeStruct((1, hp, d_state), F32),
            jax.ShapeDtypeStruct((n_sample, hp, d_state), F32),
        ],
        scratch_shapes=[
            pltpu.VMEM((8 + CHUNK, conv_dim), F32),
            pltpu.VMEM((CHUNK, conv_dim), F32),
            pltpu.VMEM((d_state, inner), F32),
            pltpu.VMEM((LANES, LANES), F32),
            pltpu.VMEM((3 * CHUNK + 32, inner), F32),
            pltpu.VMEM((CHUNK, inner), F32),
        ],
        compiler_params=_cparams(1),
        name="ssd_scan",
    )(z, xbc, dtr, hist, h0, conv_w, conv_b, dtb, alog, dskx, normw, emat)


def _band_bias_tables(ext_ref, bias_scr, n_heads):
    col = lax.broadcasted_iota(jnp.int32, (CHUNK, ATT_WIN), 1)
    band = BAND_PAST + CHUNK
    for h in range(n_heads):
        e = jnp.broadcast_to(ext_ref[h:h + 1, :], (CHUNK, ATT_WIN))
        even = pltpu.roll(e, ATT_WIN - (CHUNK - 1), 1, stride=1, stride_axis=0)
        odd = pltpu.roll(e, 1, 1, stride=1, stride_axis=0)
        bias_scr[0, h] = jnp.where(col < band, even, MASK_NEG)
        bias_scr[1, h] = jnp.where(col >= CHUNK, odd, MASK_NEG)


def _attn_prompt_kernel(q_ref, kp_ref, kc_ref, vp_ref, vc_ref, ext_ref, o_ref,
                        bias_scr, s_scr, p_scr, *, n_heads):
    i = pl.program_id(0)
    d = ATT_HEAD_DIM
    scale = d ** -0.5
    half = ATT_TILE

    @pl.when(i == 0)
    def _():
        _band_bias_tables(ext_ref, bias_scr, n_heads)
        p_scr[...] = jnp.zeros_like(p_scr)

    past_mask = jnp.where(i == 0, MASK_NEG, 0.0).astype(F32)

    for h in range(n_heads):
        hs = slice(h * d, (h + 1) * d)
        q_h = q_ref[:, hs]
        s_scr[:, 0:half] = _dot_nt(q_h, kp_ref[:, hs]) * scale + past_mask
        s_scr[:, half:2 * half] = _dot_nt(q_h, kc_ref[:, hs]) * scale
        for t in range(ATT_TILE // CHUNK):
            rs = slice(t * CHUNK, (t + 1) * CHUNK)
            c0 = LANES * (t // 2)
            ws = slice(c0, c0 + ATT_WIN)
            s = s_scr[rs, ws] + bias_scr[t % 2, h]
            m = jnp.max(s, axis=-1, keepdims=True)
            p = jnp.exp(s - m)
            r = 1.0 / jnp.sum(p, axis=-1, keepdims=True)
            p_scr[rs, ws] = (p * r).astype(BF16)
        o_ref[:, hs] = (_dot(p_scr[:, 0:half], vp_ref[:, hs])
                        + _dot(p_scr[:, half:2 * half], vc_ref[:, hs])).astype(o_ref.dtype)


def _attn_prompt(q, kb, vb, ext, *, n_tiles, n_heads):
    width = n_heads * ATT_HEAD_DIM
    prev = lambda i: jnp.maximum(i - 1, 0)
    kern = functools.partial(_attn_prompt_kernel, n_heads=n_heads)
    return pl.pallas_call(
        kern,
        grid=(n_tiles,),
        in_specs=[
            pl.BlockSpec((ATT_TILE, width), lambda i: (i, 0)),
            pl.BlockSpec((ATT_TILE, width), lambda i: (prev(i), 0)),
            pl.BlockSpec((ATT_TILE, width), lambda i: (i, 0)),
            pl.BlockSpec((ATT_TILE, width), lambda i: (prev(i), 0)),
            pl.BlockSpec((ATT_TILE, width), lambda i: (i, 0)),
            pl.BlockSpec((n_heads, ATT_WIN), lambda i: (0, 0)),
        ],
        out_specs=pl.BlockSpec((ATT_TILE, width), lambda i: (i, 0)),
        out_shape=jax.ShapeDtypeStruct((n_tiles * ATT_TILE, width), BF16),
        scratch_shapes=[
            pltpu.VMEM((2, n_heads, CHUNK, ATT_WIN), F32),
            pltpu.VMEM((ATT_TILE, 2 * ATT_TILE), F32),
            pltpu.VMEM((ATT_TILE, 2 * ATT_TILE), BF16),
        ],
        compiler_params=_cparams(1),
        name="attn_prompt",
    )(q, kb, kb, vb, vb, ext)


def _attn_sample_kernel(q_ref, kn_ref, vn_ref, ck_ref, cv_ref, ext_ref, o_ref, bias_scr, *, n_heads):
    d = ATT_HEAD_DIM
    scale = d ** -0.5

    @pl.when(pl.program_id(0) == 0)
    def _():
        _band_bias_tables(ext_ref, bias_scr, n_heads)

    for h in range(n_heads):
        hs = slice(h * d, (h + 1) * d)
        rows = pl.ds(h, BAND_PAST, stride=n_heads)
        q_h = q_ref[:, hs]
        kc_h = ck_ref[rows, :].astype(BF16)
        vc_h = cv_ref[rows, :].astype(BF16)
        s1 = _dot_nt(q_h, kc_h) * scale + bias_scr[0, h, :, 0:BAND_PAST]
        s2 = _dot_nt(q_h, kn_ref[:, hs]) * scale + bias_scr[0, h, :, BAND_PAST:BAND_PAST + CHUNK]
        m = jnp.maximum(jnp.max(s1, axis=-1, keepdims=True), jnp.max(s2, axis=-1, keepdims=True))
        p1 = jnp.exp(s1 - m)
        p2 = jnp.exp(s2 - m)
        r = 1.0 / (jnp.sum(p1, axis=-1, keepdims=True) + jnp.sum(p2, axis=-1, keepdims=True))
        o = _dot((p1 * r).astype(BF16), vc_h) + _dot((p2 * r).astype(BF16), vn_ref[:, hs])
        o_ref[:, hs] = o.astype(o_ref.dtype)


def _attn_sample(q, kb, vb, cache_k, cache_v, ext, *, row_blk0, n_sample, n_heads):
    width = n_heads * ATT_HEAD_DIM
    kern = functools.partial(_attn_sample_kernel, n_heads=n_heads)
    cache_spec = pl.BlockSpec((None, BAND_PAST * n_heads, ATT_HEAD_DIM), lambda s: (s, 0, 0))
    row_spec = pl.BlockSpec((CHUNK, width), lambda s: (row_blk0 + s, 0))
    return pl.pallas_call(
        kern,
        grid=(n_sample,),
        in_specs=[row_spec, row_spec, row_spec, cache_spec, cache_spec,
                  pl.BlockSpec((n_heads, ATT_WIN), lambda s: (0, 0))],
        out_specs=pl.BlockSpec((CHUNK, width), lambda s: (s, 0)),
        out_shape=jax.ShapeDtypeStruct((n_sample * CHUNK, width), BF16),
        scratch_shapes=[pltpu.VMEM((2, n_heads, CHUNK, ATT_WIN), F32)],
        compiler_params=_cparams(1),
        name="attn_sample",
    )(q, kb, vb, cache_k, cache_v, ext)


def _mix_kernel(s_ref, ap_ref, as_ref, w1_ref, w2_ref, g1_ref, g2_ref, o_ref, *, n_prompt_tiles):
    i = pl.program_id(1)
    a1 = _dot(s_ref[...], w1_ref[...])
    att = jnp.where(i < n_prompt_tiles, ap_ref[...], as_ref[...])
    a2 = _dot(att, w2_ref[...])
    g1 = jax.nn.sigmoid(g1_ref[...].astype(F32))
    g2 = jax.nn.sigmoid(g2_ref[...].astype(F32))
    o_ref[...] = (g1 * a1 + g2 * a2).astype(o_ref.dtype)


def _mix(ssd_y, att_p, att_s, w1, w2, gates_arr, *, gate_col0, tm, tn):
    m, k1 = ssd_y.shape
    k2 = att_p.shape[1]
    n = w1.shape[1]
    npt = att_p.shape[0] // tm
    g1_blk = gate_col0 // tn
    g2_blk = (gate_col0 + n) // tn
    kern = functools.partial(_mix_kernel, n_prompt_tiles=npt)
    return pl.pallas_call(
        kern,
        grid=(n // tn, m // tm),
        in_specs=[
            pl.BlockSpec((tm, k1), lambda j, i: (i, 0)),
            pl.BlockSpec((tm, k2), lambda j, i: (jnp.minimum(i, npt - 1), 0)),
            pl.BlockSpec((tm, k2), lambda j, i: (jnp.maximum(i - npt, 0), 0)),
            pl.BlockSpec((k1, tn), lambda j, i: (0, j)),
            pl.BlockSpec((k2, tn), lambda j, i: (0, j)),
            pl.BlockSpec((tm, tn), lambda j, i: (i, g1_blk + j)),
            pl.BlockSpec((tm, tn), lambda j, i: (i, g2_blk + j)),
        ],
        out_specs=pl.BlockSpec((tm, tn), lambda j, i: (i, j)),
        out_shape=jax.ShapeDtypeStruct((m, n), BF16),
        compiler_params=_cparams(2),
        name="branch_mix",
    )(ssd_y, att_p, att_s, w1, w2, gates_arr, gates_arr)


def _layer_norm(t, g, b):
    mu = jnp.mean(t, axis=-1, keepdims=True)
    tc = t - mu
    var = jnp.mean(tc * tc, axis=-1, keepdims=True)
    return tc * lax.rsqrt(var + LN_EPS) * g + b


def _proj_ln_kernel(*refs, alpha, n_k, n_prompt_tiles, split_resid, split_out, n_out):
    a_ref, w_ref = refs[0], refs[1]
    pos = 2
    r_refs = refs[pos:pos + (2 if split_resid else 1)]
    pos += len(r_refs)
    g_ref, b_ref = refs[pos], refs[pos + 1]
    pos += 2
    o_refs = refs[pos:pos + n_out]
    acc_ref = refs[-1]
    i = pl.program_id(0)
    kk = pl.program_id(1)
    part = _dot(a_ref[...], w_ref[...])

    @pl.when(kk == 0)
    def _():
        if split_resid:
            resid = jnp.where(i < n_prompt_tiles, r_refs[0][...], r_refs[1][...])
        else:
            resid = r_refs[0][...]
        acc_ref[...] = alpha * resid + part

    @pl.when(kk > 0)
    def _():
        acc_ref[...] += part

    if split_out:
        @pl.when(jnp.logical_and(kk == n_k - 1, i < n_prompt_tiles))
        def _():
            o_refs[0][...] = _layer_norm(acc_ref[...], g_ref[...], b_ref[...]).astype(o_refs[0].dtype)

        @pl.when(jnp.logical_and(kk == n_k - 1, i >= n_prompt_tiles))
        def _():
            o_refs[1][...] = _layer_norm(acc_ref[...], g_ref[...], b_ref[...]).astype(o_refs[1].dtype)
    else:
        @pl.when(kk == n_k - 1)
        def _():
            y = _layer_norm(acc_ref[...], g_ref[...], b_ref[...])
            for o_ref in o_refs:
                o_ref[...] = y.astype(o_ref.dtype)


def _proj_ln(a, w, resid, gamma, beta, out_dtypes, *, alpha, tm, n_k, m_prompt, split_out, name):
    m, k = a.shape
    n = w.shape[1]
    tk = k // n_k
    assert tk * n_k == k and m % tm == 0 and m_prompt % tm == 0
    npt = m_prompt // tm
    split_resid = isinstance(resid, tuple)
    p_map = lambda i, kk: (jnp.minimum(i, npt - 1), 0)
    s_map = lambda i, kk: (jnp.maximum(i - npt, 0), 0)
    row_map = lambda i, kk: (i, 0)
    w_mode = dict(pipeline_mode=pl.Buffered(1)) if n_k == 1 else {}
    in_specs = [pl.BlockSpec((tm, tk), lambda i, kk: (i, kk)),
                pl.BlockSpec((tk, n), lambda i, kk: (kk, 0), **w_mode)]
    args = [a, w]
    if split_resid:
        in_specs += [pl.BlockSpec((tm, n), p_map), pl.BlockSpec((tm, n), s_map)]
        args += list(resid)
    else:
        in_specs.append(pl.BlockSpec((tm, n), row_map))
        args.append(resid)
    in_specs += [pl.BlockSpec((1, n), lambda i, kk: (0, 0))] * 2
    args += [gamma, beta]
    if split_out:
        (dt,) = out_dtypes
        out_specs = [pl.BlockSpec((tm, n), p_map), pl.BlockSpec((tm, n), s_map)]
        out_shape = [jax.ShapeDtypeStruct((m_prompt, n), dt), jax.ShapeDtypeStruct((m - m_prompt, n), dt)]
    else:
        out_specs = [pl.BlockSpec((tm, n), row_map) for _ in out_dtypes]
        out_shape = [jax.ShapeDtypeStruct((m, n), dt) for dt in out_dtypes]
    kern = functools.partial(_proj_ln_kernel, alpha=alpha, n_k=n_k, n_prompt_tiles=npt,
                             split_resid=split_resid, split_out=split_out, n_out=len(out_shape))
    return pl.pallas_call(
        kern,
        grid=(m // tm, n_k),
        in_specs=in_specs,
        out_specs=out_specs,
        out_shape=out_shape,
        scratch_shapes=[pltpu.VMEM((tm, n), F32)],
        compiler_params=_cparams(2),
        name=name,
    )(*args)


def _glu_kernel(h_ref, wg_ref, wu_ref, o_ref, wg_sc, wu_sc):
    @pl.when(pl.program_id(1) == 0)
    def _():
        _cast_weight_tile(wg_ref, None, wg_sc, 0)
        _cast_weight_tile(wu_ref, None, wu_sc, 0)

    h = h_ref[...]
    g = _dot(h, wg_sc[...])
    u = _dot(h, wu_sc[...])
    o_ref[...] = (g * jax.nn.sigmoid(g) * u).astype(o_ref.dtype)


def _glu(h, w_gate_up, *, tm, tf):
    m, k = h.shape
    d_ff = w_gate_up.shape[1] // 2
    nf = d_ff // tf
    assert nf * tf == d_ff
    return pl.pallas_call(
        _glu_kernel,
        grid=(nf, m // tm),
        in_specs=[
            pl.BlockSpec((tm, k), lambda j, i: (i, 0)),
            pl.BlockSpec((k, tf), lambda j, i: (0, j)),
            pl.BlockSpec((k, tf), lambda j, i: (0, nf + j)),
        ],
        out_specs=pl.BlockSpec((tm, tf), lambda j, i: (i, j)),
        out_shape=jax.ShapeDtypeStruct((m, d_ff), BF16),
        scratch_shapes=[pltpu.VMEM((k, tf), BF16), pltpu.VMEM((k, tf), BF16)],
        compiler_params=_cparams(2),
        name="ffn_glu",
    )(h, w_gate_up, w_gate_up)


def kernel(x_prompt, x_sample, cache_k, cache_v, state_conv, state_ssm, w_in, conv_w, conv_b, dt_bias,
           a_log, d_skip, ssd_norm_w, rel_bias, w_ssd_out, w_att_out, w_o, ln1_g, ln1_b, w_gate_up,
           w_down, ln2_g, ln2_b):
    depth = w_in.shape[0]
    assert depth == 1
    batch, seq, d_model = x_prompt.shape
    n_sample, dec_seq, _ = x_sample.shape
    assert batch == 1 and dec_seq == CHUNK and seq % ATT_TILE == 0
    n_ssd_heads = dt_bias.shape[1]
    inner = n_ssd_heads * SSD_HEADDIM
    conv_dim = conv_w.shape[2]
    n_groups = SSD_GROUPS
    d_state = SSD_STATE
    assert conv_dim == inner + 2 * n_groups * d_state
    n_heads = rel_bias.shape[1]
    width = n_heads * ATT_HEAD_DIM
    assert cache_k.shape[2] == BAND_PAST
    alpha = (2.0 * depth) ** 0.25
    keep = min(BAND_PAST, seq)

    m_p = batch * seq
    m_s = n_sample * dec_seq
    xp2 = x_prompt.reshape(m_p, d_model)
    xs2 = x_sample.reshape(m_s, d_model)

    w2d = w_in[0]
    tn = 1024
    o_xbc = inner
    o_dt = o_xbc + conv_dim
    o_q = o_dt + n_ssd_heads
    shift = o_q % LANES
    assert o_dt % tn == 0 and inner % tn == 0 and conv_dim % tn == 0 and width % tn == 0
    q_blk = o_dt // tn
    xb, dtr = _xcast_dt(xp2, xs2, w2d, dt_col=o_dt, n_dt=n_ssd_heads, tm=512)
    (z,) = _matmul_w32(xb, w2d, lambda j: j, inner // tn, [BF16], tm=1024, tn=tn, name="in_proj_z")
    (xbc,) = _matmul_w32(xb, w2d, lambda j: inner // tn + j, conv_dim // tn, [F32], tm=1024, tn=tn,
                         name="in_proj_xbc")
    wpt = width // tn
    (qg,) = _matmul_w32(xb, w2d, lambda j: jnp.where(j < wpt, q_blk + j, q_blk + 2 * wpt + j),
                        wpt + 2 * d_model // tn, [BF16], tm=1024, tn=tn, shift=shift, name="in_proj_qg")
    kb, k_p, k_s = _matmul_w32(xb, w2d, lambda j: q_blk + wpt + j, wpt, [BF16], tm=1024, tn=tn, shift=shift,
                               head_out=(keep, m_s), name="in_proj_k")
    vb, v_p, v_s = _matmul_w32(xb, w2d, lambda j: q_blk + 2 * wpt + j, wpt, [BF16], tm=1024, tn=tn,
                               shift=shift, head_out=(keep, m_s), name="in_proj_v")

    pad_h = (0, LANES - n_ssd_heads)
    dtb = jnp.pad(dt_bias[0], pad_h).reshape(1, LANES)
    alog = jnp.pad(a_log[0], pad_h).reshape(1, LANES)
    dskx = jnp.repeat(d_skip[0], SSD_HEADDIM).reshape(1, inner)
    emat = (jnp.arange(LANES)[:, None] == (jnp.arange(inner)[None, :] // SSD_HEADDIM)).astype(BF16)
    ssd_y, conv_p, conv_s, h_p, h_s = _ssd(
        z, xbc, dtr, state_conv[0], state_ssm[0].reshape(n_sample, inner, d_state),
        conv_w[0], conv_b[0].reshape(1, conv_dim), dtb, alog, dskx, ssd_norm_w[0].reshape(1, inner), emat,
        n_prompt_chunks=m_p // CHUNK, n_sample=n_sample, inner=inner, n_groups=n_groups, d_state=d_state)

    rb = rel_bias[0]
    band = BAND_PAST + CHUNK
    n_edge = band - REL_CLIP
    n_rev = ATT_WIN - n_edge
    rev = rb[:, ::-1][:, 1:1 + n_rev]
    ext = jnp.concatenate([jnp.broadcast_to(rb[:, 2 * REL_CLIP:], (n_heads, n_edge)), rev], axis=1)
    ext = jnp.pad(ext, ((0, 0), (0, ATT_WIN - ext.shape[1])))
    att_p = _attn_prompt(qg, kb, vb, ext, n_tiles=m_p // ATT_TILE, n_heads=n_heads)
    cache_rows = (n_sample, BAND_PAST * n_heads, ATT_HEAD_DIM)
    att_s = _attn_sample(qg, kb, vb, cache_k[0].reshape(cache_rows), cache_v[0].reshape(cache_rows), ext,
                         row_blk0=m_p // CHUNK, n_sample=n_sample, n_heads=n_heads)

    mixed = _mix(ssd_y, att_p, att_s, w_ssd_out[0].astype(BF16), w_att_out[0].astype(BF16), qg,
                 gate_col0=width, tm=512, tn=1024)
    g1 = ln1_g[0].reshape(1, d_model)
    b1 = ln1_b[0].reshape(1, d_model)
    g2 = ln2_g[0].reshape(1, d_model)
    b2 = ln2_b[0].reshape(1, d_model)
    h_f, h_b = _proj_ln(mixed, w_o[0].astype(BF16), (xp2, xs2), g1, b1, [F32, BF16], alpha=alpha, tm=512,
                        n_k=1, m_prompt=m_p, split_out=False, name="wo_ln")
    act = _glu(h_b, w_gate_up[0], tm=1024, tf=512)
    y_p, y_s = _proj_ln(act, w_down[0].astype(BF16), h_f, g2, b2, [F32], alpha=alpha, tm=512, n_k=4,
                        m_prompt=m_p, split_out=True, name="down_ln")

    return (y_p.reshape(batch, seq, d_model),
            y_s.reshape(n_sample, dec_seq, d_model),
            conv_p.reshape(1, batch, CONV_W - 1, conv_dim),
            h_p.reshape(1, batch, n_ssd_heads, SSD_HEADDIM, d_state),
            k_p.reshape(1, batch, keep, n_heads, ATT_HEAD_DIM),
            v_p.reshape(1, batch, keep, n_heads, ATT_HEAD_DIM),
            conv_s.reshape(1, n_sample, CONV_W - 1, conv_dim),
            h_s.reshape(1, n_sample, n_ssd_heads, SSD_HEADDIM, d_state),
            k_s.reshape(1, n_sample, dec_seq, n_heads, ATT_HEAD_DIM),
            v_s.reshape(1, n_sample, dec_seq, n_heads, ATT_HEAD_DIM))
```

```python
import functools

import jax
import jax.numpy as jnp
from jax import lax
from jax.experimental import pallas as pl
from jax.experimental.pallas import tpu as pltpu

F32 = jnp.float32
BF16 = jnp.bfloat16

CHUNK = 64
SSD_HEADDIM = 64
SSD_GROUPS = 8
SSD_STATE = 128
CONV_W = 4
ATT_HEAD_DIM = 128
BAND_CHUNKS = 8
BAND_PAST = BAND_CHUNKS * CHUNK
REL_CLIP = 128
LN_EPS = 1e-5
RMS_EPS = 1e-5
MASK_NEG = -1e30

LANES = 128
SUBLANES = 8
VMEM_LIMIT_BYTES = 56 * 1024 * 1024

ATT_TILE = 512
ATT_WIN = 5 * LANES


def _cparams(n_axes):
    return pltpu.CompilerParams(
        dimension_semantics=("arbitrary",) * n_axes,
        vmem_limit_bytes=VMEM_LIMIT_BYTES,
    )


def _dot(a, b):
    return jnp.dot(a, b, preferred_element_type=F32)


def _dot_nt(a, b):
    return lax.dot_general(a, b, (((1,), (1,)), ((), ())), preferred_element_type=F32)


def _dot_tn(a, b):
    return lax.dot_general(a, b, (((0,), (0,)), ((), ())), preferred_element_type=F32)


def _xcast_dt_kernel(xp_ref, xs_ref, wd_ref, xb_ref, dt_ref, *, n_prompt_tiles, n_dt):
    i = pl.program_id(0)
    x = jnp.where(i < n_prompt_tiles, xp_ref[...], xs_ref[...]).astype(BF16)
    xb_ref[...] = x
    acc = _dot_nt(x, wd_ref[...].astype(BF16))
    lane = lax.broadcasted_iota(jnp.int32, acc.shape, 1)
    dt_ref[...] = jnp.where(lane < n_dt, acc, 0.0)


def _xcast_dt(xp, xs, wt, *, dt_col, n_dt, tm):
    m_p, k = xp.shape
    m_s = xs.shape[0]
    npt = m_p // tm
    n_tiles = npt + m_s // tm
    assert dt_col % LANES == 0 and n_dt <= LANES
    kern = functools.partial(_xcast_dt_kernel, n_prompt_tiles=npt, n_dt=n_dt)
    return pl.pallas_call(
        kern,
        grid=(n_tiles,),
        in_specs=[
            pl.BlockSpec((tm, k), lambda i: (jnp.minimum(i, npt - 1), 0)),
            pl.BlockSpec((tm, k), lambda i: (jnp.maximum(i - npt, 0), 0)),
            pl.BlockSpec((LANES, k), lambda i: (dt_col // LANES, 0)),
        ],
        out_specs=[pl.BlockSpec((tm, k), lambda i: (i, 0)),
                   pl.BlockSpec((tm, LANES), lambda i: (i, 0))],
        out_shape=[jax.ShapeDtypeStruct((m_p + m_s, k), BF16),
                   jax.ShapeDtypeStruct((m_p + m_s, LANES), F32)],
        compiler_params=_cparams(1),
        name="xcast_dt",
    )(xp, xs, wt)


CAST_ROWS = 256


def _cast_rows(src_ref, src_row0, dst_ref, dst_row0, n_rows):
    done = 0
    while done < n_rows:
        step = min(CAST_ROWS, n_rows - done)
        dst_ref[dst_row0 + done:dst_row0 + done + step, :] = (
            src_ref[src_row0 + done:src_row0 + done + step, :].astype(BF16))
        done += step


def _cast_weight_tile(w_ref, wsc):
    _cast_rows(w_ref, 0, wsc, 0, wsc.shape[0])


def _mm_w32_kernel(*refs, shift, n_out, head_rows):
    x_ref, wa_ref = refs[0], refs[1]
    pos = 2
    wb_ref = None
    if shift:
        wb_ref = refs[pos]
        pos += 1
    o_refs = refs[pos:pos + n_out]
    pos += n_out
    hp_ref = hs_ref = None
    if head_rows is not None:
        hp_ref, hs_ref = refs[pos], refs[pos + 1]
        pos += 2
    wsc = refs[pos]
    i = pl.program_id(1)

    @pl.when(i == 0)
    def _():
        tn = wsc.shape[0]
        _cast_rows(wa_ref, shift, wsc, 0, tn - shift)
        if shift:
            _cast_rows(wb_ref, 0, wsc, tn - shift, shift)

    acc = _dot_nt(x_ref[...], wsc[...])
    for o_ref in o_refs:
        o_ref[...] = acc.astype(o_ref.dtype)

    if head_rows is not None:
        tile_p, row0_p, tile_s = head_rows

        @pl.when(i == tile_p)
        def _():
            hp_ref[...] = acc[row0_p:, :]

        @pl.when(i == tile_s)
        def _():
            hs_ref[...] = acc


def _matmul_w32(x, wt, col_blk, n_col_tiles, out_dtypes, *, tm, tn, shift=0, head_out=None, name):
    m, k = x.shape
    assert m % tm == 0 and tn % LANES == 0 and shift % 16 == 0 and shift < tn
    in_specs = [pl.BlockSpec((tm, k), lambda j, i: (i, 0)),
                pl.BlockSpec((tn, k), lambda j, i: (col_blk(j), 0))]
    args = [x, wt]
    if shift:
        assert tn % shift == 0
        per = tn // shift
        in_specs.append(pl.BlockSpec((shift, k), lambda j, i: ((col_blk(j) + 1) * per, 0)))
        args.append(wt)
    n = n_col_tiles * tn
    out_specs = [pl.BlockSpec((tm, tn), lambda j, i: (i, j)) for _ in out_dtypes]
    out_shape = [jax.ShapeDtypeStruct((m, n), dt) for dt in out_dtypes]
    head_rows = None
    if head_out is not None:
        n_keep, m_s = head_out
        assert m_s == tm and n_keep <= tm
        tile_s = m // tm - 1
        tile_p = tile_s - 1
        head_rows = (tile_p, tm - n_keep, tile_s)
        out_specs += [pl.BlockSpec((n_keep, tn), lambda j, i: (0, j)),
                      pl.BlockSpec((m_s, tn), lambda j, i: (0, j))]
        out_shape += [jax.ShapeDtypeStruct((n_keep, n), F32), jax.ShapeDtypeStruct((m_s, n), F32)]
    kern = functools.partial(_mm_w32_kernel, shift=shift, n_out=len(out_dtypes), head_rows=head_rows)
    return pl.pallas_call(
        kern,
        grid=(n_col_tiles, m // tm),
        in_specs=in_specs,
        out_specs=out_specs,
        out_shape=out_shape,
        scratch_shapes=[pltpu.VMEM((tn, k), BF16)],
        compiler_params=_cparams(2),
        name=name,
    )(*args)


def _softplus(x):
    return jnp.maximum(x, 0.0) + jnp.log1p(jnp.exp(-jnp.abs(x)))


def _ssd_kernel(z_ref, xbc_ref, dt_ref, hist_ref, h0_ref, cw_ref, cb_ref, dtb_ref, alog_ref,
                dsk_ref, nw_ref, e_ref,
                y_ref, convp_ref, convs_ref, hop_ref, hos_ref,
                xpad, act, ht, cst, ex, ysc, *, n_prompt_chunks, inner, n_groups, d_state):
    c = pl.program_id(0)
    is_sample = c >= n_prompt_chunks
    gw = inner // n_groups
    conv_dim = inner + 2 * n_groups * d_state

    @pl.when(c == 0)
    def _():
        xpad[0:8, :] = jnp.zeros((8, conv_dim), F32)
        ht[...] = jnp.zeros_like(ht)

    @pl.when(is_sample)
    def _():
        xpad[5:8, :] = hist_ref[0]
        ht[...] = h0_ref[0].T

    xpad[8:8 + CHUNK, :] = xbc_ref[...]
    tail = xbc_ref[CHUNK - (CONV_W - 1):CHUNK, :]

    @pl.when(jnp.logical_not(is_sample))
    def _():
        convp_ref[0] = tail

    @pl.when(is_sample)
    def _():
        convs_ref[0] = tail

    lane_blk = 4 * LANES
    for j in range(conv_dim // lane_blk):
        sl = slice(j * lane_blk, (j + 1) * lane_blk)
        a = cb_ref[:, sl]
        for tap in range(CONV_W):
            a = a + xpad[5 + tap:5 + tap + CHUNK, sl] * cw_ref[tap:tap + 1, sl]
        act[:, sl] = a * jax.nn.sigmoid(a)
    xpad[5:8, :] = xpad[5 + CHUNK:8 + CHUNK, :]

    dt = _softplus(dt_ref[...] + dtb_ref[...])
    a_neg = -jnp.exp(alog_ref[...])
    da = dt * a_neg
    ri = lax.broadcasted_iota(jnp.int32, (CHUNK, CHUNK), 0)
    ci = lax.broadcasted_iota(jnp.int32, (CHUNK, CHUNK), 1)
    tri = (ri >= ci).astype(F32)
    cs = jnp.dot(tri, da, precision=lax.Precision.HIGHEST, preferred_element_type=F32)
    ecs = jnp.exp(cs)
    cs_last = cs[CHUNK - 1:CHUNK, :]
    dout = jnp.exp(cs_last - cs) * dt
    cd = jnp.exp(cs_last)
    cd_hi = cd.astype(BF16)
    cd_lo = (cd - cd_hi.astype(F32)).astype(BF16)
    lhs = jnp.concatenate(
        [dt.astype(BF16), ecs.astype(BF16), dout.astype(BF16),
         jnp.broadcast_to(cd_hi, (16, LANES)), jnp.broadcast_to(cd_lo, (16, LANES))], axis=0)
    ex[...] = _dot(lhs, e_ref[...])

    cst[...] = jnp.concatenate([cs, jnp.zeros((LANES - CHUNK, LANES), F32)], axis=0).T
    lane = lax.broadcasted_iota(jnp.int32, (CHUNK, LANES), 1)
    lo_half = lane < CHUNK
    n_pairs = inner // LANES
    ev = cst[pl.ds(0, n_pairs, stride=2), :]
    od = cst[pl.ds(1, n_pairs, stride=2), :]
    lane_p = lax.broadcasted_iota(jnp.int32, (n_pairs, LANES), 1)
    cst2 = jnp.where(lane_p < CHUNK, ev, pltpu.roll(od, CHUNK, 1))
    row = lax.broadcasted_iota(jnp.int32, (CHUNK, LANES), 0)
    causal2 = row >= jnp.where(lo_half, lane, lane - CHUNK)

    pairs_per_group = gw // LANES
    for g in range(n_groups):
        gs = slice(g * gw, (g + 1) * gw)
        bm_g = act[:, inner + g * d_state: inner + (g + 1) * d_state].astype(BF16)
        cm_g = act[:, inner + (n_groups + g) * d_state: inner + (n_groups + g + 1) * d_state].astype(BF16)
        cb2 = _dot_nt(cm_g, jnp.concatenate([bm_g, bm_g], axis=0))
        h_g = ht[:, gs]
        yoff = _dot(cm_g, h_g.astype(BF16))
        for p in range(pairs_per_group):
            k = g * pairs_per_group + p
            ls = slice(k * LANES, (k + 1) * LANES)
            csc = jnp.where(lo_half,
                            jnp.broadcast_to(cs[:, 2 * k:2 * k + 1], (CHUNK, LANES)),
                            jnp.broadcast_to(cs[:, 2 * k + 1:2 * k + 2], (CHUNK, LANES)))
            diff = csc - cst2[k:k + 1, :]
            dec = jnp.exp(jnp.where(causal2, diff, MASK_NEG))
            m2 = (cb2 * dec).astype(BF16)
            xs_p = act[:, ls]
            xdt = xs_p * ex[0:CHUNK, ls]
            x2 = jnp.concatenate([jnp.where(lo_half, xdt, 0.0), jnp.where(lo_half, 0.0, xdt)],
                                 axis=0).astype(BF16)
            yd = _dot(m2, x2)
            ysc[:, ls] = (yd + yoff[:, p * LANES:(p + 1) * LANES] * ex[CHUNK:2 * CHUNK, ls]
                          + dsk_ref[:, ls] * xs_p)
        xw = (act[:, gs] * ex[2 * CHUNK:3 * CHUNK, gs]).astype(BF16)
        st = _dot_tn(bm_g, xw)
        cdx = ex[3 * CHUNK:3 * CHUNK + 1, gs] + ex[3 * CHUNK + 16:3 * CHUNK + 17, gs]
        ht[:, gs] = h_g * cdx + st

    for g in range(n_groups):
        gs = slice(g * gw, (g + 1) * gw)
        zg = z_ref[:, gs].astype(F32)
        v = ysc[:, gs] * (zg * jax.nn.sigmoid(zg))
        ms = jnp.mean(v * v, axis=-1, keepdims=True)
        y_ref[:, gs] = (v * lax.rsqrt(ms + RMS_EPS) * nw_ref[:, gs]).astype(y_ref.dtype)

    @pl.when(c == n_prompt_chunks - 1)
    def _():
        hop_ref[0] = ht[...].T

    @pl.when(is_sample)
    def _():
        hos_ref[0] = ht[...].T


def _ssd(z, xbc, dtr, hist, h0, conv_w, conv_b, dtb, alog, dskx, normw, emat, *,
         n_prompt_chunks, n_sample, inner, n_groups, d_state):
    n_chunks = n_prompt_chunks + n_sample
    conv_dim = xbc.shape[1]
    hp = inner
    npc = n_prompt_chunks

    def stream(c):
        return jnp.maximum(c - npc, 0)

    kern = functools.partial(_ssd_kernel, n_prompt_chunks=npc, inner=inner, n_groups=n_groups,
                             d_state=d_state)
    const = lambda c: (0, 0)
    return pl.pallas_call(
        kern,
        grid=(n_chunks,),
        in_specs=[
            pl.BlockSpec((CHUNK, inner), lambda c: (c, 0)),
            pl.BlockSpec((CHUNK, conv_dim), lambda c: (c, 0)),
            pl.BlockSpec((CHUNK, LANES), lambda c: (c, 0)),
            pl.BlockSpec((1, CONV_W - 1, conv_dim), lambda c: (stream(c), 0, 0)),
            pl.BlockSpec((1, hp, d_state), lambda c: (stream(c), 0, 0)),
            pl.BlockSpec((CONV_W, conv_dim), const),
            pl.BlockSpec((1, conv_dim), const),
            pl.BlockSpec((1, LANES), const),
            pl.BlockSpec((1, LANES), const),
            pl.BlockSpec((1, inner), const),
            pl.BlockSpec((1, inner), const),
            pl.BlockSpec((LANES, inner), const),
        ],
        out_specs=[
            pl.BlockSpec((CHUNK, inner), lambda c: (c, 0)),
            pl.BlockSpec((1, CONV_W - 1, conv_dim), lambda c: (0, 0, 0)),
            pl.BlockSpec((1, CONV_W - 1, conv_dim), lambda c: (stream(c), 0, 0)),
            pl.BlockSpec((1, hp, d_state), lambda c: (0, 0, 0)),
            pl.BlockSpec((1, hp, d_state), lambda c: (stream(c), 0, 0)),
        ],
        out_shape=[
            jax.ShapeDtypeStruct((n_chunks * CHUNK, inner), BF16),
            jax.ShapeDtypeStruct((1, CONV_W - 1, conv_dim), F32),
            jax.ShapeDtypeStruct((n_sample, CONV_W - 1, conv_dim), F32),
            jax.ShapeDtypeStruct((1, hp, d_state), F32),
            jax.ShapeDtypeStruct((n_sample, hp, d_state), F32),
        ],
        scratch_shapes=[
            pltpu.VMEM((8 + CHUNK, conv_dim), F32),
            pltpu.VMEM((CHUNK, conv_dim), F32),
            pltpu.VMEM((d_state, inner), F32),
            pltpu.VMEM((LANES, LANES), F32),
            pltpu.VMEM((3 * CHUNK + 32, inner), F32),
            pltpu.VMEM((CHUNK, inner), F32),
        ],
        compiler_params=_cparams(1),
        name="ssd_scan",
    )(z, xbc, dtr, hist, h0, conv_w, conv_b, dtb, alog, dskx, normw, emat)


def _band_bias_tables(ext_ref, bias_scr, n_heads):
    col = lax.broadcasted_iota(jnp.int32, (CHUNK, ATT_WIN), 1)
    band = BAND_PAST + CHUNK
    for h in range(n_heads):
        e = jnp.broadcast_to(ext_ref[h:h + 1, :], (CHUNK, ATT_WIN))
        even = pltpu.roll(e, ATT_WIN - (CHUNK - 1), 1, stride=1, stride_axis=0)
        odd = pltpu.roll(e, 1, 1, stride=1, stride_axis=0)
        bias_scr[0, h] = jnp.where(col < band, even, MASK_NEG)
        bias_scr[1, h] = jnp.where(col >= CHUNK, odd, MASK_NEG)


def _attn_prompt_kernel(q_ref, kp_ref, kc_ref, vp_ref, vc_ref, ext_ref, o_ref,
                        bias_scr, s_scr, p_scr, *, n_heads):
    i = pl.program_id(0)
    d = ATT_HEAD_DIM
    scale = d ** -0.5
    half = ATT_TILE

    @pl.when(i == 0)
    def _():
        _band_bias_tables(ext_ref, bias_scr, n_heads)
        p_scr[...] = jnp.zeros_like(p_scr)

    past_mask = jnp.where(i == 0, MASK_NEG, 0.0).astype(F32)

    for h in range(n_heads):
        hs = slice(h * d, (h + 1) * d)
        q_h = q_ref[:, hs]
        s_scr[:, 0:half] = _dot_nt(q_h, kp_ref[:, hs]) * scale + past_mask
        s_scr[:, half:2 * half] = _dot_nt(q_h, kc_ref[:, hs]) * scale
        for t in range(ATT_TILE // CHUNK):
            rs = slice(t * CHUNK, (t + 1) * CHUNK)
            c0 = LANES * (t // 2)
            ws = slice(c0, c0 + ATT_WIN)
            s = s_scr[rs, ws] + bias_scr[t % 2, h]
            m = jnp.max(s, axis=-1, keepdims=True)
            p = jnp.exp(s - m)
            r = 1.0 / jnp.sum(p, axis=-1, keepdims=True)
            p_scr[rs, ws] = (p * r).astype(BF16)
        o_ref[:, hs] = (_dot(p_scr[:, 0:half], vp_ref[:, hs])
                        + _dot(p_scr[:, half:2 * half], vc_ref[:, hs])).astype(o_ref.dtype)


def _attn_prompt(q, kb, vb, ext, *, n_tiles, n_heads):
    width = n_heads * ATT_HEAD_DIM
    prev = lambda i: jnp.maximum(i - 1, 0)
    kern = functools.partial(_attn_prompt_kernel, n_heads=n_heads)
    return pl.pallas_call(
        kern,
        grid=(n_tiles,),
        in_specs=[
            pl.BlockSpec((ATT_TILE, width), lambda i: (i, 0)),
            pl.BlockSpec((ATT_TILE, width), lambda i: (prev(i), 0)),
            pl.BlockSpec((ATT_TILE, width), lambda i: (i, 0)),
            pl.BlockSpec((ATT_TILE, width), lambda i: (prev(i), 0)),
            pl.BlockSpec((ATT_TILE, width), lambda i: (i, 0)),
            pl.BlockSpec((n_heads, ATT_WIN), lambda i: (0, 0)),
        ],
        out_specs=pl.BlockSpec((ATT_TILE, width), lambda i: (i, 0)),
        out_shape=jax.ShapeDtypeStruct((n_tiles * ATT_TILE, width), BF16),
        scratch_shapes=[
            pltpu.VMEM((2, n_heads, CHUNK, ATT_WIN), F32),
            pltpu.VMEM((ATT_TILE, 2 * ATT_TILE), F32),
            pltpu.VMEM((ATT_TILE, 2 * ATT_TILE), BF16),
        ],
        compiler_params=_cparams(1),
        name="attn_prompt",
    )(q, kb, kb, vb, vb, ext)


def _attn_sample_kernel(q_ref, kn_ref, vn_ref, ck_ref, cv_ref, ext_ref, o_ref, bias_scr, *, n_heads):
    d = ATT_HEAD_DIM
    scale = d ** -0.5

    @pl.when(pl.program_id(0) == 0)
    def _():
        _band_bias_tables(ext_ref, bias_scr, n_heads)

    for h in range(n_heads):
        hs = slice(h * d, (h + 1) * d)
        rows = pl.ds(h, BAND_PAST, stride=n_heads)
        q_h = q_ref[:, hs]
        kc_h = ck_ref[rows, :].astype(BF16)
        vc_h = cv_ref[rows, :].astype(BF16)
        s1 = _dot_nt(q_h, kc_h) * scale + bias_scr[0, h, :, 0:BAND_PAST]
        s2 = _dot_nt(q_h, kn_ref[:, hs]) * scale + bias_scr[0, h, :, BAND_PAST:BAND_PAST + CHUNK]
        m = jnp.maximum(jnp.max(s1, axis=-1, keepdims=True), jnp.max(s2, axis=-1, keepdims=True))
        p1 = jnp.exp(s1 - m)
        p2 = jnp.exp(s2 - m)
        r = 1.0 / (jnp.sum(p1, axis=-1, keepdims=True) + jnp.sum(p2, axis=-1, keepdims=True))
        o = _dot((p1 * r).astype(BF16), vc_h) + _dot((p2 * r).astype(BF16), vn_ref[:, hs])
        o_ref[:, hs] = o.astype(o_ref.dtype)


def _attn_sample(q, kb, vb, cache_k, cache_v, ext, *, row_blk0, n_sample, n_heads):
    width = n_heads * ATT_HEAD_DIM
    kern = functools.partial(_attn_sample_kernel, n_heads=n_heads)
    cache_spec = pl.BlockSpec((None, BAND_PAST * n_heads, ATT_HEAD_DIM), lambda s: (s, 0, 0))
    row_spec = pl.BlockSpec((CHUNK, width), lambda s: (row_blk0 + s, 0))
    return pl.pallas_call(
        kern,
        grid=(n_sample,),
        in_specs=[row_spec, row_spec, row_spec, cache_spec, cache_spec,
                  pl.BlockSpec((n_heads, ATT_WIN), lambda s: (0, 0))],
        out_specs=pl.BlockSpec((CHUNK, width), lambda s: (s, 0)),
        out_shape=jax.ShapeDtypeStruct((n_sample * CHUNK, width), BF16),
        scratch_shapes=[pltpu.VMEM((2, n_heads, CHUNK, ATT_WIN), F32)],
        compiler_params=_cparams(1),
        name="attn_sample",
    )(q, kb, vb, cache_k, cache_v, ext)


def _mix_kernel(s_ref, ap_ref, as_ref, w1_ref, w2_ref, g1_ref, g2_ref, o_ref, *, n_prompt_tiles):
    i = pl.program_id(1)
    a1 = _dot(s_ref[...], w1_ref[...])
    att = jnp.where(i < n_prompt_tiles, ap_ref[...], as_ref[...])
    a2 = _dot(att, w2_ref[...])
    g1 = jax.nn.sigmoid(g1_ref[...].astype(F32))
    g2 = jax.nn.sigmoid(g2_ref[...].astype(F32))
    o_ref[...] = (g1 * a1 + g2 * a2).astype(o_ref.dtype)


def _mix(ssd_y, att_p, att_s, w1, w2, gates_arr, *, gate_col0, tm, tn):
    m, k1 = ssd_y.shape
    k2 = att_p.shape[1]
    n = w1.shape[1]
    npt = att_p.shape[0] // tm
    g1_blk = gate_col0 // tn
    g2_blk = (gate_col0 + n) // tn
    kern = functools.partial(_mix_kernel, n_prompt_tiles=npt)
    return pl.pallas_call(
        kern,
        grid=(n // tn, m // tm),
        in_specs=[
            pl.BlockSpec((tm, k1), lambda j, i: (i, 0)),
            pl.BlockSpec((tm, k2), lambda j, i: (jnp.minimum(i, npt - 1), 0)),
            pl.BlockSpec((tm, k2), lambda j, i: (jnp.maximum(i - npt, 0), 0)),
            pl.BlockSpec((k1, tn), lambda j, i: (0, j)),
            pl.BlockSpec((k2, tn), lambda j, i: (0, j)),
            pl.BlockSpec((tm, tn), lambda j, i: (i, g1_blk + j)),
            pl.BlockSpec((tm, tn), lambda j, i: (i, g2_blk + j)),
        ],
        out_specs=pl.BlockSpec((tm, tn), lambda j, i: (i, j)),
        out_shape=jax.ShapeDtypeStruct((m, n), BF16),
        compiler_params=_cparams(2),
        name="branch_mix",
    )(ssd_y, att_p, att_s, w1, w2, gates_arr, gates_arr)


def _layer_norm(t, g, b):
    mu = jnp.mean(t, axis=-1, keepdims=True)
    tc = t - mu
    var = jnp.mean(tc * tc, axis=-1, keepdims=True)
    return tc * lax.rsqrt(var + LN_EPS) * g + b


LN_SUB_ROWS = 128


def _proj_ln_kernel(*refs, alpha, n_k, n_prompt_tiles, split_resid, n_out):
    a_ref, w_ref = refs[0], refs[1]
    pos = 2
    r_refs = refs[pos:pos + (2 if split_resid else 1)]
    pos += len(r_refs)
    g_ref, b_ref = refs[pos], refs[pos + 1]
    pos += 2
    o_refs = refs[pos:pos + n_out]
    acc_ref = refs[pos + n_out] if n_k > 1 else None
    i = pl.program_id(0)
    kk = pl.program_id(1)
    tm = a_ref.shape[0]

    def resid_rows(rs):
        if split_resid:
            return jnp.where(i < n_prompt_tiles, r_refs[0][rs, :], r_refs[1][rs, :])
        return r_refs[0][rs, :]

    def finish(first):
        for r in range(tm // LN_SUB_ROWS):
            rs = slice(r * LN_SUB_ROWS, (r + 1) * LN_SUB_ROWS)
            base = alpha * resid_rows(rs) if first else acc_ref[rs, :]
            y = _layer_norm(base + _dot(a_ref[rs, :], w_ref[...]), g_ref[...], b_ref[...])
            for o_ref in o_refs:
                o_ref[rs, :] = y.astype(o_ref.dtype)

    if n_k == 1:
        finish(True)
        return

    @pl.when(kk == 0)
    def _():
        acc_ref[...] = alpha * resid_rows(slice(None)) + _dot(a_ref[...], w_ref[...])

    if n_k > 2:
        @pl.when(jnp.logical_and(kk > 0, kk < n_k - 1))
        def _():
            acc_ref[...] += _dot(a_ref[...], w_ref[...])

    @pl.when(kk == n_k - 1)
    def _():
        finish(False)


def _proj_ln(a, w, resid, gamma, beta, out_dtypes, *, alpha, tm, n_k, row0, n_rows, m_prompt=None, name):
    k = a.shape[1]
    n = w.shape[1]
    tk = k // n_k
    assert tk * n_k == k and n_rows % tm == 0 and row0 % tm == 0 and tm % LN_SUB_ROWS == 0
    blk0 = row0 // tm
    split_resid = isinstance(resid, tuple)
    row_map = lambda i, kk: (blk0 + i, 0)
    w_mode = dict(pipeline_mode=pl.Buffered(1)) if n_k == 1 else {}
    in_specs = [pl.BlockSpec((tm, tk), lambda i, kk: (blk0 + i, kk)),
                pl.BlockSpec((tk, n), lambda i, kk: (kk, 0), **w_mode)]
    args = [a, w]
    npt = 0
    if split_resid:
        assert row0 == 0 and m_prompt % tm == 0
        npt = m_prompt // tm
        in_specs += [pl.BlockSpec((tm, n), lambda i, kk: (jnp.minimum(i, npt - 1), 0)),
                     pl.BlockSpec((tm, n), lambda i, kk: (jnp.maximum(i - npt, 0), 0))]
        args += list(resid)
    else:
        in_specs.append(pl.BlockSpec((tm, n), row_map))
        args.append(resid)
    in_specs += [pl.BlockSpec((1, n), lambda i, kk: (0, 0))] * 2
    args += [gamma, beta]
    kern = functools.partial(_proj_ln_kernel, alpha=alpha, n_k=n_k, n_prompt_tiles=npt,
                             split_resid=split_resid, n_out=len(out_dtypes))
    return pl.pallas_call(
        kern,
        grid=(n_rows // tm, n_k),
        in_specs=in_specs,
        out_specs=[pl.BlockSpec((tm, n), lambda i, kk: (i, 0)) for _ in out_dtypes],
        out_shape=[jax.ShapeDtypeStruct((n_rows, n), dt) for dt in out_dtypes],
        scratch_shapes=[pltpu.VMEM((tm, n), F32)] if n_k > 1 else [],
        compiler_params=_cparams(2),
        name=name,
    )(*args)


def _glu_kernel(h_ref, wg_ref, wu_ref, o_ref, wg_sc, wu_sc):
    @pl.when(pl.program_id(1) == 0)
    def _():
        _cast_weight_tile(wg_ref, wg_sc)
        _cast_weight_tile(wu_ref, wu_sc)

    h = h_ref[...]
    g = _dot(h, wg_sc[...])
    u = _dot(h, wu_sc[...])
    o_ref[...] = (g * jax.nn.sigmoid(g) * u).astype(o_ref.dtype)


def _glu(h, w_gate_up, *, tm, tf):
    m, k = h.shape
    d_ff = w_gate_up.shape[1] // 2
    nf = d_ff // tf
    assert nf * tf == d_ff
    return pl.pallas_call(
        _glu_kernel,
        grid=(nf, m // tm),
        in_specs=[
            pl.BlockSpec((tm, k), lambda j, i: (i, 0)),
            pl.BlockSpec((k, tf), lambda j, i: (0, j)),
            pl.BlockSpec((k, tf), lambda j, i: (0, nf + j)),
        ],
        out_specs=pl.BlockSpec((tm, tf), lambda j, i: (i, j)),
        out_shape=jax.ShapeDtypeStruct((m, d_ff), BF16),
        scratch_shapes=[pltpu.VMEM((k, tf), BF16), pltpu.VMEM((k, tf), BF16)],
        compiler_params=_cparams(2),
        name="ffn_glu",
    )(h, w_gate_up, w_gate_up)


def kernel(x_prompt, x_sample, cache_k, cache_v, state_conv, state_ssm, w_in, conv_w, conv_b, dt_bias,
           a_log, d_skip, ssd_norm_w, rel_bias, w_ssd_out, w_att_out, w_o, ln1_g, ln1_b, w_gate_up,
           w_down, ln2_g, ln2_b):
    depth = w_in.shape[0]
    assert depth == 1
    batch, seq, d_model = x_prompt.shape
    n_sample, dec_seq, _ = x_sample.shape
    assert batch == 1 and dec_seq == CHUNK and seq % ATT_TILE == 0
    n_ssd_heads = dt_bias.shape[1]
    inner = n_ssd_heads * SSD_HEADDIM
    conv_dim = conv_w.shape[2]
    n_groups = SSD_GROUPS
    d_state = SSD_STATE
    assert conv_dim == inner + 2 * n_groups * d_state
    n_heads = rel_bias.shape[1]
    width = n_heads * ATT_HEAD_DIM
    assert cache_k.shape[2] == BAND_PAST
    alpha = (2.0 * depth) ** 0.25
    keep = min(BAND_PAST, seq)

    m_p = batch * seq
    m_s = n_sample * dec_seq
    xp2 = x_prompt.reshape(m_p, d_model)
    xs2 = x_sample.reshape(m_s, d_model)

    wt = w_in[0].T
    tn = 1024
    o_xbc = inner
    o_dt = o_xbc + conv_dim
    o_q = o_dt + n_ssd_heads
    shift = o_q % tn
    assert o_dt % tn == 0 and inner % tn == 0 and conv_dim % tn == 0 and width % tn == 0
    q_blk = o_dt // tn
    xb, dtr = _xcast_dt(xp2, xs2, wt, dt_col=o_dt, n_dt=n_ssd_heads, tm=512)
    (z,) = _matmul_w32(xb, wt, lambda j: j, inner // tn, [BF16], tm=1024, tn=tn, name="in_proj_z")
    (xbc,) = _matmul_w32(xb, wt, lambda j: inner // tn + j, conv_dim // tn, [F32], tm=1024, tn=tn,
                         name="in_proj_xbc")
    wpt = width // tn
    (qg,) = _matmul_w32(xb, wt, lambda j: jnp.where(j < wpt, q_blk + j, q_blk + 2 * wpt + j),
                        wpt + 2 * d_model // tn, [BF16], tm=1024, tn=tn, shift=shift, name="in_proj_qg")
    kb, k_p, k_s = _matmul_w32(xb, wt, lambda j: q_blk + wpt + j, wpt, [BF16], tm=1024, tn=tn, shift=shift,
                               head_out=(keep, m_s), name="in_proj_k")
    vb, v_p, v_s = _matmul_w32(xb, wt, lambda j: q_blk + 2 * wpt + j, wpt, [BF16], tm=1024, tn=tn,
                               shift=shift, head_out=(keep, m_s), name="in_proj_v")

    pad_h = (0, LANES - n_ssd_heads)
    dtb = jnp.pad(dt_bias[0], pad_h).reshape(1, LANES)
    alog = jnp.pad(a_log[0], pad_h).reshape(1, LANES)
    dskx = jnp.repeat(d_skip[0], SSD_HEADDIM).reshape(1, inner)
    emat = (jnp.arange(LANES)[:, None] == (jnp.arange(inner)[None, :] // SSD_HEADDIM)).astype(BF16)
    ssd_y, conv_p, conv_s, h_p, h_s = _ssd(
        z, xbc, dtr, state_conv[0], state_ssm[0].reshape(n_sample, inner, d_state),
        conv_w[0], conv_b[0].reshape(1, conv_dim), dtb, alog, dskx, ssd_norm_w[0].reshape(1, inner), emat,
        n_prompt_chunks=m_p // CHUNK, n_sample=n_sample, inner=inner, n_groups=n_groups, d_state=d_state)

    rb = rel_bias[0]
    band = BAND_PAST + CHUNK
    n_edge = band - REL_CLIP
    n_rev = ATT_WIN - n_edge
    rev = rb[:, ::-1][:, 1:1 + n_rev]
    ext = jnp.concatenate([jnp.broadcast_to(rb[:, 2 * REL_CLIP:], (n_heads, n_edge)), rev], axis=1)
    ext = jnp.pad(ext, ((0, 0), (0, ATT_WIN - ext.shape[1])))
    att_p = _attn_prompt(qg, kb, vb, ext, n_tiles=m_p // ATT_TILE, n_heads=n_heads)
    cache_rows = (n_sample, BAND_PAST * n_heads, ATT_HEAD_DIM)
    att_s = _attn_sample(qg, kb, vb, cache_k[0].reshape(cache_rows), cache_v[0].reshape(cache_rows), ext,
                         row_blk0=m_p // CHUNK, n_sample=n_sample, n_heads=n_heads)

    mixed = _mix(ssd_y, att_p, att_s, w_ssd_out[0].astype(BF16), w_att_out[0].astype(BF16), qg,
                 gate_col0=width, tm=512, tn=1024)
    g1 = ln1_g[0].reshape(1, d_model)
    b1 = ln1_b[0].reshape(1, d_model)
    g2 = ln2_g[0].reshape(1, d_model)
    b2 = ln2_b[0].reshape(1, d_model)
    h_f, h_b = _proj_ln(mixed, w_o[0].astype(BF16), (xp2, xs2), g1, b1, [F32, BF16], alpha=alpha, tm=512,
                        n_k=1, row0=0, n_rows=m_p + m_s, m_prompt=m_p, name="wo_ln")
    act = _glu(h_b, w_gate_up[0], tm=1024, tf=512)
    w_dn = w_down[0].astype(BF16)
    (y_p,) = _proj_ln(act, w_dn, h_f, g2, b2, [F32], alpha=alpha, tm=512, n_k=2, row0=0, n_rows=m_p,
                      name="down_ln_prompt")
    (y_s,) = _proj_ln(act, w_dn, h_f, g2, b2, [F32], alpha=alpha, tm=512, n_k=2, row0=m_p, n_rows=m_s,
                      name="down_ln_sample")

    return (y_p.reshape(batch, seq, d_model),
            y_s.reshape(n_sample, dec_seq, d_model),
            conv_p.reshape(1, batch, CONV_W - 1, conv_dim),
            h_p.reshape(1, batch, n_ssd_heads, SSD_HEADDIM, d_state),
            k_p.reshape(1, batch, keep, n_heads, ATT_HEAD_DIM),
            v_p.reshape(1, batch, keep, n_heads, ATT_HEAD_DIM),
            conv_s.reshape(1, n_sample, CONV_W - 1, conv_dim),
            h_s.reshape(1, n_sample, n_ssd_heads, SSD_HEADDIM, d_state),
            k_s.reshape(1, n_sample, dec_seq, n_heads, ATT_HEAD_DIM),
            v_s.reshape(1, n_sample, dec_seq, n_heads, ATT_HEAD_DIM))
```

```python
import functools

import jax
import jax.numpy as jnp
from jax import lax
from jax.experimental import pallas as pl
from jax.experimental.pallas import tpu as pltpu

F32 = jnp.float32
BF16 = jnp.bfloat16

CHUNK = 64
SSD_HEADDIM = 64
SSD_GROUPS = 8
SSD_STATE = 128
CONV_W = 4
ATT_HEAD_DIM = 128
BAND_CHUNKS = 8
BAND_PAST = BAND_CHUNKS * CHUNK
REL_CLIP = 128
LN_EPS = 1e-5
RMS_EPS = 1e-5
MASK_NEG = -1e30

LANES = 128
SUBLANES = 8
VMEM_LIMIT_BYTES = 56 * 1024 * 1024

ATT_TILE = 512
ATT_WIN = 5 * LANES


def _cparams(n_axes):
    return pltpu.CompilerParams(
        dimension_semantics=("arbitrary",) * n_axes,
        vmem_limit_bytes=VMEM_LIMIT_BYTES,
    )


def _dot(a, b):
    return jnp.dot(a, b, preferred_element_type=F32)


def _dot_nt(a, b):
    return lax.dot_general(a, b, (((1,), (1,)), ((), ())), preferred_element_type=F32)


def _dot_tn(a, b):
    return lax.dot_general(a, b, (((0,), (0,)), ((), ())), preferred_element_type=F32)


def _xcast_dt_kernel(xp_ref, xs_ref, wd_ref, xb_ref, dt_ref, *, n_prompt_tiles, n_dt):
    i = pl.program_id(0)
    x = jnp.where(i < n_prompt_tiles, xp_ref[...], xs_ref[...]).astype(BF16)
    xb_ref[...] = x
    acc = _dot_nt(x, wd_ref[...].astype(BF16))
    lane = lax.broadcasted_iota(jnp.int32, acc.shape, 1)
    dt_ref[...] = jnp.where(lane < n_dt, acc, 0.0)


def _xcast_dt(xp, xs, wt, *, dt_col, n_dt, tm):
    m_p, k = xp.shape
    m_s = xs.shape[0]
    npt = m_p // tm
    n_tiles = npt + m_s // tm
    assert dt_col % LANES == 0 and n_dt <= LANES
    kern = functools.partial(_xcast_dt_kernel, n_prompt_tiles=npt, n_dt=n_dt)
    return pl.pallas_call(
        kern,
        grid=(n_tiles,),
        in_specs=[
            pl.BlockSpec((tm, k), lambda i: (jnp.minimum(i, npt - 1), 0)),
            pl.BlockSpec((tm, k), lambda i: (jnp.maximum(i - npt, 0), 0)),
            pl.BlockSpec((LANES, k), lambda i: (dt_col // LANES, 0)),
        ],
        out_specs=[pl.BlockSpec((tm, k), lambda i: (i, 0)),
                   pl.BlockSpec((tm, LANES), lambda i: (i, 0))],
        out_shape=[jax.ShapeDtypeStruct((m_p + m_s, k), BF16),
                   jax.ShapeDtypeStruct((m_p + m_s, LANES), F32)],
        compiler_params=_cparams(1),
        name="xcast_dt",
    )(xp, xs, wt)


CAST_ROWS = 256


def _cast_rows(src_ref, src_row0, dst_ref, dst_row0, n_rows):
    done = 0
    while done < n_rows:
        step = min(CAST_ROWS, n_rows - done)
        dst_ref[dst_row0 + done:dst_row0 + done + step, :] = (
            src_ref[src_row0 + done:src_row0 + done + step, :].astype(BF16))
        done += step


def _cast_weight_tile(w_ref, wsc):
    _cast_rows(w_ref, 0, wsc, 0, wsc.shape[0])


def _mm_w32_kernel(*refs, shift, n_out, head_rows, silu):
    x_ref, wa_ref = refs[0], refs[1]
    pos = 2
    wb_ref = None
    if shift:
        wb_ref = refs[pos]
        pos += 1
    o_refs = refs[pos:pos + n_out]
    pos += n_out
    hp_ref = hs_ref = None
    if head_rows is not None:
        hp_ref, hs_ref = refs[pos], refs[pos + 1]
        pos += 2
    wsc = refs[pos]
    i = pl.program_id(1)

    @pl.when(i == 0)
    def _():
        tn = wsc.shape[0]
        _cast_rows(wa_ref, shift, wsc, 0, tn - shift)
        if shift:
            _cast_rows(wb_ref, 0, wsc, tn - shift, shift)

    acc = _dot_nt(x_ref[...], wsc[...])
    res = acc * jax.nn.sigmoid(acc) if silu else acc
    for o_ref in o_refs:
        o_ref[...] = res.astype(o_ref.dtype)

    if head_rows is not None:
        tile_p, row0_p, tile_s = head_rows

        @pl.when(i == tile_p)
        def _():
            hp_ref[...] = acc[row0_p:, :]

        @pl.when(i == tile_s)
        def _():
            hs_ref[...] = acc


def _matmul_w32(x, wt, col_blk, n_col_tiles, out_dtypes, *, tm, tn, shift=0, head_out=None, silu=False,
                name):
    m, k = x.shape
    assert m % tm == 0 and tn % LANES == 0 and shift % 16 == 0 and shift < tn
    in_specs = [pl.BlockSpec((tm, k), lambda j, i: (i, 0)),
                pl.BlockSpec((tn, k), lambda j, i: (col_blk(j), 0))]
    args = [x, wt]
    if shift:
        assert tn % shift == 0
        per = tn // shift
        in_specs.append(pl.BlockSpec((shift, k), lambda j, i: ((col_blk(j) + 1) * per, 0)))
        args.append(wt)
    n = n_col_tiles * tn
    out_specs = [pl.BlockSpec((tm, tn), lambda j, i: (i, j)) for _ in out_dtypes]
    out_shape = [jax.ShapeDtypeStruct((m, n), dt) for dt in out_dtypes]
    head_rows = None
    if head_out is not None:
        n_keep, m_s = head_out
        assert m_s == tm and n_keep <= tm
        tile_s = m // tm - 1
        tile_p = tile_s - 1
        head_rows = (tile_p, tm - n_keep, tile_s)
        out_specs += [pl.BlockSpec((n_keep, tn), lambda j, i: (0, j)),
                      pl.BlockSpec((m_s, tn), lambda j, i: (0, j))]
        out_shape += [jax.ShapeDtypeStruct((n_keep, n), F32), jax.ShapeDtypeStruct((m_s, n), F32)]
    kern = functools.partial(_mm_w32_kernel, shift=shift, n_out=len(out_dtypes), head_rows=head_rows,
                             silu=silu)
    return pl.pallas_call(
        kern,
        grid=(n_col_tiles, m // tm),
        in_specs=in_specs,
        out_specs=out_specs,
        out_shape=out_shape,
        scratch_shapes=[pltpu.VMEM((tn, k), BF16)],
        compiler_params=_cparams(2),
        name=name,
    )(*args)


def _softplus(x):
    return jnp.maximum(x, 0.0) + jnp.log1p(jnp.exp(-jnp.abs(x)))


def _ssd_kernel(zs_ref, xbc_ref, dt_ref, hist_ref, h0_ref, cw_ref, cb_ref, dtb_ref, alog_ref,
                dsk_ref, nw_ref, e_ref,
                y_ref, convp_ref, convs_ref, hop_ref, hos_ref,
                xpad, act, ht, cst, ex, ysc, *, n_prompt_chunks, inner, n_groups, d_state):
    c = pl.program_id(0)
    is_sample = c >= n_prompt_chunks
    gw = inner // n_groups
    conv_dim = inner + 2 * n_groups * d_state

    @pl.when(c == 0)
    def _():
        xpad[0:8, :] = jnp.zeros((8, conv_dim), F32)
        ht[...] = jnp.zeros_like(ht)

    @pl.when(is_sample)
    def _():
        xpad[5:8, :] = hist_ref[0]
        ht[...] = h0_ref[0].T

    xpad[8:8 + CHUNK, :] = xbc_ref[...]
    tail = xbc_ref[CHUNK - (CONV_W - 1):CHUNK, :]

    @pl.when(jnp.logical_not(is_sample))
    def _():
        convp_ref[0] = tail

    @pl.when(is_sample)
    def _():
        convs_ref[0] = tail

    for j in range(conv_dim // LANES):
        sl = slice(j * LANES, (j + 1) * LANES)
        blk = xpad[:, sl]
        a = cb_ref[:, sl]
        for tap in range(CONV_W):
            back = CONV_W - 1 - tap
            src = blk if back == 0 else pltpu.roll(blk, back, 0)
            a = a + src[SUBLANES:, :] * cw_ref[tap:tap + 1, sl]
        act[:, sl] = a * jax.nn.sigmoid(a)
    xpad[5:8, :] = xpad[5 + CHUNK:8 + CHUNK, :]

    dt = _softplus(dt_ref[...] + dtb_ref[...])
    a_neg = -jnp.exp(alog_ref[...])
    da = dt * a_neg
    ri = lax.broadcasted_iota(jnp.int32, (CHUNK, CHUNK), 0)
    ci = lax.broadcasted_iota(jnp.int32, (CHUNK, CHUNK), 1)
    tri = (ri >= ci).astype(F32)
    cs = jnp.dot(tri, da, precision=lax.Precision.HIGHEST, preferred_element_type=F32)
    ecs = jnp.exp(cs)
    cs_last = cs[CHUNK - 1:CHUNK, :]
    dout = jnp.exp(cs_last - cs) * dt
    cd = jnp.exp(cs_last)
    cd_hi = cd.astype(BF16)
    cd_lo = (cd - cd_hi.astype(F32)).astype(BF16)
    lhs = jnp.concatenate(
        [dt.astype(BF16), ecs.astype(BF16), dout.astype(BF16),
         jnp.broadcast_to(cd_hi, (16, LANES)), jnp.broadcast_to(cd_lo, (16, LANES))], axis=0)
    ex[...] = _dot(lhs, e_ref[...])

    cst[...] = jnp.concatenate([cs, jnp.zeros((LANES - CHUNK, LANES), F32)], axis=0).T
    lane = lax.broadcasted_iota(jnp.int32, (CHUNK, LANES), 1)
    lo_half = lane < CHUNK
    n_pairs = inner // LANES
    ev = cst[pl.ds(0, n_pairs, stride=2), :]
    od = cst[pl.ds(1, n_pairs, stride=2), :]
    lane_p = lax.broadcasted_iota(jnp.int32, (n_pairs, LANES), 1)
    cst2 = jnp.where(lane_p < CHUNK, ev, pltpu.roll(od, CHUNK, 1))
    row = lax.broadcasted_iota(jnp.int32, (CHUNK, LANES), 0)
    causal2 = row >= jnp.where(lo_half, lane, lane - CHUNK)

    pairs_per_group = gw // LANES
    for g in range(n_groups):
        gs = slice(g * gw, (g + 1) * gw)
        bm_g = act[:, inner + g * d_state: inner + (g + 1) * d_state].astype(BF16)
        cm_g = act[:, inner + (n_groups + g) * d_state: inner + (n_groups + g + 1) * d_state].astype(BF16)
        cb2 = _dot_nt(cm_g, jnp.concatenate([bm_g, bm_g], axis=0))
        h_g = ht[:, gs]
        yoff = _dot(cm_g, h_g.astype(BF16))
        for p in range(pairs_per_group):
            k = g * pairs_per_group + p
            ls = slice(k * LANES, (k + 1) * LANES)
            csc = jnp.where(lo_half,
                            jnp.broadcast_to(cs[:, 2 * k:2 * k + 1], (CHUNK, LANES)),
                            jnp.broadcast_to(cs[:, 2 * k + 1:2 * k + 2], (CHUNK, LANES)))
            diff = csc - cst2[k:k + 1, :]
            dec = jnp.exp(jnp.where(causal2, diff, MASK_NEG))
            m2 = (cb2 * dec).astype(BF16)
            xs_p = act[:, ls]
            xdt = xs_p * ex[0:CHUNK, ls]
            x2 = jnp.concatenate([jnp.where(lo_half, xdt, 0.0), jnp.where(lo_half, 0.0, xdt)],
                                 axis=0).astype(BF16)
            yd = _dot(m2, x2)
            ysc[:, ls] = (yd + yoff[:, p * LANES:(p + 1) * LANES] * ex[CHUNK:2 * CHUNK, ls]
                          + dsk_ref[:, ls] * xs_p)
        xw = (act[:, gs] * ex[2 * CHUNK:3 * CHUNK, gs]).astype(BF16)
        st = _dot_tn(bm_g, xw)
        cdx = ex[3 * CHUNK:3 * CHUNK + 1, gs] + ex[3 * CHUNK + 16:3 * CHUNK + 17, gs]
        ht[:, gs] = h_g * cdx + st

    for g in range(n_groups):
        gs = slice(g * gw, (g + 1) * gw)
        v = ysc[:, gs] * zs_ref[:, gs].astype(F32)
        ms = jnp.mean(v * v, axis=-1, keepdims=True)
        y_ref[:, gs] = (v * lax.rsqrt(ms + RMS_EPS) * nw_ref[:, gs]).astype(y_ref.dtype)

    @pl.when(c == n_prompt_chunks - 1)
    def _():
        hop_ref[0] = ht[...].T

    @pl.when(is_sample)
    def _():
        hos_ref[0] = ht[...].T


def _ssd(zs, xbc, dtr, hist, h0, conv_w, conv_b, dtb, alog, dskx, normw, emat, *,
         n_prompt_chunks, n_sample, inner, n_groups, d_state):
    n_chunks = n_prompt_chunks + n_sample
    conv_dim = xbc.shape[1]
    hp = inner
    npc = n_prompt_chunks

    def stream(c):
        return jnp.maximum(c - npc, 0)

    kern = functools.partial(_ssd_kernel, n_prompt_chunks=npc, inner=inner, n_groups=n_groups,
                             d_state=d_state)
    const = lambda c: (0, 0)
    return pl.pallas_call(
        kern,
        grid=(n_chunks,),
        in_specs=[
            pl.BlockSpec((CHUNK, inner), lambda c: (c, 0)),
            pl.BlockSpec((CHUNK, conv_dim), lambda c: (c, 0)),
            pl.BlockSpec((CHUNK, LANES), lambda c: (c, 0)),
            pl.BlockSpec((1, CONV_W - 1, conv_dim), lambda c: (stream(c), 0, 0)),
            pl.BlockSpec((1, hp, d_state), lambda c: (stream(c), 0, 0)),
            pl.BlockSpec((CONV_W, conv_dim), const),
            pl.BlockSpec((1, conv_dim), const),
            pl.BlockSpec((1, LANES), const),
            pl.BlockSpec((1, LANES), const),
            pl.BlockSpec((1, inner), const),
            pl.BlockSpec((1, inner), const),
            pl.BlockSpec((LANES, inner), const),
        ],
        out_specs=[
            pl.BlockSpec((CHUNK, inner), lambda c: (c, 0)),
            pl.BlockSpec((1, CONV_W - 1, conv_dim), lambda c: (0, 0, 0)),
            pl.BlockSpec((1, CONV_W - 1, conv_dim), lambda c: (stream(c), 0, 0)),
            pl.BlockSpec((1, hp, d_state), lambda c: (0, 0, 0)),
            pl.BlockSpec((1, hp, d_state), lambda c: (stream(c), 0, 0)),
        ],
        out_shape=[
            jax.ShapeDtypeStruct((n_chunks * CHUNK, inner), BF16),
            jax.ShapeDtypeStruct((1, CONV_W - 1, conv_dim), F32),
            jax.ShapeDtypeStruct((n_sample, CONV_W - 1, conv_dim), F32),
            jax.ShapeDtypeStruct((1, hp, d_state), F32),
            jax.ShapeDtypeStruct((n_sample, hp, d_state), F32),
        ],
        scratch_shapes=[
            pltpu.VMEM((8 + CHUNK, conv_dim), F32),
            pltpu.VMEM((CHUNK, conv_dim), F32),
            pltpu.VMEM((d_state, inner), F32),
            pltpu.VMEM((LANES, LANES), F32),
            pltpu.VMEM((3 * CHUNK + 32, inner), F32),
            pltpu.VMEM((CHUNK, inner), F32),
        ],
        compiler_params=_cparams(1),
        name="ssd_scan",
    )(zs, xbc, dtr, hist, h0, conv_w, conv_b, dtb, alog, dskx, normw, emat)


def _band_bias_tables(ext_ref, bias_scr, n_heads):
    col = lax.broadcasted_iota(jnp.int32, (CHUNK, ATT_WIN), 1)
    band = BAND_PAST + CHUNK
    for h in range(n_heads):
        e = jnp.broadcast_to(ext_ref[h:h + 1, :], (CHUNK, ATT_WIN))
        even = pltpu.roll(e, ATT_WIN - (CHUNK - 1), 1, stride=1, stride_axis=0)
        odd = pltpu.roll(e, 1, 1, stride=1, stride_axis=0)
        bias_scr[0, h] = jnp.where(col < band, even, MASK_NEG)
        bias_scr[1, h] = jnp.where(col >= CHUNK, odd, MASK_NEG)


def _attn_prompt_kernel(q_ref, kp_ref, kc_ref, vp_ref, vc_ref, ext_ref, o_ref,
                        bias_scr, s_scr, p_scr, *, n_heads):
    i = pl.program_id(0)
    d = ATT_HEAD_DIM
    scale = d ** -0.5
    half = ATT_TILE

    @pl.when(i == 0)
    def _():
        _band_bias_tables(ext_ref, bias_scr, n_heads)
        p_scr[...] = jnp.zeros_like(p_scr)

    past_mask = jnp.where(i == 0, MASK_NEG, 0.0).astype(F32)

    for h in range(n_heads):
        hs = slice(h * d, (h + 1) * d)
        q_h = q_ref[:, hs]
        s_scr[:, 0:half] = _dot_nt(q_h, kp_ref[:, hs]) * scale + past_mask
        s_scr[:, half:2 * half] = _dot_nt(q_h, kc_ref[:, hs]) * scale
        for t in range(ATT_TILE // CHUNK):
            rs = slice(t * CHUNK, (t + 1) * CHUNK)
            c0 = LANES * (t // 2)
            ws = slice(c0, c0 + ATT_WIN)
            s = s_scr[rs, ws] + bias_scr[t % 2, h]
            m = jnp.max(s, axis=-1, keepdims=True)
            p = jnp.exp(s - m)
            r = 1.0 / jnp.sum(p, axis=-1, keepdims=True)
            p_scr[rs, ws] = (p * r).astype(BF16)
        o_ref[:, hs] = (_dot(p_scr[:, 0:half], vp_ref[:, hs])
                        + _dot(p_scr[:, half:2 * half], vc_ref[:, hs])).astype(o_ref.dtype)


def _attn_prompt(q, kb, vb, ext, *, n_tiles, n_heads):
    width = n_heads * ATT_HEAD_DIM
    prev = lambda i: jnp.maximum(i - 1, 0)
    kern = functools.partial(_attn_prompt_kernel, n_heads=n_heads)
    return pl.pallas_call(
        kern,
        grid=(n_tiles,),
        in_specs=[
            pl.BlockSpec((ATT_TILE, width), lambda i: (i, 0)),
            pl.BlockSpec((ATT_TILE, width), lambda i: (prev(i), 0)),
            pl.BlockSpec((ATT_TILE, width), lambda i: (i, 0)),
            pl.BlockSpec((ATT_TILE, width), lambda i: (prev(i), 0)),
            pl.BlockSpec((ATT_TILE, width), lambda i: (i, 0)),
            pl.BlockSpec((n_heads, ATT_WIN), lambda i: (0, 0)),
        ],
        out_specs=pl.BlockSpec((ATT_TILE, width), lambda i: (i, 0)),
        out_shape=jax.ShapeDtypeStruct((n_tiles * ATT_TILE, width), BF16),
        scratch_shapes=[
            pltpu.VMEM((2, n_heads, CHUNK, ATT_WIN), F32),
            pltpu.VMEM((ATT_TILE, 2 * ATT_TILE), F32),
            pltpu.VMEM((ATT_TILE, 2 * ATT_TILE), BF16),
        ],
        compiler_params=_cparams(1),
        name="attn_prompt",
    )(q, kb, kb, vb, vb, ext)


def _attn_sample_kernel(q_ref, kn_ref, vn_ref, ck_ref, cv_ref, ext_ref, o_ref, bias_scr, *, n_heads):
    d = ATT_HEAD_DIM
    scale = d ** -0.5

    @pl.when(pl.program_id(0) == 0)
    def _():
        _band_bias_tables(ext_ref, bias_scr, n_heads)

    for h in range(n_heads):
        hs = slice(h * d, (h + 1) * d)
        rows = pl.ds(h, BAND_PAST, stride=n_heads)
        q_h = q_ref[:, hs]
        kc_h = ck_ref[rows, :].astype(BF16)
        vc_h = cv_ref[rows, :].astype(BF16)
        s1 = _dot_nt(q_h, kc_h) * scale + bias_scr[0, h, :, 0:BAND_PAST]
        s2 = _dot_nt(q_h, kn_ref[:, hs]) * scale + bias_scr[0, h, :, BAND_PAST:BAND_PAST + CHUNK]
        m = jnp.maximum(jnp.max(s1, axis=-1, keepdims=True), jnp.max(s2, axis=-1, keepdims=True))
        p1 = jnp.exp(s1 - m)
        p2 = jnp.exp(s2 - m)
        r = 1.0 / (jnp.sum(p1, axis=-1, keepdims=True) + jnp.sum(p2, axis=-1, keepdims=True))
        o = _dot((p1 * r).astype(BF16), vc_h) + _dot((p2 * r).astype(BF16), vn_ref[:, hs])
        o_ref[:, hs] = o.astype(o_ref.dtype)


def _attn_sample(q, kb, vb, cache_k, cache_v, ext, *, row_blk0, n_sample, n_heads):
    width = n_heads * ATT_HEAD_DIM
    kern = functools.partial(_attn_sample_kernel, n_heads=n_heads)
    cache_spec = pl.BlockSpec((None, BAND_PAST * n_heads, ATT_HEAD_DIM), lambda s: (s, 0, 0))
    row_spec = pl.BlockSpec((CHUNK, width), lambda s: (row_blk0 + s, 0))
    return pl.pallas_call(
        kern,
        grid=(n_sample,),
        in_specs=[row_spec, row_spec, row_spec, cache_spec, cache_spec,
                  pl.BlockSpec((n_heads, ATT_WIN), lambda s: (0, 0))],
        out_specs=pl.BlockSpec((CHUNK, width), lambda s: (s, 0)),
        out_shape=jax.ShapeDtypeStruct((n_sample * CHUNK, width), BF16),
        scratch_shapes=[pltpu.VMEM((2, n_heads, CHUNK, ATT_WIN), F32)],
        compiler_params=_cparams(1),
        name="attn_sample",
    )(q, kb, vb, cache_k, cache_v, ext)


def _mix_kernel(s_ref, ap_ref, as_ref, w1_ref, w2_ref, g1_ref, g2_ref, o_ref, *, n_prompt_tiles):
    i = pl.program_id(1)
    a1 = _dot(s_ref[...], w1_ref[...])
    att = jnp.where(i < n_prompt_tiles, ap_ref[...], as_ref[...])
    a2 = _dot(att, w2_ref[...])
    g1 = jax.nn.sigmoid(g1_ref[...].astype(F32))
    g2 = jax.nn.sigmoid(g2_ref[...].astype(F32))
    o_ref[...] = (g1 * a1 + g2 * a2).astype(o_ref.dtype)


def _mix(ssd_y, att_p, att_s, w1, w2, gates_arr, *, gate_col0, tm, tn):
    m, k1 = ssd_y.shape
    k2 = att_p.shape[1]
    n = w1.shape[1]
    npt = att_p.shape[0] // tm
    g1_blk = gate_col0 // tn
    g2_blk = (gate_col0 + n) // tn
    kern = functools.partial(_mix_kernel, n_prompt_tiles=npt)
    return pl.pallas_call(
        kern,
        grid=(n // tn, m // tm),
        in_specs=[
            pl.BlockSpec((tm, k1), lambda j, i: (i, 0)),
            pl.BlockSpec((tm, k2), lambda j, i: (jnp.minimum(i, npt - 1), 0)),
            pl.BlockSpec((tm, k2), lambda j, i: (jnp.maximum(i - npt, 0), 0)),
            pl.BlockSpec((k1, tn), lambda j, i: (0, j)),
            pl.BlockSpec((k2, tn), lambda j, i: (0, j)),
            pl.BlockSpec((tm, tn), lambda j, i: (i, g1_blk + j)),
            pl.BlockSpec((tm, tn), lambda j, i: (i, g2_blk + j)),
        ],
        out_specs=pl.BlockSpec((tm, tn), lambda j, i: (i, j)),
        out_shape=jax.ShapeDtypeStruct((m, n), BF16),
        compiler_params=_cparams(2),
        name="branch_mix",
    )(ssd_y, att_p, att_s, w1, w2, gates_arr, gates_arr)


def _layer_norm(t, g, b):
    mu = jnp.mean(t, axis=-1, keepdims=True)
    tc = t - mu
    var = jnp.mean(tc * tc, axis=-1, keepdims=True)
    return tc * lax.rsqrt(var + LN_EPS) * g + b


LN_SUB_ROWS = 128


def _proj_ln_kernel(*refs, alpha, n_k, n_prompt_tiles, split_resid, n_out):
    a_ref, w_ref = refs[0], refs[1]
    pos = 2
    r_refs = refs[pos:pos + (2 if split_resid else 1)]
    pos += len(r_refs)
    g_ref, b_ref = refs[pos], refs[pos + 1]
    pos += 2
    o_refs = refs[pos:pos + n_out]
    acc_ref = refs[pos + n_out] if n_k > 1 else None
    i = pl.program_id(0)
    kk = pl.program_id(1)
    tm = a_ref.shape[0]

    def resid_rows(rs):
        if split_resid:
            return jnp.where(i < n_prompt_tiles, r_refs[0][rs, :], r_refs[1][rs, :])
        return r_refs[0][rs, :]

    def finish(first):
        for r in range(tm // LN_SUB_ROWS):
            rs = slice(r * LN_SUB_ROWS, (r + 1) * LN_SUB_ROWS)
            base = alpha * resid_rows(rs) if first else acc_ref[rs, :]
            y = _layer_norm(base + _dot(a_ref[rs, :], w_ref[...]), g_ref[...], b_ref[...])
            for o_ref in o_refs:
                o_ref[rs, :] = y.astype(o_ref.dtype)

    if n_k == 1:
        finish(True)
        return

    @pl.when(kk == 0)
    def _():
        acc_ref[...] = alpha * resid_rows(slice(None)) + _dot(a_ref[...], w_ref[...])

    if n_k > 2:
        @pl.when(jnp.logical_and(kk > 0, kk < n_k - 1))
        def _():
            acc_ref[...] += _dot(a_ref[...], w_ref[...])

    @pl.when(kk == n_k - 1)
    def _():
        finish(False)


def _proj_ln(a, w, resid, gamma, beta, out_dtypes, *, alpha, tm, n_k, row0, n_rows, m_prompt=None, name):
    k = a.shape[1]
    n = w.shape[1]
    tk = k // n_k
    assert tk * n_k == k and n_rows % tm == 0 and row0 % tm == 0 and tm % LN_SUB_ROWS == 0
    blk0 = row0 // tm
    split_resid = isinstance(resid, tuple)
    row_map = lambda i, kk: (blk0 + i, 0)
    w_mode = dict(pipeline_mode=pl.Buffered(1)) if n_k == 1 else {}
    in_specs = [pl.BlockSpec((tm, tk), lambda i, kk: (blk0 + i, kk)),
                pl.BlockSpec((tk, n), lambda i, kk: (kk, 0), **w_mode)]
    args = [a, w]
    npt = 0
    if split_resid:
        assert row0 == 0 and m_prompt % tm == 0
        npt = m_prompt // tm
        in_specs += [pl.BlockSpec((tm, n), lambda i, kk: (jnp.minimum(i, npt - 1), 0)),
                     pl.BlockSpec((tm, n), lambda i, kk: (jnp.maximum(i - npt, 0), 0))]
        args += list(resid)
    else:
        in_specs.append(pl.BlockSpec((tm, n), row_map))
        args.append(resid)
    in_specs += [pl.BlockSpec((1, n), lambda i, kk: (0, 0))] * 2
    args += [gamma, beta]
    kern = functools.partial(_proj_ln_kernel, alpha=alpha, n_k=n_k, n_prompt_tiles=npt,
                             split_resid=split_resid, n_out=len(out_dtypes))
    return pl.pallas_call(
        kern,
        grid=(n_rows // tm, n_k),
        in_specs=in_specs,
        out_specs=[pl.BlockSpec((tm, n), lambda i, kk: (i, 0)) for _ in out_dtypes],
        out_shape=[jax.ShapeDtypeStruct((n_rows, n), dt) for dt in out_dtypes],
        scratch_shapes=[pltpu.VMEM((tm, n), F32)] if n_k > 1 else [],
        compiler_params=_cparams(2),
        name=name,
    )(*args)


def _glu_kernel(h_ref, wg_ref, wu_ref, o_ref, wg_sc, wu_sc):
    @pl.when(pl.program_id(1) == 0)
    def _():
        _cast_weight_tile(wg_ref, wg_sc)
        _cast_weight_tile(wu_ref, wu_sc)

    h = h_ref[...]
    g = _dot(h, wg_sc[...])
    u = _dot(h, wu_sc[...])
    o_ref[...] = (g * jax.nn.sigmoid(g) * u).astype(o_ref.dtype)


def _glu(h, w_gate_up, *, tm, tf):
    m, k = h.shape
    d_ff = w_gate_up.shape[1] // 2
    nf = d_ff // tf
    assert nf * tf == d_ff
    return pl.pallas_call(
        _glu_kernel,
        grid=(nf, m // tm),
        in_specs=[
            pl.BlockSpec((tm, k), lambda j, i: (i, 0)),
            pl.BlockSpec((k, tf), lambda j, i: (0, j)),
            pl.BlockSpec((k, tf), lambda j, i: (0, nf + j)),
        ],
        out_specs=pl.BlockSpec((tm, tf), lambda j, i: (i, j)),
        out_shape=jax.ShapeDtypeStruct((m, d_ff), BF16),
        scratch_shapes=[pltpu.VMEM((k, tf), BF16), pltpu.VMEM((k, tf), BF16)],
        compiler_params=_cparams(2),
        name="ffn_glu",
    )(h, w_gate_up, w_gate_up)


def kernel(x_prompt, x_sample, cache_k, cache_v, state_conv, state_ssm, w_in, conv_w, conv_b, dt_bias,
           a_log, d_skip, ssd_norm_w, rel_bias, w_ssd_out, w_att_out, w_o, ln1_g, ln1_b, w_gate_up,
           w_down, ln2_g, ln2_b):
    depth = w_in.shape[0]
    assert depth == 1
    batch, seq, d_model = x_prompt.shape
    n_sample, dec_seq, _ = x_sample.shape
    assert batch == 1 and dec_seq == CHUNK and seq % ATT_TILE == 0
    n_ssd_heads = dt_bias.shape[1]
    inner = n_ssd_heads * SSD_HEADDIM
    conv_dim = conv_w.shape[2]
    n_groups = SSD_GROUPS
    d_state = SSD_STATE
    assert conv_dim == inner + 2 * n_groups * d_state
    n_heads = rel_bias.shape[1]
    width = n_heads * ATT_HEAD_DIM
    assert cache_k.shape[2] == BAND_PAST
    alpha = (2.0 * depth) ** 0.25
    keep = min(BAND_PAST, seq)

    m_p = batch * seq
    m_s = n_sample * dec_seq
    xp2 = x_prompt.reshape(m_p, d_model)
    xs2 = x_sample.reshape(m_s, d_model)

    wt = w_in[0].T
    tn = 1024
    o_xbc = inner
    o_dt = o_xbc + conv_dim
    o_q = o_dt + n_ssd_heads
    shift = o_q % tn
    assert o_dt % tn == 0 and inner % tn == 0 and conv_dim % tn == 0 and width % tn == 0
    q_blk = o_dt // tn
    xb, dtr = _xcast_dt(xp2, xs2, wt, dt_col=o_dt, n_dt=n_ssd_heads, tm=512)
    (zs,) = _matmul_w32(xb, wt, lambda j: j, inner // tn, [BF16], tm=1536, tn=tn, silu=True,
                        name="in_proj_z")
    (xbc,) = _matmul_w32(xb, wt, lambda j: inner // tn + j, conv_dim // tn, [F32], tm=1536, tn=tn,
                         name="in_proj_xbc")
    wpt = width // tn
    (qg,) = _matmul_w32(xb, wt, lambda j: jnp.where(j < wpt, q_blk + j, q_blk + 2 * wpt + j),
                        wpt + 2 * d_model // tn, [BF16], tm=1536, tn=tn, shift=shift, name="in_proj_qg")
    kb, k_p, k_s = _matmul_w32(xb, wt, lambda j: q_blk + wpt + j, wpt, [BF16], tm=1024, tn=tn, shift=shift,
                               head_out=(keep, m_s), name="in_proj_k")
    vb, v_p, v_s = _matmul_w32(xb, wt, lambda j: q_blk + 2 * wpt + j, wpt, [BF16], tm=1024, tn=tn,
                               shift=shift, head_out=(keep, m_s), name="in_proj_v")

    pad_h = (0, LANES - n_ssd_heads)
    dtb = jnp.pad(dt_bias[0], pad_h).reshape(1, LANES)
    alog = jnp.pad(a_log[0], pad_h).reshape(1, LANES)
    dskx = jnp.repeat(d_skip[0], SSD_HEADDIM).reshape(1, inner)
    emat = (jnp.arange(LANES)[:, None] == (jnp.arange(inner)[None, :] // SSD_HEADDIM)).astype(BF16)
    ssd_y, conv_p, conv_s, h_p, h_s = _ssd(
        zs, xbc, dtr, state_conv[0], state_ssm[0].reshape(n_sample, inner, d_state),
        conv_w[0], conv_b[0].reshape(1, conv_dim), dtb, alog, dskx, ssd_norm_w[0].reshape(1, inner), emat,
        n_prompt_chunks=m_p // CHUNK, n_sample=n_sample, inner=inner, n_groups=n_groups, d_state=d_state)

    rb = rel_bias[0]
    band = BAND_PAST + CHUNK
    n_edge = band - REL_CLIP
    n_rev = ATT_WIN - n_edge
    rev = rb[:, ::-1][:, 1:1 + n_rev]
    ext = jnp.concatenate([jnp.broadcast_to(rb[:, 2 * REL_CLIP:], (n_heads, n_edge)), rev], axis=1)
    ext = jnp.pad(ext, ((0, 0), (0, ATT_WIN - ext.shape[1])))
    att_p = _attn_prompt(qg, kb, vb, ext, n_tiles=m_p // ATT_TILE, n_heads=n_heads)
    cache_rows = (n_sample, BAND_PAST * n_heads, ATT_HEAD_DIM)
    att_s = _attn_sample(qg, kb, vb, cache_k[0].reshape(cache_rows), cache_v[0].reshape(cache_rows), ext,
                         row_blk0=m_p // CHUNK, n_sample=n_sample, n_heads=n_heads)

    mixed = _mix(ssd_y, att_p, att_s, w_ssd_out[0].astype(BF16), w_att_out[0].astype(BF16), qg,
                 gate_col0=width, tm=512, tn=1024)
    g1 = ln1_g[0].reshape(1, d_model)
    b1 = ln1_b[0].reshape(1, d_model)
    g2 = ln2_g[0].reshape(1, d_model)
    b2 = ln2_b[0].reshape(1, d_model)
    h_f, h_b = _proj_ln(mixed, w_o[0].astype(BF16), (xp2, xs2), g1, b1, [F32, BF16], alpha=alpha, tm=512,
                        n_k=1, row0=0, n_rows=m_p + m_s, m_prompt=m_p, name="wo_ln")
    act = _glu(h_b, w_gate_up[0], tm=1536, tf=512)
    w_dn = w_down[0].astype(BF16)
    (y_p,) = _proj_ln(act, w_dn, h_f, g2, b2, [F32], alpha=alpha, tm=512, n_k=2, row0=0, n_rows=m_p,
                      name="down_ln_prompt")
    (y_s,) = _proj_ln(act, w_dn, h_f, g2, b2, [F32], alpha=alpha, tm=512, n_k=2, row0=m_p, n_rows=m_s,
                      name="down_ln_sample")

    return (y_p.reshape(batch, seq, d_model),
            y_s.reshape(n_sample, dec_seq, d_model),
            conv_p.reshape(1, batch, CONV_W - 1, conv_dim),
            h_p.reshape(1, batch, n_ssd_heads, SSD_HEADDIM, d_state),
            k_p.reshape(1, batch, keep, n_heads, ATT_HEAD_DIM),
            v_p.reshape(1, batch, keep, n_heads, ATT_HEAD_DIM),
            conv_s.reshape(1, n_sample, CONV_W - 1, conv_dim),
            h_s.reshape(1, n_sample, n_ssd_heads, SSD_HEADDIM, d_state),
            k_s.reshape(1, n_sample, dec_seq, n_heads, ATT_HEAD_DIM),
            v_s.reshape(1, n_sample, dec_seq, n_heads, ATT_HEAD_DIM))
```

```python
import functools

import jax
import jax.numpy as jnp
from jax import lax
from jax.experimental import pallas as pl
from jax.experimental.pallas import tpu as pltpu

F32 = jnp.float32
BF16 = jnp.bfloat16

CHUNK = 64
SSD_HEADDIM = 64
SSD_GROUPS = 8
SSD_STATE = 128
CONV_W = 4
ATT_HEAD_DIM = 128
BAND_CHUNKS = 8
BAND_PAST = BAND_CHUNKS * CHUNK
REL_CLIP = 128
LN_EPS = 1e-5
RMS_EPS = 1e-5
MASK_NEG = -1e30

LANES = 128
SUBLANES = 8
VMEM_LIMIT_BYTES = 56 * 1024 * 1024

SSD_PROMPT_CPS = 2
ATT_TILE = 512
ATT_WIN = 5 * LANES


def _cparams(n_axes):
    return pltpu.CompilerParams(
        dimension_semantics=("arbitrary",) * n_axes,
        vmem_limit_bytes=VMEM_LIMIT_BYTES,
    )


def _dot(a, b):
    return jnp.dot(a, b, preferred_element_type=F32)


def _dot_nt(a, b):
    return lax.dot_general(a, b, (((1,), (1,)), ((), ())), preferred_element_type=F32)


def _dot_tn(a, b):
    return lax.dot_general(a, b, (((0,), (0,)), ((), ())), preferred_element_type=F32)


def _xcast_dt_kernel(xp_ref, xs_ref, wd_ref, xb_ref, dt_ref, *, n_prompt_tiles, n_dt):
    i = pl.program_id(0)
    x = jnp.where(i < n_prompt_tiles, xp_ref[...], xs_ref[...]).astype(BF16)
    xb_ref[...] = x
    acc = _dot_nt(x, wd_ref[...].astype(BF16))
    lane = lax.broadcasted_iota(jnp.int32, acc.shape, 1)
    dt_ref[...] = jnp.where(lane < n_dt, acc, 0.0)


def _xcast_dt(xp, xs, wt, *, dt_col, n_dt, tm):
    m_p, k = xp.shape
    m_s = xs.shape[0]
    npt = m_p // tm
    n_tiles = npt + m_s // tm
    assert dt_col % LANES == 0 and n_dt <= LANES
    kern = functools.partial(_xcast_dt_kernel, n_prompt_tiles=npt, n_dt=n_dt)
    return pl.pallas_call(
        kern,
        grid=(n_tiles,),
        in_specs=[
            pl.BlockSpec((tm, k), lambda i: (jnp.minimum(i, npt - 1), 0)),
            pl.BlockSpec((tm, k), lambda i: (jnp.maximum(i - npt, 0), 0)),
            pl.BlockSpec((LANES, k), lambda i: (dt_col // LANES, 0)),
        ],
        out_specs=[pl.BlockSpec((tm, k), lambda i: (i, 0)),
                   pl.BlockSpec((tm, LANES), lambda i: (i, 0))],
        out_shape=[jax.ShapeDtypeStruct((m_p + m_s, k), BF16),
                   jax.ShapeDtypeStruct((m_p + m_s, LANES), F32)],
        compiler_params=_cparams(1),
        name="xcast_dt",
    )(xp, xs, wt)


CAST_ROWS = 256


def _cast_rows(src_ref, src_row0, dst_ref, dst_row0, n_rows):
    done = 0
    while done < n_rows:
        step = min(CAST_ROWS, n_rows - done)
        dst_ref[dst_row0 + done:dst_row0 + done + step, :] = (
            src_ref[src_row0 + done:src_row0 + done + step, :].astype(BF16))
        done += step


def _cast_weight_tile(w_ref, wsc):
    _cast_rows(w_ref, 0, wsc, 0, wsc.shape[0])


def _mm_w32_kernel(*refs, shift, n_out, head_rows, silu):
    x_ref, wa_ref = refs[0], refs[1]
    pos = 2
    wb_ref = None
    if shift:
        wb_ref = refs[pos]
        pos += 1
    o_refs = refs[pos:pos + n_out]
    pos += n_out
    hp_ref = hs_ref = None
    if head_rows is not None:
        hp_ref, hs_ref = refs[pos], refs[pos + 1]
        pos += 2
    wsc = refs[pos]
    i = pl.program_id(1)

    @pl.when(i == 0)
    def _():
        tn = wsc.shape[0]
        _cast_rows(wa_ref, shift, wsc, 0, tn - shift)
        if shift:
            _cast_rows(wb_ref, 0, wsc, tn - shift, shift)

    acc = _dot_nt(x_ref[...], wsc[...])
    res = acc * jax.nn.sigmoid(acc) if silu else acc
    for o_ref in o_refs:
        o_ref[...] = res.astype(o_ref.dtype)

    if head_rows is not None:
        tile_p, row0_p, tile_s = head_rows

        @pl.when(i == tile_p)
        def _():
            hp_ref[...] = acc[row0_p:, :]

        @pl.when(i == tile_s)
        def _():
            hs_ref[...] = acc


def _matmul_w32(x, wt, col_blk, n_col_tiles, out_dtypes, *, tm, tn, shift=0, head_out=None, silu=False,
                name):
    m, k = x.shape
    assert m % tm == 0 and tn % LANES == 0 and shift % 16 == 0 and shift < tn
    in_specs = [pl.BlockSpec((tm, k), lambda j, i: (i, 0)),
                pl.BlockSpec((tn, k), lambda j, i: (col_blk(j), 0))]
    args = [x, wt]
    if shift:
        assert tn % shift == 0
        per = tn // shift
        in_specs.append(pl.BlockSpec((shift, k), lambda j, i: ((col_blk(j) + 1) * per, 0)))
        args.append(wt)
    n = n_col_tiles * tn
    out_specs = [pl.BlockSpec((tm, tn), lambda j, i: (i, j)) for _ in out_dtypes]
    out_shape = [jax.ShapeDtypeStruct((m, n), dt) for dt in out_dtypes]
    head_rows = None
    if head_out is not None:
        n_keep, m_s = head_out
        assert m_s == tm and n_keep <= tm
        tile_s = m // tm - 1
        tile_p = tile_s - 1
        head_rows = (tile_p, tm - n_keep, tile_s)
        out_specs += [pl.BlockSpec((n_keep, tn), lambda j, i: (0, j)),
                      pl.BlockSpec((m_s, tn), lambda j, i: (0, j))]
        out_shape += [jax.ShapeDtypeStruct((n_keep, n), F32), jax.ShapeDtypeStruct((m_s, n), F32)]
    kern = functools.partial(_mm_w32_kernel, shift=shift, n_out=len(out_dtypes), head_rows=head_rows,
                             silu=silu)
    return pl.pallas_call(
        kern,
        grid=(n_col_tiles, m // tm),
        in_specs=in_specs,
        out_specs=out_specs,
        out_shape=out_shape,
        scratch_shapes=[pltpu.VMEM((tn, k), BF16)],
        compiler_params=_cparams(2),
        name=name,
    )(*args)


def _softplus(x):
    return jnp.maximum(x, 0.0) + jnp.log1p(jnp.exp(-jnp.abs(x)))


def _ssd_chunk(u, zs_ref, dt_ref, cw_ref, cb_ref, dtb_ref, alog_ref, dsk_ref, nw_ref, e_ref, y_ref,
               xpad, act, ht, cst, ex, ysc, *, inner, n_groups, d_state):
    r0 = u * CHUNK
    rows = slice(r0, r0 + CHUNK)
    gw = inner // n_groups
    conv_dim = inner + 2 * n_groups * d_state

    for j in range(conv_dim // LANES):
        sl = slice(j * LANES, (j + 1) * LANES)
        blk = xpad[r0:r0 + SUBLANES + CHUNK, sl]
        a = cb_ref[:, sl]
        for tap in range(CONV_W):
            back = CONV_W - 1 - tap
            src = blk if back == 0 else pltpu.roll(blk, back, 0)
            a = a + src[SUBLANES:, :] * cw_ref[tap:tap + 1, sl]
        act[rows, sl] = a * jax.nn.sigmoid(a)

    dt = _softplus(dt_ref[rows, :] + dtb_ref[...])
    a_neg = -jnp.exp(alog_ref[...])
    da = dt * a_neg
    ri = lax.broadcasted_iota(jnp.int32, (CHUNK, CHUNK), 0)
    ci = lax.broadcasted_iota(jnp.int32, (CHUNK, CHUNK), 1)
    tri = (ri >= ci).astype(F32)
    cs = jnp.dot(tri, da, precision=lax.Precision.HIGHEST, preferred_element_type=F32)
    ecs = jnp.exp(cs)
    cs_last = cs[CHUNK - 1:CHUNK, :]
    dout = jnp.exp(cs_last - cs) * dt
    cd = jnp.exp(cs_last)
    cd_hi = cd.astype(BF16)
    cd_lo = (cd - cd_hi.astype(F32)).astype(BF16)
    lhs = jnp.concatenate(
        [dt.astype(BF16), ecs.astype(BF16), dout.astype(BF16),
         jnp.broadcast_to(cd_hi, (16, LANES)), jnp.broadcast_to(cd_lo, (16, LANES))], axis=0)
    ex[u] = _dot(lhs, e_ref[...])

    cst[u] = jnp.concatenate([cs, jnp.zeros((LANES - CHUNK, LANES), F32)], axis=0).T
    lane = lax.broadcasted_iota(jnp.int32, (CHUNK, LANES), 1)
    lo_half = lane < CHUNK
    n_pairs = inner // LANES
    ev = cst[u, pl.ds(0, n_pairs, stride=2), :]
    od = cst[u, pl.ds(1, n_pairs, stride=2), :]
    lane_p = lax.broadcasted_iota(jnp.int32, (n_pairs, LANES), 1)
    cst2 = jnp.where(lane_p < CHUNK, ev, pltpu.roll(od, CHUNK, 1))
    row = lax.broadcasted_iota(jnp.int32, (CHUNK, LANES), 0)
    causal2 = row >= jnp.where(lo_half, lane, lane - CHUNK)

    pairs_per_group = gw // LANES
    for g in range(n_groups):
        gs = slice(g * gw, (g + 1) * gw)
        bm_g = act[rows, inner + g * d_state: inner + (g + 1) * d_state].astype(BF16)
        cm_g = act[rows, inner + (n_groups + g) * d_state: inner + (n_groups + g + 1) * d_state].astype(BF16)
        cb2 = _dot_nt(cm_g, jnp.concatenate([bm_g, bm_g], axis=0))
        h_g = ht[:, gs]
        yoff = _dot(cm_g, h_g.astype(BF16))
        for p in range(pairs_per_group):
            k = g * pairs_per_group + p
            ls = slice(k * LANES, (k + 1) * LANES)
            csc = jnp.where(lo_half,
                            jnp.broadcast_to(cs[:, 2 * k:2 * k + 1], (CHUNK, LANES)),
                            jnp.broadcast_to(cs[:, 2 * k + 1:2 * k + 2], (CHUNK, LANES)))
            diff = csc - cst2[k:k + 1, :]
            dec = jnp.exp(jnp.where(causal2, diff, MASK_NEG))
            m2 = (cb2 * dec).astype(BF16)
            xs_p = act[rows, ls]
            xdt = xs_p * ex[u, 0:CHUNK, ls]
            x2 = jnp.concatenate([jnp.where(lo_half, xdt, 0.0), jnp.where(lo_half, 0.0, xdt)],
                                 axis=0).astype(BF16)
            yd = _dot(m2, x2)
            ysc[rows, ls] = (yd + yoff[:, p * LANES:(p + 1) * LANES] * ex[u, CHUNK:2 * CHUNK, ls]
                             + dsk_ref[:, ls] * xs_p)
        xw = (act[rows, gs] * ex[u, 2 * CHUNK:3 * CHUNK, gs]).astype(BF16)
        st = _dot_tn(bm_g, xw)
        cdx = ex[u, 3 * CHUNK:3 * CHUNK + 1, gs] + ex[u, 3 * CHUNK + 16:3 * CHUNK + 17, gs]
        ht[:, gs] = h_g * cdx + st

    for g in range(n_groups):
        gs = slice(g * gw, (g + 1) * gw)
        v = ysc[rows, gs] * zs_ref[rows, gs].astype(F32)
        ms = jnp.mean(v * v, axis=-1, keepdims=True)
        y_ref[rows, gs] = (v * lax.rsqrt(ms + RMS_EPS) * nw_ref[:, gs]).astype(y_ref.dtype)


def _ssd_kernel(zs_ref, xbc_ref, dt_ref, hist_ref, h0_ref, cw_ref, cb_ref, dtb_ref, alog_ref,
                dsk_ref, nw_ref, e_ref,
                y_ref, conv_ref, ho_ref,
                xpad, act, ht, cst, ex, ysc, *, cps, per_stream, inner, n_groups, d_state):
    c = pl.program_id(0)
    n_rows = cps * CHUNK
    conv_dim = inner + 2 * n_groups * d_state

    @pl.when(c == 0)
    def _():
        xpad[0:SUBLANES, :] = jnp.zeros((SUBLANES, conv_dim), F32)
        if not per_stream:
            ht[...] = jnp.zeros_like(ht)

    if per_stream:
        xpad[5:8, :] = hist_ref[0]
        ht[...] = h0_ref[0].T

    xpad[SUBLANES:SUBLANES + n_rows, :] = xbc_ref[...]
    conv_ref[0] = xbc_ref[n_rows - (CONV_W - 1):n_rows, :]

    for u in range(cps):
        _ssd_chunk(u, zs_ref, dt_ref, cw_ref, cb_ref, dtb_ref, alog_ref, dsk_ref, nw_ref, e_ref, y_ref,
                   xpad, act, ht, cst, ex, ysc, inner=inner, n_groups=n_groups, d_state=d_state)
    xpad[5:8, :] = xpad[5 + n_rows:8 + n_rows, :]

    if per_stream:
        ho_ref[0] = ht[...].T
    else:
        @pl.when(c == pl.num_programs(0) - 1)
        def _():
            ho_ref[0] = ht[...].T


def _ssd(zs, xbc, dtr, hist, h0, conv_w, conv_b, dtb, alog, dskx, normw, emat, *,
         row0, n_streams, chunks_per_stream, cps, per_stream, inner, n_groups, d_state, name):
    conv_dim = xbc.shape[1]
    hp = inner
    n_rows = cps * CHUNK
    assert chunks_per_stream % cps == 0 and row0 % n_rows == 0
    assert per_stream == (chunks_per_stream == cps) and (per_stream or n_streams == 1)
    steps_per_stream = chunks_per_stream // cps
    blk0 = row0 // n_rows
    stream = (lambda c: c) if per_stream else (lambda c: 0)

    kern = functools.partial(_ssd_kernel, cps=cps, per_stream=per_stream, inner=inner, n_groups=n_groups,
                             d_state=d_state)
    const = lambda c: (0, 0)
    row_map = lambda c: (blk0 + c, 0)
    return pl.pallas_call(
        kern,
        grid=(n_streams * steps_per_stream,),
        in_specs=[
            pl.BlockSpec((n_rows, inner), row_map),
            pl.BlockSpec((n_rows, conv_dim), row_map),
            pl.BlockSpec((n_rows, LANES), row_map),
            pl.BlockSpec((1, CONV_W - 1, conv_dim), lambda c: (stream(c), 0, 0)),
            pl.BlockSpec((1, hp, d_state), lambda c: (stream(c), 0, 0)),
            pl.BlockSpec((CONV_W, conv_dim), const),
            pl.BlockSpec((1, conv_dim), const),
            pl.BlockSpec((1, LANES), const),
            pl.BlockSpec((1, LANES), const),
            pl.BlockSpec((1, inner), const),
            pl.BlockSpec((1, inner), const),
            pl.BlockSpec((LANES, inner), const),
        ],
        out_specs=[
            pl.BlockSpec((n_rows, inner), lambda c: (c, 0)),
            pl.BlockSpec((1, CONV_W - 1, conv_dim), lambda c: (stream(c), 0, 0)),
            pl.BlockSpec((1, hp, d_state), lambda c: (stream(c), 0, 0)),
        ],
        out_shape=[
            jax.ShapeDtypeStruct((n_streams * chunks_per_stream * CHUNK, inner), BF16),
            jax.ShapeDtypeStruct((n_streams, CONV_W - 1, conv_dim), F32),
            jax.ShapeDtypeStruct((n_streams, hp, d_state), F32),
        ],
        scratch_shapes=[
            pltpu.VMEM((SUBLANES + n_rows, conv_dim), F32),
            pltpu.VMEM((n_rows, conv_dim), F32),
            pltpu.VMEM((d_state, inner), F32),
            pltpu.VMEM((cps, LANES, LANES), F32),
            pltpu.VMEM((cps, 3 * CHUNK + 32, inner), F32),
            pltpu.VMEM((n_rows, inner), F32),
        ],
        compiler_params=_cparams(1),
        name=name,
    )(zs, xbc, dtr, hist, h0, conv_w, conv_b, dtb, alog, dskx, normw, emat)


def _band_bias_tables(ext_ref, bias_scr, n_heads):
    col = lax.broadcasted_iota(jnp.int32, (CHUNK, ATT_WIN), 1)
    band = BAND_PAST + CHUNK
    for h in range(n_heads):
        e = jnp.broadcast_to(ext_ref[h:h + 1, :], (CHUNK, ATT_WIN))
        even = pltpu.roll(e, ATT_WIN - (CHUNK - 1), 1, stride=1, stride_axis=0)
        odd = pltpu.roll(e, 1, 1, stride=1, stride_axis=0)
        bias_scr[0, h] = jnp.where(col < band, even, MASK_NEG)
        bias_scr[1, h] = jnp.where(col >= CHUNK, odd, MASK_NEG)


def _attn_prompt_kernel(q_ref, kp_ref, kc_ref, vp_ref, vc_ref, ext_ref, o_ref,
                        bias_scr, s_scr, p_scr, *, n_heads):
    i = pl.program_id(0)
    d = ATT_HEAD_DIM
    scale = d ** -0.5
    half = ATT_TILE

    @pl.when(i == 0)
    def _():
        _band_bias_tables(ext_ref, bias_scr, n_heads)
        p_scr[...] = jnp.zeros_like(p_scr)

    past_mask = jnp.where(i == 0, MASK_NEG, 0.0).astype(F32)

    for h in range(n_heads):
        hs = slice(h * d, (h + 1) * d)
        q_h = q_ref[:, hs]
        s_scr[:, 0:half] = _dot_nt(q_h, kp_ref[:, hs]) * scale + past_mask
        s_scr[:, half:2 * half] = _dot_nt(q_h, kc_ref[:, hs]) * scale
        for t in range(ATT_TILE // CHUNK):
            rs = slice(t * CHUNK, (t + 1) * CHUNK)
            c0 = LANES * (t // 2)
            ws = slice(c0, c0 + ATT_WIN)
            s = s_scr[rs, ws] + bias_scr[t % 2, h]
            m = jnp.max(s, axis=-1, keepdims=True)
            p = jnp.exp(s - m)
            r = 1.0 / jnp.sum(p, axis=-1, keepdims=True)
            p_scr[rs, ws] = (p * r).astype(BF16)
        o_ref[:, hs] = (_dot(p_scr[:, 0:half], vp_ref[:, hs])
                        + _dot(p_scr[:, half:2 * half], vc_ref[:, hs])).astype(o_ref.dtype)


def _attn_prompt(q, kb, vb, ext, *, n_tiles, n_heads):
    width = n_heads * ATT_HEAD_DIM
    prev = lambda i: jnp.maximum(i - 1, 0)
    kern = functools.partial(_attn_prompt_kernel, n_heads=n_heads)
    return pl.pallas_call(
        kern,
        grid=(n_tiles,),
        in_specs=[
            pl.BlockSpec((ATT_TILE, width), lambda i: (i, 0)),
            pl.BlockSpec((ATT_TILE, width), lambda i: (prev(i), 0)),
            pl.BlockSpec((ATT_TILE, width), lambda i: (i, 0)),
            pl.BlockSpec((ATT_TILE, width), lambda i: (prev(i), 0)),
            pl.BlockSpec((ATT_TILE, width), lambda i: (i, 0)),
            pl.BlockSpec((n_heads, ATT_WIN), lambda i: (0, 0)),
        ],
        out_specs=pl.BlockSpec((ATT_TILE, width), lambda i: (i, 0)),
        out_shape=jax.ShapeDtypeStruct((n_tiles * ATT_TILE, width), BF16),
        scratch_shapes=[
            pltpu.VMEM((2, n_heads, CHUNK, ATT_WIN), F32),
            pltpu.VMEM((ATT_TILE, 2 * ATT_TILE), F32),
            pltpu.VMEM((ATT_TILE, 2 * ATT_TILE), BF16),
        ],
        compiler_params=_cparams(1),
        name="attn_prompt",
    )(q, kb, kb, vb, vb, ext)


def _attn_sample_kernel(q_ref, kn_ref, vn_ref, ck_ref, cv_ref, ext_ref, o_ref, bias_scr, *, n_heads):
    d = ATT_HEAD_DIM
    scale = d ** -0.5

    @pl.when(pl.program_id(0) == 0)
    def _():
        _band_bias_tables(ext_ref, bias_scr, n_heads)

    for h in range(n_heads):
        hs = slice(h * d, (h + 1) * d)
        rows = pl.ds(h, BAND_PAST, stride=n_heads)
        q_h = q_ref[:, hs]
        kc_h = ck_ref[rows, :].astype(BF16)
        vc_h = cv_ref[rows, :].astype(BF16)
        s1 = _dot_nt(q_h, kc_h) * scale + bias_scr[0, h, :, 0:BAND_PAST]
        s2 = _dot_nt(q_h, kn_ref[:, hs]) * scale + bias_scr[0, h, :, BAND_PAST:BAND_PAST + CHUNK]
        m = jnp.maximum(jnp.max(s1, axis=-1, keepdims=True), jnp.max(s2, axis=-1, keepdims=True))
        p1 = jnp.exp(s1 - m)
        p2 = jnp.exp(s2 - m)
        r = 1.0 / (jnp.sum(p1, axis=-1, keepdims=True) + jnp.sum(p2, axis=-1, keepdims=True))
        o = _dot((p1 * r).astype(BF16), vc_h) + _dot((p2 * r).astype(BF16), vn_ref[:, hs])
        o_ref[:, hs] = o.astype(o_ref.dtype)


def _attn_sample(q, kb, vb, cache_k, cache_v, ext, *, row_blk0, n_sample, n_heads):
    width = n_heads * ATT_HEAD_DIM
    kern = functools.partial(_attn_sample_kernel, n_heads=n_heads)
    cache_spec = pl.BlockSpec((None, BAND_PAST * n_heads, ATT_HEAD_DIM), lambda s: (s, 0, 0))
    row_spec = pl.BlockSpec((CHUNK, width), lambda s: (row_blk0 + s, 0))
    return pl.pallas_call(
        kern,
        grid=(n_sample,),
        in_specs=[row_spec, row_spec, row_spec, cache_spec, cache_spec,
                  pl.BlockSpec((n_heads, ATT_WIN), lambda s: (0, 0))],
        out_specs=pl.BlockSpec((CHUNK, width), lambda s: (s, 0)),
        out_shape=jax.ShapeDtypeStruct((n_sample * CHUNK, width), BF16),
        scratch_shapes=[pltpu.VMEM((2, n_heads, CHUNK, ATT_WIN), F32)],
        compiler_params=_cparams(1),
        name="attn_sample",
    )(q, kb, vb, cache_k, cache_v, ext)


def _mix_kernel(sp_ref, ss_ref, ap_ref, as_ref, w1_ref, w2_ref, g1_ref, g2_ref, o_ref, *, n_prompt_tiles):
    i = pl.program_id(1)
    is_p = i < n_prompt_tiles
    a1 = _dot(jnp.where(is_p, sp_ref[...], ss_ref[...]), w1_ref[...])
    a2 = _dot(jnp.where(is_p, ap_ref[...], as_ref[...]), w2_ref[...])
    g1 = jax.nn.sigmoid(g1_ref[...].astype(F32))
    g2 = jax.nn.sigmoid(g2_ref[...].astype(F32))
    o_ref[...] = (g1 * a1 + g2 * a2).astype(o_ref.dtype)


def _mix(ssd_p, ssd_s, att_p, att_s, w1, w2, gates_arr, *, gate_col0, tm, tn):
    k1 = ssd_p.shape[1]
    k2 = att_p.shape[1]
    m = att_p.shape[0] + att_s.shape[0]
    n = w1.shape[1]
    npt = att_p.shape[0] // tm
    assert ssd_p.shape[0] == att_p.shape[0] and ssd_s.shape[0] == att_s.shape[0]
    g1_blk = gate_col0 // tn
    g2_blk = (gate_col0 + n) // tn
    kern = functools.partial(_mix_kernel, n_prompt_tiles=npt)
    p_map = lambda j, i: (jnp.minimum(i, npt - 1), 0)
    s_map = lambda j, i: (jnp.maximum(i - npt, 0), 0)
    return pl.pallas_call(
        kern,
        grid=(n // tn, m // tm),
        in_specs=[
            pl.BlockSpec((tm, k1), p_map),
            pl.BlockSpec((tm, k1), s_map),
            pl.BlockSpec((tm, k2), p_map),
            pl.BlockSpec((tm, k2), s_map),
            pl.BlockSpec((k1, tn), lambda j, i: (0, j), pipeline_mode=pl.Buffered(1)),
            pl.BlockSpec((k2, tn), lambda j, i: (0, j), pipeline_mode=pl.Buffered(1)),
            pl.BlockSpec((tm, tn), lambda j, i: (i, g1_blk + j)),
            pl.BlockSpec((tm, tn), lambda j, i: (i, g2_blk + j)),
        ],
        out_specs=pl.BlockSpec((tm, tn), lambda j, i: (i, j)),
        out_shape=jax.ShapeDtypeStruct((m, n), BF16),
        compiler_params=_cparams(2),
        name="branch_mix",
    )(ssd_p, ssd_s, att_p, att_s, w1, w2, gates_arr, gates_arr)


def _layer_norm(t, g, b):
    mu = jnp.mean(t, axis=-1, keepdims=True)
    tc = t - mu
    var = jnp.mean(tc * tc, axis=-1, keepdims=True)
    return tc * lax.rsqrt(var + LN_EPS) * g + b


LN_SUB_ROWS = 128


def _proj_ln_kernel(*refs, alpha, n_k, n_prompt_tiles, split_resid, n_out):
    a_ref, w_ref = refs[0], refs[1]
    pos = 2
    r_refs = refs[pos:pos + (2 if split_resid else 1)]
    pos += len(r_refs)
    g_ref, b_ref = refs[pos], refs[pos + 1]
    pos += 2
    o_refs = refs[pos:pos + n_out]
    acc_ref = refs[pos + n_out] if n_k > 1 else None
    i = pl.program_id(0)
    kk = pl.program_id(1)
    tm = a_ref.shape[0]

    def resid_rows(rs):
        if split_resid:
            return jnp.where(i < n_prompt_tiles, r_refs[0][rs, :], r_refs[1][rs, :])
        return r_refs[0][rs, :]

    def finish(first):
        for r in range(tm // LN_SUB_ROWS):
            rs = slice(r * LN_SUB_ROWS, (r + 1) * LN_SUB_ROWS)
            base = alpha * resid_rows(rs) if first else acc_ref[rs, :]
            y = _layer_norm(base + _dot(a_ref[rs, :], w_ref[...]), g_ref[...], b_ref[...])
            for o_ref in o_refs:
                o_ref[rs, :] = y.astype(o_ref.dtype)

    if n_k == 1:
        finish(True)
        return

    @pl.when(kk == 0)
    def _():
        acc_ref[...] = alpha * resid_rows(slice(None)) + _dot(a_ref[...], w_ref[...])

    if n_k > 2:
        @pl.when(jnp.logical_and(kk > 0, kk < n_k - 1))
        def _():
            acc_ref[...] += _dot(a_ref[...], w_ref[...])

    @pl.when(kk == n_k - 1)
    def _():
        finish(False)


def _proj_ln(a, w, resid, gamma, beta, out_dtypes, *, alpha, tm, n_k, row0, n_rows, m_prompt=None, name):
    k = a.shape[1]
    n = w.shape[1]
    tk = k // n_k
    assert tk * n_k == k and n_rows % tm == 0 and row0 % tm == 0 and tm % LN_SUB_ROWS == 0
    blk0 = row0 // tm
    split_resid = isinstance(resid, tuple)
    row_map = lambda i, kk: (blk0 + i, 0)
    w_mode = dict(pipeline_mode=pl.Buffered(1)) if n_k == 1 else {}
    in_specs = [pl.BlockSpec((tm, tk), lambda i, kk: (blk0 + i, kk)),
                pl.BlockSpec((tk, n), lambda i, kk: (kk, 0), **w_mode)]
    args = [a, w]
    npt = 0
    if split_resid:
        assert row0 == 0 and m_prompt % tm == 0
        npt = m_prompt // tm
        in_specs += [pl.BlockSpec((tm, n), lambda i, kk: (jnp.minimum(i, npt - 1), 0)),
                     pl.BlockSpec((tm, n), lambda i, kk: (jnp.maximum(i - npt, 0), 0))]
        args += list(resid)
    else:
        in_specs.append(pl.BlockSpec((tm, n), row_map))
        args.append(resid)
    in_specs += [pl.BlockSpec((1, n), lambda i, kk: (0, 0))] * 2
    args += [gamma, beta]
    kern = functools.partial(_proj_ln_kernel, alpha=alpha, n_k=n_k, n_prompt_tiles=npt,
                             split_resid=split_resid, n_out=len(out_dtypes))
    return pl.pallas_call(
        kern,
        grid=(n_rows // tm, n_k),
        in_specs=in_specs,
        out_specs=[pl.BlockSpec((tm, n), lambda i, kk: (i, 0)) for _ in out_dtypes],
        out_shape=[jax.ShapeDtypeStruct((n_rows, n), dt) for dt in out_dtypes],
        scratch_shapes=[pltpu.VMEM((tm, n), F32)] if n_k > 1 else [],
        compiler_params=_cparams(2),
        name=name,
    )(*args)


def _glu_kernel(h_ref, wg_ref, wu_ref, o_ref, wg_sc, wu_sc):
    @pl.when(pl.program_id(1) == 0)
    def _():
        _cast_weight_tile(wg_ref, wg_sc)
        _cast_weight_tile(wu_ref, wu_sc)

    h = h_ref[...]
    g = _dot(h, wg_sc[...])
    u = _dot(h, wu_sc[...])
    o_ref[...] = (g * jax.nn.sigmoid(g) * u).astype(o_ref.dtype)


def _glu(h, w_gate_up, *, tm, tf):
    m, k = h.shape
    d_ff = w_gate_up.shape[1] // 2
    nf = d_ff // tf
    assert nf * tf == d_ff
    return pl.pallas_call(
        _glu_kernel,
        grid=(nf, m // tm),
        in_specs=[
            pl.BlockSpec((tm, k), lambda j, i: (i, 0)),
            pl.BlockSpec((k, tf), lambda j, i: (0, j)),
            pl.BlockSpec((k, tf), lambda j, i: (0, nf + j)),
        ],
        out_specs=pl.BlockSpec((tm, tf), lambda j, i: (i, j)),
        out_shape=jax.ShapeDtypeStruct((m, d_ff), BF16),
        scratch_shapes=[pltpu.VMEM((k, tf), BF16), pltpu.VMEM((k, tf), BF16)],
        compiler_params=_cparams(2),
        name="ffn_glu",
    )(h, w_gate_up, w_gate_up)


def kernel(x_prompt, x_sample, cache_k, cache_v, state_conv, state_ssm, w_in, conv_w, conv_b, dt_bias,
           a_log, d_skip, ssd_norm_w, rel_bias, w_ssd_out, w_att_out, w_o, ln1_g, ln1_b, w_gate_up,
           w_down, ln2_g, ln2_b):
    depth = w_in.shape[0]
    assert depth == 1
    batch, seq, d_model = x_prompt.shape
    n_sample, dec_seq, _ = x_sample.shape
    assert batch == 1 and dec_seq == CHUNK and seq % ATT_TILE == 0
    n_ssd_heads = dt_bias.shape[1]
    inner = n_ssd_heads * SSD_HEADDIM
    conv_dim = conv_w.shape[2]
    n_groups = SSD_GROUPS
    d_state = SSD_STATE
    assert conv_dim == inner + 2 * n_groups * d_state
    n_heads = rel_bias.shape[1]
    width = n_heads * ATT_HEAD_DIM
    assert cache_k.shape[2] == BAND_PAST
    alpha = (2.0 * depth) ** 0.25
    keep = min(BAND_PAST, seq)

    m_p = batch * seq
    m_s = n_sample * dec_seq
    xp2 = x_prompt.reshape(m_p, d_model)
    xs2 = x_sample.reshape(m_s, d_model)

    wt = w_in[0].T
    tn = 1024
    o_xbc = inner
    o_dt = o_xbc + conv_dim
    o_q = o_dt + n_ssd_heads
    shift = o_q % tn
    assert o_dt % tn == 0 and inner % tn == 0 and conv_dim % tn == 0 and width % tn == 0
    q_blk = o_dt // tn
    xb, dtr = _xcast_dt(xp2, xs2, wt, dt_col=o_dt, n_dt=n_ssd_heads, tm=512)
    (zs,) = _matmul_w32(xb, wt, lambda j: j, inner // tn, [BF16], tm=1536, tn=tn, silu=True,
                        name="in_proj_z")
    (xbc,) = _matmul_w32(xb, wt, lambda j: inner // tn + j, conv_dim // tn, [F32], tm=1536, tn=tn,
                         name="in_proj_xbc")
    wpt = width // tn
    (qg,) = _matmul_w32(xb, wt, lambda j: jnp.where(j < wpt, q_blk + j, q_blk + 2 * wpt + j),
                        wpt + 2 * d_model // tn, [BF16], tm=1536, tn=tn, shift=shift, name="in_proj_qg")
    kb, k_p, k_s = _matmul_w32(xb, wt, lambda j: q_blk + wpt + j, wpt, [BF16], tm=1024, tn=tn, shift=shift,
                               head_out=(keep, m_s), name="in_proj_k")
    vb, v_p, v_s = _matmul_w32(xb, wt, lambda j: q_blk + 2 * wpt + j, wpt, [BF16], tm=1024, tn=tn,
                               shift=shift, head_out=(keep, m_s), name="in_proj_v")

    pad_h = (0, LANES - n_ssd_heads)
    dtb = jnp.pad(dt_bias[0], pad_h).reshape(1, LANES)
    alog = jnp.pad(a_log[0], pad_h).reshape(1, LANES)
    dskx = jnp.repeat(d_skip[0], SSD_HEADDIM).reshape(1, inner)
    emat = (jnp.arange(LANES)[:, None] == (jnp.arange(inner)[None, :] // SSD_HEADDIM)).astype(BF16)
    ssd_args = (zs, xbc, dtr, state_conv[0], state_ssm[0].reshape(n_sample, inner, d_state),
                conv_w[0], conv_b[0].reshape(1, conv_dim), dtb, alog, dskx, ssd_norm_w[0].reshape(1, inner), emat)
    ssd_dims = dict(inner=inner, n_groups=n_groups, d_state=d_state)
    ssd_p, conv_p, h_p = _ssd(*ssd_args, row0=0, n_streams=batch, chunks_per_stream=seq // CHUNK,
                              cps=SSD_PROMPT_CPS, per_stream=False, name="ssd_scan_prompt", **ssd_dims)
    ssd_s, conv_s, h_s = _ssd(*ssd_args, row0=m_p, n_streams=n_sample, chunks_per_stream=1, cps=1,
                              per_stream=True, name="ssd_scan_sample", **ssd_dims)

    rb = rel_bias[0]
    band = BAND_PAST + CHUNK
    n_edge = band - REL_CLIP
    n_rev = ATT_WIN - n_edge
    rev = rb[:, ::-1][:, 1:1 + n_rev]
    ext = jnp.concatenate([jnp.broadcast_to(rb[:, 2 * REL_CLIP:], (n_heads, n_edge)), rev], axis=1)
    ext = jnp.pad(ext, ((0, 0), (0, ATT_WIN - ext.shape[1])))
    att_p = _attn_prompt(qg, kb, vb, ext, n_tiles=m_p // ATT_TILE, n_heads=n_heads)
    cache_rows = (n_sample, BAND_PAST * n_heads, ATT_HEAD_DIM)
    att_s = _attn_sample(qg, kb, vb, cache_k[0].reshape(cache_rows), cache_v[0].reshape(cache_rows), ext,
                         row_blk0=m_p // CHUNK, n_sample=n_sample, n_heads=n_heads)

    mixed = _mix(ssd_p, ssd_s, att_p, att_s, w_ssd_out[0].astype(BF16), w_att_out[0].astype(BF16), qg,
                 gate_col0=width, tm=512, tn=1024)
    g1 = ln1_g[0].reshape(1, d_model)
    b1 = ln1_b[0].reshape(1, d_model)
    g2 = ln2_g[0].reshape(1, d_model)
    b2 = ln2_b[0].reshape(1, d_model)
    h_f, h_b = _proj_ln(mixed, w_o[0].astype(BF16), (xp2, xs2), g1, b1, [F32, BF16], alpha=alpha, tm=512,
                        n_k=1, row0=0, n_rows=m_p + m_s, m_prompt=m_p, name="wo_ln")
    act = _glu(h_b, w_gate_up[0], tm=1536, tf=512)
    w_dn = w_down[0].astype(BF16)
    (y_p,) = _proj_ln(act, w_dn, h_f, g2, b2, [F32], alpha=alpha, tm=512, n_k=2, row0=0, n_rows=m_p,
                      name="down_ln_prompt")
    (y_s,) = _proj_ln(act, w_dn, h_f, g2, b2, [F32], alpha=alpha, tm=512, n_k=2, row0=m_p, n_rows=m_s,
                      name="down_ln_sample")

    return (y_p.reshape(batch, seq, d_model),
            y_s.reshape(n_sample, dec_seq, d_model),
            conv_p.reshape(1, batch, CONV_W - 1, conv_dim),
            h_p.reshape(1, batch, n_ssd_heads, SSD_HEADDIM, d_state),
            k_p.reshape(1, batch, keep, n_heads, ATT_HEAD_DIM),
            v_p.reshape(1, batch, keep, n_heads, ATT_HEAD_DIM),
            conv_s.reshape(1, n_sample, CONV_W - 1, conv_dim),
            h_s.reshape(1, n_sample, n_ssd_heads, SSD_HEADDIM, d_state),
            k_s.reshape(1, n_sample, dec_seq, n_heads, ATT_HEAD_DIM),
            v_s.reshape(1, n_sample, dec_seq, n_heads, ATT_HEAD_DIM))
```

```python
import functools

import jax
import jax.numpy as jnp
from jax import lax
from jax.experimental import pallas as pl
from jax.experimental.pallas import tpu as pltpu

F32 = jnp.float32
BF16 = jnp.bfloat16

CHUNK = 64
SSD_HEADDIM = 64
SSD_GROUPS = 8
SSD_STATE = 128
CONV_W = 4
ATT_HEAD_DIM = 128
BAND_CHUNKS = 8
BAND_PAST = BAND_CHUNKS * CHUNK
REL_CLIP = 128
LN_EPS = 1e-5
RMS_EPS = 1e-5
MASK_NEG = -1e30

LANES = 128
SUBLANES = 8
VMEM_LIMIT_BYTES = 56 * 1024 * 1024

SSD_PROMPT_CPS = 2
SIDE_CAST_ROWS = 128
DOWN_CAST_ROWS = 176
ATT_TILE = 512
ATT_WIN = 5 * LANES


def _cparams(n_axes):
    return pltpu.CompilerParams(
        dimension_semantics=("arbitrary",) * n_axes,
        vmem_limit_bytes=VMEM_LIMIT_BYTES,
    )


def _dot(a, b):
    return jnp.dot(a, b, preferred_element_type=F32)


def _dot_nt(a, b):
    return lax.dot_general(a, b, (((1,), (1,)), ((), ())), preferred_element_type=F32)


def _dot_tn(a, b):
    return lax.dot_general(a, b, (((0,), (0,)), ((), ())), preferred_element_type=F32)


def _xcast_dt_kernel(xp_ref, xs_ref, wd_ref, xb_ref, dt_ref, *, n_prompt_tiles, n_dt):
    i = pl.program_id(0)
    x = jnp.where(i < n_prompt_tiles, xp_ref[...], xs_ref[...]).astype(BF16)
    xb_ref[...] = x
    acc = _dot_nt(x, wd_ref[...].astype(BF16))
    lane = lax.broadcasted_iota(jnp.int32, acc.shape, 1)
    dt_ref[...] = jnp.where(lane < n_dt, acc, 0.0)


def _xcast_dt(xp, xs, wt, *, dt_col, n_dt, tm):
    m_p, k = xp.shape
    m_s = xs.shape[0]
    npt = m_p // tm
    n_tiles = npt + m_s // tm
    assert dt_col % LANES == 0 and n_dt <= LANES
    kern = functools.partial(_xcast_dt_kernel, n_prompt_tiles=npt, n_dt=n_dt)
    return pl.pallas_call(
        kern,
        grid=(n_tiles,),
        in_specs=[
            pl.BlockSpec((tm, k), lambda i: (jnp.minimum(i, npt - 1), 0)),
            pl.BlockSpec((tm, k), lambda i: (jnp.maximum(i - npt, 0), 0)),
            pl.BlockSpec((LANES, k), lambda i: (dt_col // LANES, 0)),
        ],
        out_specs=[pl.BlockSpec((tm, k), lambda i: (i, 0)),
                   pl.BlockSpec((tm, LANES), lambda i: (i, 0))],
        out_shape=[jax.ShapeDtypeStruct((m_p + m_s, k), BF16),
                   jax.ShapeDtypeStruct((m_p + m_s, LANES), F32)],
        compiler_params=_cparams(1),
        name="xcast_dt",
    )(xp, xs, wt)


CAST_ROWS = 256


def _cast_rows(src_ref, src_row0, dst_ref, dst_row0, n_rows):
    done = 0
    while done < n_rows:
        step = min(CAST_ROWS, n_rows - done)
        dst_ref[dst_row0 + done:dst_row0 + done + step, :] = (
            src_ref[src_row0 + done:src_row0 + done + step, :].astype(BF16))
        done += step


def _cast_weight_tile(w_ref, wsc):
    _cast_rows(w_ref, 0, wsc, 0, wsc.shape[0])


def _side_cast_specs(side, n_inner):
    in_specs, out_specs, out_shape = [], [], []
    for w, rows in side:
        n_blk = w.shape[0] // rows
        assert n_blk * rows == w.shape[0] and rows % 16 == 0
        idx = lambda j, i, n_blk=n_blk: (jnp.minimum(j * n_inner + i, n_blk - 1), 0)
        in_specs.append(pl.BlockSpec((rows, w.shape[1]), idx))
        out_specs.append(pl.BlockSpec((rows, w.shape[1]), idx))
        out_shape.append(jax.ShapeDtypeStruct(w.shape, BF16))
    return in_specs, out_specs, out_shape


def _mm_w32_kernel(*refs, shift, n_out, head_rows, silu, n_side):
    x_ref, wa_ref = refs[0], refs[1]
    pos = 2
    wb_ref = None
    if shift:
        wb_ref = refs[pos]
        pos += 1
    side_in = refs[pos:pos + n_side]
    pos += n_side
    o_refs = refs[pos:pos + n_out]
    pos += n_out
    hp_ref = hs_ref = None
    if head_rows is not None:
        hp_ref, hs_ref = refs[pos], refs[pos + 1]
        pos += 2
    side_out = refs[pos:pos + n_side]
    pos += n_side
    wsc = refs[pos]
    i = pl.program_id(1)

    @pl.when(i == 0)
    def _():
        tn = wsc.shape[0]
        _cast_rows(wa_ref, shift, wsc, 0, tn - shift)
        if shift:
            _cast_rows(wb_ref, 0, wsc, tn - shift, shift)

    for s_in, s_out in zip(side_in, side_out):
        s_out[...] = s_in[...].astype(BF16)

    acc = _dot_nt(x_ref[...], wsc[...])
    res = acc * jax.nn.sigmoid(acc) if silu else acc
    for o_ref in o_refs:
        o_ref[...] = res.astype(o_ref.dtype)

    if head_rows is not None:
        tile_p, row0_p, tile_s = head_rows

        @pl.when(i == tile_p)
        def _():
            hp_ref[...] = acc[row0_p:, :]

        @pl.when(i == tile_s)
        def _():
            hs_ref[...] = acc


def _matmul_w32(x, wt, col_blk, n_col_tiles, out_dtypes, *, tm, tn, shift=0, head_out=None, silu=False,
                side=(), name):
    m, k = x.shape
    assert m % tm == 0 and tn % LANES == 0 and shift % 16 == 0 and shift < tn
    in_specs = [pl.BlockSpec((tm, k), lambda j, i: (i, 0)),
                pl.BlockSpec((tn, k), lambda j, i: (col_blk(j), 0))]
    args = [x, wt]
    if shift:
        assert tn % shift == 0
        per = tn // shift
        in_specs.append(pl.BlockSpec((shift, k), lambda j, i: ((col_blk(j) + 1) * per, 0)))
        args.append(wt)
    side_in, side_out, side_shape = _side_cast_specs(side, m // tm)
    assert all(w.shape[0] // rows <= n_col_tiles * (m // tm) for w, rows in side)
    in_specs += side_in
    args += [w for w, _ in side]
    n = n_col_tiles * tn
    out_specs = [pl.BlockSpec((tm, tn), lambda j, i: (i, j)) for _ in out_dtypes]
    out_shape = [jax.ShapeDtypeStruct((m, n), dt) for dt in out_dtypes]
    head_rows = None
    if head_out is not None:
        n_keep, m_s = head_out
        assert m_s == tm and n_keep <= tm
        tile_s = m // tm - 1
        tile_p = tile_s - 1
        head_rows = (tile_p, tm - n_keep, tile_s)
        out_specs += [pl.BlockSpec((n_keep, tn), lambda j, i: (0, j)),
                      pl.BlockSpec((m_s, tn), lambda j, i: (0, j))]
        out_shape += [jax.ShapeDtypeStruct((n_keep, n), F32), jax.ShapeDtypeStruct((m_s, n), F32)]
    out_specs += side_out
    out_shape += side_shape
    kern = functools.partial(_mm_w32_kernel, shift=shift, n_out=len(out_dtypes), head_rows=head_rows,
                             silu=silu, n_side=len(side))
    return pl.pallas_call(
        kern,
        grid=(n_col_tiles, m // tm),
        in_specs=in_specs,
        out_specs=out_specs,
        out_shape=out_shape,
        scratch_shapes=[pltpu.VMEM((tn, k), BF16)],
        compiler_params=_cparams(2),
        name=name,
    )(*args)


def _softplus(x):
    return jnp.maximum(x, 0.0) + jnp.log1p(jnp.exp(-jnp.abs(x)))


def _ssd_chunk(u, zs_ref, dt_ref, cw_ref, cb_ref, dtb_ref, alog_ref, dsk_ref, nw_ref, e_ref, y_ref,
               xpad, act, ht, cst, ex, ysc, *, inner, n_groups, d_state):
    r0 = u * CHUNK
    rows = slice(r0, r0 + CHUNK)
    gw = inner // n_groups
    conv_dim = inner + 2 * n_groups * d_state

    for j in range(conv_dim // LANES):
        sl = slice(j * LANES, (j + 1) * LANES)
        blk = xpad[r0:r0 + SUBLANES + CHUNK, sl]
        a = cb_ref[:, sl]
        for tap in range(CONV_W):
            back = CONV_W - 1 - tap
            src = blk if back == 0 else pltpu.roll(blk, back, 0)
            a = a + src[SUBLANES:, :] * cw_ref[tap:tap + 1, sl]
        act[rows, sl] = a * jax.nn.sigmoid(a)

    dt = _softplus(dt_ref[rows, :] + dtb_ref[...])
    a_neg = -jnp.exp(alog_ref[...])
    da = dt * a_neg
    ri = lax.broadcasted_iota(jnp.int32, (CHUNK, CHUNK), 0)
    ci = lax.broadcasted_iota(jnp.int32, (CHUNK, CHUNK), 1)
    tri = (ri >= ci).astype(F32)
    cs = jnp.dot(tri, da, precision=lax.Precision.HIGHEST, preferred_element_type=F32)
    ecs = jnp.exp(cs)
    cs_last = cs[CHUNK - 1:CHUNK, :]
    dout = jnp.exp(cs_last - cs) * dt
    cd = jnp.exp(cs_last)
    cd_hi = cd.astype(BF16)
    cd_lo = (cd - cd_hi.astype(F32)).astype(BF16)
    lhs = jnp.concatenate(
        [dt.astype(BF16), ecs.astype(BF16), dout.astype(BF16),
         jnp.broadcast_to(cd_hi, (16, LANES)), jnp.broadcast_to(cd_lo, (16, LANES))], axis=0)
    ex[u] = _dot(lhs, e_ref[...])

    cst[u] = jnp.concatenate([cs, jnp.zeros((LANES - CHUNK, LANES), F32)], axis=0).T
    lane = lax.broadcasted_iota(jnp.int32, (CHUNK, LANES), 1)
    lo_half = lane < CHUNK
    n_pairs = inner // LANES
    ev = cst[u, pl.ds(0, n_pairs, stride=2), :]
    od = cst[u, pl.ds(1, n_pairs, stride=2), :]
    lane_p = lax.broadcasted_iota(jnp.int32, (n_pairs, LANES), 1)
    cst2 = jnp.where(lane_p < CHUNK, ev, pltpu.roll(od, CHUNK, 1))
    row = lax.broadcasted_iota(jnp.int32, (CHUNK, LANES), 0)
    causal2 = row >= jnp.where(lo_half, lane, lane - CHUNK)

    pairs_per_group = gw // LANES
    for g in range(n_groups):
        gs = slice(g * gw, (g + 1) * gw)
        bm_g = act[rows, inner + g * d_state: inner + (g + 1) * d_state].astype(BF16)
        cm_g = act[rows, inner + (n_groups + g) * d_state: inner + (n_groups + g + 1) * d_state].astype(BF16)
        cb2 = _dot_nt(cm_g, jnp.concatenate([bm_g, bm_g], axis=0))
        h_g = ht[:, gs]
        yoff = _dot(cm_g, h_g.astype(BF16))
        for p in range(pairs_per_group):
            k = g * pairs_per_group + p
            ls = slice(k * LANES, (k + 1) * LANES)
            csc = jnp.where(lo_half,
                            jnp.broadcast_to(cs[:, 2 * k:2 * k + 1], (CHUNK, LANES)),
                            jnp.broadcast_to(cs[:, 2 * k + 1:2 * k + 2], (CHUNK, LANES)))
            diff = csc - cst2[k:k + 1, :]
            dec = jnp.exp(jnp.where(causal2, diff, MASK_NEG))
            m2 = (cb2 * dec).astype(BF16)
            xs_p = act[rows, ls]
            xdt = xs_p * ex[u, 0:CHUNK, ls]
            x2 = jnp.concatenate([jnp.where(lo_half, xdt, 0.0), jnp.where(lo_half, 0.0, xdt)],
                                 axis=0).astype(BF16)
            yd = _dot(m2, x2)
            ysc[rows, ls] = (yd + yoff[:, p * LANES:(p + 1) * LANES] * ex[u, CHUNK:2 * CHUNK, ls]
                             + dsk_ref[:, ls] * xs_p)
        xw = (act[rows, gs] * ex[u, 2 * CHUNK:3 * CHUNK, gs]).astype(BF16)
        st = _dot_tn(bm_g, xw)
        cdx = ex[u, 3 * CHUNK:3 * CHUNK + 1, gs] + ex[u, 3 * CHUNK + 16:3 * CHUNK + 17, gs]
        ht[:, gs] = h_g * cdx + st

    for g in range(n_groups):
        gs = slice(g * gw, (g + 1) * gw)
        v = ysc[rows, gs] * zs_ref[rows, gs].astype(F32)
        ms = jnp.mean(v * v, axis=-1, keepdims=True)
        y_ref[rows, gs] = (v * lax.rsqrt(ms + RMS_EPS) * nw_ref[:, gs]).astype(y_ref.dtype)


def _ssd_kernel(zs_ref, xbc_ref, dt_ref, hist_ref, h0_ref, cw_ref, cb_ref, dtb_ref, alog_ref,
                dsk_ref, nw_ref, e_ref,
                y_ref, conv_ref, ho_ref,
                xpad, act, ht, cst, ex, ysc, *, cps, per_stream, inner, n_groups, d_state):
    c = pl.program_id(0)
    n_rows = cps * CHUNK
    conv_dim = inner + 2 * n_groups * d_state

    @pl.when(c == 0)
    def _():
        xpad[0:SUBLANES, :] = jnp.zeros((SUBLANES, conv_dim), F32)
        if not per_stream:
            ht[...] = jnp.zeros_like(ht)

    if per_stream:
        xpad[5:8, :] = hist_ref[0]
        ht[...] = h0_ref[0].T

    xpad[SUBLANES:SUBLANES + n_rows, :] = xbc_ref[...]
    conv_ref[0] = xbc_ref[n_rows - (CONV_W - 1):n_rows, :]

    for u in range(cps):
        _ssd_chunk(u, zs_ref, dt_ref, cw_ref, cb_ref, dtb_ref, alog_ref, dsk_ref, nw_ref, e_ref, y_ref,
                   xpad, act, ht, cst, ex, ysc, inner=inner, n_groups=n_groups, d_state=d_state)
    xpad[5:8, :] = xpad[5 + n_rows:8 + n_rows, :]

    if per_stream:
        ho_ref[0] = ht[...].T
    else:
        @pl.when(c == pl.num_programs(0) - 1)
        def _():
            ho_ref[0] = ht[...].T


def _ssd(zs, xbc, dtr, hist, h0, conv_w, conv_b, dtb, alog, dskx, normw, emat, *,
         row0, n_streams, chunks_per_stream, cps, per_stream, inner, n_groups, d_state, name):
    conv_dim = xbc.shape[1]
    hp = inner
    n_rows = cps * CHUNK
    assert chunks_per_stream % cps == 0 and row0 % n_rows == 0
    assert per_stream == (chunks_per_stream == cps) and (per_stream or n_streams == 1)
    steps_per_stream = chunks_per_stream // cps
    blk0 = row0 // n_rows
    stream = (lambda c: c) if per_stream else (lambda c: 0)

    kern = functools.partial(_ssd_kernel, cps=cps, per_stream=per_stream, inner=inner, n_groups=n_groups,
                             d_state=d_state)
    const = lambda c: (0, 0)
    row_map = lambda c: (blk0 + c, 0)
    return pl.pallas_call(
        kern,
        grid=(n_streams * steps_per_stream,),
        in_specs=[
            pl.BlockSpec((n_rows, inner), row_map),
            pl.BlockSpec((n_rows, conv_dim), row_map),
            pl.BlockSpec((n_rows, LANES), row_map),
            pl.BlockSpec((1, CONV_W - 1, conv_dim), lambda c: (stream(c), 0, 0)),
            pl.BlockSpec((1, hp, d_state), lambda c: (stream(c), 0, 0)),
            pl.BlockSpec((CONV_W, conv_dim), const),
            pl.BlockSpec((1, conv_dim), const),
            pl.BlockSpec((1, LANES), const),
            pl.BlockSpec((1, LANES), const),
            pl.BlockSpec((1, inner), const),
            pl.BlockSpec((1, inner), const),
            pl.BlockSpec((LANES, inner), const),
        ],
        out_specs=[
            pl.BlockSpec((n_rows, inner), lambda c: (c, 0)),
            pl.BlockSpec((1, CONV_W - 1, conv_dim), lambda c: (stream(c), 0, 0)),
            pl.BlockSpec((1, hp, d_state), lambda c: (stream(c), 0, 0)),
        ],
        out_shape=[
            jax.ShapeDtypeStruct((n_streams * chunks_per_stream * CHUNK, inner), BF16),
            jax.ShapeDtypeStruct((n_streams, CONV_W - 1, conv_dim), F32),
            jax.ShapeDtypeStruct((n_streams, hp, d_state), F32),
        ],
        scratch_shapes=[
            pltpu.VMEM((SUBLANES + n_rows, conv_dim), F32),
            pltpu.VMEM((n_rows, conv_dim), F32),
            pltpu.VMEM((d_state, inner), F32),
            pltpu.VMEM((cps, LANES, LANES), F32),
            pltpu.VMEM((cps, 3 * CHUNK + 32, inner), F32),
            pltpu.VMEM((n_rows, inner), F32),
        ],
        compiler_params=_cparams(1),
        name=name,
    )(zs, xbc, dtr, hist, h0, conv_w, conv_b, dtb, alog, dskx, normw, emat)


def _band_bias_tables(ext_ref, bias_scr, n_heads):
    col = lax.broadcasted_iota(jnp.int32, (CHUNK, ATT_WIN), 1)
    band = BAND_PAST + CHUNK
    for h in range(n_heads):
        e = jnp.broadcast_to(ext_ref[h:h + 1, :], (CHUNK, ATT_WIN))
        even = pltpu.roll(e, ATT_WIN - (CHUNK - 1), 1, stride=1, stride_axis=0)
        odd = pltpu.roll(e, 1, 1, stride=1, stride_axis=0)
        bias_scr[0, h] = jnp.where(col < band, even, MASK_NEG)
        bias_scr[1, h] = jnp.where(col >= CHUNK, odd, MASK_NEG)


def _attn_prompt_kernel(q_ref, kp_ref, kc_ref, vp_ref, vc_ref, ext_ref, o_ref,
                        bias_scr, s_scr, p_scr, *, n_heads):
    i = pl.program_id(0)
    d = ATT_HEAD_DIM
    scale = d ** -0.5
    half = ATT_TILE

    @pl.when(i == 0)
    def _():
        _band_bias_tables(ext_ref, bias_scr, n_heads)
        p_scr[...] = jnp.zeros_like(p_scr)

    past_mask = jnp.where(i == 0, MASK_NEG, 0.0).astype(F32)

    for h in range(n_heads):
        hs = slice(h * d, (h + 1) * d)
        q_h = q_ref[:, hs]
        s_scr[:, 0:half] = _dot_nt(q_h, kp_ref[:, hs]) * scale + past_mask
        s_scr[:, half:2 * half] = _dot_nt(q_h, kc_ref[:, hs]) * scale
        for t in range(ATT_TILE // CHUNK):
            rs = slice(t * CHUNK, (t + 1) * CHUNK)
            c0 = LANES * (t // 2)
            ws = slice(c0, c0 + ATT_WIN)
            s = s_scr[rs, ws] + bias_scr[t % 2, h]
            m = jnp.max(s, axis=-1, keepdims=True)
            p = jnp.exp(s - m)
            r = 1.0 / jnp.sum(p, axis=-1, keepdims=True)
            p_scr[rs, ws] = (p * r).astype(BF16)
        o_ref[:, hs] = (_dot(p_scr[:, 0:half], vp_ref[:, hs])
                        + _dot(p_scr[:, half:2 * half], vc_ref[:, hs])).astype(o_ref.dtype)


def _attn_prompt(q, kb, vb, ext, *, n_tiles, n_heads):
    width = n_heads * ATT_HEAD_DIM
    prev = lambda i: jnp.maximum(i - 1, 0)
    kern = functools.partial(_attn_prompt_kernel, n_heads=n_heads)
    return pl.pallas_call(
        kern,
        grid=(n_tiles,),
        in_specs=[
            pl.BlockSpec((ATT_TILE, width), lambda i: (i, 0)),
            pl.BlockSpec((ATT_TILE, width), lambda i: (prev(i), 0)),
            pl.BlockSpec((ATT_TILE, width), lambda i: (i, 0)),
            pl.BlockSpec((ATT_TILE, width), lambda i: (prev(i), 0)),
            pl.BlockSpec((ATT_TILE, width), lambda i: (i, 0)),
            pl.BlockSpec((n_heads, ATT_WIN), lambda i: (0, 0)),
        ],
        out_specs=pl.BlockSpec((ATT_TILE, width), lambda i: (i, 0)),
        out_shape=jax.ShapeDtypeStruct((n_tiles * ATT_TILE, width), BF16),
        scratch_shapes=[
            pltpu.VMEM((2, n_heads, CHUNK, ATT_WIN), F32),
            pltpu.VMEM((ATT_TILE, 2 * ATT_TILE), F32),
            pltpu.VMEM((ATT_TILE, 2 * ATT_TILE), BF16),
        ],
        compiler_params=_cparams(1),
        name="attn_prompt",
    )(q, kb, kb, vb, vb, ext)


def _attn_sample_kernel(q_ref, kn_ref, vn_ref, ck_ref, cv_ref, ext_ref, o_ref, bias_scr, *, n_heads):
    d = ATT_HEAD_DIM
    scale = d ** -0.5

    @pl.when(pl.program_id(0) == 0)
    def _():
        _band_bias_tables(ext_ref, bias_scr, n_heads)

    for h in range(n_heads):
        hs = slice(h * d, (h + 1) * d)
        rows = pl.ds(h, BAND_PAST, stride=n_heads)
        q_h = q_ref[:, hs]
        kc_h = ck_ref[rows, :].astype(BF16)
        vc_h = cv_ref[rows, :].astype(BF16)
        s1 = _dot_nt(q_h, kc_h) * scale + bias_scr[0, h, :, 0:BAND_PAST]
        s2 = _dot_nt(q_h, kn_ref[:, hs]) * scale + bias_scr[0, h, :, BAND_PAST:BAND_PAST + CHUNK]
        m = jnp.maximum(jnp.max(s1, axis=-1, keepdims=True), jnp.max(s2, axis=-1, keepdims=True))
        p1 = jnp.exp(s1 - m)
        p2 = jnp.exp(s2 - m)
        r = 1.0 / (jnp.sum(p1, axis=-1, keepdims=True) + jnp.sum(p2, axis=-1, keepdims=True))
        o = _dot((p1 * r).astype(BF16), vc_h) + _dot((p2 * r).astype(BF16), vn_ref[:, hs])
        o_ref[:, hs] = o.astype(o_ref.dtype)


def _attn_sample(q, kb, vb, cache_k, cache_v, ext, *, row_blk0, n_sample, n_heads):
    width = n_heads * ATT_HEAD_DIM
    kern = functools.partial(_attn_sample_kernel, n_heads=n_heads)
    cache_spec = pl.BlockSpec((None, BAND_PAST * n_heads, ATT_HEAD_DIM), lambda s: (s, 0, 0))
    row_spec = pl.BlockSpec((CHUNK, width), lambda s: (row_blk0 + s, 0))
    return pl.pallas_call(
        kern,
        grid=(n_sample,),
        in_specs=[row_spec, row_spec, row_spec, cache_spec, cache_spec,
                  pl.BlockSpec((n_heads, ATT_WIN), lambda s: (0, 0))],
        out_specs=pl.BlockSpec((CHUNK, width), lambda s: (s, 0)),
        out_shape=jax.ShapeDtypeStruct((n_sample * CHUNK, width), BF16),
        scratch_shapes=[pltpu.VMEM((2, n_heads, CHUNK, ATT_WIN), F32)],
        compiler_params=_cparams(1),
        name="attn_sample",
    )(q, kb, vb, cache_k, cache_v, ext)


def _mix_kernel(sp_ref, ss_ref, ap_ref, as_ref, w1_ref, w2_ref, g1_ref, g2_ref, o_ref, *, n_prompt_tiles):
    i = pl.program_id(1)
    is_p = i < n_prompt_tiles
    a1 = _dot(jnp.where(is_p, sp_ref[...], ss_ref[...]), w1_ref[...])
    a2 = _dot(jnp.where(is_p, ap_ref[...], as_ref[...]), w2_ref[...])
    g1 = jax.nn.sigmoid(g1_ref[...].astype(F32))
    g2 = jax.nn.sigmoid(g2_ref[...].astype(F32))
    o_ref[...] = (g1 * a1 + g2 * a2).astype(o_ref.dtype)


def _mix(ssd_p, ssd_s, att_p, att_s, w1, w2, gates_arr, *, gate_col0, tm, tn):
    k1 = ssd_p.shape[1]
    k2 = att_p.shape[1]
    m = att_p.shape[0] + att_s.shape[0]
    n = w1.shape[1]
    npt = att_p.shape[0] // tm
    assert ssd_p.shape[0] == att_p.shape[0] and ssd_s.shape[0] == att_s.shape[0]
    g1_blk = gate_col0 // tn
    g2_blk = (gate_col0 + n) // tn
    kern = functools.partial(_mix_kernel, n_prompt_tiles=npt)
    p_map = lambda j, i: (jnp.minimum(i, npt - 1), 0)
    s_map = lambda j, i: (jnp.maximum(i - npt, 0), 0)
    return pl.pallas_call(
        kern,
        grid=(n // tn, m // tm),
        in_specs=[
            pl.BlockSpec((tm, k1), p_map),
            pl.BlockSpec((tm, k1), s_map),
            pl.BlockSpec((tm, k2), p_map),
            pl.BlockSpec((tm, k2), s_map),
            pl.BlockSpec((k1, tn), lambda j, i: (0, j), pipeline_mode=pl.Buffered(1)),
            pl.BlockSpec((k2, tn), lambda j, i: (0, j), pipeline_mode=pl.Buffered(1)),
            pl.BlockSpec((tm, tn), lambda j, i: (i, g1_blk + j)),
            pl.BlockSpec((tm, tn), lambda j, i: (i, g2_blk + j)),
        ],
        out_specs=pl.BlockSpec((tm, tn), lambda j, i: (i, j)),
        out_shape=jax.ShapeDtypeStruct((m, n), BF16),
        compiler_params=_cparams(2),
        name="branch_mix",
    )(ssd_p, ssd_s, att_p, att_s, w1, w2, gates_arr, gates_arr)


def _layer_norm(t, g, b):
    mu = jnp.mean(t, axis=-1, keepdims=True)
    tc = t - mu
    var = jnp.mean(tc * tc, axis=-1, keepdims=True)
    return tc * lax.rsqrt(var + LN_EPS) * g + b


LN_SUB_ROWS = 256


def _proj_ln_kernel(*refs, alpha, n_k, n_prompt_tiles, split_resid, n_out):
    a_ref, w_ref = refs[0], refs[1]
    pos = 2
    r_refs = refs[pos:pos + (2 if split_resid else 1)]
    pos += len(r_refs)
    g_ref, b_ref = refs[pos], refs[pos + 1]
    pos += 2
    o_refs = refs[pos:pos + n_out]
    acc_ref = refs[pos + n_out] if n_k > 1 else None
    i = pl.program_id(0)
    kk = pl.program_id(1)
    tm = a_ref.shape[0]

    def resid_rows(rs):
        if split_resid:
            return jnp.where(i < n_prompt_tiles, r_refs[0][rs, :], r_refs[1][rs, :])
        return r_refs[0][rs, :]

    def finish(first):
        for r in range(tm // LN_SUB_ROWS):
            rs = slice(r * LN_SUB_ROWS, (r + 1) * LN_SUB_ROWS)
            base = alpha * resid_rows(rs) if first else acc_ref[rs, :]
            y = _layer_norm(base + _dot(a_ref[rs, :], w_ref[...]), g_ref[...], b_ref[...])
            for o_ref in o_refs:
                o_ref[rs, :] = y.astype(o_ref.dtype)

    if n_k == 1:
        finish(True)
        return

    @pl.when(kk == 0)
    def _():
        acc_ref[...] = alpha * resid_rows(slice(None)) + _dot(a_ref[...], w_ref[...])

    if n_k > 2:
        @pl.when(jnp.logical_and(kk > 0, kk < n_k - 1))
        def _():
            acc_ref[...] += _dot(a_ref[...], w_ref[...])

    @pl.when(kk == n_k - 1)
    def _():
        finish(False)


def _proj_ln(a, w, resid, gamma, beta, out_dtypes, *, alpha, tm, n_k, row0, n_rows, m_prompt=None, name):
    k = a.shape[1]
    n = w.shape[1]
    tk = k // n_k
    assert tk * n_k == k and n_rows % tm == 0 and row0 % tm == 0 and tm % LN_SUB_ROWS == 0
    blk0 = row0 // tm
    split_resid = isinstance(resid, tuple)
    row_map = lambda i, kk: (blk0 + i, 0)
    w_mode = dict(pipeline_mode=pl.Buffered(1)) if n_k == 1 else {}
    in_specs = [pl.BlockSpec((tm, tk), lambda i, kk: (blk0 + i, kk)),
                pl.BlockSpec((tk, n), lambda i, kk: (kk, 0), **w_mode)]
    args = [a, w]
    npt = 0
    if split_resid:
        assert row0 == 0 and m_prompt % tm == 0
        npt = m_prompt // tm
        in_specs += [pl.BlockSpec((tm, n), lambda i, kk: (jnp.minimum(i, npt - 1), 0)),
                     pl.BlockSpec((tm, n), lambda i, kk: (jnp.maximum(i - npt, 0), 0))]
        args += list(resid)
    else:
        in_specs.append(pl.BlockSpec((tm, n), row_map))
        args.append(resid)
    in_specs += [pl.BlockSpec((1, n), lambda i, kk: (0, 0))] * 2
    args += [gamma, beta]
    kern = functools.partial(_proj_ln_kernel, alpha=alpha, n_k=n_k, n_prompt_tiles=npt,
                             split_resid=split_resid, n_out=len(out_dtypes))
    return pl.pallas_call(
        kern,
        grid=(n_rows // tm, n_k),
        in_specs=in_specs,
        out_specs=[pl.BlockSpec((tm, n), lambda i, kk: (i, 0)) for _ in out_dtypes],
        out_shape=[jax.ShapeDtypeStruct((n_rows, n), dt) for dt in out_dtypes],
        scratch_shapes=[pltpu.VMEM((tm, n), F32)] if n_k > 1 else [],
        compiler_params=_cparams(2),
        name=name,
    )(*args)


def _glu_kernel(*refs, n_side):
    h_ref, wg_ref, wu_ref = refs[:3]
    side_in = refs[3:3 + n_side]
    o_ref = refs[3 + n_side]
    side_out = refs[4 + n_side:4 + 2 * n_side]
    wg_sc, wu_sc = refs[4 + 2 * n_side:]

    @pl.when(pl.program_id(1) == 0)
    def _():
        _cast_weight_tile(wg_ref, wg_sc)
        _cast_weight_tile(wu_ref, wu_sc)

    for s_in, s_out in zip(side_in, side_out):
        s_out[...] = s_in[...].astype(BF16)

    h = h_ref[...]
    g = _dot(h, wg_sc[...])
    u = _dot(h, wu_sc[...])
    o_ref[...] = (g * jax.nn.sigmoid(g) * u).astype(o_ref.dtype)


def _glu(h, w_gate_up, *, tm, tf, side=()):
    m, k = h.shape
    d_ff = w_gate_up.shape[1] // 2
    nf = d_ff // tf
    assert nf * tf == d_ff
    side_in, side_out, side_shape = _side_cast_specs(side, m // tm)
    assert all(w.shape[0] // rows <= nf * (m // tm) for w, rows in side)
    return pl.pallas_call(
        functools.partial(_glu_kernel, n_side=len(side)),
        grid=(nf, m // tm),
        in_specs=[
            pl.BlockSpec((tm, k), lambda j, i: (i, 0)),
            pl.BlockSpec((k, tf), lambda j, i: (0, j)),
            pl.BlockSpec((k, tf), lambda j, i: (0, nf + j)),
        ] + side_in,
        out_specs=[pl.BlockSpec((tm, tf), lambda j, i: (i, j))] + side_out,
        out_shape=[jax.ShapeDtypeStruct((m, d_ff), BF16)] + side_shape,
        scratch_shapes=[pltpu.VMEM((k, tf), BF16), pltpu.VMEM((k, tf), BF16)],
        compiler_params=_cparams(2),
        name="ffn_glu",
    )(h, w_gate_up, w_gate_up, *[w for w, _ in side])


def kernel(x_prompt, x_sample, cache_k, cache_v, state_conv, state_ssm, w_in, conv_w, conv_b, dt_bias,
           a_log, d_skip, ssd_norm_w, rel_bias, w_ssd_out, w_att_out, w_o, ln1_g, ln1_b, w_gate_up,
           w_down, ln2_g, ln2_b):
    depth = w_in.shape[0]
    assert depth == 1
    batch, seq, d_model = x_prompt.shape
    n_sample, dec_seq, _ = x_sample.shape
    assert batch == 1 and dec_seq == CHUNK and seq % ATT_TILE == 0
    n_ssd_heads = dt_bias.shape[1]
    inner = n_ssd_heads * SSD_HEADDIM
    conv_dim = conv_w.shape[2]
    n_groups = SSD_GROUPS
    d_state = SSD_STATE
    assert conv_dim == inner + 2 * n_groups * d_state
    n_heads = rel_bias.shape[1]
    width = n_heads * ATT_HEAD_DIM
    assert cache_k.shape[2] == BAND_PAST
    alpha = (2.0 * depth) ** 0.25
    keep = min(BAND_PAST, seq)

    m_p = batch * seq
    m_s = n_sample * dec_seq
    xp2 = x_prompt.reshape(m_p, d_model)
    xs2 = x_sample.reshape(m_s, d_model)

    wt = w_in[0].T
    tn = 1024
    o_xbc = inner
    o_dt = o_xbc + conv_dim
    o_q = o_dt + n_ssd_heads
    shift = o_q % tn
    assert o_dt % tn == 0 and inner % tn == 0 and conv_dim % tn == 0 and width % tn == 0
    q_blk = o_dt // tn
    xb, dtr = _xcast_dt(xp2, xs2, wt, dt_col=o_dt, n_dt=n_ssd_heads, tm=512)
    zs, w_att_b, w_o_b = _matmul_w32(xb, wt, lambda j: j, inner // tn, [BF16], tm=1536, tn=tn, silu=True,
                                     side=[(w_att_out[0], SIDE_CAST_ROWS), (w_o[0], SIDE_CAST_ROWS)],
                                     name="in_proj_z")
    (xbc,) = _matmul_w32(xb, wt, lambda j: inner // tn + j, conv_dim // tn, [F32], tm=1536, tn=tn,
                         name="in_proj_xbc")
    wpt = width // tn
    qg, w_ssd_b = _matmul_w32(xb, wt, lambda j: jnp.where(j < wpt, q_blk + j, q_blk + 2 * wpt + j),
                              wpt + 2 * d_model // tn, [BF16], tm=1536, tn=tn, shift=shift,
                              side=[(w_ssd_out[0], SIDE_CAST_ROWS)], name="in_proj_qg")
    kb, k_p, k_s = _matmul_w32(xb, wt, lambda j: q_blk + wpt + j, wpt, [BF16], tm=1024, tn=tn, shift=shift,
                               head_out=(keep, m_s), name="in_proj_k")
    vb, v_p, v_s = _matmul_w32(xb, wt, lambda j: q_blk + 2 * wpt + j, wpt, [BF16], tm=1024, tn=tn,
                               shift=shift, head_out=(keep, m_s), name="in_proj_v")

    pad_h = (0, LANES - n_ssd_heads)
    dtb = jnp.pad(dt_bias[0], pad_h).reshape(1, LANES)
    alog = jnp.pad(a_log[0], pad_h).reshape(1, LANES)
    dskx = jnp.repeat(d_skip[0], SSD_HEADDIM).reshape(1, inner)
    emat = (jnp.arange(LANES)[:, None] == (jnp.arange(inner)[None, :] // SSD_HEADDIM)).astype(BF16)
    ssd_args = (zs, xbc, dtr, state_conv[0], state_ssm[0].reshape(n_sample, inner, d_state),
                conv_w[0], conv_b[0].reshape(1, conv_dim), dtb, alog, dskx, ssd_norm_w[0].reshape(1, inner), emat)
    ssd_dims = dict(inner=inner, n_groups=n_groups, d_state=d_state)
    ssd_p, conv_p, h_p = _ssd(*ssd_args, row0=0, n_streams=batch, chunks_per_stream=seq // CHUNK,
                              cps=SSD_PROMPT_CPS, per_stream=False, name="ssd_scan_prompt", **ssd_dims)
    ssd_s, conv_s, h_s = _ssd(*ssd_args, row0=m_p, n_streams=n_sample, chunks_per_stream=1, cps=1,
                              per_stream=True, name="ssd_scan_sample", **ssd_dims)

    rb = rel_bias[0]
    band = BAND_PAST + CHUNK
    n_edge = band - REL_CLIP
    n_rev = ATT_WIN - n_edge
    rev = rb[:, ::-1][:, 1:1 + n_rev]
    ext = jnp.concatenate([jnp.broadcast_to(rb[:, 2 * REL_CLIP:], (n_heads, n_edge)), rev], axis=1)
    ext = jnp.pad(ext, ((0, 0), (0, ATT_WIN - ext.shape[1])))
    att_p = _attn_prompt(qg, kb, vb, ext, n_tiles=m_p // ATT_TILE, n_heads=n_heads)
    cache_rows = (n_sample, BAND_PAST * n_heads, ATT_HEAD_DIM)
    att_s = _attn_sample(qg, kb, vb, cache_k[0].reshape(cache_rows), cache_v[0].reshape(cache_rows), ext,
                         row_blk0=m_p // CHUNK, n_sample=n_sample, n_heads=n_heads)

    mixed = _mix(ssd_p, ssd_s, att_p, att_s, w_ssd_b, w_att_b, qg, gate_col0=width, tm=512, tn=1024)
    g1 = ln1_g[0].reshape(1, d_model)
    b1 = ln1_b[0].reshape(1, d_model)
    g2 = ln2_g[0].reshape(1, d_model)
    b2 = ln2_b[0].reshape(1, d_model)
    h_f, h_b = _proj_ln(mixed, w_o_b, (xp2, xs2), g1, b1, [F32, BF16], alpha=alpha, tm=512,
                        n_k=1, row0=0, n_rows=m_p + m_s, m_prompt=m_p, name="wo_ln")
    act, w_dn = _glu(h_b, w_gate_up[0], tm=1536, tf=512, side=[(w_down[0], DOWN_CAST_ROWS)])
    (y_p,) = _proj_ln(act, w_dn, h_f, g2, b2, [F32], alpha=alpha, tm=512, n_k=2, row0=0, n_rows=m_p,
                      name="down_ln_prompt")
    (y_s,) = _proj_ln(act, w_dn, h_f, g2, b2, [F32], alpha=alpha, tm=512, n_k=2, row0=m_p, n_rows=m_s,
                      name="down_ln_sample")

    return (y_p.reshape(batch, seq, d_model),
            y_s.reshape(n_sample, dec_seq, d_model),
            conv_p.reshape(1, batch, CONV_W - 1, conv_dim),
            h_p.reshape(1, batch, n_ssd_heads, SSD_HEADDIM, d_state),
            k_p.reshape(1, batch, keep, n_heads, ATT_HEAD_DIM),
            v_p.reshape(1, batch, keep, n_heads, ATT_HEAD_DIM),
            conv_s.reshape(1, n_sample, CONV_W - 1, conv_dim),
            h_s.reshape(1, n_sample, n_ssd_heads, SSD_HEADDIM, d_state),
            k_s.reshape(1, n_sample, dec_seq, n_heads, ATT_HEAD_DIM),
            v_s.reshape(1, n_sample, dec_seq, n_heads, ATT_HEAD_DIM))
```

```python
import functools

import jax
import jax.numpy as jnp
from jax import lax
from jax.experimental import pallas as pl
from jax.experimental.pallas import tpu as pltpu

F32 = jnp.float32
BF16 = jnp.bfloat16

CHUNK = 64
SSD_HEADDIM = 64
SSD_GROUPS = 8
SSD_STATE = 128
CONV_W = 4
ATT_HEAD_DIM = 128
BAND_CHUNKS = 8
BAND_PAST = BAND_CHUNKS * CHUNK
REL_CLIP = 128
LN_EPS = 1e-5
RMS_EPS = 1e-5
MASK_NEG = -1e30

LANES = 128
SUBLANES = 8
VMEM_LIMIT_BYTES = 56 * 1024 * 1024

SSD_PROMPT_CPS = 4
SIDE_CAST_ROWS = 128
DOWN_CAST_ROWS = 176
ATT_TILE = 512
ATT_SAMPLE_SPB = 2
ATT_WIN = 5 * LANES


def _cparams(n_axes):
    return pltpu.CompilerParams(
        dimension_semantics=("arbitrary",) * n_axes,
        vmem_limit_bytes=VMEM_LIMIT_BYTES,
    )


def _dot(a, b):
    return jnp.dot(a, b, preferred_element_type=F32)


def _dot_nt(a, b):
    return lax.dot_general(a, b, (((1,), (1,)), ((), ())), preferred_element_type=F32)


def _dot_tn(a, b):
    return lax.dot_general(a, b, (((0,), (0,)), ((), ())), preferred_element_type=F32)


def _xcast_dt_kernel(xp_ref, xs_ref, wd_ref, xb_ref, dt_ref, *, n_prompt_tiles, n_dt):
    i = pl.program_id(0)
    x = jnp.where(i < n_prompt_tiles, xp_ref[...], xs_ref[...]).astype(BF16)
    xb_ref[...] = x
    acc = _dot_nt(x, wd_ref[...].astype(BF16))
    lane = lax.broadcasted_iota(jnp.int32, acc.shape, 1)
    dt_ref[...] = jnp.where(lane < n_dt, acc, 0.0)


def _xcast_dt(xp, xs, wt, *, dt_col, n_dt, tm):
    m_p, k = xp.shape
    m_s = xs.shape[0]
    npt = m_p // tm
    n_tiles = npt + m_s // tm
    assert dt_col % LANES == 0 and n_dt <= LANES
    kern = functools.partial(_xcast_dt_kernel, n_prompt_tiles=npt, n_dt=n_dt)
    return pl.pallas_call(
        kern,
        grid=(n_tiles,),
        in_specs=[
            pl.BlockSpec((tm, k), lambda i: (jnp.minimum(i, npt - 1), 0)),
            pl.BlockSpec((tm, k), lambda i: (jnp.maximum(i - npt, 0), 0)),
            pl.BlockSpec((LANES, k), lambda i: (dt_col // LANES, 0)),
        ],
        out_specs=[pl.BlockSpec((tm, k), lambda i: (i, 0)),
                   pl.BlockSpec((tm, LANES), lambda i: (i, 0))],
        out_shape=[jax.ShapeDtypeStruct((m_p + m_s, k), BF16),
                   jax.ShapeDtypeStruct((m_p + m_s, LANES), F32)],
        compiler_params=_cparams(1),
        name="xcast_dt",
    )(xp, xs, wt)


CAST_ROWS = 256


def _cast_rows(src_ref, src_row0, dst_ref, dst_row0, n_rows):
    done = 0
    while done < n_rows:
        step = min(CAST_ROWS, n_rows - done)
        dst_ref[dst_row0 + done:dst_row0 + done + step, :] = (
            src_ref[src_row0 + done:src_row0 + done + step, :].astype(BF16))
        done += step


def _cast_weight_tile(w_ref, wsc):
    _cast_rows(w_ref, 0, wsc, 0, wsc.shape[0])


def _side_cast_specs(side, n_inner):
    in_specs, out_specs, out_shape = [], [], []
    for w, rows in side:
        n_blk = w.shape[0] // rows
        assert n_blk * rows == w.shape[0] and rows % 16 == 0
        idx = lambda j, i, n_blk=n_blk: (jnp.minimum(j * n_inner + i, n_blk - 1), 0)
        in_specs.append(pl.BlockSpec((rows, w.shape[1]), idx))
        out_specs.append(pl.BlockSpec((rows, w.shape[1]), idx))
        out_shape.append(jax.ShapeDtypeStruct(w.shape, BF16))
    return in_specs, out_specs, out_shape


def _mm_w32_kernel(*refs, shift, n_out, head_rows, silu, n_side):
    x_ref, wa_ref = refs[0], refs[1]
    pos = 2
    wb_ref = None
    if shift:
        wb_ref = refs[pos]
        pos += 1
    side_in = refs[pos:pos + n_side]
    pos += n_side
    o_refs = refs[pos:pos + n_out]
    pos += n_out
    hp_ref = hs_ref = None
    if head_rows is not None:
        hp_ref, hs_ref = refs[pos], refs[pos + 1]
        pos += 2
    side_out = refs[pos:pos + n_side]
    pos += n_side
    wsc = refs[pos]
    i = pl.program_id(1)

    @pl.when(i == 0)
    def _():
        tn = wsc.shape[0]
        _cast_rows(wa_ref, shift, wsc, 0, tn - shift)
        if shift:
            _cast_rows(wb_ref, 0, wsc, tn - shift, shift)

    for s_in, s_out in zip(side_in, side_out):
        s_out[...] = s_in[...].astype(BF16)

    acc = _dot_nt(x_ref[...], wsc[...])
    res = acc * jax.nn.sigmoid(acc) if silu else acc
    for o_ref in o_refs:
        o_ref[...] = res.astype(o_ref.dtype)

    if head_rows is not None:
        (tile_p, row0_p), (tile_s, row0_s) = head_rows

        @pl.when(i == tile_p)
        def _():
            hp_ref[...] = acc[row0_p:row0_p + hp_ref.shape[0], :]

        @pl.when(i == tile_s)
        def _():
            hs_ref[...] = acc[row0_s:row0_s + hs_ref.shape[0], :]


def _matmul_w32(x, wt, col_blk, n_col_tiles, out_dtypes, *, tm, tn, shift=0, head_out=None, silu=False,
                side=(), name):
    m, k = x.shape
    assert m % tm == 0 and tn % LANES == 0 and shift % 16 == 0 and shift < tn
    in_specs = [pl.BlockSpec((tm, k), lambda j, i: (i, 0)),
                pl.BlockSpec((tn, k), lambda j, i: (col_blk(j), 0))]
    args = [x, wt]
    if shift:
        assert tn % shift == 0
        per = tn // shift
        in_specs.append(pl.BlockSpec((shift, k), lambda j, i: ((col_blk(j) + 1) * per, 0)))
        args.append(wt)
    side_in, side_out, side_shape = _side_cast_specs(side, m // tm)
    assert all(w.shape[0] // rows <= n_col_tiles * (m // tm) for w, rows in side)
    in_specs += side_in
    args += [w for w, _ in side]
    n = n_col_tiles * tn
    out_specs = [pl.BlockSpec((tm, tn), lambda j, i: (i, j)) for _ in out_dtypes]
    out_shape = [jax.ShapeDtypeStruct((m, n), dt) for dt in out_dtypes]
    head_rows = None
    if head_out is not None:
        n_keep, m_s = head_out
        m_p = m - m_s
        head_rows = (divmod(m_p - n_keep, tm), divmod(m_p, tm))
        assert head_rows[0][1] + n_keep <= tm and head_rows[1][1] + m_s <= tm
        out_specs += [pl.BlockSpec((n_keep, tn), lambda j, i: (0, j)),
                      pl.BlockSpec((m_s, tn), lambda j, i: (0, j))]
        out_shape += [jax.ShapeDtypeStruct((n_keep, n), F32), jax.ShapeDtypeStruct((m_s, n), F32)]
    out_specs += side_out
    out_shape += side_shape
    kern = functools.partial(_mm_w32_kernel, shift=shift, n_out=len(out_dtypes), head_rows=head_rows,
                             silu=silu, n_side=len(side))
    return pl.pallas_call(
        kern,
        grid=(n_col_tiles, m // tm),
        in_specs=in_specs,
        out_specs=out_specs,
        out_shape=out_shape,
        scratch_shapes=[pltpu.VMEM((tn, k), BF16)],
        compiler_params=_cparams(2),
        name=name,
    )(*args)


def _softplus(x):
    return jnp.maximum(x, 0.0) + jnp.log1p(jnp.exp(-jnp.abs(x)))


def _ssd_chunk(u, zs_ref, dt_ref, cw_ref, cb_ref, dtb_ref, alog_ref, dsk_ref, nw_ref, e_ref, y_ref,
               xpad, act, ht, cst, ex, ysc, *, inner, n_groups, d_state):
    r0 = u * CHUNK
    rows = slice(r0, r0 + CHUNK)
    gw = inner // n_groups
    conv_dim = inner + 2 * n_groups * d_state

    for j in range(conv_dim // LANES):
        sl = slice(j * LANES, (j + 1) * LANES)
        blk = xpad[r0:r0 + SUBLANES + CHUNK, sl]
        a = cb_ref[:, sl]
        for tap in range(CONV_W):
            back = CONV_W - 1 - tap
            src = blk if back == 0 else pltpu.roll(blk, back, 0)
            a = a + src[SUBLANES:, :] * cw_ref[tap:tap + 1, sl]
        act[rows, sl] = a * jax.nn.sigmoid(a)

    dt = _softplus(dt_ref[rows, :] + dtb_ref[...])
    a_neg = -jnp.exp(alog_ref[...])
    da = dt * a_neg
    ri = lax.broadcasted_iota(jnp.int32, (CHUNK, CHUNK), 0)
    ci = lax.broadcasted_iota(jnp.int32, (CHUNK, CHUNK), 1)
    tri = (ri >= ci).astype(F32)
    cs = jnp.dot(tri, da, precision=lax.Precision.HIGHEST, preferred_element_type=F32)
    ecs = jnp.exp(cs)
    cs_last = cs[CHUNK - 1:CHUNK, :]
    dout = jnp.exp(cs_last - cs) * dt
    cd = jnp.exp(cs_last)
    cd_hi = cd.astype(BF16)
    cd_lo = (cd - cd_hi.astype(F32)).astype(BF16)
    lhs = jnp.concatenate(
        [dt.astype(BF16), ecs.astype(BF16), dout.astype(BF16),
         jnp.broadcast_to(cd_hi, (16, LANES)), jnp.broadcast_to(cd_lo, (16, LANES))], axis=0)
    ex[u] = _dot(lhs, e_ref[...])

    cst[u] = jnp.concatenate([cs, jnp.zeros((LANES - CHUNK, LANES), F32)], axis=0).T
    lane = lax.broadcasted_iota(jnp.int32, (CHUNK, LANES), 1)
    lo_half = lane < CHUNK
    n_pairs = inner // LANES
    ev = cst[u, pl.ds(0, n_pairs, stride=2), :]
    od = cst[u, pl.ds(1, n_pairs, stride=2), :]
    lane_p = lax.broadcasted_iota(jnp.int32, (n_pairs, LANES), 1)
    cst2 = jnp.where(lane_p < CHUNK, ev, pltpu.roll(od, CHUNK, 1))
    row = lax.broadcasted_iota(jnp.int32, (CHUNK, LANES), 0)
    causal2 = row >= jnp.where(lo_half, lane, lane - CHUNK)

    pairs_per_group = gw // LANES
    for g in range(n_groups):
        gs = slice(g * gw, (g + 1) * gw)
        bm_g = act[rows, inner + g * d_state: inner + (g + 1) * d_state].astype(BF16)
        cm_g = act[rows, inner + (n_groups + g) * d_state: inner + (n_groups + g + 1) * d_state].astype(BF16)
        cb2 = _dot_nt(cm_g, jnp.concatenate([bm_g, bm_g], axis=0))
        h_g = ht[:, gs]
        yoff = _dot(cm_g, h_g.astype(BF16))
        for p in range(pairs_per_group):
            k = g * pairs_per_group + p
            ls = slice(k * LANES, (k + 1) * LANES)
            csc = jnp.where(lo_half,
                            jnp.broadcast_to(cs[:, 2 * k:2 * k + 1], (CHUNK, LANES)),
                            jnp.broadcast_to(cs[:, 2 * k + 1:2 * k + 2], (CHUNK, LANES)))
            diff = csc - cst2[k:k + 1, :]
            dec = jnp.exp(jnp.where(causal2, diff, MASK_NEG))
            m2 = (cb2 * dec).astype(BF16)
            xs_p = act[rows, ls]
            xdt = xs_p * ex[u, 0:CHUNK, ls]
            x2 = jnp.concatenate([jnp.where(lo_half, xdt, 0.0), jnp.where(lo_half, 0.0, xdt)],
                                 axis=0).astype(BF16)
            yd = _dot(m2, x2)
            ysc[rows, ls] = (yd + yoff[:, p * LANES:(p + 1) * LANES] * ex[u, CHUNK:2 * CHUNK, ls]
                             + dsk_ref[:, ls] * xs_p)
        xw = (act[rows, gs] * ex[u, 2 * CHUNK:3 * CHUNK, gs]).astype(BF16)
        st = _dot_tn(bm_g, xw)
        cdx = ex[u, 3 * CHUNK:3 * CHUNK + 1, gs] + ex[u, 3 * CHUNK + 16:3 * CHUNK + 17, gs]
        ht[:, gs] = h_g * cdx + st

    for g in range(n_groups):
        gs = slice(g * gw, (g + 1) * gw)
        v = ysc[rows, gs] * zs_ref[rows, gs].astype(F32)
        ms = jnp.mean(v * v, axis=-1, keepdims=True)
        y_ref[rows, gs] = (v * lax.rsqrt(ms + RMS_EPS) * nw_ref[:, gs]).astype(y_ref.dtype)


def _ssd_kernel(zs_ref, xbc_ref, dt_ref, hist_ref, h0_ref, cw_ref, cb_ref, dtb_ref, alog_ref,
                dsk_ref, nw_ref, e_ref,
                y_ref, conv_ref, ho_ref,
                xpad, act, ht, cst, ex, ysc, *, cps, per_stream, inner, n_groups, d_state):
    c = pl.program_id(0)
    n_rows = cps * CHUNK
    conv_dim = inner + 2 * n_groups * d_state

    @pl.when(c == 0)
    def _():
        xpad[0:SUBLANES, :] = jnp.zeros((SUBLANES, conv_dim), F32)
        if not per_stream:
            ht[...] = jnp.zeros_like(ht)

    if per_stream:
        xpad[5:8, :] = hist_ref[0]
        ht[...] = h0_ref[0].T

    xpad[SUBLANES:SUBLANES + n_rows, :] = xbc_ref[...]
    conv_ref[0] = xbc_ref[n_rows - (CONV_W - 1):n_rows, :]

    for u in range(cps):
        _ssd_chunk(u, zs_ref, dt_ref, cw_ref, cb_ref, dtb_ref, alog_ref, dsk_ref, nw_ref, e_ref, y_ref,
                   xpad, act, ht, cst, ex, ysc, inner=inner, n_groups=n_groups, d_state=d_state)
    xpad[5:8, :] = xpad[5 + n_rows:8 + n_rows, :]

    if per_stream:
        ho_ref[0] = ht[...].T
    else:
        @pl.when(c == pl.num_programs(0) - 1)
        def _():
            ho_ref[0] = ht[...].T


def _ssd(zs, xbc, dtr, hist, h0, conv_w, conv_b, dtb, alog, dskx, normw, emat, *,
         row0, n_streams, chunks_per_stream, cps, per_stream, inner, n_groups, d_state, name):
    conv_dim = xbc.shape[1]
    hp = inner
    n_rows = cps * CHUNK
    assert chunks_per_stream % cps == 0 and row0 % n_rows == 0
    assert per_stream == (chunks_per_stream == cps) and (per_stream or n_streams == 1)
    steps_per_stream = chunks_per_stream // cps
    blk0 = row0 // n_rows
    stream = (lambda c: c) if per_stream else (lambda c: 0)

    kern = functools.partial(_ssd_kernel, cps=cps, per_stream=per_stream, inner=inner, n_groups=n_groups,
                             d_state=d_state)
    const = lambda c: (0, 0)
    row_map = lambda c: (blk0 + c, 0)
    return pl.pallas_call(
        kern,
        grid=(n_streams * steps_per_stream,),
        in_specs=[
            pl.BlockSpec((n_rows, inner), row_map),
            pl.BlockSpec((n_rows, conv_dim), row_map),
            pl.BlockSpec((n_rows, LANES), row_map),
            pl.BlockSpec((1, CONV_W - 1, conv_dim), lambda c: (stream(c), 0, 0)),
            pl.BlockSpec((1, hp, d_state), lambda c: (stream(c), 0, 0)),
            pl.BlockSpec((CONV_W, conv_dim), const),
            pl.BlockSpec((1, conv_dim), const),
            pl.BlockSpec((1, LANES), const),
            pl.BlockSpec((1, LANES), const),
            pl.BlockSpec((1, inner), const),
            pl.BlockSpec((1, inner), const),
            pl.BlockSpec((LANES, inner), const),
        ],
        out_specs=[
            pl.BlockSpec((n_rows, inner), lambda c: (c, 0)),
            pl.BlockSpec((1, CONV_W - 1, conv_dim), lambda c: (stream(c), 0, 0)),
            pl.BlockSpec((1, hp, d_state), lambda c: (stream(c), 0, 0)),
        ],
        out_shape=[
            jax.ShapeDtypeStruct((n_streams * chunks_per_stream * CHUNK, inner), BF16),
            jax.ShapeDtypeStruct((n_streams, CONV_W - 1, conv_dim), F32),
            jax.ShapeDtypeStruct((n_streams, hp, d_state), F32),
        ],
        scratch_shapes=[
            pltpu.VMEM((SUBLANES + n_rows, conv_dim), F32),
            pltpu.VMEM((n_rows, conv_dim), F32),
            pltpu.VMEM((d_state, inner), F32),
            pltpu.VMEM((cps, LANES, LANES), F32),
            pltpu.VMEM((cps, 3 * CHUNK + 32, inner), F32),
            pltpu.VMEM((n_rows, inner), F32),
        ],
        compiler_params=_cparams(1),
        name=name,
    )(zs, xbc, dtr, hist, h0, conv_w, conv_b, dtb, alog, dskx, normw, emat)


def _band_bias_tables(ext_ref, bias_scr, n_heads):
    col = lax.broadcasted_iota(jnp.int32, (CHUNK, ATT_WIN), 1)
    band = BAND_PAST + CHUNK
    for h in range(n_heads):
        e = jnp.broadcast_to(ext_ref[h:h + 1, :], (CHUNK, ATT_WIN))
        even = pltpu.roll(e, ATT_WIN - (CHUNK - 1), 1, stride=1, stride_axis=0)
        odd = pltpu.roll(e, 1, 1, stride=1, stride_axis=0)
        bias_scr[0, h] = jnp.where(col < band, even, MASK_NEG)
        bias_scr[1, h] = jnp.where(col >= CHUNK, odd, MASK_NEG)


def _attn_prompt_kernel(q_ref, kp_ref, kc_ref, vp_ref, vc_ref, ext_ref, o_ref,
                        bias_scr, s_scr, p_scr, *, n_heads):
    i = pl.program_id(0)
    d = ATT_HEAD_DIM
    scale = d ** -0.5
    half = ATT_TILE

    @pl.when(i == 0)
    def _():
        _band_bias_tables(ext_ref, bias_scr, n_heads)
        p_scr[...] = jnp.zeros_like(p_scr)

    past_mask = jnp.where(i == 0, MASK_NEG, 0.0).astype(F32)

    for h in range(n_heads):
        hs = slice(h * d, (h + 1) * d)
        q_h = q_ref[:, hs]
        s_scr[:, 0:half] = _dot_nt(q_h, kp_ref[:, hs]) * scale + past_mask
        s_scr[:, half:2 * half] = _dot_nt(q_h, kc_ref[:, hs]) * scale
        for t in range(ATT_TILE // CHUNK):
            rs = slice(t * CHUNK, (t + 1) * CHUNK)
            c0 = LANES * (t // 2)
            ws = slice(c0, c0 + ATT_WIN)
            s = s_scr[rs, ws] + bias_scr[t % 2, h]
            m = jnp.max(s, axis=-1, keepdims=True)
            p = jnp.exp(s - m)
            r = 1.0 / jnp.sum(p, axis=-1, keepdims=True)
            p_scr[rs, ws] = (p * r).astype(BF16)
        o_ref[:, hs] = (_dot(p_scr[:, 0:half], vp_ref[:, hs])
                        + _dot(p_scr[:, half:2 * half], vc_ref[:, hs])).astype(o_ref.dtype)


def _attn_prompt(q, kb, vb, ext, *, n_tiles, n_heads):
    width = n_heads * ATT_HEAD_DIM
    prev = lambda i: jnp.maximum(i - 1, 0)
    kern = functools.partial(_attn_prompt_kernel, n_heads=n_heads)
    return pl.pallas_call(
        kern,
        grid=(n_tiles,),
        in_specs=[
            pl.BlockSpec((ATT_TILE, width), lambda i: (i, 0)),
            pl.BlockSpec((ATT_TILE, width), lambda i: (prev(i), 0)),
            pl.BlockSpec((ATT_TILE, width), lambda i: (i, 0)),
            pl.BlockSpec((ATT_TILE, width), lambda i: (prev(i), 0)),
            pl.BlockSpec((ATT_TILE, width), lambda i: (i, 0)),
            pl.BlockSpec((n_heads, ATT_WIN), lambda i: (0, 0)),
        ],
        out_specs=pl.BlockSpec((ATT_TILE, width), lambda i: (i, 0)),
        out_shape=jax.ShapeDtypeStruct((n_tiles * ATT_TILE, width), BF16),
        scratch_shapes=[
            pltpu.VMEM((2, n_heads, CHUNK, ATT_WIN), F32),
            pltpu.VMEM((ATT_TILE, 2 * ATT_TILE), F32),
            pltpu.VMEM((ATT_TILE, 2 * ATT_TILE), BF16),
        ],
        compiler_params=_cparams(1),
        name="attn_prompt",
    )(q, kb, kb, vb, vb, ext)


def _sublane_transpose8(t):
    sub = lax.broadcasted_iota(jnp.int32, (SUBLANES, LANES), 0)
    for k in (4, 2, 1):
        keep = (sub // k) % 2 == 0
        new = [None] * SUBLANES
        for i in range(SUBLANES):
            if (i // k) % 2 == 0:
                j = i + k
                new[i] = jnp.where(keep, t[i], pltpu.roll(t[j], k, 0))
                new[j] = jnp.where(keep, pltpu.roll(t[i], SUBLANES - k, 0), t[j])
        t = new
    return t


def _split_heads(c_ref, u, dst, n_heads):
    halves = n_heads // SUBLANES

    def body(g, carry):
        p0 = pl.multiple_of(g * SUBLANES, SUBLANES)
        for hh in range(halves):
            tiles = [c_ref[u, pl.ds(pl.multiple_of((p0 + p) * n_heads + hh * SUBLANES, SUBLANES), SUBLANES), :]
                     for p in range(SUBLANES)]
            for a, tile in enumerate(_sublane_transpose8(tiles)):
                dst[hh * SUBLANES + a, pl.ds(p0, SUBLANES), :] = tile
        return carry

    lax.fori_loop(0, BAND_PAST // SUBLANES, body, 0, unroll=2)


def _attn_sample_kernel(q_ref, kn_ref, vn_ref, ck_ref, cv_ref, ext_ref, o_ref, bias_scr, kh_scr, vh_scr,
                        s_scr, p_scr, *, n_heads):
    d = ATT_HEAD_DIM
    scale = d ** -0.5

    @pl.when(pl.program_id(0) == 0)
    def _():
        _band_bias_tables(ext_ref, bias_scr, n_heads)

    for u in range(ck_ref.shape[0]):
        qr = slice(u * CHUNK, (u + 1) * CHUNK)
        _split_heads(ck_ref, u, kh_scr, n_heads)
        _split_heads(cv_ref, u, vh_scr, n_heads)
        band = BAND_PAST + CHUNK
        for h in range(n_heads):
            hs = slice(h * d, (h + 1) * d)
            q_h = q_ref[qr, hs]
            s_scr[h, :, 0:BAND_PAST] = (_dot_nt(q_h, kh_scr[h].astype(BF16)) * scale
                                        + bias_scr[0, h, :, 0:BAND_PAST])
            s_scr[h, :, BAND_PAST:band] = (_dot_nt(q_h, kn_ref[qr, hs]) * scale
                                           + bias_scr[0, h, :, BAND_PAST:band])
        for h in range(n_heads):
            s = s_scr[h, :, 0:band]
            p = jnp.exp(s - jnp.max(s, axis=-1, keepdims=True))
            r = 1.0 / jnp.sum(p, axis=-1, keepdims=True)
            p_scr[h, :, 0:band] = (p * r).astype(BF16)
        for h in range(n_heads):
            hs = slice(h * d, (h + 1) * d)
            o = (_dot(p_scr[h, :, 0:BAND_PAST], vh_scr[h].astype(BF16))
                 + _dot(p_scr[h, :, BAND_PAST:band], vn_ref[qr, hs]))
            o_ref[qr, hs] = o.astype(o_ref.dtype)


def _attn_sample(q, kb, vb, cache_k, cache_v, ext, *, row_blk0, n_sample, n_heads, spb):
    width = n_heads * ATT_HEAD_DIM
    assert n_sample % spb == 0 and row_blk0 % spb == 0
    kern = functools.partial(_attn_sample_kernel, n_heads=n_heads)
    cache_spec = pl.BlockSpec((spb, BAND_PAST * n_heads, ATT_HEAD_DIM), lambda s: (s, 0, 0))
    row_spec = pl.BlockSpec((spb * CHUNK, width), lambda s: (row_blk0 // spb + s, 0))
    return pl.pallas_call(
        kern,
        grid=(n_sample // spb,),
        in_specs=[row_spec, row_spec, row_spec, cache_spec, cache_spec,
                  pl.BlockSpec((n_heads, ATT_WIN), lambda s: (0, 0))],
        out_specs=pl.BlockSpec((spb * CHUNK, width), lambda s: (s, 0)),
        out_shape=jax.ShapeDtypeStruct((n_sample * CHUNK, width), BF16),
        scratch_shapes=[pltpu.VMEM((2, n_heads, CHUNK, ATT_WIN), F32),
                        pltpu.VMEM((n_heads, BAND_PAST, ATT_HEAD_DIM), F32),
                        pltpu.VMEM((n_heads, BAND_PAST, ATT_HEAD_DIM), F32),
                        pltpu.VMEM((n_heads, CHUNK, ATT_WIN), F32),
                        pltpu.VMEM((n_heads, CHUNK, ATT_WIN), BF16)],
        compiler_params=_cparams(1),
        name="attn_sample",
    )(q, kb, vb, cache_k, cache_v, ext)


def _mix_kernel(sp_ref, ss_ref, ap_ref, as_ref, w1_ref, w2_ref, g1_ref, g2_ref, o_ref, *, n_prompt_tiles):
    i = pl.program_id(1)
    is_p = i < n_prompt_tiles
    a1 = _dot(jnp.where(is_p, sp_ref[...], ss_ref[...]), w1_ref[...])
    a2 = _dot(jnp.where(is_p, ap_ref[...], as_ref[...]), w2_ref[...])
    g1 = jax.nn.sigmoid(g1_ref[...].astype(F32))
    g2 = jax.nn.sigmoid(g2_ref[...].astype(F32))
    o_ref[...] = (g1 * a1 + g2 * a2).astype(o_ref.dtype)


def _mix(ssd_p, ssd_s, att_p, att_s, w1, w2, gates_arr, *, gate_col0, tm, tn):
    k1 = ssd_p.shape[1]
    k2 = att_p.shape[1]
    m = att_p.shape[0] + att_s.shape[0]
    n = w1.shape[1]
    npt = att_p.shape[0] // tm
    assert ssd_p.shape[0] == att_p.shape[0] and ssd_s.shape[0] == att_s.shape[0]
    g1_blk = gate_col0 // tn
    g2_blk = (gate_col0 + n) // tn
    kern = functools.partial(_mix_kernel, n_prompt_tiles=npt)
    p_map = lambda j, i: (jnp.minimum(i, npt - 1), 0)
    s_map = lambda j, i: (jnp.maximum(i - npt, 0), 0)
    return pl.pallas_call(
        kern,
        grid=(n // tn, m // tm),
        in_specs=[
            pl.BlockSpec((tm, k1), p_map),
            pl.BlockSpec((tm, k1), s_map),
            pl.BlockSpec((tm, k2), p_map),
            pl.BlockSpec((tm, k2), s_map),
            pl.BlockSpec((k1, tn), lambda j, i: (0, j), pipeline_mode=pl.Buffered(1)),
            pl.BlockSpec((k2, tn), lambda j, i: (0, j), pipeline_mode=pl.Buffered(1)),
            pl.BlockSpec((tm, tn), lambda j, i: (i, g1_blk + j)),
            pl.BlockSpec((tm, tn), lambda j, i: (i, g2_blk + j)),
        ],
        out_specs=pl.BlockSpec((tm, tn), lambda j, i: (i, j)),
        out_shape=jax.ShapeDtypeStruct((m, n), BF16),
        compiler_params=_cparams(2),
        name="branch_mix",
    )(ssd_p, ssd_s, att_p, att_s, w1, w2, gates_arr, gates_arr)


def _layer_norm(t, g, b):
    mu = jnp.mean(t, axis=-1, keepdims=True)
    tc = t - mu
    var = jnp.mean(tc * tc, axis=-1, keepdims=True)
    return tc * lax.rsqrt(var + LN_EPS) * g + b


LN_SUB_ROWS = 256


def _proj_ln_kernel(*refs, alpha, n_k, n_prompt_tiles, split_resid, n_out):
    a_ref, w_ref = refs[0], refs[1]
    pos = 2
    r_refs = refs[pos:pos + (2 if split_resid else 1)]
    pos += len(r_refs)
    g_ref, b_ref = refs[pos], refs[pos + 1]
    pos += 2
    o_refs = refs[pos:pos + n_out]
    acc_ref = refs[pos + n_out] if n_k > 1 else None
    i = pl.program_id(0)
    kk = pl.program_id(1)
    tm = a_ref.shape[0]

    def resid_rows(rs):
        if split_resid:
            return jnp.where(i < n_prompt_tiles, r_refs[0][rs, :], r_refs[1][rs, :])
        return r_refs[0][rs, :]

    def finish(first):
        for r in range(tm // LN_SUB_ROWS):
            rs = slice(r * LN_SUB_ROWS, (r + 1) * LN_SUB_ROWS)
            base = alpha * resid_rows(rs) if first else acc_ref[rs, :]
            y = _layer_norm(base + _dot(a_ref[rs, :], w_ref[...]), g_ref[...], b_ref[...])
            for o_ref in o_refs:
                o_ref[rs, :] = y.astype(o_ref.dtype)

    if n_k == 1:
        finish(True)
        return

    @pl.when(kk == 0)
    def _():
        acc_ref[...] = alpha * resid_rows(slice(None)) + _dot(a_ref[...], w_ref[...])

    if n_k > 2:
        @pl.when(jnp.logical_and(kk > 0, kk < n_k - 1))
        def _():
            acc_ref[...] += _dot(a_ref[...], w_ref[...])

    @pl.when(kk == n_k - 1)
    def _():
        finish(False)


def _proj_ln(a, w, resid, gamma, beta, out_dtypes, *, alpha, tm, n_k, row0, n_rows, m_prompt=None, name):
    k = a.shape[1]
    n = w.shape[1]
    tk = k // n_k
    assert tk * n_k == k and n_rows % tm == 0 and row0 % tm == 0 and tm % LN_SUB_ROWS == 0
    blk0 = row0 // tm
    split_resid = isinstance(resid, tuple)
    row_map = lambda i, kk: (blk0 + i, 0)
    w_mode = dict(pipeline_mode=pl.Buffered(1)) if n_k == 1 else {}
    in_specs = [pl.BlockSpec((tm, tk), lambda i, kk: (blk0 + i, kk)),
                pl.BlockSpec((tk, n), lambda i, kk: (kk, 0), **w_mode)]
    args = [a, w]
    npt = 0
    if split_resid:
        assert row0 == 0 and m_prompt % tm == 0
        npt = m_prompt // tm
        in_specs += [pl.BlockSpec((tm, n), lambda i, kk: (jnp.minimum(i, npt - 1), 0)),
                     pl.BlockSpec((tm, n), lambda i, kk: (jnp.maximum(i - npt, 0), 0))]
        args += list(resid)
    else:
        in_specs.append(pl.BlockSpec((tm, n), row_map))
        args.append(resid)
    in_specs += [pl.BlockSpec((1, n), lambda i, kk: (0, 0))] * 2
    args += [gamma, beta]
    kern = functools.partial(_proj_ln_kernel, alpha=alpha, n_k=n_k, n_prompt_tiles=npt,
                             split_resid=split_resid, n_out=len(out_dtypes))
    return pl.pallas_call(
        kern,
        grid=(n_rows // tm, n_k),
        in_specs=in_specs,
        out_specs=[pl.BlockSpec((tm, n), lambda i, kk: (i, 0)) for _ in out_dtypes],
        out_shape=[jax.ShapeDtypeStruct((n_rows, n), dt) for dt in out_dtypes],
        scratch_shapes=[pltpu.VMEM((tm, n), F32)] if n_k > 1 else [],
        compiler_params=_cparams(2),
        name=name,
    )(*args)


def _glu_kernel(*refs, n_side):
    h_ref, wg_ref, wu_ref = refs[:3]
    side_in = refs[3:3 + n_side]
    o_ref = refs[3 + n_side]
    side_out = refs[4 + n_side:4 + 2 * n_side]
    wg_sc, wu_sc = refs[4 + 2 * n_side:]

    @pl.when(pl.program_id(1) == 0)
    def _():
        _cast_weight_tile(wg_ref, wg_sc)
        _cast_weight_tile(wu_ref, wu_sc)

    for s_in, s_out in zip(side_in, side_out):
        s_out[...] = s_in[...].astype(BF16)

    h = h_ref[...]
    g = _dot(h, wg_sc[...])
    u = _dot(h, wu_sc[...])
    o_ref[...] = (g * jax.nn.sigmoid(g) * u).astype(o_ref.dtype)


def _glu(h, w_gate_up, *, tm, tf, side=()):
    m, k = h.shape
    d_ff = w_gate_up.shape[1] // 2
    nf = d_ff // tf
    assert nf * tf == d_ff
    side_in, side_out, side_shape = _side_cast_specs(side, m // tm)
    assert all(w.shape[0] // rows <= nf * (m // tm) for w, rows in side)
    return pl.pallas_call(
        functools.partial(_glu_kernel, n_side=len(side)),
        grid=(nf, m // tm),
        in_specs=[
            pl.BlockSpec((tm, k), lambda j, i: (i, 0)),
            pl.BlockSpec((k, tf), lambda j, i: (0, j)),
            pl.BlockSpec((k, tf), lambda j, i: (0, nf + j)),
        ] + side_in,
        out_specs=[pl.BlockSpec((tm, tf), lambda j, i: (i, j))] + side_out,
        out_shape=[jax.ShapeDtypeStruct((m, d_ff), BF16)] + side_shape,
        scratch_shapes=[pltpu.VMEM((k, tf), BF16), pltpu.VMEM((k, tf), BF16)],
        compiler_params=_cparams(2),
        name="ffn_glu",
    )(h, w_gate_up, w_gate_up, *[w for w, _ in side])


def kernel(x_prompt, x_sample, cache_k, cache_v, state_conv, state_ssm, w_in, conv_w, conv_b, dt_bias,
           a_log, d_skip, ssd_norm_w, rel_bias, w_ssd_out, w_att_out, w_o, ln1_g, ln1_b, w_gate_up,
           w_down, ln2_g, ln2_b):
    depth = w_in.shape[0]
    assert depth == 1
    batch, seq, d_model = x_prompt.shape
    n_sample, dec_seq, _ = x_sample.shape
    assert batch == 1 and dec_seq == CHUNK and seq % ATT_TILE == 0
    n_ssd_heads = dt_bias.shape[1]
    inner = n_ssd_heads * SSD_HEADDIM
    conv_dim = conv_w.shape[2]
    n_groups = SSD_GROUPS
    d_state = SSD_STATE
    assert conv_dim == inner + 2 * n_groups * d_state
    n_heads = rel_bias.shape[1]
    width = n_heads * ATT_HEAD_DIM
    assert cache_k.shape[2] == BAND_PAST
    alpha = (2.0 * depth) ** 0.25
    keep = min(BAND_PAST, seq)

    m_p = batch * seq
    m_s = n_sample * dec_seq
    xp2 = x_prompt.reshape(m_p, d_model)
    xs2 = x_sample.reshape(m_s, d_model)

    wt = w_in[0].T
    tn = 1024
    o_xbc = inner
    o_dt = o_xbc + conv_dim
    o_q = o_dt + n_ssd_heads
    shift = o_q % tn
    assert o_dt % tn == 0 and inner % tn == 0 and conv_dim % tn == 0 and width % tn == 0
    q_blk = o_dt // tn
    xb, dtr = _xcast_dt(xp2, xs2, wt, dt_col=o_dt, n_dt=n_ssd_heads, tm=512)
    zs, w_att_b, w_o_b = _matmul_w32(xb, wt, lambda j: j, inner // tn, [BF16], tm=1536, tn=tn, silu=True,
                                     side=[(w_att_out[0], SIDE_CAST_ROWS), (w_o[0], SIDE_CAST_ROWS)],
                                     name="in_proj_z")
    (xbc,) = _matmul_w32(xb, wt, lambda j: inner // tn + j, conv_dim // tn, [F32], tm=1536, tn=tn,
                         name="in_proj_xbc")
    wpt = width // tn
    qg, w_ssd_b = _matmul_w32(xb, wt, lambda j: jnp.where(j < wpt, q_blk + j, q_blk + 2 * wpt + j),
                              wpt + 2 * d_model // tn, [BF16], tm=1536, tn=tn, shift=shift,
                              side=[(w_ssd_out[0], SIDE_CAST_ROWS)], name="in_proj_qg")
    kb, k_p, k_s = _matmul_w32(xb, wt, lambda j: q_blk + wpt + j, wpt, [BF16], tm=1024, tn=tn, shift=shift,
                               head_out=(keep, m_s), name="in_proj_k")
    vb, v_p, v_s = _matmul_w32(xb, wt, lambda j: q_blk + 2 * wpt + j, wpt, [BF16], tm=1024, tn=tn,
                               shift=shift, head_out=(keep, m_s), name="in_proj_v")

    pad_h = (0, LANES - n_ssd_heads)
    dtb = jnp.pad(dt_bias[0], pad_h).reshape(1, LANES)
    alog = jnp.pad(a_log[0], pad_h).reshape(1, LANES)
    dskx = jnp.repeat(d_skip[0], SSD_HEADDIM).reshape(1, inner)
    emat = (jnp.arange(LANES)[:, None] == (jnp.arange(inner)[None, :] // SSD_HEADDIM)).astype(BF16)
    ssd_args = (zs, xbc, dtr, state_conv[0], state_ssm[0].reshape(n_sample, inner, d_state),
                conv_w[0], conv_b[0].reshape(1, conv_dim), dtb, alog, dskx, ssd_norm_w[0].reshape(1, inner), emat)
    ssd_dims = dict(inner=inner, n_groups=n_groups, d_state=d_state)
    ssd_p, conv_p, h_p = _ssd(*ssd_args, row0=0, n_streams=batch, chunks_per_stream=seq // CHUNK,
                              cps=SSD_PROMPT_CPS, per_stream=False, name="ssd_scan_prompt", **ssd_dims)
    ssd_s, conv_s, h_s = _ssd(*ssd_args, row0=m_p, n_streams=n_sample, chunks_per_stream=1, cps=1,
                              per_stream=True, name="ssd_scan_sample", **ssd_dims)

    rb = rel_bias[0]
    band = BAND_PAST + CHUNK
    n_edge = band - REL_CLIP
    n_rev = ATT_WIN - n_edge
    rev = rb[:, ::-1][:, 1:1 + n_rev]
    ext = jnp.concatenate([jnp.broadcast_to(rb[:, 2 * REL_CLIP:], (n_heads, n_edge)), rev], axis=1)
    ext = jnp.pad(ext, ((0, 0), (0, ATT_WIN - ext.shape[1])))
    att_p = _attn_prompt(qg, kb, vb, ext, n_tiles=m_p // ATT_TILE, n_heads=n_heads)
    cache_rows = (n_sample, BAND_PAST * n_heads, ATT_HEAD_DIM)
    att_s = _attn_sample(qg, kb, vb, cache_k[0].reshape(cache_rows), cache_v[0].reshape(cache_rows), ext,
                         row_blk0=m_p // CHUNK, n_sample=n_sample, n_heads=n_heads, spb=ATT_SAMPLE_SPB)

    mixed = _mix(ssd_p, ssd_s, att_p, att_s, w_ssd_b, w_att_b, qg, gate_col0=width, tm=512, tn=1024)
    g1 = ln1_g[0].reshape(1, d_model)
    b1 = ln1_b[0].reshape(1, d_model)
    g2 = ln2_g[0].reshape(1, d_model)
    b2 = ln2_b[0].reshape(1, d_model)
    h_f, h_b = _proj_ln(mixed, w_o_b, (xp2, xs2), g1, b1, [F32, BF16], alpha=alpha, tm=512,
                        n_k=1, row0=0, n_rows=m_p + m_s, m_prompt=m_p, name="wo_ln")
    act, w_dn = _glu(h_b, w_gate_up[0], tm=1536, tf=512, side=[(w_down[0], DOWN_CAST_ROWS)])
    (y_p,) = _proj_ln(act, w_dn, h_f, g2, b2, [F32], alpha=alpha, tm=512, n_k=2, row0=0, n_rows=m_p,
                      name="down_ln_prompt")
    (y_s,) = _proj_ln(act, w_dn, h_f, g2, b2, [F32], alpha=alpha, tm=512, n_k=2, row0=m_p, n_rows=m_s,
                      name="down_ln_sample")

    return (y_p.reshape(batch, seq, d_model),
            y_s.reshape(n_sample, dec_seq, d_model),
            conv_p.reshape(1, batch, CONV_W - 1, conv_dim),
            h_p.reshape(1, batch, n_ssd_heads, SSD_HEADDIM, d_state),
            k_p.reshape(1, batch, keep, n_heads, ATT_HEAD_DIM),
            v_p.reshape(1, batch, keep, n_heads, ATT_HEAD_DIM),
            conv_s.reshape(1, n_sample, CONV_W - 1, conv_dim),
            h_s.reshape(1, n_sample, n_ssd_heads, SSD_HEADDIM, d_state),
            k_s.reshape(1, n_sample, dec_seq, n_heads, ATT_HEAD_DIM),
            v_s.reshape(1, n_sample, dec_seq, n_heads, ATT_HEAD_DIM))
```

```python
import functools

import jax
import jax.numpy as jnp
from jax import lax
from jax.experimental import pallas as pl
from jax.experimental.pallas import tpu as pltpu

F32 = jnp.float32
BF16 = jnp.bfloat16

CHUNK = 64
SSD_HEADDIM = 64
SSD_GROUPS = 8
SSD_STATE = 128
CONV_W = 4
ATT_HEAD_DIM = 128
BAND_CHUNKS = 8
BAND_PAST = BAND_CHUNKS * CHUNK
REL_CLIP = 128
LN_EPS = 1e-5
RMS_EPS = 1e-5
MASK_NEG = -1e30

LANES = 128
SUBLANES = 8
VMEM_LIMIT_BYTES = 56 * 1024 * 1024

SSD_PROMPT_CPS = 4
SIDE_CAST_ROWS = 128
DOWN_CAST_ROWS = 176
ATT_TILE = 512
ATT_SAMPLE_SPB = 2
ATT_WIN = 5 * LANES


def _cparams(n_axes):
    return pltpu.CompilerParams(
        dimension_semantics=("arbitrary",) * n_axes,
        vmem_limit_bytes=VMEM_LIMIT_BYTES,
    )


def _sigmoid(x):
    return 0.5 * jnp.tanh(0.5 * x) + 0.5


def _silu(x):
    h = 0.5 * x
    return h + h * jnp.tanh(h)


def _dot(a, b):
    return jnp.dot(a, b, preferred_element_type=F32)


def _dot_nt(a, b):
    return lax.dot_general(a, b, (((1,), (1,)), ((), ())), preferred_element_type=F32)


def _dot_tn(a, b):
    return lax.dot_general(a, b, (((0,), (0,)), ((), ())), preferred_element_type=F32)


def _xcast_dt_kernel(xp_ref, xs_ref, wd_ref, xb_ref, dt_ref, *, n_prompt_tiles, n_dt):
    i = pl.program_id(0)
    x = jnp.where(i < n_prompt_tiles, xp_ref[...], xs_ref[...]).astype(BF16)
    xb_ref[...] = x
    acc = _dot_nt(x, wd_ref[...].astype(BF16))
    lane = lax.broadcasted_iota(jnp.int32, acc.shape, 1)
    dt_ref[...] = jnp.where(lane < n_dt, acc, 0.0)


def _xcast_dt(xp, xs, wt, *, dt_col, n_dt, tm):
    m_p, k = xp.shape
    m_s = xs.shape[0]
    npt = m_p // tm
    n_tiles = npt + m_s // tm
    assert dt_col % LANES == 0 and n_dt <= LANES
    kern = functools.partial(_xcast_dt_kernel, n_prompt_tiles=npt, n_dt=n_dt)
    return pl.pallas_call(
        kern,
        grid=(n_tiles,),
        in_specs=[
            pl.BlockSpec((tm, k), lambda i: (jnp.minimum(i, npt - 1), 0)),
            pl.BlockSpec((tm, k), lambda i: (jnp.maximum(i - npt, 0), 0)),
            pl.BlockSpec((LANES, k), lambda i: (dt_col // LANES, 0)),
        ],
        out_specs=[pl.BlockSpec((tm, k), lambda i: (i, 0)),
                   pl.BlockSpec((tm, LANES), lambda i: (i, 0))],
        out_shape=[jax.ShapeDtypeStruct((m_p + m_s, k), BF16),
                   jax.ShapeDtypeStruct((m_p + m_s, LANES), F32)],
        compiler_params=_cparams(1),
        name="xcast_dt",
    )(xp, xs, wt)


CAST_ROWS = 256


def _cast_rows(src_ref, src_row0, dst_ref, dst_row0, n_rows):
    done = 0
    while done < n_rows:
        step = min(CAST_ROWS, n_rows - done)
        dst_ref[dst_row0 + done:dst_row0 + done + step, :] = (
            src_ref[src_row0 + done:src_row0 + done + step, :].astype(BF16))
        done += step


def _cast_weight_tile(w_ref, wsc):
    _cast_rows(w_ref, 0, wsc, 0, wsc.shape[0])


def _side_cast_specs(side, n_inner):
    in_specs, out_specs, out_shape = [], [], []
    for w, rows in side:
        n_blk = w.shape[0] // rows
        assert n_blk * rows == w.shape[0] and rows % 16 == 0
        idx = lambda j, i, n_blk=n_blk: (jnp.minimum(j * n_inner + i, n_blk - 1), 0)
        in_specs.append(pl.BlockSpec((rows, w.shape[1]), idx))
        out_specs.append(pl.BlockSpec((rows, w.shape[1]), idx))
        out_shape.append(jax.ShapeDtypeStruct(w.shape, BF16))
    return in_specs, out_specs, out_shape


def _mm_w32_kernel(*refs, shift, n_out, head_rows, silu, n_side):
    x_ref, wa_ref = refs[0], refs[1]
    pos = 2
    wb_ref = None
    if shift:
        wb_ref = refs[pos]
        pos += 1
    side_in = refs[pos:pos + n_side]
    pos += n_side
    o_refs = refs[pos:pos + n_out]
    pos += n_out
    hp_ref = hs_ref = None
    if head_rows is not None:
        hp_ref, hs_ref = refs[pos], refs[pos + 1]
        pos += 2
    side_out = refs[pos:pos + n_side]
    pos += n_side
    wsc = refs[pos]
    i = pl.program_id(1)

    @pl.when(i == 0)
    def _():
        tn = wsc.shape[0]
        _cast_rows(wa_ref, shift, wsc, 0, tn - shift)
        if shift:
            _cast_rows(wb_ref, 0, wsc, tn - shift, shift)

    for s_in, s_out in zip(side_in, side_out):
        s_out[...] = s_in[...].astype(BF16)

    acc = _dot_nt(x_ref[...], wsc[...])
    res = _silu(acc) if silu else acc
    for o_ref in o_refs:
        o_ref[...] = res.astype(o_ref.dtype)

    if head_rows is not None:
        (tile_p, row0_p), (tile_s, row0_s) = head_rows

        @pl.when(i == tile_p)
        def _():
            hp_ref[...] = acc[row0_p:row0_p + hp_ref.shape[0], :]

        @pl.when(i == tile_s)
        def _():
            hs_ref[...] = acc[row0_s:row0_s + hs_ref.shape[0], :]


def _matmul_w32(x, wt, col_blk, n_col_tiles, out_dtypes, *, tm, tn, shift=0, head_out=None, silu=False,
                side=(), name):
    m, k = x.shape
    assert m % tm == 0 and tn % LANES == 0 and shift % 16 == 0 and shift < tn
    in_specs = [pl.BlockSpec((tm, k), lambda j, i: (i, 0)),
                pl.BlockSpec((tn, k), lambda j, i: (col_blk(j), 0))]
    args = [x, wt]
    if shift:
        assert tn % shift == 0
        per = tn // shift
        in_specs.append(pl.BlockSpec((shift, k), lambda j, i: ((col_blk(j) + 1) * per, 0)))
        args.append(wt)
    side_in, side_out, side_shape = _side_cast_specs(side, m // tm)
    assert all(w.shape[0] // rows <= n_col_tiles * (m // tm) for w, rows in side)
    in_specs += side_in
    args += [w for w, _ in side]
    n = n_col_tiles * tn
    out_specs = [pl.BlockSpec((tm, tn), lambda j, i: (i, j)) for _ in out_dtypes]
    out_shape = [jax.ShapeDtypeStruct((m, n), dt) for dt in out_dtypes]
    head_rows = None
    if head_out is not None:
        n_keep, m_s = head_out
        m_p = m - m_s
        head_rows = (divmod(m_p - n_keep, tm), divmod(m_p, tm))
        assert head_rows[0][1] + n_keep <= tm and head_rows[1][1] + m_s <= tm
        out_specs += [pl.BlockSpec((n_keep, tn), lambda j, i: (0, j)),
                      pl.BlockSpec((m_s, tn), lambda j, i: (0, j))]
        out_shape += [jax.ShapeDtypeStruct((n_keep, n), F32), jax.ShapeDtypeStruct((m_s, n), F32)]
    out_specs += side_out
    out_shape += side_shape
    kern = functools.partial(_mm_w32_kernel, shift=shift, n_out=len(out_dtypes), head_rows=head_rows,
                             silu=silu, n_side=len(side))
    return pl.pallas_call(
        kern,
        grid=(n_col_tiles, m // tm),
        in_specs=in_specs,
        out_specs=out_specs,
        out_shape=out_shape,
        scratch_shapes=[pltpu.VMEM((tn, k), BF16)],
        compiler_params=_cparams(2),
        name=name,
    )(*args)


def _softplus(x):
    return jnp.maximum(x, 0.0) + jnp.log1p(jnp.exp(-jnp.abs(x)))


def _ssd_chunk(u, zs_ref, dt_ref, cw_ref, cb_ref, dtb_ref, alog_ref, dsk_ref, nw_ref, e_ref, y_ref,
               xpad, act, ht, cst, ex, ysc, *, inner, n_groups, d_state):
    r0 = u * CHUNK
    rows = slice(r0, r0 + CHUNK)
    gw = inner // n_groups
    conv_dim = inner + 2 * n_groups * d_state

    for j in range(conv_dim // LANES):
        sl = slice(j * LANES, (j + 1) * LANES)
        blk = xpad[r0:r0 + SUBLANES + CHUNK, sl]
        prev = pltpu.roll(blk, 1, 0)
        near = blk * cw_ref[3:4, sl] + prev * cw_ref[2:3, sl]
        far = pltpu.roll(blk * cw_ref[1:2, sl] + prev * cw_ref[0:1, sl], 2, 0)
        a = cb_ref[:, sl] + (near + far)[SUBLANES:, :]
        act[rows, sl] = _silu(a)

    dt = _softplus(dt_ref[rows, :] + dtb_ref[...])
    a_neg = -jnp.exp(alog_ref[...])
    da = dt * a_neg
    ri = lax.broadcasted_iota(jnp.int32, (CHUNK, CHUNK), 0)
    ci = lax.broadcasted_iota(jnp.int32, (CHUNK, CHUNK), 1)
    tri = (ri >= ci).astype(F32)
    cs = jnp.dot(tri, da, precision=lax.Precision.HIGHEST, preferred_element_type=F32)
    ecs = jnp.exp(cs)
    cs_last = cs[CHUNK - 1:CHUNK, :]
    dout = jnp.exp(cs_last - cs) * dt
    cd = jnp.exp(cs_last)
    cd_hi = cd.astype(BF16)
    cd_lo = (cd - cd_hi.astype(F32)).astype(BF16)
    lhs = jnp.concatenate(
        [dt.astype(BF16), ecs.astype(BF16), dout.astype(BF16),
         jnp.broadcast_to(cd_hi, (16, LANES)), jnp.broadcast_to(cd_lo, (16, LANES))], axis=0)
    ex[u] = _dot(lhs, e_ref[...])

    cst[u] = jnp.concatenate([cs, jnp.zeros((LANES - CHUNK, LANES), F32)], axis=0).T
    lane = lax.broadcasted_iota(jnp.int32, (CHUNK, LANES), 1)
    lo_half = lane < CHUNK
    n_pairs = inner // LANES
    ev = cst[u, pl.ds(0, n_pairs, stride=2), :]
    od = cst[u, pl.ds(1, n_pairs, stride=2), :]
    lane_p = lax.broadcasted_iota(jnp.int32, (n_pairs, LANES), 1)
    cst2 = jnp.where(lane_p < CHUNK, ev, pltpu.roll(od, CHUNK, 1))
    row = lax.broadcasted_iota(jnp.int32, (CHUNK, LANES), 0)
    causal2 = row >= jnp.where(lo_half, lane, lane - CHUNK)

    pairs_per_group = gw // LANES
    for g in range(n_groups):
        gs = slice(g * gw, (g + 1) * gw)
        bm_g = act[rows, inner + g * d_state: inner + (g + 1) * d_state].astype(BF16)
        cm_g = act[rows, inner + (n_groups + g) * d_state: inner + (n_groups + g + 1) * d_state].astype(BF16)
        cb2 = _dot_nt(cm_g, jnp.concatenate([bm_g, bm_g], axis=0))
        h_g = ht[:, gs]
        yoff = _dot(cm_g, h_g.astype(BF16))
        for p in range(pairs_per_group):
            k = g * pairs_per_group + p
            ls = slice(k * LANES, (k + 1) * LANES)
            csc = jnp.where(lo_half,
                            jnp.broadcast_to(cs[:, 2 * k:2 * k + 1], (CHUNK, LANES)),
                            jnp.broadcast_to(cs[:, 2 * k + 1:2 * k + 2], (CHUNK, LANES)))
            diff = csc - cst2[k:k + 1, :]
            dec = jnp.exp(jnp.where(causal2, diff, MASK_NEG))
            m2 = (cb2 * dec).astype(BF16)
            xs_p = act[rows, ls]
            xdt = xs_p * ex[u, 0:CHUNK, ls]
            x2 = jnp.concatenate([jnp.where(lo_half, xdt, 0.0), jnp.where(lo_half, 0.0, xdt)],
                                 axis=0).astype(BF16)
            yd = _dot(m2, x2)
            ysc[rows, ls] = (yd + yoff[:, p * LANES:(p + 1) * LANES] * ex[u, CHUNK:2 * CHUNK, ls]
                             + dsk_ref[:, ls] * xs_p)
        xw = (act[rows, gs] * ex[u, 2 * CHUNK:3 * CHUNK, gs]).astype(BF16)
        st = _dot_tn(bm_g, xw)
        cdx = ex[u, 3 * CHUNK:3 * CHUNK + 1, gs] + ex[u, 3 * CHUNK + 16:3 * CHUNK + 17, gs]
        ht[:, gs] = h_g * cdx + st

    for g in range(n_groups):
        gs = slice(g * gw, (g + 1) * gw)
        v = ysc[rows, gs] * zs_ref[rows, gs].astype(F32)
        ms = jnp.mean(v * v, axis=-1, keepdims=True)
        y_ref[rows, gs] = (v * lax.rsqrt(ms + RMS_EPS) * nw_ref[:, gs]).astype(y_ref.dtype)


def _ssd_kernel(zs_ref, xbc_ref, dt_ref, hist_ref, h0_ref, cw_ref, cb_ref, dtb_ref, alog_ref,
                dsk_ref, nw_ref, e_ref,
                y_ref, conv_ref, ho_ref,
                xpad, act, ht, cst, ex, ysc, *, cps, per_stream, inner, n_groups, d_state):
    c = pl.program_id(0)
    n_rows = cps * CHUNK
    conv_dim = inner + 2 * n_groups * d_state

    @pl.when(c == 0)
    def _():
        xpad[0:SUBLANES, :] = jnp.zeros((SUBLANES, conv_dim), F32)
        if not per_stream:
            ht[...] = jnp.zeros_like(ht)

    if per_stream:
        xpad[5:8, :] = hist_ref[0]
        ht[...] = h0_ref[0].T

    xpad[SUBLANES:SUBLANES + n_rows, :] = xbc_ref[...]
    conv_ref[0] = xbc_ref[n_rows - (CONV_W - 1):n_rows, :]

    for u in range(cps):
        _ssd_chunk(u, zs_ref, dt_ref, cw_ref, cb_ref, dtb_ref, alog_ref, dsk_ref, nw_ref, e_ref, y_ref,
                   xpad, act, ht, cst, ex, ysc, inner=inner, n_groups=n_groups, d_state=d_state)
    xpad[5:8, :] = xpad[5 + n_rows:8 + n_rows, :]

    if per_stream:
        ho_ref[0] = ht[...].T
    else:
        @pl.when(c == pl.num_programs(0) - 1)
        def _():
            ho_ref[0] = ht[...].T


def _ssd(zs, xbc, dtr, hist, h0, conv_w, conv_b, dtb, alog, dskx, normw, emat, *,
         row0, n_streams, chunks_per_stream, cps, per_stream, inner, n_groups, d_state, name):
    conv_dim = xbc.shape[1]
    hp = inner
    n_rows = cps * CHUNK
    assert chunks_per_stream % cps == 0 and row0 % n_rows == 0
    assert per_stream == (chunks_per_stream == cps) and (per_stream or n_streams == 1)
    steps_per_stream = chunks_per_stream // cps
    blk0 = row0 // n_rows
    stream = (lambda c: c) if per_stream else (lambda c: 0)

    kern = functools.partial(_ssd_kernel, cps=cps, per_stream=per_stream, inner=inner, n_groups=n_groups,
                             d_state=d_state)
    const = lambda c: (0, 0)
    row_map = lambda c: (blk0 + c, 0)
    return pl.pallas_call(
        kern,
        grid=(n_streams * steps_per_stream,),
        in_specs=[
            pl.BlockSpec((n_rows, inner), row_map),
            pl.BlockSpec((n_rows, conv_dim), row_map),
            pl.BlockSpec((n_rows, LANES), row_map),
            pl.BlockSpec((1, CONV_W - 1, conv_dim), lambda c: (stream(c), 0, 0)),
            pl.BlockSpec((1, hp, d_state), lambda c: (stream(c), 0, 0)),
            pl.BlockSpec((CONV_W, conv_dim), const),
            pl.BlockSpec((1, conv_dim), const),
            pl.BlockSpec((1, LANES), const),
            pl.BlockSpec((1, LANES), const),
            pl.BlockSpec((1, inner), const),
            pl.BlockSpec((1, inner), const),
            pl.BlockSpec((LANES, inner), const),
        ],
        out_specs=[
            pl.BlockSpec((n_rows, inner), lambda c: (c, 0)),
            pl.BlockSpec((1, CONV_W - 1, conv_dim), lambda c: (stream(c), 0, 0)),
            pl.BlockSpec((1, hp, d_state), lambda c: (stream(c), 0, 0)),
        ],
        out_shape=[
            jax.ShapeDtypeStruct((n_streams * chunks_per_stream * CHUNK, inner), BF16),
            jax.ShapeDtypeStruct((n_streams, CONV_W - 1, conv_dim), F32),
            jax.ShapeDtypeStruct((n_streams, hp, d_state), F32),
        ],
        scratch_shapes=[
            pltpu.VMEM((SUBLANES + n_rows, conv_dim), F32),
            pltpu.VMEM((n_rows, conv_dim), F32),
            pltpu.VMEM((d_state, inner), F32),
            pltpu.VMEM((cps, LANES, LANES), F32),
            pltpu.VMEM((cps, 3 * CHUNK + 32, inner), F32),
            pltpu.VMEM((n_rows, inner), F32),
        ],
        compiler_params=_cparams(1),
        name=name,
    )(zs, xbc, dtr, hist, h0, conv_w, conv_b, dtb, alog, dskx, normw, emat)


def _band_bias_tables(ext_ref, bias_scr, n_heads):
    col = lax.broadcasted_iota(jnp.int32, (CHUNK, ATT_WIN), 1)
    band = BAND_PAST + CHUNK
    for h in range(n_heads):
        e = jnp.broadcast_to(ext_ref[h:h + 1, :], (CHUNK, ATT_WIN))
        even = pltpu.roll(e, ATT_WIN - (CHUNK - 1), 1, stride=1, stride_axis=0)
        odd = pltpu.roll(e, 1, 1, stride=1, stride_axis=0)
        bias_scr[0, h] = jnp.where(col < band, even, MASK_NEG)
        bias_scr[1, h] = jnp.where(col >= CHUNK, odd, MASK_NEG)


ATT_BLOCKS = BAND_PAST // ATT_TILE + 1


def _attn_prompt_kernel(*refs, n_heads):
    q_ref = refs[0]
    k_refs = refs[1:1 + ATT_BLOCKS]
    v_refs = refs[1 + ATT_BLOCKS:1 + 2 * ATT_BLOCKS]
    ext_ref, o_ref, bias_scr, s_scr, p_scr = refs[1 + 2 * ATT_BLOCKS:]
    i = pl.program_id(0)
    d = ATT_HEAD_DIM
    scale = d ** -0.5
    blk = ATT_TILE

    @pl.when(i == 0)
    def _():
        _band_bias_tables(ext_ref, bias_scr, n_heads)
        p_scr[...] = jnp.zeros_like(p_scr)

    past_mask = [jnp.where(i < ATT_BLOCKS - 1 - b, MASK_NEG, 0.0).astype(F32) for b in range(ATT_BLOCKS - 1)]

    for h in range(n_heads):
        hs = slice(h * d, (h + 1) * d)
        q_h = q_ref[:, hs]
        for b in range(ATT_BLOCKS):
            s = _dot_nt(q_h, k_refs[b][:, hs]) * scale
            s_scr[:, b * blk:(b + 1) * blk] = s + past_mask[b] if b < ATT_BLOCKS - 1 else s
        for t in range(ATT_TILE // CHUNK):
            rs = slice(t * CHUNK, (t + 1) * CHUNK)
            c0 = LANES * (t // 2)
            ws = slice(c0, c0 + ATT_WIN)
            s = s_scr[rs, ws] + bias_scr[t % 2, h]
            m = jnp.max(s, axis=-1, keepdims=True)
            p = jnp.exp(s - m)
            r = 1.0 / jnp.sum(p, axis=-1, keepdims=True)
            p_scr[rs, ws] = (p * r).astype(BF16)
        o = _dot(p_scr[:, 0:blk], v_refs[0][:, hs])
        for b in range(1, ATT_BLOCKS):
            o = o + _dot(p_scr[:, b * blk:(b + 1) * blk], v_refs[b][:, hs])
        o_ref[:, hs] = o.astype(o_ref.dtype)


def _attn_prompt(q, kb, vb, ext, *, n_tiles, n_heads):
    width = n_heads * ATT_HEAD_DIM
    assert BAND_PAST % ATT_TILE == 0 and ATT_TILE % (2 * CHUNK) == 0
    kern = functools.partial(_attn_prompt_kernel, n_heads=n_heads)
    window = [pl.BlockSpec((ATT_TILE, width), lambda i, back=back: (jnp.maximum(i - back, 0), 0))
              for back in range(ATT_BLOCKS - 1, -1, -1)]
    return pl.pallas_call(
        kern,
        grid=(n_tiles,),
        in_specs=[pl.BlockSpec((ATT_TILE, width), lambda i: (i, 0))] + window + window
                 + [pl.BlockSpec((n_heads, ATT_WIN), lambda i: (0, 0))],
        out_specs=pl.BlockSpec((ATT_TILE, width), lambda i: (i, 0)),
        out_shape=jax.ShapeDtypeStruct((n_tiles * ATT_TILE, width), BF16),
        scratch_shapes=[
            pltpu.VMEM((2, n_heads, CHUNK, ATT_WIN), F32),
            pltpu.VMEM((ATT_TILE, ATT_BLOCKS * ATT_TILE), F32),
            pltpu.VMEM((ATT_TILE, ATT_BLOCKS * ATT_TILE), BF16),
        ],
        compiler_params=_cparams(1),
        name="attn_prompt",
    )(q, *([kb] * ATT_BLOCKS), *([vb] * ATT_BLOCKS), ext)


def _sublane_transpose8(t):
    sub = lax.broadcasted_iota(jnp.int32, (SUBLANES, LANES), 0)
    for k in (4, 2, 1):
        keep = (sub // k) % 2 == 0
        new = [None] * SUBLANES
        for i in range(SUBLANES):
            if (i // k) % 2 == 0:
                j = i + k
                new[i] = jnp.where(keep, t[i], pltpu.roll(t[j], k, 0))
                new[j] = jnp.where(keep, pltpu.roll(t[i], SUBLANES - k, 0), t[j])
        t = new
    return t


def _split_heads(c_ref, u, dst, n_heads):
    halves = n_heads // SUBLANES

    def body(g, carry):
        p0 = pl.multiple_of(g * SUBLANES, SUBLANES)
        for hh in range(halves):
            tiles = [c_ref[u, pl.ds(pl.multiple_of((p0 + p) * n_heads + hh * SUBLANES, SUBLANES), SUBLANES), :]
                     for p in range(SUBLANES)]
            for a, tile in enumerate(_sublane_transpose8(tiles)):
                dst[hh * SUBLANES + a, pl.ds(p0, SUBLANES), :] = tile
        return carry

    lax.fori_loop(0, BAND_PAST // SUBLANES, body, 0, unroll=2)


def _attn_sample_kernel(q_ref, kn_ref, vn_ref, ck_ref, cv_ref, ext_ref, o_ref, bias_scr, kh_scr, vh_scr,
                        s_scr, p_scr, *, n_heads):
    d = ATT_HEAD_DIM
    scale = d ** -0.5

    @pl.when(pl.program_id(0) == 0)
    def _():
        _band_bias_tables(ext_ref, bias_scr, n_heads)

    for u in range(ck_ref.shape[0]):
        qr = slice(u * CHUNK, (u + 1) * CHUNK)
        _split_heads(ck_ref, u, kh_scr, n_heads)
        _split_heads(cv_ref, u, vh_scr, n_heads)
        band = BAND_PAST + CHUNK
        for h in range(n_heads):
            hs = slice(h * d, (h + 1) * d)
            q_h = q_ref[qr, hs]
            s_scr[h, :, 0:BAND_PAST] = (_dot_nt(q_h, kh_scr[h].astype(BF16)) * scale
                                        + bias_scr[0, h, :, 0:BAND_PAST])
            s_scr[h, :, BAND_PAST:band] = (_dot_nt(q_h, kn_ref[qr, hs]) * scale
                                           + bias_scr[0, h, :, BAND_PAST:band])
        for h in range(n_heads):
            s = s_scr[h, :, 0:band]
            p = jnp.exp(s - jnp.max(s, axis=-1, keepdims=True))
            r = 1.0 / jnp.sum(p, axis=-1, keepdims=True)
            p_scr[h, :, 0:band] = (p * r).astype(BF16)
        for h in range(n_heads):
            hs = slice(h * d, (h + 1) * d)
            o = (_dot(p_scr[h, :, 0:BAND_PAST], vh_scr[h].astype(BF16))
                 + _dot(p_scr[h, :, BAND_PAST:band], vn_ref[qr, hs]))
            o_ref[qr, hs] = o.astype(o_ref.dtype)


def _attn_sample(q, kb, vb, cache_k, cache_v, ext, *, row_blk0, n_sample, n_heads, spb):
    width = n_heads * ATT_HEAD_DIM
    assert n_sample % spb == 0 and row_blk0 % spb == 0
    kern = functools.partial(_attn_sample_kernel, n_heads=n_heads)
    cache_spec = pl.BlockSpec((spb, BAND_PAST * n_heads, ATT_HEAD_DIM), lambda s: (s, 0, 0))
    row_spec = pl.BlockSpec((spb * CHUNK, width), lambda s: (row_blk0 // spb + s, 0))
    return pl.pallas_call(
        kern,
        grid=(n_sample // spb,),
        in_specs=[row_spec, row_spec, row_spec, cache_spec, cache_spec,
                  pl.BlockSpec((n_heads, ATT_WIN), lambda s: (0, 0))],
        out_specs=pl.BlockSpec((spb * CHUNK, width), lambda s: (s, 0)),
        out_shape=jax.ShapeDtypeStruct((n_sample * CHUNK, width), BF16),
        scratch_shapes=[pltpu.VMEM((2, n_heads, CHUNK, ATT_WIN), F32),
                        pltpu.VMEM((n_heads, BAND_PAST, ATT_HEAD_DIM), F32),
                        pltpu.VMEM((n_heads, BAND_PAST, ATT_HEAD_DIM), F32),
                        pltpu.VMEM((n_heads, CHUNK, ATT_WIN), F32),
                        pltpu.VMEM((n_heads, CHUNK, ATT_WIN), BF16)],
        compiler_params=_cparams(1),
        name="attn_sample",
    )(q, kb, vb, cache_k, cache_v, ext)


def _mix_kernel(sp_ref, ss_ref, ap_ref, as_ref, w1_ref, w2_ref, g1_ref, g2_ref, o_ref, *, n_prompt_tiles):
    i = pl.program_id(1)
    is_p = i < n_prompt_tiles
    a1 = _dot(jnp.where(is_p, sp_ref[...], ss_ref[...]), w1_ref[...])
    a2 = _dot(jnp.where(is_p, ap_ref[...], as_ref[...]), w2_ref[...])
    g1 = _sigmoid(g1_ref[...].astype(F32))
    g2 = _sigmoid(g2_ref[...].astype(F32))
    o_ref[...] = (g1 * a1 + g2 * a2).astype(o_ref.dtype)


def _mix(ssd_p, ssd_s, att_p, att_s, w1, w2, gates_arr, *, gate_col0, tm, tn):
    k1 = ssd_p.shape[1]
    k2 = att_p.shape[1]
    m = att_p.shape[0] + att_s.shape[0]
    n = w1.shape[1]
    npt = att_p.shape[0] // tm
    assert ssd_p.shape[0] == att_p.shape[0] and ssd_s.shape[0] == att_s.shape[0]
    g1_blk = gate_col0 // tn
    g2_blk = (gate_col0 + n) // tn
    kern = functools.partial(_mix_kernel, n_prompt_tiles=npt)
    p_map = lambda j, i: (jnp.minimum(i, npt - 1), 0)
    s_map = lambda j, i: (jnp.maximum(i - npt, 0), 0)
    return pl.pallas_call(
        kern,
        grid=(n // tn, m // tm),
        in_specs=[
            pl.BlockSpec((tm, k1), p_map),
            pl.BlockSpec((tm, k1), s_map),
            pl.BlockSpec((tm, k2), p_map),
            pl.BlockSpec((tm, k2), s_map),
            pl.BlockSpec((k1, tn), lambda j, i: (0, j), pipeline_mode=pl.Buffered(1)),
            pl.BlockSpec((k2, tn), lambda j, i: (0, j), pipeline_mode=pl.Buffered(1)),
            pl.BlockSpec((tm, tn), lambda j, i: (i, g1_blk + j)),
            pl.BlockSpec((tm, tn), lambda j, i: (i, g2_blk + j)),
        ],
        out_specs=pl.BlockSpec((tm, tn), lambda j, i: (i, j)),
        out_shape=jax.ShapeDtypeStruct((m, n), BF16),
        compiler_params=_cparams(2),
        name="branch_mix",
    )(ssd_p, ssd_s, att_p, att_s, w1, w2, gates_arr, gates_arr)


def _layer_norm(t, g, b):
    mu = jnp.mean(t, axis=-1, keepdims=True)
    tc = t - mu
    var = jnp.mean(tc * tc, axis=-1, keepdims=True)
    return tc * lax.rsqrt(var + LN_EPS) * g + b


LN_SUB_ROWS = 256


def _proj_ln_kernel(*refs, alpha, n_k, n_prompt_tiles, split_resid, n_out):
    a_ref, w_ref = refs[0], refs[1]
    pos = 2
    r_refs = refs[pos:pos + (2 if split_resid else 1)]
    pos += len(r_refs)
    g_ref, b_ref = refs[pos], refs[pos + 1]
    pos += 2
    o_refs = refs[pos:pos + n_out]
    acc_ref = refs[pos + n_out] if n_k > 1 else None
    i = pl.program_id(0)
    kk = pl.program_id(1)
    tm = a_ref.shape[0]

    def resid_rows(rs):
        if split_resid:
            return jnp.where(i < n_prompt_tiles, r_refs[0][rs, :], r_refs[1][rs, :])
        return r_refs[0][rs, :]

    def finish(first):
        for r in range(tm // LN_SUB_ROWS):
            rs = slice(r * LN_SUB_ROWS, (r + 1) * LN_SUB_ROWS)
            base = alpha * resid_rows(rs) if first else acc_ref[rs, :]
            y = _layer_norm(base + _dot(a_ref[rs, :], w_ref[...]), g_ref[...], b_ref[...])
            for o_ref in o_refs:
                o_ref[rs, :] = y.astype(o_ref.dtype)

    if n_k == 1:
        finish(True)
        return

    @pl.when(kk == 0)
    def _():
        acc_ref[...] = alpha * resid_rows(slice(None)) + _dot(a_ref[...], w_ref[...])

    if n_k > 2:
        @pl.when(jnp.logical_and(kk > 0, kk < n_k - 1))
        def _():
            acc_ref[...] += _dot(a_ref[...], w_ref[...])

    @pl.when(kk == n_k - 1)
    def _():
        finish(False)


def _proj_ln(a, w, resid, gamma, beta, out_dtypes, *, alpha, tm, n_k, row0, n_rows, m_prompt=None, name):
    k = a.shape[1]
    n = w.shape[1]
    tk = k // n_k
    assert tk * n_k == k and n_rows % tm == 0 and row0 % tm == 0 and tm % LN_SUB_ROWS == 0
    blk0 = row0 // tm
    split_resid = isinstance(resid, tuple)
    row_map = lambda i, kk: (blk0 + i, 0)
    w_mode = dict(pipeline_mode=pl.Buffered(1)) if n_k == 1 else {}
    in_specs = [pl.BlockSpec((tm, tk), lambda i, kk: (blk0 + i, kk)),
                pl.BlockSpec((tk, n), lambda i, kk: (kk, 0), **w_mode)]
    args = [a, w]
    npt = 0
    if split_resid:
        assert row0 == 0 and m_prompt % tm == 0
        npt = m_prompt // tm
        in_specs += [pl.BlockSpec((tm, n), lambda i, kk: (jnp.minimum(i, npt - 1), 0)),
                     pl.BlockSpec((tm, n), lambda i, kk: (jnp.maximum(i - npt, 0), 0))]
        args += list(resid)
    else:
        in_specs.append(pl.BlockSpec((tm, n), row_map))
        args.append(resid)
    in_specs += [pl.BlockSpec((1, n), lambda i, kk: (0, 0))] * 2
    args += [gamma, beta]
    kern = functools.partial(_proj_ln_kernel, alpha=alpha, n_k=n_k, n_prompt_tiles=npt,
                             split_resid=split_resid, n_out=len(out_dtypes))
    return pl.pallas_call(
        kern,
        grid=(n_rows // tm, n_k),
        in_specs=in_specs,
        out_specs=[pl.BlockSpec((tm, n), lambda i, kk: (i, 0)) for _ in out_dtypes],
        out_shape=[jax.ShapeDtypeStruct((n_rows, n), dt) for dt in out_dtypes],
        scratch_shapes=[pltpu.VMEM((tm, n), F32)] if n_k > 1 else [],
        compiler_params=_cparams(2),
        name=name,
    )(*args)


def _glu_kernel(*refs, n_side):
    h_ref, wg_ref, wu_ref = refs[:3]
    side_in = refs[3:3 + n_side]
    o_ref = refs[3 + n_side]
    side_out = refs[4 + n_side:4 + 2 * n_side]
    wg_sc, wu_sc = refs[4 + 2 * n_side:]

    @pl.when(pl.program_id(1) == 0)
    def _():
        _cast_weight_tile(wg_ref, wg_sc)
        _cast_weight_tile(wu_ref, wu_sc)

    for s_in, s_out in zip(side_in, side_out):
        s_out[...] = s_in[...].astype(BF16)

    h = h_ref[...]
    g = _dot(h, wg_sc[...])
    u = _dot(h, wu_sc[...])
    o_ref[...] = (g * jax.nn.sigmoid(g) * u).astype(o_ref.dtype)


def _glu(h, w_gate_up, *, tm, tf, side=()):
    m, k = h.shape
    d_ff = w_gate_up.shape[1] // 2
    nf = d_ff // tf
    assert nf * tf == d_ff
    side_in, side_out, side_shape = _side_cast_specs(side, m // tm)
    assert all(w.shape[0] // rows <= nf * (m // tm) for w, rows in side)
    return pl.pallas_call(
        functools.partial(_glu_kernel, n_side=len(side)),
        grid=(nf, m // tm),
        in_specs=[
            pl.BlockSpec((tm, k), lambda j, i: (i, 0)),
            pl.BlockSpec((k, tf), lambda j, i: (0, j)),
            pl.BlockSpec((k, tf), lambda j, i: (0, nf + j)),
        ] + side_in,
        out_specs=[pl.BlockSpec((tm, tf), lambda j, i: (i, j))] + side_out,
        out_shape=[jax.ShapeDtypeStruct((m, d_ff), BF16)] + side_shape,
        scratch_shapes=[pltpu.VMEM((k, tf), BF16), pltpu.VMEM((k, tf), BF16)],
        compiler_params=_cparams(2),
        name="ffn_glu",
    )(h, w_gate_up, w_gate_up, *[w for w, _ in side])


def kernel(x_prompt, x_sample, cache_k, cache_v, state_conv, state_ssm, w_in, conv_w, conv_b, dt_bias,
           a_log, d_skip, ssd_norm_w, rel_bias, w_ssd_out, w_att_out, w_o, ln1_g, ln1_b, w_gate_up,
           w_down, ln2_g, ln2_b):
    depth = w_in.shape[0]
    assert depth == 1
    batch, seq, d_model = x_prompt.shape
    n_sample, dec_seq, _ = x_sample.shape
    assert batch == 1 and dec_seq == CHUNK and seq % ATT_TILE == 0
    n_ssd_heads = dt_bias.shape[1]
    inner = n_ssd_heads * SSD_HEADDIM
    conv_dim = conv_w.shape[2]
    n_groups = SSD_GROUPS
    d_state = SSD_STATE
    assert conv_dim == inner + 2 * n_groups * d_state
    n_heads = rel_bias.shape[1]
    width = n_heads * ATT_HEAD_DIM
    assert cache_k.shape[2] == BAND_PAST
    alpha = (2.0 * depth) ** 0.25
    keep = min(BAND_PAST, seq)

    m_p = batch * seq
    m_s = n_sample * dec_seq
    xp2 = x_prompt.reshape(m_p, d_model)
    xs2 = x_sample.reshape(m_s, d_model)

    wt = w_in[0].T
    tn = 1024
    o_xbc = inner
    o_dt = o_xbc + conv_dim
    o_q = o_dt + n_ssd_heads
    shift = o_q % tn
    assert o_dt % tn == 0 and inner % tn == 0 and conv_dim % tn == 0 and width % tn == 0
    q_blk = o_dt // tn
    xb, dtr = _xcast_dt(xp2, xs2, wt, dt_col=o_dt, n_dt=n_ssd_heads, tm=512)
    zs, w_att_b, w_o_b = _matmul_w32(xb, wt, lambda j: j, inner // tn, [BF16], tm=1536, tn=tn, silu=True,
                                     side=[(w_att_out[0], SIDE_CAST_ROWS), (w_o[0], SIDE_CAST_ROWS)],
                                     name="in_proj_z")
    (xbc,) = _matmul_w32(xb, wt, lambda j: inner // tn + j, conv_dim // tn, [F32], tm=1536, tn=tn,
                         name="in_proj_xbc")
    wpt = width // tn
    qg, w_ssd_b = _matmul_w32(xb, wt, lambda j: jnp.where(j < wpt, q_blk + j, q_blk + 2 * wpt + j),
                              wpt + 2 * d_model // tn, [BF16], tm=1536, tn=tn, shift=shift,
                              side=[(w_ssd_out[0], SIDE_CAST_ROWS)], name="in_proj_qg")
    kb, k_p, k_s = _matmul_w32(xb, wt, lambda j: q_blk + wpt + j, wpt, [BF16], tm=1024, tn=tn, shift=shift,
                               head_out=(keep, m_s), name="in_proj_k")
    vb, v_p, v_s = _matmul_w32(xb, wt, lambda j: q_blk + 2 * wpt + j, wpt, [BF16], tm=1024, tn=tn,
                               shift=shift, head_out=(keep, m_s), name="in_proj_v")

    pad_h = (0, LANES - n_ssd_heads)
    dtb = jnp.pad(dt_bias[0], pad_h).reshape(1, LANES)
    alog = jnp.pad(a_log[0], pad_h).reshape(1, LANES)
    dskx = jnp.repeat(d_skip[0], SSD_HEADDIM).reshape(1, inner)
    emat = (jnp.arange(LANES)[:, None] == (jnp.arange(inner)[None, :] // SSD_HEADDIM)).astype(BF16)
    ssd_args = (zs, xbc, dtr, state_conv[0], state_ssm[0].reshape(n_sample, inner, d_state),
                conv_w[0], conv_b[0].reshape(1, conv_dim), dtb, alog, dskx, ssd_norm_w[0].reshape(1, inner), emat)
    ssd_dims = dict(inner=inner, n_groups=n_groups, d_state=d_state)
    ssd_p, conv_p, h_p = _ssd(*ssd_args, row0=0, n_streams=batch, chunks_per_stream=seq // CHUNK,
                              cps=SSD_PROMPT_CPS, per_stream=False, name="ssd_scan_prompt", **ssd_dims)
    ssd_s, conv_s, h_s = _ssd(*ssd_args, row0=m_p, n_streams=n_sample, chunks_per_stream=1, cps=1,
                              per_stream=True, name="ssd_scan_sample", **ssd_dims)

    rb = rel_bias[0]
    band = BAND_PAST + CHUNK
    n_edge = band - REL_CLIP
    n_rev = ATT_WIN - n_edge
    rev = rb[:, ::-1][:, 1:1 + n_rev]
    ext = jnp.concatenate([jnp.broadcast_to(rb[:, 2 * REL_CLIP:], (n_heads, n_edge)), rev], axis=1)
    ext = jnp.pad(ext, ((0, 0), (0, ATT_WIN - ext.shape[1])))
    att_p = _attn_prompt(qg, kb, vb, ext, n_tiles=m_p // ATT_TILE, n_heads=n_heads)
    cache_rows = (n_sample, BAND_PAST * n_heads, ATT_HEAD_DIM)
    att_s = _attn_sample(qg, kb, vb, cache_k[0].reshape(cache_rows), cache_v[0].reshape(cache_rows), ext,
                         row_blk0=m_p // CHUNK, n_sample=n_sample, n_heads=n_heads, spb=ATT_SAMPLE_SPB)

    mixed = _mix(ssd_p, ssd_s, att_p, att_s, w_ssd_b, w_att_b, qg, gate_col0=width, tm=512, tn=1024)
    g1 = ln1_g[0].reshape(1, d_model)
    b1 = ln1_b[0].reshape(1, d_model)
    g2 = ln2_g[0].reshape(1, d_model)
    b2 = ln2_b[0].reshape(1, d_model)
    h_f, h_b = _proj_ln(mixed, w_o_b, (xp2, xs2), g1, b1, [F32, BF16], alpha=alpha, tm=512,
                        n_k=1, row0=0, n_rows=m_p + m_s, m_prompt=m_p, name="wo_ln")
    act, w_dn = _glu(h_b, w_gate_up[0], tm=1536, tf=512, side=[(w_down[0], DOWN_CAST_ROWS)])
    (y_p,) = _proj_ln(act, w_dn, h_f, g2, b2, [F32], alpha=alpha, tm=512, n_k=2, row0=0, n_rows=m_p,
                      name="down_ln_prompt")
    (y_s,) = _proj_ln(act, w_dn, h_f, g2, b2, [F32], alpha=alpha, tm=512, n_k=2, row0=m_p, n_rows=m_s,
                      name="down_ln_sample")

    return (y_p.reshape(batch, seq, d_model),
            y_s.reshape(n_sample, dec_seq, d_model),
            conv_p.reshape(1, batch, CONV_W - 1, conv_dim),
            h_p.reshape(1, batch, n_ssd_heads, SSD_HEADDIM, d_state),
            k_p.reshape(1, batch, keep, n_heads, ATT_HEAD_DIM),
            v_p.reshape(1, batch, keep, n_heads, ATT_HEAD_DIM),
            conv_s.reshape(1, n_sample, CONV_W - 1, conv_dim),
            h_s.reshape(1, n_sample, n_ssd_heads, SSD_HEADDIM, d_state),
            k_s.reshape(1, n_sample, dec_seq, n_heads, ATT_HEAD_DIM),
            v_s.reshape(1, n_sample, dec_seq, n_heads, ATT_HEAD_DIM))
```

```python
import functools

import jax
import jax.numpy as jnp
from jax import lax
from jax.experimental import pallas as pl
from jax.experimental.pallas import tpu as pltpu

F32 = jnp.float32
BF16 = jnp.bfloat16

CHUNK = 64
SSD_HEADDIM = 64
SSD_GROUPS = 8
SSD_STATE = 128
CONV_W = 4
ATT_HEAD_DIM = 128
BAND_CHUNKS = 8
BAND_PAST = BAND_CHUNKS * CHUNK
REL_CLIP = 128
LN_EPS = 1e-5
RMS_EPS = 1e-5
MASK_NEG = -1e30

LANES = 128
SUBLANES = 8
VMEM_LIMIT_BYTES = 56 * 1024 * 1024

SSD_PROMPT_CPS = 4
SIDE_CAST_ROWS = 128
DOWN_CAST_ROWS = 176
ATT_TILE = 512
ATT_SAMPLE_SPB = 2
GLU_SUB_BLOCKS = 2
SILU_SUB_BLOCKS = 2
ATT_WIN = 5 * LANES


def _cparams(n_axes):
    return pltpu.CompilerParams(
        dimension_semantics=("arbitrary",) * n_axes,
        vmem_limit_bytes=VMEM_LIMIT_BYTES,
    )


def _sigmoid(x):
    return 0.5 * jnp.tanh(0.5 * x) + 0.5


def _silu(x):
    h = 0.5 * x
    return h + h * jnp.tanh(h)


def _dot(a, b):
    return jnp.dot(a, b, preferred_element_type=F32)


def _dot_nt(a, b):
    return lax.dot_general(a, b, (((1,), (1,)), ((), ())), preferred_element_type=F32)


def _dot_tn(a, b):
    return lax.dot_general(a, b, (((0,), (0,)), ((), ())), preferred_element_type=F32)


def _xcast_dt_kernel(xp_ref, xs_ref, wd_ref, xb_ref, dt_ref, *, n_prompt_tiles, n_dt):
    i = pl.program_id(0)
    x = jnp.where(i < n_prompt_tiles, xp_ref[...], xs_ref[...]).astype(BF16)
    xb_ref[...] = x
    acc = _dot_nt(x, wd_ref[...].astype(BF16))
    lane = lax.broadcasted_iota(jnp.int32, acc.shape, 1)
    dt_ref[...] = jnp.where(lane < n_dt, acc, 0.0)


def _xcast_dt(xp, xs, wt, *, dt_col, n_dt, tm):
    m_p, k = xp.shape
    m_s = xs.shape[0]
    npt = m_p // tm
    n_tiles = npt + m_s // tm
    assert dt_col % LANES == 0 and n_dt <= LANES
    kern = functools.partial(_xcast_dt_kernel, n_prompt_tiles=npt, n_dt=n_dt)
    return pl.pallas_call(
        kern,
        grid=(n_tiles,),
        in_specs=[
            pl.BlockSpec((tm, k), lambda i: (jnp.minimum(i, npt - 1), 0)),
            pl.BlockSpec((tm, k), lambda i: (jnp.maximum(i - npt, 0), 0)),
            pl.BlockSpec((LANES, k), lambda i: (dt_col // LANES, 0)),
        ],
        out_specs=[pl.BlockSpec((tm, k), lambda i: (i, 0)),
                   pl.BlockSpec((tm, LANES), lambda i: (i, 0))],
        out_shape=[jax.ShapeDtypeStruct((m_p + m_s, k), BF16),
                   jax.ShapeDtypeStruct((m_p + m_s, LANES), F32)],
        compiler_params=_cparams(1),
        name="xcast_dt",
    )(xp, xs, wt)


CAST_ROWS = 256


def _cast_rows(src_ref, src_row0, dst_ref, dst_row0, n_rows):
    done = 0
    while done < n_rows:
        step = min(CAST_ROWS, n_rows - done)
        dst_ref[dst_row0 + done:dst_row0 + done + step, :] = (
            src_ref[src_row0 + done:src_row0 + done + step, :].astype(BF16))
        done += step


def _cast_weight_tile(w_ref, wsc):
    _cast_rows(w_ref, 0, wsc, 0, wsc.shape[0])


def _side_cast_specs(side, n_inner):
    in_specs, out_specs, out_shape = [], [], []
    for w, rows in side:
        n_blk = w.shape[0] // rows
        assert n_blk * rows == w.shape[0] and rows % 16 == 0
        idx = lambda j, i, n_blk=n_blk: (jnp.minimum(j * n_inner + i, n_blk - 1), 0)
        in_specs.append(pl.BlockSpec((rows, w.shape[1]), idx))
        out_specs.append(pl.BlockSpec((rows, w.shape[1]), idx))
        out_shape.append(jax.ShapeDtypeStruct(w.shape, BF16))
    return in_specs, out_specs, out_shape


def _mm_w32_kernel(*refs, shift, n_out, head_rows, silu, n_side):
    x_ref, wa_ref = refs[0], refs[1]
    pos = 2
    wb_ref = None
    if shift:
        wb_ref = refs[pos]
        pos += 1
    side_in = refs[pos:pos + n_side]
    pos += n_side
    o_refs = refs[pos:pos + n_out]
    pos += n_out
    hp_ref = hs_ref = None
    if head_rows is not None:
        hp_ref, hs_ref = refs[pos], refs[pos + 1]
        pos += 2
    side_out = refs[pos:pos + n_side]
    pos += n_side
    wsc = refs[pos]
    i = pl.program_id(1)

    @pl.when(i == 0)
    def _():
        tn = wsc.shape[0]
        _cast_rows(wa_ref, shift, wsc, 0, tn - shift)
        if shift:
            _cast_rows(wb_ref, 0, wsc, tn - shift, shift)

    for s_in, s_out in zip(side_in, side_out):
        s_out[...] = s_in[...].astype(BF16)

    if silu:
        sub = x_ref.shape[0] // SILU_SUB_BLOCKS
        for r in range(SILU_SUB_BLOCKS):
            rs = slice(r * sub, (r + 1) * sub)
            res = _silu(_dot_nt(x_ref[rs, :], wsc[...]))
            for o_ref in o_refs:
                o_ref[rs, :] = res.astype(o_ref.dtype)
        return

    acc = _dot_nt(x_ref[...], wsc[...])
    for o_ref in o_refs:
        o_ref[...] = acc.astype(o_ref.dtype)

    if head_rows is not None:
        (tile_p, row0_p), (tile_s, row0_s) = head_rows

        @pl.when(i == tile_p)
        def _():
            hp_ref[...] = acc[row0_p:row0_p + hp_ref.shape[0], :]

        @pl.when(i == tile_s)
        def _():
            hs_ref[...] = acc[row0_s:row0_s + hs_ref.shape[0], :]


def _matmul_w32(x, wt, col_blk, n_col_tiles, out_dtypes, *, tm, tn, shift=0, head_out=None, silu=False,
                side=(), name):
    m, k = x.shape
    assert m % tm == 0 and tn % LANES == 0 and shift % 16 == 0 and shift < tn
    assert not (silu and head_out is not None) and tm % (16 * SILU_SUB_BLOCKS) == 0
    in_specs = [pl.BlockSpec((tm, k), lambda j, i: (i, 0)),
                pl.BlockSpec((tn, k), lambda j, i: (col_blk(j), 0))]
    args = [x, wt]
    if shift:
        assert tn % shift == 0
        per = tn // shift
        in_specs.append(pl.BlockSpec((shift, k), lambda j, i: ((col_blk(j) + 1) * per, 0)))
        args.append(wt)
    side_in, side_out, side_shape = _side_cast_specs(side, m // tm)
    assert all(w.shape[0] // rows <= n_col_tiles * (m // tm) for w, rows in side)
    in_specs += side_in
    args += [w for w, _ in side]
    n = n_col_tiles * tn
    out_specs = [pl.BlockSpec((tm, tn), lambda j, i: (i, j)) for _ in out_dtypes]
    out_shape = [jax.ShapeDtypeStruct((m, n), dt) for dt in out_dtypes]
    head_rows = None
    if head_out is not None:
        n_keep, m_s = head_out
        m_p = m - m_s
        head_rows = (divmod(m_p - n_keep, tm), divmod(m_p, tm))
        assert head_rows[0][1] + n_keep <= tm and head_rows[1][1] + m_s <= tm
        out_specs += [pl.BlockSpec((n_keep, tn), lambda j, i: (0, j)),
                      pl.BlockSpec((m_s, tn), lambda j, i: (0, j))]
        out_shape += [jax.ShapeDtypeStruct((n_keep, n), F32), jax.ShapeDtypeStruct((m_s, n), F32)]
    out_specs += side_out
    out_shape += side_shape
    kern = functools.partial(_mm_w32_kernel, shift=shift, n_out=len(out_dtypes), head_rows=head_rows,
                             silu=silu, n_side=len(side))
    return pl.pallas_call(
        kern,
        grid=(n_col_tiles, m // tm),
        in_specs=in_specs,
        out_specs=out_specs,
        out_shape=out_shape,
        scratch_shapes=[pltpu.VMEM((tn, k), BF16)],
        compiler_params=_cparams(2),
        name=name,
    )(*args)


def _softplus(x):
    return jnp.maximum(x, 0.0) + jnp.log1p(jnp.exp(-jnp.abs(x)))


def _ssd_chunk(u, zs_ref, dt_ref, cw_ref, cb_ref, dtb_ref, alog_ref, dsk_ref, nw_ref, e_ref, y_ref,
               xpad, act, ht, cst, ex, ysc, *, inner, n_groups, d_state):
    r0 = u * CHUNK
    rows = slice(r0, r0 + CHUNK)
    gw = inner // n_groups
    conv_dim = inner + 2 * n_groups * d_state

    for j in range(conv_dim // LANES):
        sl = slice(j * LANES, (j + 1) * LANES)
        blk = xpad[r0:r0 + SUBLANES + CHUNK, sl]
        prev = pltpu.roll(blk, 1, 0)
        near = blk * cw_ref[3:4, sl] + prev * cw_ref[2:3, sl]
        far = pltpu.roll(blk * cw_ref[1:2, sl] + prev * cw_ref[0:1, sl], 2, 0)
        a = cb_ref[:, sl] + (near + far)[SUBLANES:, :]
        act[rows, sl] = _silu(a)

    dt = _softplus(dt_ref[rows, :] + dtb_ref[...])
    a_neg = -jnp.exp(alog_ref[...])
    da = dt * a_neg
    ri = lax.broadcasted_iota(jnp.int32, (CHUNK, CHUNK), 0)
    ci = lax.broadcasted_iota(jnp.int32, (CHUNK, CHUNK), 1)
    tri = (ri >= ci).astype(F32)
    cs = jnp.dot(tri, da, precision=lax.Precision.HIGHEST, preferred_element_type=F32)
    ecs = jnp.exp(cs)
    cs_last = cs[CHUNK - 1:CHUNK, :]
    dout = jnp.exp(cs_last - cs) * dt
    cd = jnp.exp(cs_last)
    cd_hi = cd.astype(BF16)
    cd_lo = (cd - cd_hi.astype(F32)).astype(BF16)
    lhs = jnp.concatenate(
        [dt.astype(BF16), ecs.astype(BF16), dout.astype(BF16),
         jnp.broadcast_to(cd_hi, (16, LANES)), jnp.broadcast_to(cd_lo, (16, LANES))], axis=0)
    ex[u] = _dot(lhs, e_ref[...])

    cst[u] = jnp.concatenate([cs, jnp.zeros((LANES - CHUNK, LANES), F32)], axis=0).T
    lane = lax.broadcasted_iota(jnp.int32, (CHUNK, LANES), 1)
    lo_half = lane < CHUNK
    n_pairs = inner // LANES
    ev = cst[u, pl.ds(0, n_pairs, stride=2), :]
    od = cst[u, pl.ds(1, n_pairs, stride=2), :]
    lane_p = lax.broadcasted_iota(jnp.int32, (n_pairs, LANES), 1)
    cst2 = jnp.where(lane_p < CHUNK, ev, pltpu.roll(od, CHUNK, 1))
    row = lax.broadcasted_iota(jnp.int32, (CHUNK, LANES), 0)
    causal2 = row >= jnp.where(lo_half, lane, lane - CHUNK)

    pairs_per_group = gw // LANES
    for g in range(n_groups):
        gs = slice(g * gw, (g + 1) * gw)
        bm_g = act[rows, inner + g * d_state: inner + (g + 1) * d_state].astype(BF16)
        cm_g = act[rows, inner + (n_groups + g) * d_state: inner + (n_groups + g + 1) * d_state].astype(BF16)
        cb2 = _dot_nt(cm_g, jnp.concatenate([bm_g, bm_g], axis=0))
        h_g = ht[:, gs]
        yoff = _dot(cm_g, h_g.astype(BF16))
        for p in range(pairs_per_group):
            k = g * pairs_per_group + p
            ls = slice(k * LANES, (k + 1) * LANES)
            csc = jnp.where(lo_half,
                            jnp.broadcast_to(cs[:, 2 * k:2 * k + 1], (CHUNK, LANES)),
                            jnp.broadcast_to(cs[:, 2 * k + 1:2 * k + 2], (CHUNK, LANES)))
            diff = csc - cst2[k:k + 1, :]
            dec = jnp.exp(jnp.where(causal2, diff, MASK_NEG))
            m2 = (cb2 * dec).astype(BF16)
            xs_p = act[rows, ls]
            xdt = xs_p * ex[u, 0:CHUNK, ls]
            x2 = jnp.concatenate([jnp.where(lo_half, xdt, 0.0), jnp.where(lo_half, 0.0, xdt)],
                                 axis=0).astype(BF16)
            yd = _dot(m2, x2)
            ysc[rows, ls] = (yd + yoff[:, p * LANES:(p + 1) * LANES] * ex[u, CHUNK:2 * CHUNK, ls]
                             + dsk_ref[:, ls] * xs_p)
        xw = (act[rows, gs] * ex[u, 2 * CHUNK:3 * CHUNK, gs]).astype(BF16)
        st = _dot_tn(bm_g, xw)
        cdx = ex[u, 3 * CHUNK:3 * CHUNK + 1, gs] + ex[u, 3 * CHUNK + 16:3 * CHUNK + 17, gs]
        ht[:, gs] = h_g * cdx + st

    for g in range(n_groups):
        gs = slice(g * gw, (g + 1) * gw)
        v = ysc[rows, gs] * zs_ref[rows, gs].astype(F32)
        ms = jnp.mean(v * v, axis=-1, keepdims=True)
        y_ref[rows, gs] = (v * lax.rsqrt(ms + RMS_EPS) * nw_ref[:, gs]).astype(y_ref.dtype)


def _ssd_kernel(zs_ref, xbc_ref, dt_ref, hist_ref, h0_ref, cw_ref, cb_ref, dtb_ref, alog_ref,
                dsk_ref, nw_ref, e_ref,
                y_ref, conv_ref, ho_ref,
                xpad, act, ht, cst, ex, ysc, *, cps, per_stream, inner, n_groups, d_state):
    c = pl.program_id(0)
    n_rows = cps * CHUNK
    conv_dim = inner + 2 * n_groups * d_state

    @pl.when(c == 0)
    def _():
        xpad[0:SUBLANES, :] = jnp.zeros((SUBLANES, conv_dim), F32)
        if not per_stream:
            ht[...] = jnp.zeros_like(ht)

    if per_stream:
        xpad[5:8, :] = hist_ref[0]
        ht[...] = h0_ref[0].T

    xpad[SUBLANES:SUBLANES + n_rows, :] = xbc_ref[...]
    conv_ref[0] = xbc_ref[n_rows - (CONV_W - 1):n_rows, :]

    for u in range(cps):
        _ssd_chunk(u, zs_ref, dt_ref, cw_ref, cb_ref, dtb_ref, alog_ref, dsk_ref, nw_ref, e_ref, y_ref,
                   xpad, act, ht, cst, ex, ysc, inner=inner, n_groups=n_groups, d_state=d_state)
    xpad[5:8, :] = xpad[5 + n_rows:8 + n_rows, :]

    if per_stream:
        ho_ref[0] = ht[...].T
    else:
        @pl.when(c == pl.num_programs(0) - 1)
        def _():
            ho_ref[0] = ht[...].T


def _ssd(zs, xbc, dtr, hist, h0, conv_w, conv_b, dtb, alog, dskx, normw, emat, *,
         row0, n_streams, chunks_per_stream, cps, per_stream, inner, n_groups, d_state, name):
    conv_dim = xbc.shape[1]
    hp = inner
    n_rows = cps * CHUNK
    assert chunks_per_stream % cps == 0 and row0 % n_rows == 0
    assert per_stream == (chunks_per_stream == cps) and (per_stream or n_streams == 1)
    steps_per_stream = chunks_per_stream // cps
    blk0 = row0 // n_rows
    stream = (lambda c: c) if per_stream else (lambda c: 0)

    kern = functools.partial(_ssd_kernel, cps=cps, per_stream=per_stream, inner=inner, n_groups=n_groups,
                             d_state=d_state)
    const = lambda c: (0, 0)
    row_map = lambda c: (blk0 + c, 0)
    return pl.pallas_call(
        kern,
        grid=(n_streams * steps_per_stream,),
        in_specs=[
            pl.BlockSpec((n_rows, inner), row_map),
            pl.BlockSpec((n_rows, conv_dim), row_map),
            pl.BlockSpec((n_rows, LANES), row_map),
            pl.BlockSpec((1, CONV_W - 1, conv_dim), lambda c: (stream(c), 0, 0)),
            pl.BlockSpec((1, hp, d_state), lambda c: (stream(c), 0, 0)),
            pl.BlockSpec((CONV_W, conv_dim), const),
            pl.BlockSpec((1, conv_dim), const),
            pl.BlockSpec((1, LANES), const),
            pl.BlockSpec((1, LANES), const),
            pl.BlockSpec((1, inner), const),
            pl.BlockSpec((1, inner), const),
            pl.BlockSpec((LANES, inner), const),
        ],
        out_specs=[
            pl.BlockSpec((n_rows, inner), lambda c: (c, 0)),
            pl.BlockSpec((1, CONV_W - 1, conv_dim), lambda c: (stream(c), 0, 0)),
            pl.BlockSpec((1, hp, d_state), lambda c: (stream(c), 0, 0)),
        ],
        out_shape=[
            jax.ShapeDtypeStruct((n_streams * chunks_per_stream * CHUNK, inner), BF16),
            jax.ShapeDtypeStruct((n_streams, CONV_W - 1, conv_dim), F32),
            jax.ShapeDtypeStruct((n_streams, hp, d_state), F32),
        ],
        scratch_shapes=[
            pltpu.VMEM((SUBLANES + n_rows, conv_dim), F32),
            pltpu.VMEM((n_rows, conv_dim), F32),
            pltpu.VMEM((d_state, inner), F32),
            pltpu.VMEM((cps, LANES, LANES), F32),
            pltpu.VMEM((cps, 3 * CHUNK + 32, inner), F32),
            pltpu.VMEM((n_rows, inner), F32),
        ],
        compiler_params=_cparams(1),
        name=name,
    )(zs, xbc, dtr, hist, h0, conv_w, conv_b, dtb, alog, dskx, normw, emat)


def _band_bias_tables(ext_ref, bias_scr, n_heads):
    col = lax.broadcasted_iota(jnp.int32, (CHUNK, ATT_WIN), 1)
    band = BAND_PAST + CHUNK
    for h in range(n_heads):
        e = jnp.broadcast_to(ext_ref[h:h + 1, :], (CHUNK, ATT_WIN))
        even = pltpu.roll(e, ATT_WIN - (CHUNK - 1), 1, stride=1, stride_axis=0)
        odd = pltpu.roll(e, 1, 1, stride=1, stride_axis=0)
        bias_scr[0, h] = jnp.where(col < band, even, MASK_NEG)
        bias_scr[1, h] = jnp.where(col >= CHUNK, odd, MASK_NEG)


ATT_BLOCKS = BAND_PAST // ATT_TILE + 1


def _attn_prompt_kernel(*refs, n_heads):
    q_ref = refs[0]
    k_refs = refs[1:1 + ATT_BLOCKS]
    v_refs = refs[1 + ATT_BLOCKS:1 + 2 * ATT_BLOCKS]
    ext_ref, o_ref, bias_scr, s_scr, p_scr = refs[1 + 2 * ATT_BLOCKS:]
    i = pl.program_id(0)
    d = ATT_HEAD_DIM
    scale = d ** -0.5
    blk = ATT_TILE

    @pl.when(i == 0)
    def _():
        _band_bias_tables(ext_ref, bias_scr, n_heads)
        p_scr[...] = jnp.zeros_like(p_scr)

    past_mask = [jnp.where(i < ATT_BLOCKS - 1 - b, MASK_NEG, 0.0).astype(F32) for b in range(ATT_BLOCKS - 1)]

    for h in range(n_heads):
        hs = slice(h * d, (h + 1) * d)
        q_h = q_ref[:, hs]
        for b in range(ATT_BLOCKS):
            s = _dot_nt(q_h, k_refs[b][:, hs]) * scale
            s_scr[:, b * blk:(b + 1) * blk] = s + past_mask[b] if b < ATT_BLOCKS - 1 else s
        for t in range(ATT_TILE // CHUNK):
            rs = slice(t * CHUNK, (t + 1) * CHUNK)
            c0 = LANES * (t // 2)
            ws = slice(c0, c0 + ATT_WIN)
            s = s_scr[rs, ws] + bias_scr[t % 2, h]
            m = jnp.max(s, axis=-1, keepdims=True)
            p = jnp.exp(s - m)
            r = 1.0 / jnp.sum(p, axis=-1, keepdims=True)
            p_scr[rs, ws] = (p * r).astype(BF16)
        o = _dot(p_scr[:, 0:blk], v_refs[0][:, hs])
        for b in range(1, ATT_BLOCKS):
            o = o + _dot(p_scr[:, b * blk:(b + 1) * blk], v_refs[b][:, hs])
        o_ref[:, hs] = o.astype(o_ref.dtype)


def _attn_prompt(q, kb, vb, ext, *, n_tiles, n_heads):
    width = n_heads * ATT_HEAD_DIM
    assert BAND_PAST % ATT_TILE == 0 and ATT_TILE % (2 * CHUNK) == 0
    kern = functools.partial(_attn_prompt_kernel, n_heads=n_heads)
    window = [pl.BlockSpec((ATT_TILE, width), lambda i, back=back: (jnp.maximum(i - back, 0), 0))
              for back in range(ATT_BLOCKS - 1, -1, -1)]
    return pl.pallas_call(
        kern,
        grid=(n_tiles,),
        in_specs=[pl.BlockSpec((ATT_TILE, width), lambda i: (i, 0))] + window + window
                 + [pl.BlockSpec((n_heads, ATT_WIN), lambda i: (0, 0))],
        out_specs=pl.BlockSpec((ATT_TILE, width), lambda i: (i, 0)),
        out_shape=jax.ShapeDtypeStruct((n_tiles * ATT_TILE, width), BF16),
        scratch_shapes=[
            pltpu.VMEM((2, n_heads, CHUNK, ATT_WIN), F32),
            pltpu.VMEM((ATT_TILE, ATT_BLOCKS * ATT_TILE), F32),
            pltpu.VMEM((ATT_TILE, ATT_BLOCKS * ATT_TILE), BF16),
        ],
        compiler_params=_cparams(1),
        name="attn_prompt",
    )(q, *([kb] * ATT_BLOCKS), *([vb] * ATT_BLOCKS), ext)


def _sublane_transpose8(t):
    sub = lax.broadcasted_iota(jnp.int32, (SUBLANES, LANES), 0)
    for k in (4, 2, 1):
        keep = (sub // k) % 2 == 0
        new = [None] * SUBLANES
        for i in range(SUBLANES):
            if (i // k) % 2 == 0:
                j = i + k
                new[i] = jnp.where(keep, t[i], pltpu.roll(t[j], k, 0))
                new[j] = jnp.where(keep, pltpu.roll(t[i], SUBLANES - k, 0), t[j])
        t = new
    return t


def _split_heads(c_ref, u, dst, n_heads):
    halves = n_heads // SUBLANES

    def body(g, carry):
        p0 = pl.multiple_of(g * SUBLANES, SUBLANES)
        for hh in range(halves):
            tiles = [c_ref[u, pl.ds(pl.multiple_of((p0 + p) * n_heads + hh * SUBLANES, SUBLANES), SUBLANES), :]
                     for p in range(SUBLANES)]
            for a, tile in enumerate(_sublane_transpose8(tiles)):
                dst[hh * SUBLANES + a, pl.ds(p0, SUBLANES), :] = tile
        return carry

    lax.fori_loop(0, BAND_PAST // SUBLANES, body, 0, unroll=2)


def _attn_sample_kernel(q_ref, kn_ref, vn_ref, ck_ref, cv_ref, ext_ref, o_ref, bias_scr, kh_scr, vh_scr,
                        s_scr, p_scr, *, n_heads):
    d = ATT_HEAD_DIM
    scale = d ** -0.5

    @pl.when(pl.program_id(0) == 0)
    def _():
        _band_bias_tables(ext_ref, bias_scr, n_heads)

    for u in range(ck_ref.shape[0]):
        qr = slice(u * CHUNK, (u + 1) * CHUNK)
        _split_heads(ck_ref, u, kh_scr, n_heads)
        _split_heads(cv_ref, u, vh_scr, n_heads)
        band = BAND_PAST + CHUNK
        for h in range(n_heads):
            hs = slice(h * d, (h + 1) * d)
            q_h = q_ref[qr, hs]
            s_scr[h, :, 0:BAND_PAST] = (_dot_nt(q_h, kh_scr[h].astype(BF16)) * scale
                                        + bias_scr[0, h, :, 0:BAND_PAST])
            s_scr[h, :, BAND_PAST:band] = (_dot_nt(q_h, kn_ref[qr, hs]) * scale
                                           + bias_scr[0, h, :, BAND_PAST:band])
        for h in range(n_heads):
            s = s_scr[h, :, 0:band]
            p = jnp.exp(s - jnp.max(s, axis=-1, keepdims=True))
            r = 1.0 / jnp.sum(p, axis=-1, keepdims=True)
            p_scr[h, :, 0:band] = (p * r).astype(BF16)
        for h in range(n_heads):
            hs = slice(h * d, (h + 1) * d)
            o = (_dot(p_scr[h, :, 0:BAND_PAST], vh_scr[h].astype(BF16))
                 + _dot(p_scr[h, :, BAND_PAST:band], vn_ref[qr, hs]))
            o_ref[qr, hs] = o.astype(o_ref.dtype)


def _attn_sample(q, kb, vb, cache_k, cache_v, ext, *, row_blk0, n_sample, n_heads, spb):
    width = n_heads * ATT_HEAD_DIM
    assert n_sample % spb == 0 and row_blk0 % spb == 0
    kern = functools.partial(_attn_sample_kernel, n_heads=n_heads)
    cache_spec = pl.BlockSpec((spb, BAND_PAST * n_heads, ATT_HEAD_DIM), lambda s: (s, 0, 0))
    row_spec = pl.BlockSpec((spb * CHUNK, width), lambda s: (row_blk0 // spb + s, 0))
    return pl.pallas_call(
        kern,
        grid=(n_sample // spb,),
        in_specs=[row_spec, row_spec, row_spec, cache_spec, cache_spec,
                  pl.BlockSpec((n_heads, ATT_WIN), lambda s: (0, 0))],
        out_specs=pl.BlockSpec((spb * CHUNK, width), lambda s: (s, 0)),
        out_shape=jax.ShapeDtypeStruct((n_sample * CHUNK, width), BF16),
        scratch_shapes=[pltpu.VMEM((2, n_heads, CHUNK, ATT_WIN), F32),
                        pltpu.VMEM((n_heads, BAND_PAST, ATT_HEAD_DIM), F32),
                        pltpu.VMEM((n_heads, BAND_PAST, ATT_HEAD_DIM), F32),
                        pltpu.VMEM((n_heads, CHUNK, ATT_WIN), F32),
                        pltpu.VMEM((n_heads, CHUNK, ATT_WIN), BF16)],
        compiler_params=_cparams(1),
        name="attn_sample",
    )(q, kb, vb, cache_k, cache_v, ext)


def _mix_kernel(sp_ref, ss_ref, ap_ref, as_ref, w1_ref, w2_ref, g1_ref, g2_ref, o_ref, *, n_prompt_tiles):
    i = pl.program_id(1)
    is_p = i < n_prompt_tiles
    a1 = _dot(jnp.where(is_p, sp_ref[...], ss_ref[...]), w1_ref[...])
    a2 = _dot(jnp.where(is_p, ap_ref[...], as_ref[...]), w2_ref[...])
    g1 = _sigmoid(g1_ref[...].astype(F32))
    g2 = _sigmoid(g2_ref[...].astype(F32))
    o_ref[...] = (g1 * a1 + g2 * a2).astype(o_ref.dtype)


def _mix(ssd_p, ssd_s, att_p, att_s, w1, w2, gates_arr, *, gate_col0, tm, tn):
    k1 = ssd_p.shape[1]
    k2 = att_p.shape[1]
    m = att_p.shape[0] + att_s.shape[0]
    n = w1.shape[1]
    npt = att_p.shape[0] // tm
    assert ssd_p.shape[0] == att_p.shape[0] and ssd_s.shape[0] == att_s.shape[0]
    g1_blk = gate_col0 // tn
    g2_blk = (gate_col0 + n) // tn
    kern = functools.partial(_mix_kernel, n_prompt_tiles=npt)
    p_map = lambda j, i: (jnp.minimum(i, npt - 1), 0)
    s_map = lambda j, i: (jnp.maximum(i - npt, 0), 0)
    return pl.pallas_call(
        kern,
        grid=(n // tn, m // tm),
        in_specs=[
            pl.BlockSpec((tm, k1), p_map),
            pl.BlockSpec((tm, k1), s_map),
            pl.BlockSpec((tm, k2), p_map),
            pl.BlockSpec((tm, k2), s_map),
            pl.BlockSpec((k1, tn), lambda j, i: (0, j), pipeline_mode=pl.Buffered(1)),
            pl.BlockSpec((k2, tn), lambda j, i: (0, j), pipeline_mode=pl.Buffered(1)),
            pl.BlockSpec((tm, tn), lambda j, i: (i, g1_blk + j)),
            pl.BlockSpec((tm, tn), lambda j, i: (i, g2_blk + j)),
        ],
        out_specs=pl.BlockSpec((tm, tn), lambda j, i: (i, j)),
        out_shape=jax.ShapeDtypeStruct((m, n), BF16),
        compiler_params=_cparams(2),
        name="branch_mix",
    )(ssd_p, ssd_s, att_p, att_s, w1, w2, gates_arr, gates_arr)


def _layer_norm(t, g, b):
    mu = jnp.mean(t, axis=-1, keepdims=True)
    tc = t - mu
    var = jnp.mean(tc * tc, axis=-1, keepdims=True)
    return tc * lax.rsqrt(var + LN_EPS) * g + b


LN_SUB_ROWS = 256


def _proj_ln_kernel(*refs, alpha, n_k, n_prompt_tiles, split_resid, n_out):
    a_ref, w_ref = refs[0], refs[1]
    pos = 2
    r_refs = refs[pos:pos + (2 if split_resid else 1)]
    pos += len(r_refs)
    g_ref, b_ref = refs[pos], refs[pos + 1]
    pos += 2
    o_refs = refs[pos:pos + n_out]
    acc_ref = refs[pos + n_out] if n_k > 1 else None
    i = pl.program_id(0)
    kk = pl.program_id(1)
    tm = a_ref.shape[0]

    def resid_rows(rs):
        if split_resid:
            return jnp.where(i < n_prompt_tiles, r_refs[0][rs, :], r_refs[1][rs, :])
        return r_refs[0][rs, :]

    def finish(first):
        for r in range(tm // LN_SUB_ROWS):
            rs = slice(r * LN_SUB_ROWS, (r + 1) * LN_SUB_ROWS)
            base = alpha * resid_rows(rs) if first else acc_ref[rs, :]
            y = _layer_norm(base + _dot(a_ref[rs, :], w_ref[...]), g_ref[...], b_ref[...])
            for o_ref in o_refs:
                o_ref[rs, :] = y.astype(o_ref.dtype)

    if n_k == 1:
        finish(True)
        return

    @pl.when(kk == 0)
    def _():
        acc_ref[...] = alpha * resid_rows(slice(None)) + _dot(a_ref[...], w_ref[...])

    if n_k > 2:
        @pl.when(jnp.logical_and(kk > 0, kk < n_k - 1))
        def _():
            acc_ref[...] += _dot(a_ref[...], w_ref[...])

    @pl.when(kk == n_k - 1)
    def _():
        finish(False)


def _proj_ln(a, w, resid, gamma, beta, out_dtypes, *, alpha, tm, n_k, row0, n_rows, m_prompt=None, name):
    k = a.shape[1]
    n = w.shape[1]
    tk = k // n_k
    assert tk * n_k == k and n_rows % tm == 0 and row0 % tm == 0 and tm % LN_SUB_ROWS == 0
    blk0 = row0 // tm
    split_resid = isinstance(resid, tuple)
    row_map = lambda i, kk: (blk0 + i, 0)
    w_mode = dict(pipeline_mode=pl.Buffered(1)) if n_k == 1 else {}
    in_specs = [pl.BlockSpec((tm, tk), lambda i, kk: (blk0 + i, kk)),
                pl.BlockSpec((tk, n), lambda i, kk: (kk, 0), **w_mode)]
    args = [a, w]
    npt = 0
    if split_resid:
        assert row0 == 0 and m_prompt % tm == 0
        npt = m_prompt // tm
        in_specs += [pl.BlockSpec((tm, n), lambda i, kk: (jnp.minimum(i, npt - 1), 0)),
                     pl.BlockSpec((tm, n), lambda i, kk: (jnp.maximum(i - npt, 0), 0))]
        args += list(resid)
    else:
        in_specs.append(pl.BlockSpec((tm, n), row_map))
        args.append(resid)
    in_specs += [pl.BlockSpec((1, n), lambda i, kk: (0, 0))] * 2
    args += [gamma, beta]
    kern = functools.partial(_proj_ln_kernel, alpha=alpha, n_k=n_k, n_prompt_tiles=npt,
                             split_resid=split_resid, n_out=len(out_dtypes))
    return pl.pallas_call(
        kern,
        grid=(n_rows // tm, n_k),
        in_specs=in_specs,
        out_specs=[pl.BlockSpec((tm, n), lambda i, kk: (i, 0)) for _ in out_dtypes],
        out_shape=[jax.ShapeDtypeStruct((n_rows, n), dt) for dt in out_dtypes],
        scratch_shapes=[pltpu.VMEM((tm, n), F32)] if n_k > 1 else [],
        compiler_params=_cparams(2),
        name=name,
    )(*args)


def _glu_kernel(*refs, n_side):
    h_ref, wg_ref, wu_ref = refs[:3]
    side_in = refs[3:3 + n_side]
    o_ref = refs[3 + n_side]
    side_out = refs[4 + n_side:4 + 2 * n_side]
    wg_sc, wu_sc = refs[4 + 2 * n_side:]

    @pl.when(pl.program_id(1) == 0)
    def _():
        _cast_weight_tile(wg_ref, wg_sc)
        _cast_weight_tile(wu_ref, wu_sc)

    for s_in, s_out in zip(side_in, side_out):
        s_out[...] = s_in[...].astype(BF16)

    tm = h_ref.shape[0]
    sub = tm // GLU_SUB_BLOCKS
    for r in range(GLU_SUB_BLOCKS):
        rs = slice(r * sub, (r + 1) * sub)
        h = h_ref[rs, :]
        g = _dot(h, wg_sc[...])
        u = _dot(h, wu_sc[...])
        o_ref[rs, :] = (g * jax.nn.sigmoid(g) * u).astype(o_ref.dtype)


def _glu(h, w_gate_up, *, tm, tf, side=()):
    m, k = h.shape
    d_ff = w_gate_up.shape[1] // 2
    nf = d_ff // tf
    assert nf * tf == d_ff
    side_in, side_out, side_shape = _side_cast_specs(side, m // tm)
    assert all(w.shape[0] // rows <= nf * (m // tm) for w, rows in side)
    return pl.pallas_call(
        functools.partial(_glu_kernel, n_side=len(side)),
        grid=(nf, m // tm),
        in_specs=[
            pl.BlockSpec((tm, k), lambda j, i: (i, 0)),
            pl.BlockSpec((k, tf), lambda j, i: (0, j)),
            pl.BlockSpec((k, tf), lambda j, i: (0, nf + j)),
        ] + side_in,
        out_specs=[pl.BlockSpec((tm, tf), lambda j, i: (i, j))] + side_out,
        out_shape=[jax.ShapeDtypeStruct((m, d_ff), BF16)] + side_shape,
        scratch_shapes=[pltpu.VMEM((k, tf), BF16), pltpu.VMEM((k, tf), BF16)],
        compiler_params=_cparams(2),
        name="ffn_glu",
    )(h, w_gate_up, w_gate_up, *[w for w, _ in side])


def kernel(x_prompt, x_sample, cache_k, cache_v, state_conv, state_ssm, w_in, conv_w, conv_b, dt_bias,
           a_log, d_skip, ssd_norm_w, rel_bias, w_ssd_out, w_att_out, w_o, ln1_g, ln1_b, w_gate_up,
           w_down, ln2_g, ln2_b):
    depth = w_in.shape[0]
    assert depth == 1
    batch, seq, d_model = x_prompt.shape
    n_sample, dec_seq, _ = x_sample.shape
    assert batch == 1 and dec_seq == CHUNK and seq % ATT_TILE == 0
    n_ssd_heads = dt_bias.shape[1]
    inner = n_ssd_heads * SSD_HEADDIM
    conv_dim = conv_w.shape[2]
    n_groups = SSD_GROUPS
    d_state = SSD_STATE
    assert conv_dim == inner + 2 * n_groups * d_state
    n_heads = rel_bias.shape[1]
    width = n_heads * ATT_HEAD_DIM
    assert cache_k.shape[2] == BAND_PAST
    alpha = (2.0 * depth) ** 0.25
    keep = min(BAND_PAST, seq)

    m_p = batch * seq
    m_s = n_sample * dec_seq
    xp2 = x_prompt.reshape(m_p, d_model)
    xs2 = x_sample.reshape(m_s, d_model)

    wt = w_in[0].T
    tn = 1024
    o_xbc = inner
    o_dt = o_xbc + conv_dim
    o_q = o_dt + n_ssd_heads
    shift = o_q % tn
    assert o_dt % tn == 0 and inner % tn == 0 and conv_dim % tn == 0 and width % tn == 0
    q_blk = o_dt // tn
    xb, dtr = _xcast_dt(xp2, xs2, wt, dt_col=o_dt, n_dt=n_ssd_heads, tm=1024)
    zs, w_att_b, w_o_b = _matmul_w32(xb, wt, lambda j: j, inner // tn, [BF16], tm=1536, tn=tn, silu=True,
                                     side=[(w_att_out[0], SIDE_CAST_ROWS), (w_o[0], SIDE_CAST_ROWS)],
                                     name="in_proj_z")
    (xbc,) = _matmul_w32(xb, wt, lambda j: inner // tn + j, conv_dim // tn, [F32], tm=1536, tn=tn,
                         name="in_proj_xbc")
    wpt = width // tn
    qg, w_ssd_b = _matmul_w32(xb, wt, lambda j: jnp.where(j < wpt, q_blk + j, q_blk + 2 * wpt + j),
                              wpt + 2 * d_model // tn, [BF16], tm=1536, tn=tn, shift=shift,
                              side=[(w_ssd_out[0], SIDE_CAST_ROWS)], name="in_proj_qg")
    kb, k_p, k_s = _matmul_w32(xb, wt, lambda j: q_blk + wpt + j, wpt, [BF16], tm=1024, tn=tn, shift=shift,
                               head_out=(keep, m_s), name="in_proj_k")
    vb, v_p, v_s = _matmul_w32(xb, wt, lambda j: q_blk + 2 * wpt + j, wpt, [BF16], tm=1024, tn=tn,
                               shift=shift, head_out=(keep, m_s), name="in_proj_v")

    pad_h = (0, LANES - n_ssd_heads)
    dtb = jnp.pad(dt_bias[0], pad_h).reshape(1, LANES)
    alog = jnp.pad(a_log[0], pad_h).reshape(1, LANES)
    dskx = jnp.repeat(d_skip[0], SSD_HEADDIM).reshape(1, inner)
    emat = (jnp.arange(LANES)[:, None] == (jnp.arange(inner)[None, :] // SSD_HEADDIM)).astype(BF16)
    ssd_args = (zs, xbc, dtr, state_conv[0], state_ssm[0].reshape(n_sample, inner, d_state),
                conv_w[0], conv_b[0].reshape(1, conv_dim), dtb, alog, dskx, ssd_norm_w[0].reshape(1, inner), emat)
    ssd_dims = dict(inner=inner, n_groups=n_groups, d_state=d_state)
    ssd_p, conv_p, h_p = _ssd(*ssd_args, row0=0, n_streams=batch, chunks_per_stream=seq // CHUNK,
                              cps=SSD_PROMPT_CPS, per_stream=False, name="ssd_scan_prompt", **ssd_dims)
    ssd_s, conv_s, h_s = _ssd(*ssd_args, row0=m_p, n_streams=n_sample, chunks_per_stream=1, cps=1,
                              per_stream=True, name="ssd_scan_sample", **ssd_dims)

    rb = rel_bias[0]
    band = BAND_PAST + CHUNK
    n_edge = band - REL_CLIP
    n_rev = ATT_WIN - n_edge
    rev = rb[:, ::-1][:, 1:1 + n_rev]
    ext = jnp.concatenate([jnp.broadcast_to(rb[:, 2 * REL_CLIP:], (n_heads, n_edge)), rev], axis=1)
    ext = jnp.pad(ext, ((0, 0), (0, ATT_WIN - ext.shape[1])))
    att_p = _attn_prompt(qg, kb, vb, ext, n_tiles=m_p // ATT_TILE, n_heads=n_heads)
    cache_rows = (n_sample, BAND_PAST * n_heads, ATT_HEAD_DIM)
    att_s = _attn_sample(qg, kb, vb, cache_k[0].reshape(cache_rows), cache_v[0].reshape(cache_rows), ext,
                         row_blk0=m_p // CHUNK, n_sample=n_sample, n_heads=n_heads, spb=ATT_SAMPLE_SPB)

    mixed = _mix(ssd_p, ssd_s, att_p, att_s, w_ssd_b, w_att_b, qg, gate_col0=width, tm=512, tn=1024)
    g1 = ln1_g[0].reshape(1, d_model)
    b1 = ln1_b[0].reshape(1, d_model)
    g2 = ln2_g[0].reshape(1, d_model)
    b2 = ln2_b[0].reshape(1, d_model)
    h_f, h_b = _proj_ln(mixed, w_o_b, (xp2, xs2), g1, b1, [F32, BF16], alpha=alpha, tm=512,
                        n_k=1, row0=0, n_rows=m_p + m_s, m_prompt=m_p, name="wo_ln")
    act, w_dn = _glu(h_b, w_gate_up[0], tm=1536, tf=512, side=[(w_down[0], DOWN_CAST_ROWS)])
    (y_p,) = _proj_ln(act, w_dn, h_f, g2, b2, [F32], alpha=alpha, tm=512, n_k=2, row0=0, n_rows=m_p,
                      name="down_ln_prompt")
    (y_s,) = _proj_ln(act, w_dn, h_f, g2, b2, [F32], alpha=alpha, tm=512, n_k=2, row0=m_p, n_rows=m_s,
                      name="down_ln_sample")

    return (y_p.reshape(batch, seq, d_model),
            y_s.reshape(n_sample, dec_seq, d_model),
            conv_p.reshape(1, batch, CONV_W - 1, conv_dim),
            h_p.reshape(1, batch, n_ssd_heads, SSD_HEADDIM, d_state),
            k_p.reshape(1, batch, keep, n_heads, ATT_HEAD_DIM),
            v_p.reshape(1, batch, keep, n_heads, ATT_HEAD_DIM),
            conv_s.reshape(1, n_sample, CONV_W - 1, conv_dim),
            h_s.reshape(1, n_sample, n_ssd_heads, SSD_HEADDIM, d_state),
            k_s.reshape(1, n_sample, dec_seq, n_heads, ATT_HEAD_DIM),
            v_s.reshape(1, n_sample, dec_seq, n_heads, ATT_HEAD_DIM))
```

```python
import functools

import jax
import jax.numpy as jnp
from jax import lax
from jax.experimental import pallas as pl
from jax.experimental.pallas import tpu as pltpu

F32 = jnp.float32
BF16 = jnp.bfloat16

CHUNK = 64
SSD_HEADDIM = 64
SSD_GROUPS = 8
SSD_STATE = 128
CONV_W = 4
ATT_HEAD_DIM = 128
BAND_CHUNKS = 8
BAND_PAST = BAND_CHUNKS * CHUNK
REL_CLIP = 128
LN_EPS = 1e-5
RMS_EPS = 1e-5
MASK_NEG = -1e30

LANES = 128
SUBLANES = 8
VMEM_LIMIT_BYTES = 56 * 1024 * 1024

SSD_PROMPT_CPS = 4
SIDE_CAST_ROWS = 128
DOWN_CAST_ROWS = 176
ATT_TILE = 512
ATT_SAMPLE_SPB = 2
GLU_SUB_BLOCKS = 2
SILU_SUB_BLOCKS = 2
ATT_WIN = 5 * LANES


def _cparams(n_axes):
    return pltpu.CompilerParams(
        dimension_semantics=("arbitrary",) * n_axes,
        vmem_limit_bytes=VMEM_LIMIT_BYTES,
    )


def _sigmoid(x):
    return 0.5 * jnp.tanh(0.5 * x) + 0.5


def _silu(x):
    h = 0.5 * x
    return h + h * jnp.tanh(h)


def _dot(a, b):
    return jnp.dot(a, b, preferred_element_type=F32)


def _dot_nt(a, b):
    return lax.dot_general(a, b, (((1,), (1,)), ((), ())), preferred_element_type=F32)


def _dot_tn(a, b):
    return lax.dot_general(a, b, (((0,), (0,)), ((), ())), preferred_element_type=F32)


def _xcast_dt_kernel(xp_ref, xs_ref, wd_ref, xb_ref, dt_ref, *, n_prompt_tiles, n_dt):
    i = pl.program_id(0)
    x = jnp.where(i < n_prompt_tiles, xp_ref[...], xs_ref[...]).astype(BF16)
    xb_ref[...] = x
    acc = _dot_nt(x, wd_ref[...].astype(BF16))
    lane = lax.broadcasted_iota(jnp.int32, acc.shape, 1)
    dt_ref[...] = jnp.where(lane < n_dt, acc, 0.0)


def _xcast_dt(xp, xs, wt, *, dt_col, n_dt, tm):
    m_p, k = xp.shape
    m_s = xs.shape[0]
    npt = m_p // tm
    n_tiles = npt + m_s // tm
    assert dt_col % LANES == 0 and n_dt <= LANES
    kern = functools.partial(_xcast_dt_kernel, n_prompt_tiles=npt, n_dt=n_dt)
    return pl.pallas_call(
        kern,
        grid=(n_tiles,),
        in_specs=[
            pl.BlockSpec((tm, k), lambda i: (jnp.minimum(i, npt - 1), 0)),
            pl.BlockSpec((tm, k), lambda i: (jnp.maximum(i - npt, 0), 0)),
            pl.BlockSpec((LANES, k), lambda i: (dt_col // LANES, 0)),
        ],
        out_specs=[pl.BlockSpec((tm, k), lambda i: (i, 0)),
                   pl.BlockSpec((tm, LANES), lambda i: (i, 0))],
        out_shape=[jax.ShapeDtypeStruct((m_p + m_s, k), BF16),
                   jax.ShapeDtypeStruct((m_p + m_s, LANES), F32)],
        compiler_params=_cparams(1),
        name="xcast_dt",
    )(xp, xs, wt)


CAST_ROWS = 256


def _cast_rows(src_ref, src_row0, dst_ref, dst_row0, n_rows):
    done = 0
    while done < n_rows:
        step = min(CAST_ROWS, n_rows - done)
        dst_ref[dst_row0 + done:dst_row0 + done + step, :] = (
            src_ref[src_row0 + done:src_row0 + done + step, :].astype(BF16))
        done += step


def _cast_weight_tile(w_ref, wsc):
    _cast_rows(w_ref, 0, wsc, 0, wsc.shape[0])


def _side_cast_specs(side, n_inner):
    in_specs, out_specs, out_shape = [], [], []
    for w, rows in side:
        n_blk = w.shape[0] // rows
        assert n_blk * rows == w.shape[0] and rows % 16 == 0
        idx = lambda j, i, n_blk=n_blk: (jnp.minimum(j * n_inner + i, n_blk - 1), 0)
        in_specs.append(pl.BlockSpec((rows, w.shape[1]), idx))
        out_specs.append(pl.BlockSpec((rows, w.shape[1]), idx))
        out_shape.append(jax.ShapeDtypeStruct(w.shape, BF16))
    return in_specs, out_specs, out_shape


def _mm_w32_kernel(*refs, shift, n_out, head_rows, silu, n_side):
    x_ref, wa_ref = refs[0], refs[1]
    pos = 2
    wb_ref = None
    if shift:
        wb_ref = refs[pos]
        pos += 1
    side_in = refs[pos:pos + n_side]
    pos += n_side
    o_refs = refs[pos:pos + n_out]
    pos += n_out
    hp_ref = hs_ref = None
    if head_rows is not None:
        hp_ref, hs_ref = refs[pos], refs[pos + 1]
        pos += 2
    side_out = refs[pos:pos + n_side]
    pos += n_side
    wsc = refs[pos]
    i = pl.program_id(1)

    @pl.when(i == 0)
    def _():
        tn = wsc.shape[0]
        _cast_rows(wa_ref, shift, wsc, 0, tn - shift)
        if shift:
            _cast_rows(wb_ref, 0, wsc, tn - shift, shift)

    for s_in, s_out in zip(side_in, side_out):
        s_out[...] = s_in[...].astype(BF16)

    if silu:
        sub = x_ref.shape[0] // SILU_SUB_BLOCKS
        for r in range(SILU_SUB_BLOCKS):
            rs = slice(r * sub, (r + 1) * sub)
            res = _silu(_dot_nt(x_ref[rs, :], wsc[...]))
            for o_ref in o_refs:
                o_ref[rs, :] = res.astype(o_ref.dtype)
        return

    acc = _dot_nt(x_ref[...], wsc[...])
    for o_ref in o_refs:
        o_ref[...] = acc.astype(o_ref.dtype)

    if head_rows is not None:
        (tile_p, row0_p), (tile_s, row0_s) = head_rows

        @pl.when(i == tile_p)
        def _():
            hp_ref[...] = acc[row0_p:row0_p + hp_ref.shape[0], :]

        @pl.when(i == tile_s)
        def _():
            hs_ref[...] = acc[row0_s:row0_s + hs_ref.shape[0], :]


def _matmul_w32(x, wt, col_blk, n_col_tiles, out_dtypes, *, tm, tn, shift=0, head_out=None, silu=False,
                side=(), name):
    m, k = x.shape
    assert m % tm == 0 and tn % LANES == 0 and shift % 16 == 0 and shift < tn
    assert not (silu and head_out is not None) and tm % (16 * SILU_SUB_BLOCKS) == 0
    in_specs = [pl.BlockSpec((tm, k), lambda j, i: (i, 0)),
                pl.BlockSpec((tn, k), lambda j, i: (col_blk(j), 0))]
    args = [x, wt]
    if shift:
        assert tn % shift == 0
        per = tn // shift
        in_specs.append(pl.BlockSpec((shift, k), lambda j, i: ((col_blk(j) + 1) * per, 0)))
        args.append(wt)
    side_in, side_out, side_shape = _side_cast_specs(side, m // tm)
    assert all(w.shape[0] // rows <= n_col_tiles * (m // tm) for w, rows in side)
    in_specs += side_in
    args += [w for w, _ in side]
    n = n_col_tiles * tn
    out_specs = [pl.BlockSpec((tm, tn), lambda j, i: (i, j)) for _ in out_dtypes]
    out_shape = [jax.ShapeDtypeStruct((m, n), dt) for dt in out_dtypes]
    head_rows = None
    if head_out is not None:
        n_keep, m_s = head_out
        m_p = m - m_s
        head_rows = (divmod(m_p - n_keep, tm), divmod(m_p, tm))
        assert head_rows[0][1] + n_keep <= tm and head_rows[1][1] + m_s <= tm
        out_specs += [pl.BlockSpec((n_keep, tn), lambda j, i: (0, j)),
                      pl.BlockSpec((m_s, tn), lambda j, i: (0, j))]
        out_shape += [jax.ShapeDtypeStruct((n_keep, n), F32), jax.ShapeDtypeStruct((m_s, n), F32)]
    out_specs += side_out
    out_shape += side_shape
    kern = functools.partial(_mm_w32_kernel, shift=shift, n_out=len(out_dtypes), head_rows=head_rows,
                             silu=silu, n_side=len(side))
    return pl.pallas_call(
        kern,
        grid=(n_col_tiles, m // tm),
        in_specs=in_specs,
        out_specs=out_specs,
        out_shape=out_shape,
        scratch_shapes=[pltpu.VMEM((tn, k), BF16)],
        compiler_params=_cparams(2),
        name=name,
    )(*args)


def _softplus(x):
    return jnp.maximum(x, 0.0) + jnp.log1p(jnp.exp(-jnp.abs(x)))


def _ssd_chunk(u, zs_ref, dt_ref, cw_ref, cb_ref, dtb_ref, alog_ref, dsk_ref, nw_ref, e_ref, y_ref,
               xpad, act, ht, cst, ex, ysc, *, inner, n_groups, d_state):
    r0 = u * CHUNK
    rows = slice(r0, r0 + CHUNK)
    gw = inner // n_groups
    conv_dim = inner + 2 * n_groups * d_state

    for j in range(conv_dim // LANES):
        sl = slice(j * LANES, (j + 1) * LANES)
        blk = xpad[r0:r0 + SUBLANES + CHUNK, sl]
        prev = pltpu.roll(blk, 1, 0)
        near = blk * cw_ref[3:4, sl] + prev * cw_ref[2:3, sl]
        far = pltpu.roll(blk * cw_ref[1:2, sl] + prev * cw_ref[0:1, sl], 2, 0)
        a = cb_ref[:, sl] + (near + far)[SUBLANES:, :]
        act[rows, sl] = _silu(a)

    dt = _softplus(dt_ref[rows, :] + dtb_ref[...])
    a_neg = -jnp.exp(alog_ref[...])
    da = dt * a_neg
    ri = lax.broadcasted_iota(jnp.int32, (CHUNK, CHUNK), 0)
    ci = lax.broadcasted_iota(jnp.int32, (CHUNK, CHUNK), 1)
    tri = (ri >= ci).astype(F32)
    cs = jnp.dot(tri, da, precision=lax.Precision.HIGHEST, preferred_element_type=F32)
    ecs = jnp.exp(cs)
    cs_last = cs[CHUNK - 1:CHUNK, :]
    dout = jnp.exp(cs_last - cs) * dt
    cd = jnp.exp(cs_last)
    cd_hi = cd.astype(BF16)
    cd_lo = (cd - cd_hi.astype(F32)).astype(BF16)
    lhs = jnp.concatenate(
        [dt.astype(BF16), ecs.astype(BF16), dout.astype(BF16),
         jnp.broadcast_to(cd_hi, (16, LANES)), jnp.broadcast_to(cd_lo, (16, LANES))], axis=0)
    ex[u] = _dot(lhs, e_ref[...])

    cst[u] = jnp.concatenate([cs, jnp.zeros((LANES - CHUNK, LANES), F32)], axis=0).T
    lane = lax.broadcasted_iota(jnp.int32, (CHUNK, LANES), 1)
    lo_half = lane < CHUNK
    n_pairs = inner // LANES
    ev = cst[u, pl.ds(0, n_pairs, stride=2), :]
    od = cst[u, pl.ds(1, n_pairs, stride=2), :]
    lane_p = lax.broadcasted_iota(jnp.int32, (n_pairs, LANES), 1)
    cst2 = jnp.where(lane_p < CHUNK, ev, pltpu.roll(od, CHUNK, 1))
    row = lax.broadcasted_iota(jnp.int32, (CHUNK, LANES), 0)
    causal2 = row >= jnp.where(lo_half, lane, lane - CHUNK)

    pairs_per_group = gw // LANES
    for g in range(n_groups):
        gs = slice(g * gw, (g + 1) * gw)
        bm_g = act[rows, inner + g * d_state: inner + (g + 1) * d_state].astype(BF16)
        cm_g = act[rows, inner + (n_groups + g) * d_state: inner + (n_groups + g + 1) * d_state].astype(BF16)
        cb2 = _dot_nt(cm_g, jnp.concatenate([bm_g, bm_g], axis=0))
        h_g = ht[:, gs]
        yoff = _dot(cm_g, h_g.astype(BF16))
        for p in range(pairs_per_group):
            k = g * pairs_per_group + p
            ls = slice(k * LANES, (k + 1) * LANES)
            csc = jnp.where(lo_half,
                            jnp.broadcast_to(cs[:, 2 * k:2 * k + 1], (CHUNK, LANES)),
                            jnp.broadcast_to(cs[:, 2 * k + 1:2 * k + 2], (CHUNK, LANES)))
            diff = csc - cst2[k:k + 1, :]
            dec = jnp.exp(jnp.where(causal2, diff, MASK_NEG))
            m2 = (cb2 * dec).astype(BF16)
            xs_p = act[rows, ls]
            xdt = xs_p * ex[u, 0:CHUNK, ls]
            x2 = jnp.concatenate([jnp.where(lo_half, xdt, 0.0), jnp.where(lo_half, 0.0, xdt)],
                                 axis=0).astype(BF16)
            yd = _dot(m2, x2)
            ysc[rows, ls] = (yd + yoff[:, p * LANES:(p + 1) * LANES] * ex[u, CHUNK:2 * CHUNK, ls]
                             + dsk_ref[:, ls] * xs_p)
        xw = (act[rows, gs] * ex[u, 2 * CHUNK:3 * CHUNK, gs]).astype(BF16)
        st = _dot_tn(bm_g, xw)
        cdx = ex[u, 3 * CHUNK:3 * CHUNK + 1, gs] + ex[u, 3 * CHUNK + 16:3 * CHUNK + 17, gs]
        ht[:, gs] = h_g * cdx + st

    for g in range(n_groups):
        gs = slice(g * gw, (g + 1) * gw)
        v = ysc[rows, gs] * zs_ref[rows, gs].astype(F32)
        ms = jnp.mean(v * v, axis=-1, keepdims=True)
        y_ref[rows, gs] = (v * lax.rsqrt(ms + RMS_EPS) * nw_ref[:, gs]).astype(y_ref.dtype)


def _ssd_kernel(*refs, cps, per_stream, aliased, inner, n_groups, d_state):
    (zs_ref, xbc_ref, dt_ref, hist_ref, h0_ref, cw_ref, cb_ref, dtb_ref, alog_ref,
     dsk_ref, nw_ref, e_ref) = refs[:12]
    y_ref, conv_ref, ho_ref, xpad, act, ht, cst, ex, ysc = refs[12 + int(aliased):]
    c = pl.program_id(0)
    n_rows = cps * CHUNK
    conv_dim = inner + 2 * n_groups * d_state

    @pl.when(c == 0)
    def _():
        xpad[0:SUBLANES, :] = jnp.zeros((SUBLANES, conv_dim), F32)
        if not per_stream:
            ht[...] = jnp.zeros_like(ht)

    if per_stream:
        xpad[5:8, :] = hist_ref[0]
        ht[...] = h0_ref[0].T

    xpad[SUBLANES:SUBLANES + n_rows, :] = xbc_ref[...]
    conv_ref[0] = xbc_ref[n_rows - (CONV_W - 1):n_rows, :]

    for u in range(cps):
        _ssd_chunk(u, zs_ref, dt_ref, cw_ref, cb_ref, dtb_ref, alog_ref, dsk_ref, nw_ref, e_ref, y_ref,
                   xpad, act, ht, cst, ex, ysc, inner=inner, n_groups=n_groups, d_state=d_state)
    xpad[5:8, :] = xpad[5 + n_rows:8 + n_rows, :]

    if per_stream:
        ho_ref[0] = ht[...].T
    else:
        @pl.when(c == pl.num_programs(0) - 1)
        def _():
            ho_ref[0] = ht[...].T


def _ssd(zs, xbc, dtr, hist, h0, conv_w, conv_b, dtb, alog, dskx, normw, emat, *,
         row0, n_streams, chunks_per_stream, cps, per_stream, inner, n_groups, d_state, name, y_into=None):
    conv_dim = xbc.shape[1]
    hp = inner
    n_rows = cps * CHUNK
    assert chunks_per_stream % cps == 0 and row0 % n_rows == 0
    assert per_stream == (chunks_per_stream == cps) and (per_stream or n_streams == 1)
    steps_per_stream = chunks_per_stream // cps
    blk0 = row0 // n_rows
    stream = (lambda c: c) if per_stream else (lambda c: 0)
    aliased = y_into is not None

    kern = functools.partial(_ssd_kernel, cps=cps, per_stream=per_stream, aliased=aliased, inner=inner,
                             n_groups=n_groups, d_state=d_state)
    const = lambda c: (0, 0)
    row_map = lambda c: (blk0 + c, 0)
    extra_specs = [pl.BlockSpec(memory_space=pl.ANY)] if aliased else []
    extra_args = [y_into] if aliased else []
    return pl.pallas_call(
        kern,
        grid=(n_streams * steps_per_stream,),
        input_output_aliases={12: 0} if aliased else {},
        in_specs=[
            pl.BlockSpec((n_rows, inner), row_map),
            pl.BlockSpec((n_rows, conv_dim), row_map),
            pl.BlockSpec((n_rows, LANES), row_map),
            pl.BlockSpec((1, CONV_W - 1, conv_dim), lambda c: (stream(c), 0, 0)),
            pl.BlockSpec((1, hp, d_state), lambda c: (stream(c), 0, 0)),
            pl.BlockSpec((CONV_W, conv_dim), const),
            pl.BlockSpec((1, conv_dim), const),
            pl.BlockSpec((1, LANES), const),
            pl.BlockSpec((1, LANES), const),
            pl.BlockSpec((1, inner), const),
            pl.BlockSpec((1, inner), const),
            pl.BlockSpec((LANES, inner), const),
        ] + extra_specs,
        out_specs=[
            pl.BlockSpec((n_rows, inner), row_map),
            pl.BlockSpec((1, CONV_W - 1, conv_dim), lambda c: (stream(c), 0, 0)),
            pl.BlockSpec((1, hp, d_state), lambda c: (stream(c), 0, 0)),
        ],
        out_shape=[
            jax.ShapeDtypeStruct((zs.shape[0], inner), BF16),
            jax.ShapeDtypeStruct((n_streams, CONV_W - 1, conv_dim), F32),
            jax.ShapeDtypeStruct((n_streams, hp, d_state), F32),
        ],
        scratch_shapes=[
            pltpu.VMEM((SUBLANES + n_rows, conv_dim), F32),
            pltpu.VMEM((n_rows, conv_dim), F32),
            pltpu.VMEM((d_state, inner), F32),
            pltpu.VMEM((cps, LANES, LANES), F32),
            pltpu.VMEM((cps, 3 * CHUNK + 32, inner), F32),
            pltpu.VMEM((n_rows, inner), F32),
        ],
        compiler_params=_cparams(1),
        name=name,
    )(zs, xbc, dtr, hist, h0, conv_w, conv_b, dtb, alog, dskx, normw, emat, *extra_args)


def _band_bias_tables(ext_ref, bias_scr, n_heads):
    col = lax.broadcasted_iota(jnp.int32, (CHUNK, ATT_WIN), 1)
    band = BAND_PAST + CHUNK
    for h in range(n_heads):
        e = jnp.broadcast_to(ext_ref[h:h + 1, :], (CHUNK, ATT_WIN))
        even = pltpu.roll(e, ATT_WIN - (CHUNK - 1), 1, stride=1, stride_axis=0)
        odd = pltpu.roll(e, 1, 1, stride=1, stride_axis=0)
        bias_scr[0, h] = jnp.where(col < band, even, MASK_NEG)
        bias_scr[1, h] = jnp.where(col >= CHUNK, odd, MASK_NEG)


ATT_BLOCKS = BAND_PAST // ATT_TILE + 1


def _attn_prompt_kernel(*refs, n_heads):
    q_ref = refs[0]
    k_refs = refs[1:1 + ATT_BLOCKS]
    v_refs = refs[1 + ATT_BLOCKS:1 + 2 * ATT_BLOCKS]
    ext_ref, o_ref, bias_scr, s_scr, p_scr = refs[1 + 2 * ATT_BLOCKS:]
    i = pl.program_id(0)
    d = ATT_HEAD_DIM
    scale = d ** -0.5
    blk = ATT_TILE

    @pl.when(i == 0)
    def _():
        _band_bias_tables(ext_ref, bias_scr, n_heads)
        p_scr[...] = jnp.zeros_like(p_scr)

    past_mask = [jnp.where(i < ATT_BLOCKS - 1 - b, MASK_NEG, 0.0).astype(F32) for b in range(ATT_BLOCKS - 1)]

    for h in range(n_heads):
        hs = slice(h * d, (h + 1) * d)
        q_h = q_ref[:, hs]
        for b in range(ATT_BLOCKS):
            s = _dot_nt(q_h, k_refs[b][:, hs]) * scale
            s_scr[:, b * blk:(b + 1) * blk] = s + past_mask[b] if b < ATT_BLOCKS - 1 else s
        for t in range(ATT_TILE // CHUNK):
            rs = slice(t * CHUNK, (t + 1) * CHUNK)
            c0 = LANES * (t // 2)
            ws = slice(c0, c0 + ATT_WIN)
            s = s_scr[rs, ws] + bias_scr[t % 2, h]
            m = jnp.max(s, axis=-1, keepdims=True)
            p = jnp.exp(s - m)
            r = 1.0 / jnp.sum(p, axis=-1, keepdims=True)
            p_scr[rs, ws] = (p * r).astype(BF16)
        o = _dot(p_scr[:, 0:blk], v_refs[0][:, hs])
        for b in range(1, ATT_BLOCKS):
            o = o + _dot(p_scr[:, b * blk:(b + 1) * blk], v_refs[b][:, hs])
        o_ref[:, hs] = o.astype(o_ref.dtype)


def _attn_prompt(q, kb, vb, ext, *, n_tiles, n_heads):
    width = n_heads * ATT_HEAD_DIM
    assert BAND_PAST % ATT_TILE == 0 and ATT_TILE % (2 * CHUNK) == 0
    kern = functools.partial(_attn_prompt_kernel, n_heads=n_heads)
    window = [pl.BlockSpec((ATT_TILE, width), lambda i, back=back: (jnp.maximum(i - back, 0), 0))
              for back in range(ATT_BLOCKS - 1, -1, -1)]
    return pl.pallas_call(
        kern,
        grid=(n_tiles,),
        in_specs=[pl.BlockSpec((ATT_TILE, width), lambda i: (i, 0))] + window + window
                 + [pl.BlockSpec((n_heads, ATT_WIN), lambda i: (0, 0))],
        out_specs=pl.BlockSpec((ATT_TILE, width), lambda i: (i, 0)),
        out_shape=jax.ShapeDtypeStruct((q.shape[0], width), BF16),
        scratch_shapes=[
            pltpu.VMEM((2, n_heads, CHUNK, ATT_WIN), F32),
            pltpu.VMEM((ATT_TILE, ATT_BLOCKS * ATT_TILE), F32),
            pltpu.VMEM((ATT_TILE, ATT_BLOCKS * ATT_TILE), BF16),
        ],
        compiler_params=_cparams(1),
        name="attn_prompt",
    )(q, *([kb] * ATT_BLOCKS), *([vb] * ATT_BLOCKS), ext)


def _sublane_transpose8(t):
    sub = lax.broadcasted_iota(jnp.int32, (SUBLANES, LANES), 0)
    for k in (4, 2, 1):
        keep = (sub // k) % 2 == 0
        new = [None] * SUBLANES
        for i in range(SUBLANES):
            if (i // k) % 2 == 0:
                j = i + k
                new[i] = jnp.where(keep, t[i], pltpu.roll(t[j], k, 0))
                new[j] = jnp.where(keep, pltpu.roll(t[i], SUBLANES - k, 0), t[j])
        t = new
    return t


def _split_heads(c_ref, u, dst, n_heads):
    halves = n_heads // SUBLANES

    def body(g, carry):
        p0 = pl.multiple_of(g * SUBLANES, SUBLANES)
        for hh in range(halves):
            tiles = [c_ref[u, pl.ds(pl.multiple_of((p0 + p) * n_heads + hh * SUBLANES, SUBLANES), SUBLANES), :]
                     for p in range(SUBLANES)]
            for a, tile in enumerate(_sublane_transpose8(tiles)):
                dst[hh * SUBLANES + a, pl.ds(p0, SUBLANES), :] = tile
        return carry

    lax.fori_loop(0, BAND_PAST // SUBLANES, body, 0, unroll=2)


def _attn_sample_kernel(q_ref, kn_ref, vn_ref, ck_ref, cv_ref, ext_ref, att_in_ref, o_ref,
                        bias_scr, kh_scr, vh_scr, s_scr, p_scr, *, n_heads):
    del att_in_ref
    d = ATT_HEAD_DIM
    scale = d ** -0.5

    @pl.when(pl.program_id(0) == 0)
    def _():
        _band_bias_tables(ext_ref, bias_scr, n_heads)

    for u in range(ck_ref.shape[0]):
        qr = slice(u * CHUNK, (u + 1) * CHUNK)
        _split_heads(ck_ref, u, kh_scr, n_heads)
        _split_heads(cv_ref, u, vh_scr, n_heads)
        band = BAND_PAST + CHUNK
        for h in range(n_heads):
            hs = slice(h * d, (h + 1) * d)
            q_h = q_ref[qr, hs]
            s_scr[h, :, 0:BAND_PAST] = (_dot_nt(q_h, kh_scr[h].astype(BF16)) * scale
                                        + bias_scr[0, h, :, 0:BAND_PAST])
            s_scr[h, :, BAND_PAST:band] = (_dot_nt(q_h, kn_ref[qr, hs]) * scale
                                           + bias_scr[0, h, :, BAND_PAST:band])
        for h in range(n_heads):
            s = s_scr[h, :, 0:band]
            p = jnp.exp(s - jnp.max(s, axis=-1, keepdims=True))
            r = 1.0 / jnp.sum(p, axis=-1, keepdims=True)
            p_scr[h, :, 0:band] = (p * r).astype(BF16)
        for h in range(n_heads):
            hs = slice(h * d, (h + 1) * d)
            o = (_dot(p_scr[h, :, 0:BAND_PAST], vh_scr[h].astype(BF16))
                 + _dot(p_scr[h, :, BAND_PAST:band], vn_ref[qr, hs]))
            o_ref[qr, hs] = o.astype(o_ref.dtype)


def _attn_sample(q, kb, vb, cache_k, cache_v, ext, att_into, *, row_blk0, n_sample, n_heads, spb):
    width = n_heads * ATT_HEAD_DIM
    assert n_sample % spb == 0 and row_blk0 % spb == 0
    kern = functools.partial(_attn_sample_kernel, n_heads=n_heads)
    cache_spec = pl.BlockSpec((spb, BAND_PAST * n_heads, ATT_HEAD_DIM), lambda s: (s, 0, 0))
    row_spec = pl.BlockSpec((spb * CHUNK, width), lambda s: (row_blk0 // spb + s, 0))
    return pl.pallas_call(
        kern,
        grid=(n_sample // spb,),
        in_specs=[row_spec, row_spec, row_spec, cache_spec, cache_spec,
                  pl.BlockSpec((n_heads, ATT_WIN), lambda s: (0, 0)),
                  pl.BlockSpec(memory_space=pl.ANY)],
        out_specs=row_spec,
        out_shape=jax.ShapeDtypeStruct(att_into.shape, BF16),
        input_output_aliases={6: 0},
        scratch_shapes=[pltpu.VMEM((2, n_heads, CHUNK, ATT_WIN), F32),
                        pltpu.VMEM((n_heads, BAND_PAST, ATT_HEAD_DIM), F32),
                        pltpu.VMEM((n_heads, BAND_PAST, ATT_HEAD_DIM), F32),
                        pltpu.VMEM((n_heads, CHUNK, ATT_WIN), F32),
                        pltpu.VMEM((n_heads, CHUNK, ATT_WIN), BF16)],
        compiler_params=_cparams(1),
        name="attn_sample",
    )(q, kb, vb, cache_k, cache_v, ext, att_into)


def _mix_kernel(s_ref, a_ref, w1_ref, w2_ref, g1_ref, g2_ref, o_ref):
    a1 = _dot(s_ref[...], w1_ref[...])
    a2 = _dot(a_ref[...], w2_ref[...])
    g1 = _sigmoid(g1_ref[...].astype(F32))
    g2 = _sigmoid(g2_ref[...].astype(F32))
    o_ref[...] = (g1 * a1 + g2 * a2).astype(o_ref.dtype)


def _mix(ssd_y, att, w1, w2, gates_arr, *, gate_col0, tm, tn):
    m, k1 = ssd_y.shape
    k2 = att.shape[1]
    n = w1.shape[1]
    assert att.shape[0] == m and m % tm == 0
    g1_blk = gate_col0 // tn
    g2_blk = (gate_col0 + n) // tn
    return pl.pallas_call(
        _mix_kernel,
        grid=(n // tn, m // tm),
        in_specs=[
            pl.BlockSpec((tm, k1), lambda j, i: (i, 0)),
            pl.BlockSpec((tm, k2), lambda j, i: (i, 0)),
            pl.BlockSpec((k1, tn), lambda j, i: (0, j), pipeline_mode=pl.Buffered(1)),
            pl.BlockSpec((k2, tn), lambda j, i: (0, j), pipeline_mode=pl.Buffered(1)),
            pl.BlockSpec((tm, tn), lambda j, i: (i, g1_blk + j)),
            pl.BlockSpec((tm, tn), lambda j, i: (i, g2_blk + j)),
        ],
        out_specs=pl.BlockSpec((tm, tn), lambda j, i: (i, j)),
        out_shape=jax.ShapeDtypeStruct((m, n), BF16),
        compiler_params=_cparams(2),
        name="branch_mix",
    )(ssd_y, att, w1, w2, gates_arr, gates_arr)


def _layer_norm(t, g, b):
    mu = jnp.mean(t, axis=-1, keepdims=True)
    tc = t - mu
    var = jnp.mean(tc * tc, axis=-1, keepdims=True)
    return tc * lax.rsqrt(var + LN_EPS) * g + b


LN_SUB_ROWS = 256


def _proj_ln_kernel(*refs, alpha, n_k, n_prompt_tiles, split_resid, n_out):
    a_ref, w_ref = refs[0], refs[1]
    pos = 2
    r_refs = refs[pos:pos + (2 if split_resid else 1)]
    pos += len(r_refs)
    g_ref, b_ref = refs[pos], refs[pos + 1]
    pos += 2
    o_refs = refs[pos:pos + n_out]
    acc_ref = refs[pos + n_out] if n_k > 1 else None
    i = pl.program_id(0)
    kk = pl.program_id(1)
    tm = a_ref.shape[0]

    def resid_rows(rs):
        if split_resid:
            return jnp.where(i < n_prompt_tiles, r_refs[0][rs, :], r_refs[1][rs, :])
        return r_refs[0][rs, :]

    def finish(first):
        for r in range(tm // LN_SUB_ROWS):
            rs = slice(r * LN_SUB_ROWS, (r + 1) * LN_SUB_ROWS)
            base = alpha * resid_rows(rs) if first else acc_ref[rs, :]
            y = _layer_norm(base + _dot(a_ref[rs, :], w_ref[...]), g_ref[...], b_ref[...])
            for o_ref in o_refs:
                o_ref[rs, :] = y.astype(o_ref.dtype)

    if n_k == 1:
        finish(True)
        return

    @pl.when(kk == 0)
    def _():
        acc_ref[...] = alpha * resid_rows(slice(None)) + _dot(a_ref[...], w_ref[...])

    if n_k > 2:
        @pl.when(jnp.logical_and(kk > 0, kk < n_k - 1))
        def _():
            acc_ref[...] += _dot(a_ref[...], w_ref[...])

    @pl.when(kk == n_k - 1)
    def _():
        finish(False)


def _proj_ln(a, w, resid, gamma, beta, out_dtypes, *, alpha, tm, n_k, row0, n_rows, m_prompt=None, name):
    k = a.shape[1]
    n = w.shape[1]
    tk = k // n_k
    assert tk * n_k == k and n_rows % tm == 0 and row0 % tm == 0 and tm % LN_SUB_ROWS == 0
    blk0 = row0 // tm
    split_resid = isinstance(resid, tuple)
    row_map = lambda i, kk: (blk0 + i, 0)
    w_mode = dict(pipeline_mode=pl.Buffered(1)) if n_k == 1 else {}
    in_specs = [pl.BlockSpec((tm, tk), lambda i, kk: (blk0 + i, kk)),
                pl.BlockSpec((tk, n), lambda i, kk: (kk, 0), **w_mode)]
    args = [a, w]
    npt = 0
    if split_resid:
        assert row0 == 0 and m_prompt % tm == 0
        npt = m_prompt // tm
        in_specs += [pl.BlockSpec((tm, n), lambda i, kk: (jnp.minimum(i, npt - 1), 0)),
                     pl.BlockSpec((tm, n), lambda i, kk: (jnp.maximum(i - npt, 0), 0))]
        args += list(resid)
    else:
        in_specs.append(pl.BlockSpec((tm, n), row_map))
        args.append(resid)
    in_specs += [pl.BlockSpec((1, n), lambda i, kk: (0, 0))] * 2
    args += [gamma, beta]
    kern = functools.partial(_proj_ln_kernel, alpha=alpha, n_k=n_k, n_prompt_tiles=npt,
                             split_resid=split_resid, n_out=len(out_dtypes))
    return pl.pallas_call(
        kern,
        grid=(n_rows // tm, n_k),
        in_specs=in_specs,
        out_specs=[pl.BlockSpec((tm, n), lambda i, kk: (i, 0)) for _ in out_dtypes],
        out_shape=[jax.ShapeDtypeStruct((n_rows, n), dt) for dt in out_dtypes],
        scratch_shapes=[pltpu.VMEM((tm, n), F32)] if n_k > 1 else [],
        compiler_params=_cparams(2),
        name=name,
    )(*args)


def _glu_kernel(*refs, n_side):
    h_ref, wg_ref, wu_ref = refs[:3]
    side_in = refs[3:3 + n_side]
    o_ref = refs[3 + n_side]
    side_out = refs[4 + n_side:4 + 2 * n_side]
    wg_sc, wu_sc = refs[4 + 2 * n_side:]

    @pl.when(pl.program_id(1) == 0)
    def _():
        _cast_weight_tile(wg_ref, wg_sc)
        _cast_weight_tile(wu_ref, wu_sc)

    for s_in, s_out in zip(side_in, side_out):
        s_out[...] = s_in[...].astype(BF16)

    tm = h_ref.shape[0]
    sub = tm // GLU_SUB_BLOCKS
    for r in range(GLU_SUB_BLOCKS):
        rs = slice(r * sub, (r + 1) * sub)
        h = h_ref[rs, :]
        g = _dot(h, wg_sc[...])
        u = _dot(h, wu_sc[...])
        o_ref[rs, :] = (g * jax.nn.sigmoid(g) * u).astype(o_ref.dtype)


def _glu(h, w_gate_up, *, tm, tf, side=()):
    m, k = h.shape
    d_ff = w_gate_up.shape[1] // 2
    nf = d_ff // tf
    assert nf * tf == d_ff
    side_in, side_out, side_shape = _side_cast_specs(side, m // tm)
    assert all(w.shape[0] // rows <= nf * (m // tm) for w, rows in side)
    return pl.pallas_call(
        functools.partial(_glu_kernel, n_side=len(side)),
        grid=(nf, m // tm),
        in_specs=[
            pl.BlockSpec((tm, k), lambda j, i: (i, 0)),
            pl.BlockSpec((k, tf), lambda j, i: (0, j)),
            pl.BlockSpec((k, tf), lambda j, i: (0, nf + j)),
        ] + side_in,
        out_specs=[pl.BlockSpec((tm, tf), lambda j, i: (i, j))] + side_out,
        out_shape=[jax.ShapeDtypeStruct((m, d_ff), BF16)] + side_shape,
        scratch_shapes=[pltpu.VMEM((k, tf), BF16), pltpu.VMEM((k, tf), BF16)],
        compiler_params=_cparams(2),
        name="ffn_glu",
    )(h, w_gate_up, w_gate_up, *[w for w, _ in side])


def kernel(x_prompt, x_sample, cache_k, cache_v, state_conv, state_ssm, w_in, conv_w, conv_b, dt_bias,
           a_log, d_skip, ssd_norm_w, rel_bias, w_ssd_out, w_att_out, w_o, ln1_g, ln1_b, w_gate_up,
           w_down, ln2_g, ln2_b):
    depth = w_in.shape[0]
    assert depth == 1
    batch, seq, d_model = x_prompt.shape
    n_sample, dec_seq, _ = x_sample.shape
    assert batch == 1 and dec_seq == CHUNK and seq % ATT_TILE == 0
    n_ssd_heads = dt_bias.shape[1]
    inner = n_ssd_heads * SSD_HEADDIM
    conv_dim = conv_w.shape[2]
    n_groups = SSD_GROUPS
    d_state = SSD_STATE
    assert conv_dim == inner + 2 * n_groups * d_state
    n_heads = rel_bias.shape[1]
    width = n_heads * ATT_HEAD_DIM
    assert cache_k.shape[2] == BAND_PAST
    alpha = (2.0 * depth) ** 0.25
    keep = min(BAND_PAST, seq)

    m_p = batch * seq
    m_s = n_sample * dec_seq
    xp2 = x_prompt.reshape(m_p, d_model)
    xs2 = x_sample.reshape(m_s, d_model)

    wt = w_in[0].T
    tn = 1024
    o_xbc = inner
    o_dt = o_xbc + conv_dim
    o_q = o_dt + n_ssd_heads
    shift = o_q % tn
    assert o_dt % tn == 0 and inner % tn == 0 and conv_dim % tn == 0 and width % tn == 0
    q_blk = o_dt // tn
    xb, dtr = _xcast_dt(xp2, xs2, wt, dt_col=o_dt, n_dt=n_ssd_heads, tm=1024)
    zs, w_att_b, w_o_b = _matmul_w32(xb, wt, lambda j: j, inner // tn, [BF16], tm=1536, tn=tn, silu=True,
                                     side=[(w_att_out[0], SIDE_CAST_ROWS), (w_o[0], SIDE_CAST_ROWS)],
                                     name="in_proj_z")
    (xbc,) = _matmul_w32(xb, wt, lambda j: inner // tn + j, conv_dim // tn, [F32], tm=1536, tn=tn,
                         name="in_proj_xbc")
    wpt = width // tn
    qg, w_ssd_b = _matmul_w32(xb, wt, lambda j: jnp.where(j < wpt, q_blk + j, q_blk + 2 * wpt + j),
                              wpt + 2 * d_model // tn, [BF16], tm=1536, tn=tn, shift=shift,
                              side=[(w_ssd_out[0], SIDE_CAST_ROWS)], name="in_proj_qg")
    kb, k_p, k_s = _matmul_w32(xb, wt, lambda j: q_blk + wpt + j, wpt, [BF16], tm=1024, tn=tn, shift=shift,
                               head_out=(keep, m_s), name="in_proj_k")
    vb, v_p, v_s = _matmul_w32(xb, wt, lambda j: q_blk + 2 * wpt + j, wpt, [BF16], tm=1024, tn=tn,
                               shift=shift, head_out=(keep, m_s), name="in_proj_v")

    pad_h = (0, LANES - n_ssd_heads)
    dtb = jnp.pad(dt_bias[0], pad_h).reshape(1, LANES)
    alog = jnp.pad(a_log[0], pad_h).reshape(1, LANES)
    dskx = jnp.repeat(d_skip[0], SSD_HEADDIM).reshape(1, inner)
    emat = (jnp.arange(LANES)[:, None] == (jnp.arange(inner)[None, :] // SSD_HEADDIM)).astype(BF16)
    ssd_args = (zs, xbc, dtr, state_conv[0], state_ssm[0].reshape(n_sample, inner, d_state),
                conv_w[0], conv_b[0].reshape(1, conv_dim), dtb, alog, dskx, ssd_norm_w[0].reshape(1, inner), emat)
    ssd_dims = dict(inner=inner, n_groups=n_groups, d_state=d_state)
    ssd_y, conv_p, h_p = _ssd(*ssd_args, row0=0, n_streams=batch, chunks_per_stream=seq // CHUNK,
                              cps=SSD_PROMPT_CPS, per_stream=False, name="ssd_scan_prompt", **ssd_dims)
    ssd_y, conv_s, h_s = _ssd(*ssd_args, row0=m_p, n_streams=n_sample, chunks_per_stream=1, cps=1,
                              per_stream=True, name="ssd_scan_sample", y_into=ssd_y, **ssd_dims)

    rb = rel_bias[0]
    band = BAND_PAST + CHUNK
    n_edge = band - REL_CLIP
    n_rev = ATT_WIN - n_edge
    rev = rb[:, ::-1][:, 1:1 + n_rev]
    ext = jnp.concatenate([jnp.broadcast_to(rb[:, 2 * REL_CLIP:], (n_heads, n_edge)), rev], axis=1)
    ext = jnp.pad(ext, ((0, 0), (0, ATT_WIN - ext.shape[1])))
    att = _attn_prompt(qg, kb, vb, ext, n_tiles=m_p // ATT_TILE, n_heads=n_heads)
    cache_rows = (n_sample, BAND_PAST * n_heads, ATT_HEAD_DIM)
    att = _attn_sample(qg, kb, vb, cache_k[0].reshape(cache_rows), cache_v[0].reshape(cache_rows), ext, att,
                       row_blk0=m_p // CHUNK, n_sample=n_sample, n_heads=n_heads, spb=ATT_SAMPLE_SPB)

    mixed = _mix(ssd_y, att, w_ssd_b, w_att_b, qg, gate_col0=width, tm=768, tn=1024)
    g1 = ln1_g[0].reshape(1, d_model)
    b1 = ln1_b[0].reshape(1, d_model)
    g2 = ln2_g[0].reshape(1, d_model)
    b2 = ln2_b[0].reshape(1, d_model)
    h_f, h_b = _proj_ln(mixed, w_o_b, (xp2, xs2), g1, b1, [F32, BF16], alpha=alpha, tm=512,
                        n_k=1, row0=0, n_rows=m_p + m_s, m_prompt=m_p, name="wo_ln")
    act, w_dn = _glu(h_b, w_gate_up[0], tm=1536, tf=512, side=[(w_down[0], DOWN_CAST_ROWS)])
    (y_p,) = _proj_ln(act, w_dn, h_f, g2, b2, [F32], alpha=alpha, tm=512, n_k=2, row0=0, n_rows=m_p,
                      name="down_ln_prompt")
    (y_s,) = _proj_ln(act, w_dn, h_f, g2, b2, [F32], alpha=alpha, tm=512, n_k=2, row0=m_p, n_rows=m_s,
                      name="down_ln_sample")

    return (y_p.reshape(batch, seq, d_model),
            y_s.reshape(n_sample, dec_seq, d_model),
            conv_p.reshape(1, batch, CONV_W - 1, conv_dim),
            h_p.reshape(1, batch, n_ssd_heads, SSD_HEADDIM, d_state),
            k_p.reshape(1, batch, keep, n_heads, ATT_HEAD_DIM),
            v_p.reshape(1, batch, keep, n_heads, ATT_HEAD_DIM),
            conv_s.reshape(1, n_sample, CONV_W - 1, conv_dim),
            h_s.reshape(1, n_sample, n_ssd_heads, SSD_HEADDIM, d_state),
            k_s.reshape(1, n_sample, dec_seq, n_heads, ATT_HEAD_DIM),
            v_s.reshape(1, n_sample, dec_seq, n_heads, ATT_HEAD_DIM))
```

```python
import functools

import jax
import jax.numpy as jnp
from jax import lax
from jax.experimental import pallas as pl
from jax.experimental.pallas import tpu as pltpu

F32 = jnp.float32
BF16 = jnp.bfloat16

CHUNK = 64
SSD_HEADDIM = 64
SSD_GROUPS = 8
SSD_STATE = 128
CONV_W = 4
ATT_HEAD_DIM = 128
BAND_CHUNKS = 8
BAND_PAST = BAND_CHUNKS * CHUNK
REL_CLIP = 128
LN_EPS = 1e-5
RMS_EPS = 1e-5
MASK_NEG = -1e30

LANES = 128
SUBLANES = 8
VMEM_LIMIT_BYTES = 56 * 1024 * 1024

SSD_PROMPT_CPS = 4
SIDE_CAST_ROWS = 128
DOWN_CAST_ROWS = 176
ATT_TILE = 512
ATT_SAMPLE_SPB = 2
GLU_SUB_BLOCKS = 2
SILU_SUB_BLOCKS = 2
ATT_WIN = 5 * LANES


def _cparams(n_axes):
    return pltpu.CompilerParams(
        dimension_semantics=("arbitrary",) * n_axes,
        vmem_limit_bytes=VMEM_LIMIT_BYTES,
    )


def _sigmoid(x):
    return 0.5 * jnp.tanh(0.5 * x) + 0.5


def _silu(x):
    h = 0.5 * x
    return h + h * jnp.tanh(h)


def _dot(a, b):
    return jnp.dot(a, b, preferred_element_type=F32)


def _dot_nt(a, b):
    return lax.dot_general(a, b, (((1,), (1,)), ((), ())), preferred_element_type=F32)


def _dot_tn(a, b):
    return lax.dot_general(a, b, (((0,), (0,)), ((), ())), preferred_element_type=F32)


def _xcast_dt_kernel(xp_ref, xs_ref, wd_ref, xb_ref, dt_ref, *, n_prompt_tiles, n_dt):
    i = pl.program_id(0)
    x = jnp.where(i < n_prompt_tiles, xp_ref[...], xs_ref[...]).astype(BF16)
    xb_ref[...] = x
    acc = _dot_nt(x, wd_ref[...].astype(BF16))
    lane = lax.broadcasted_iota(jnp.int32, acc.shape, 1)
    dt_ref[...] = jnp.where(lane < n_dt, acc, 0.0)


def _xcast_dt(xp, xs, wt, *, dt_col, n_dt, tm):
    m_p, k = xp.shape
    m_s = xs.shape[0]
    npt = m_p // tm
    n_tiles = npt + m_s // tm
    assert dt_col % LANES == 0 and n_dt <= LANES
    kern = functools.partial(_xcast_dt_kernel, n_prompt_tiles=npt, n_dt=n_dt)
    return pl.pallas_call(
        kern,
        grid=(n_tiles,),
        in_specs=[
            pl.BlockSpec((tm, k), lambda i: (jnp.minimum(i, npt - 1), 0)),
            pl.BlockSpec((tm, k), lambda i: (jnp.maximum(i - npt, 0), 0)),
            pl.BlockSpec((LANES, k), lambda i: (dt_col // LANES, 0)),
        ],
        out_specs=[pl.BlockSpec((tm, k), lambda i: (i, 0)),
                   pl.BlockSpec((tm, LANES), lambda i: (i, 0))],
        out_shape=[jax.ShapeDtypeStruct((m_p + m_s, k), BF16),
                   jax.ShapeDtypeStruct((m_p + m_s, LANES), F32)],
        compiler_params=_cparams(1),
        name="xcast_dt",
    )(xp, xs, wt)


CAST_ROWS = 256


def _cast_rows(src_ref, src_row0, dst_ref, dst_row0, n_rows):
    done = 0
    while done < n_rows:
        step = min(CAST_ROWS, n_rows - done)
        dst_ref[dst_row0 + done:dst_row0 + done + step, :] = (
            src_ref[src_row0 + done:src_row0 + done + step, :].astype(BF16))
        done += step


def _cast_weight_tile(w_ref, wsc):
    _cast_rows(w_ref, 0, wsc, 0, wsc.shape[0])


def _side_cast_specs(side, n_inner):
    in_specs, out_specs, out_shape = [], [], []
    for w, rows in side:
        n_blk = w.shape[0] // rows
        assert n_blk * rows == w.shape[0] and rows % 16 == 0
        idx = lambda j, i, n_blk=n_blk: (jnp.minimum(j * n_inner + i, n_blk - 1), 0)
        in_specs.append(pl.BlockSpec((rows, w.shape[1]), idx))
        out_specs.append(pl.BlockSpec((rows, w.shape[1]), idx))
        out_shape.append(jax.ShapeDtypeStruct(w.shape, BF16))
    return in_specs, out_specs, out_shape


def _mm_w32_kernel(*refs, shift, n_out, head_rows, silu, n_side):
    x_ref, wa_ref = refs[0], refs[1]
    pos = 2
    wb_ref = None
    if shift:
        wb_ref = refs[pos]
        pos += 1
    side_in = refs[pos:pos + n_side]
    pos += n_side
    o_refs = refs[pos:pos + n_out]
    pos += n_out
    hp_ref = hs_ref = None
    if head_rows is not None:
        hp_ref, hs_ref = refs[pos], refs[pos + 1]
        pos += 2
    side_out = refs[pos:pos + n_side]
    pos += n_side
    wsc = refs[pos]
    i = pl.program_id(1)

    @pl.when(i == 0)
    def _():
        tn = wsc.shape[0]
        _cast_rows(wa_ref, shift, wsc, 0, tn - shift)
        if shift:
            _cast_rows(wb_ref, 0, wsc, tn - shift, shift)

    for s_in, s_out in zip(side_in, side_out):
        s_out[...] = s_in[...].astype(BF16)

    if silu:
        sub = x_ref.shape[0] // SILU_SUB_BLOCKS
        for r in range(SILU_SUB_BLOCKS):
            rs = slice(r * sub, (r + 1) * sub)
            res = _silu(_dot_nt(x_ref[rs, :], wsc[...]))
            for o_ref in o_refs:
                o_ref[rs, :] = res.astype(o_ref.dtype)
        return

    acc = _dot_nt(x_ref[...], wsc[...])
    for o_ref in o_refs:
        o_ref[...] = acc.astype(o_ref.dtype)

    if head_rows is not None:
        (tile_p, row0_p), (tile_s, row0_s) = head_rows

        @pl.when(i == tile_p)
        def _():
            hp_ref[...] = acc[row0_p:row0_p + hp_ref.shape[0], :]

        @pl.when(i == tile_s)
        def _():
            hs_ref[...] = acc[row0_s:row0_s + hs_ref.shape[0], :]


def _matmul_w32(x, wt, col_blk, n_col_tiles, out_dtypes, *, tm, tn, shift=0, head_out=None, silu=False,
                side=(), name):
    m, k = x.shape
    assert m % tm == 0 and tn % LANES == 0 and shift % 16 == 0 and shift < tn
    assert not (silu and head_out is not None) and tm % (16 * SILU_SUB_BLOCKS) == 0
    in_specs = [pl.BlockSpec((tm, k), lambda j, i: (i, 0)),
                pl.BlockSpec((tn, k), lambda j, i: (col_blk(j), 0))]
    args = [x, wt]
    if shift:
        assert tn % shift == 0
        per = tn // shift
        in_specs.append(pl.BlockSpec((shift, k), lambda j, i: ((col_blk(j) + 1) * per, 0)))
        args.append(wt)
    side_in, side_out, side_shape = _side_cast_specs(side, m // tm)
    assert all(w.shape[0] // rows <= n_col_tiles * (m // tm) for w, rows in side)
    in_specs += side_in
    args += [w for w, _ in side]
    n = n_col_tiles * tn
    out_specs = [pl.BlockSpec((tm, tn), lambda j, i: (i, j)) for _ in out_dtypes]
    out_shape = [jax.ShapeDtypeStruct((m, n), dt) for dt in out_dtypes]
    head_rows = None
    if head_out is not None:
        n_keep, m_s = head_out
        m_p = m - m_s
        head_rows = (divmod(m_p - n_keep, tm), divmod(m_p, tm))
        assert head_rows[0][1] + n_keep <= tm and head_rows[1][1] + m_s <= tm
        out_specs += [pl.BlockSpec((n_keep, tn), lambda j, i: (0, j)),
                      pl.BlockSpec((m_s, tn), lambda j, i: (0, j))]
        out_shape += [jax.ShapeDtypeStruct((n_keep, n), F32), jax.ShapeDtypeStruct((m_s, n), F32)]
    out_specs += side_out
    out_shape += side_shape
    kern = functools.partial(_mm_w32_kernel, shift=shift, n_out=len(out_dtypes), head_rows=head_rows,
                             silu=silu, n_side=len(side))
    return pl.pallas_call(
        kern,
        grid=(n_col_tiles, m // tm),
        in_specs=in_specs,
        out_specs=out_specs,
        out_shape=out_shape,
        scratch_shapes=[pltpu.VMEM((tn, k), BF16)],
        compiler_params=_cparams(2),
        name=name,
    )(*args)


def _softplus(x):
    return jnp.maximum(x, 0.0) + jnp.log1p(jnp.exp(-jnp.abs(x)))


def _ssd_chunk(u, zs_ref, dt_ref, cw_ref, cb_ref, dtb_ref, alog_ref, dsk_ref, nw_ref, e_ref, y_ref,
               xpad, act, ht, cst, ex, ysc, *, inner, n_groups, d_state):
    r0 = u * CHUNK
    rows = slice(r0, r0 + CHUNK)
    gw = inner // n_groups
    conv_dim = inner + 2 * n_groups * d_state

    for j in range(conv_dim // LANES):
        sl = slice(j * LANES, (j + 1) * LANES)
        blk = xpad[r0:r0 + SUBLANES + CHUNK, sl]
        prev = pltpu.roll(blk, 1, 0)
        near = blk * cw_ref[3:4, sl] + prev * cw_ref[2:3, sl]
        far = pltpu.roll(blk * cw_ref[1:2, sl] + prev * cw_ref[0:1, sl], 2, 0)
        a = cb_ref[:, sl] + (near + far)[SUBLANES:, :]
        act[rows, sl] = _silu(a)

    dt = _softplus(dt_ref[rows, :] + dtb_ref[...])
    a_neg = -jnp.exp(alog_ref[...])
    da = dt * a_neg
    ri = lax.broadcasted_iota(jnp.int32, (CHUNK, CHUNK), 0)
    ci = lax.broadcasted_iota(jnp.int32, (CHUNK, CHUNK), 1)
    tri = (ri >= ci).astype(F32)
    cs = jnp.dot(tri, da, precision=lax.Precision.HIGHEST, preferred_element_type=F32)
    ecs = jnp.exp(cs)
    cs_last = cs[CHUNK - 1:CHUNK, :]
    dout = jnp.exp(cs_last - cs) * dt
    cd = jnp.exp(cs_last)
    cd_hi = cd.astype(BF16)
    cd_lo = (cd - cd_hi.astype(F32)).astype(BF16)
    lhs = jnp.concatenate(
        [dt.astype(BF16), ecs.astype(BF16), dout.astype(BF16),
         jnp.broadcast_to(cd_hi, (16, LANES)), jnp.broadcast_to(cd_lo, (16, LANES))], axis=0)
    ex[u] = _dot(lhs, e_ref[...])

    cst[u] = jnp.concatenate([cs, jnp.zeros((LANES - CHUNK, LANES), F32)], axis=0).T
    lane = lax.broadcasted_iota(jnp.int32, (CHUNK, LANES), 1)
    lo_half = lane < CHUNK
    n_pairs = inner // LANES
    ev = cst[u, pl.ds(0, n_pairs, stride=2), :]
    od = cst[u, pl.ds(1, n_pairs, stride=2), :]
    lane_p = lax.broadcasted_iota(jnp.int32, (n_pairs, LANES), 1)
    cst2 = jnp.where(lane_p < CHUNK, ev, pltpu.roll(od, CHUNK, 1))
    row = lax.broadcasted_iota(jnp.int32, (CHUNK, LANES), 0)
    causal2 = row >= jnp.where(lo_half, lane, lane - CHUNK)

    pairs_per_group = gw // LANES
    for g in range(n_groups):
        gs = slice(g * gw, (g + 1) * gw)
        bm_g = act[rows, inner + g * d_state: inner + (g + 1) * d_state].astype(BF16)
        cm_g = act[rows, inner + (n_groups + g) * d_state: inner + (n_groups + g + 1) * d_state].astype(BF16)
        cb2 = _dot_nt(cm_g, jnp.concatenate([bm_g, bm_g], axis=0))
        h_g = ht[:, gs]
        yoff = _dot(cm_g, h_g.astype(BF16))
        for p in range(pairs_per_group):
            k = g * pairs_per_group + p
            ls = slice(k * LANES, (k + 1) * LANES)
            csc = jnp.where(lo_half,
                            jnp.broadcast_to(cs[:, 2 * k:2 * k + 1], (CHUNK, LANES)),
                            jnp.broadcast_to(cs[:, 2 * k + 1:2 * k + 2], (CHUNK, LANES)))
            diff = csc - cst2[k:k + 1, :]
            dec = jnp.exp(jnp.where(causal2, diff, MASK_NEG))
            m2 = (cb2 * dec).astype(BF16)
            xs_p = act[rows, ls]
            xdt = xs_p * ex[u, 0:CHUNK, ls]
            x2 = jnp.concatenate([jnp.where(lo_half, xdt, 0.0), jnp.where(lo_half, 0.0, xdt)],
                                 axis=0).astype(BF16)
            yd = _dot(m2, x2)
            ysc[rows, ls] = (yd + yoff[:, p * LANES:(p + 1) * LANES] * ex[u, CHUNK:2 * CHUNK, ls]
                             + dsk_ref[:, ls] * xs_p)
        xw = (act[rows, gs] * ex[u, 2 * CHUNK:3 * CHUNK, gs]).astype(BF16)
        st = _dot_tn(bm_g, xw)
        cdx = ex[u, 3 * CHUNK:3 * CHUNK + 1, gs] + ex[u, 3 * CHUNK + 16:3 * CHUNK + 17, gs]
        ht[:, gs] = h_g * cdx + st

    for g in range(n_groups):
        gs = slice(g * gw, (g + 1) * gw)
        v = ysc[rows, gs] * zs_ref[rows, gs].astype(F32)
        ms = jnp.mean(v * v, axis=-1, keepdims=True)
        y_ref[rows, gs] = (v * lax.rsqrt(ms + RMS_EPS) * nw_ref[:, gs]).astype(y_ref.dtype)


def _ssd_kernel(*refs, cps, per_stream, aliased, inner, n_groups, d_state):
    (zs_ref, xbc_ref, dt_ref, hist_ref, h0_ref, cw_ref, cb_ref, dtb_ref, alog_ref,
     dsk_ref, nw_ref, e_ref) = refs[:12]
    y_ref, conv_ref, ho_ref, xpad, act, ht, cst, ex, ysc = refs[12 + int(aliased):]
    c = pl.program_id(0)
    n_rows = cps * CHUNK
    conv_dim = inner + 2 * n_groups * d_state

    @pl.when(c == 0)
    def _():
        xpad[0:SUBLANES, :] = jnp.zeros((SUBLANES, conv_dim), F32)
        if not per_stream:
            ht[...] = jnp.zeros_like(ht)

    if per_stream:
        xpad[5:8, :] = hist_ref[0]
        ht[...] = h0_ref[0].T

    xpad[SUBLANES:SUBLANES + n_rows, :] = xbc_ref[...]
    conv_ref[0] = xbc_ref[n_rows - (CONV_W - 1):n_rows, :]

    for u in range(cps):
        _ssd_chunk(u, zs_ref, dt_ref, cw_ref, cb_ref, dtb_ref, alog_ref, dsk_ref, nw_ref, e_ref, y_ref,
                   xpad, act, ht, cst, ex, ysc, inner=inner, n_groups=n_groups, d_state=d_state)
    xpad[5:8, :] = xpad[5 + n_rows:8 + n_rows, :]

    if per_stream:
        ho_ref[0] = ht[...].T
    else:
        @pl.when(c == pl.num_programs(0) - 1)
        def _():
            ho_ref[0] = ht[...].T


def _ssd(zs, xbc, dtr, hist, h0, conv_w, conv_b, dtb, alog, dskx, normw, emat, *,
         row0, n_streams, chunks_per_stream, cps, per_stream, inner, n_groups, d_state, name, y_into=None):
    conv_dim = xbc.shape[1]
    hp = inner
    n_rows = cps * CHUNK
    assert chunks_per_stream % cps == 0 and row0 % n_rows == 0
    assert per_stream == (chunks_per_stream == cps) and (per_stream or n_streams == 1)
    steps_per_stream = chunks_per_stream // cps
    blk0 = row0 // n_rows
    stream = (lambda c: c) if per_stream else (lambda c: 0)
    aliased = y_into is not None

    kern = functools.partial(_ssd_kernel, cps=cps, per_stream=per_stream, aliased=aliased, inner=inner,
                             n_groups=n_groups, d_state=d_state)
    const = lambda c: (0, 0)
    row_map = lambda c: (blk0 + c, 0)
    extra_specs = [pl.BlockSpec(memory_space=pl.ANY)] if aliased else []
    extra_args = [y_into] if aliased else []
    return pl.pallas_call(
        kern,
        grid=(n_streams * steps_per_stream,),
        input_output_aliases={12: 0} if aliased else {},
        in_specs=[
            pl.BlockSpec((n_rows, inner), row_map),
            pl.BlockSpec((n_rows, conv_dim), row_map),
            pl.BlockSpec((n_rows, LANES), row_map),
            pl.BlockSpec((1, CONV_W - 1, conv_dim), lambda c: (stream(c), 0, 0)),
            pl.BlockSpec((1, hp, d_state), lambda c: (stream(c), 0, 0)),
            pl.BlockSpec((CONV_W, conv_dim), const),
            pl.BlockSpec((1, conv_dim), const),
            pl.BlockSpec((1, LANES), const),
            pl.BlockSpec((1, LANES), const),
            pl.BlockSpec((1, inner), const),
            pl.BlockSpec((1, inner), const),
            pl.BlockSpec((LANES, inner), const),
        ] + extra_specs,
        out_specs=[
            pl.BlockSpec((n_rows, inner), row_map),
            pl.BlockSpec((1, CONV_W - 1, conv_dim), lambda c: (stream(c), 0, 0)),
            pl.BlockSpec((1, hp, d_state), lambda c: (stream(c), 0, 0)),
        ],
        out_shape=[
            jax.ShapeDtypeStruct((zs.shape[0], inner), BF16),
            jax.ShapeDtypeStruct((n_streams, CONV_W - 1, conv_dim), F32),
            jax.ShapeDtypeStruct((n_streams, hp, d_state), F32),
        ],
        scratch_shapes=[
            pltpu.VMEM((SUBLANES + n_rows, conv_dim), F32),
            pltpu.VMEM((n_rows, conv_dim), F32),
            pltpu.VMEM((d_state, inner), F32),
            pltpu.VMEM((cps, LANES, LANES), F32),
            pltpu.VMEM((cps, 3 * CHUNK + 32, inner), F32),
            pltpu.VMEM((n_rows, inner), F32),
        ],
        compiler_params=_cparams(1),
        name=name,
    )(zs, xbc, dtr, hist, h0, conv_w, conv_b, dtb, alog, dskx, normw, emat, *extra_args)


def _band_bias_tables(ext_ref, bias_scr, n_heads):
    col = lax.broadcasted_iota(jnp.int32, (CHUNK, ATT_WIN), 1)
    band = BAND_PAST + CHUNK
    for h in range(n_heads):
        e = jnp.broadcast_to(ext_ref[h:h + 1, :], (CHUNK, ATT_WIN))
        even = pltpu.roll(e, ATT_WIN - (CHUNK - 1), 1, stride=1, stride_axis=0)
        odd = pltpu.roll(e, 1, 1, stride=1, stride_axis=0)
        bias_scr[0, h] = jnp.where(col < band, even, MASK_NEG)
        bias_scr[1, h] = jnp.where(col >= CHUNK, odd, MASK_NEG)


ATT_BLOCKS = BAND_PAST // ATT_TILE + 1


def _attn_prompt_kernel(*refs, n_heads):
    q_ref = refs[0]
    k_refs = refs[1:1 + ATT_BLOCKS]
    v_refs = refs[1 + ATT_BLOCKS:1 + 2 * ATT_BLOCKS]
    ext_ref, o_ref, bias_scr, s_scr, p_scr = refs[1 + 2 * ATT_BLOCKS:]
    i = pl.program_id(0)
    d = ATT_HEAD_DIM
    scale = d ** -0.5
    blk = ATT_TILE

    @pl.when(i == 0)
    def _():
        _band_bias_tables(ext_ref, bias_scr, n_heads)
        p_scr[...] = jnp.zeros_like(p_scr)

    past_mask = [jnp.where(i < ATT_BLOCKS - 1 - b, MASK_NEG, 0.0).astype(F32) for b in range(ATT_BLOCKS - 1)]

    for h in range(n_heads):
        hs = slice(h * d, (h + 1) * d)
        q_h = q_ref[:, hs]
        for b in range(ATT_BLOCKS):
            s = _dot_nt(q_h, k_refs[b][:, hs]) * scale
            s_scr[:, b * blk:(b + 1) * blk] = s + past_mask[b] if b < ATT_BLOCKS - 1 else s
        for t in range(ATT_TILE // CHUNK):
            rs = slice(t * CHUNK, (t + 1) * CHUNK)
            c0 = LANES * (t // 2)
            ws = slice(c0, c0 + ATT_WIN)
            s = s_scr[rs, ws] + bias_scr[t % 2, h]
            m = jnp.max(s, axis=-1, keepdims=True)
            p = jnp.exp(s - m)
            r = 1.0 / jnp.sum(p, axis=-1, keepdims=True)
            p_scr[rs, ws] = (p * r).astype(BF16)
        o = _dot(p_scr[:, 0:blk], v_refs[0][:, hs])
        for b in range(1, ATT_BLOCKS):
            o = o + _dot(p_scr[:, b * blk:(b + 1) * blk], v_refs[b][:, hs])
        o_ref[:, hs] = o.astype(o_ref.dtype)


def _attn_prompt(q, kb, vb, ext, *, n_tiles, n_heads):
    width = n_heads * ATT_HEAD_DIM
    assert BAND_PAST % ATT_TILE == 0 and ATT_TILE % (2 * CHUNK) == 0
    kern = functools.partial(_attn_prompt_kernel, n_heads=n_heads)
    window = [pl.BlockSpec((ATT_TILE, width), lambda i, back=back: (jnp.maximum(i - back, 0), 0))
              for back in range(ATT_BLOCKS - 1, -1, -1)]
    return pl.pallas_call(
        kern,
        grid=(n_tiles,),
        in_specs=[pl.BlockSpec((ATT_TILE, width), lambda i: (i, 0))] + window + window
                 + [pl.BlockSpec((n_heads, ATT_WIN), lambda i: (0, 0))],
        out_specs=pl.BlockSpec((ATT_TILE, width), lambda i: (i, 0)),
        out_shape=jax.ShapeDtypeStruct((q.shape[0], width), BF16),
        scratch_shapes=[
            pltpu.VMEM((2, n_heads, CHUNK, ATT_WIN), F32),
            pltpu.VMEM((ATT_TILE, ATT_BLOCKS * ATT_TILE), F32),
            pltpu.VMEM((ATT_TILE, ATT_BLOCKS * ATT_TILE), BF16),
        ],
        compiler_params=_cparams(1),
        name="attn_prompt",
    )(q, *([kb] * ATT_BLOCKS), *([vb] * ATT_BLOCKS), ext)


def _sublane_transpose8(t):
    sub = lax.broadcasted_iota(jnp.int32, (SUBLANES, LANES), 0)
    for k in (4, 2, 1):
        keep = (sub // k) % 2 == 0
        new = [None] * SUBLANES
        for i in range(SUBLANES):
            if (i // k) % 2 == 0:
                j = i + k
                new[i] = jnp.where(keep, t[i], pltpu.roll(t[j], k, 0))
                new[j] = jnp.where(keep, pltpu.roll(t[i], SUBLANES - k, 0), t[j])
        t = new
    return t


def _split_heads(c_ref, u, dst, n_heads):
    halves = n_heads // SUBLANES

    def body(g, carry):
        p0 = pl.multiple_of(g * SUBLANES, SUBLANES)
        for hh in range(halves):
            tiles = [c_ref[u, pl.ds(pl.multiple_of((p0 + p) * n_heads + hh * SUBLANES, SUBLANES), SUBLANES), :]
                     for p in range(SUBLANES)]
            for a, tile in enumerate(_sublane_transpose8(tiles)):
                dst[hh * SUBLANES + a, pl.ds(p0, SUBLANES), :] = tile
        return carry

    lax.fori_loop(0, BAND_PAST // SUBLANES, body, 0, unroll=2)


def _attn_sample_kernel(q_ref, kn_ref, vn_ref, ck_ref, cv_ref, ext_ref, att_in_ref, o_ref,
                        bias_scr, kh_scr, vh_scr, s_scr, p_scr, *, n_heads):
    del att_in_ref
    d = ATT_HEAD_DIM
    scale = d ** -0.5

    @pl.when(pl.program_id(0) == 0)
    def _():
        _band_bias_tables(ext_ref, bias_scr, n_heads)

    for u in range(ck_ref.shape[0]):
        qr = slice(u * CHUNK, (u + 1) * CHUNK)
        _split_heads(ck_ref, u, kh_scr, n_heads)
        _split_heads(cv_ref, u, vh_scr, n_heads)
        band = BAND_PAST + CHUNK
        for h in range(n_heads):
            hs = slice(h * d, (h + 1) * d)
            q_h = q_ref[qr, hs]
            s_scr[h, :, 0:BAND_PAST] = (_dot_nt(q_h, kh_scr[h].astype(BF16)) * scale
                                        + bias_scr[0, h, :, 0:BAND_PAST])
            s_scr[h, :, BAND_PAST:band] = (_dot_nt(q_h, kn_ref[qr, hs]) * scale
                                           + bias_scr[0, h, :, BAND_PAST:band])
        for h in range(n_heads):
            s = s_scr[h, :, 0:band]
            p = jnp.exp(s - jnp.max(s, axis=-1, keepdims=True))
            r = 1.0 / jnp.sum(p, axis=-1, keepdims=True)
            p_scr[h, :, 0:band] = (p * r).astype(BF16)
        for h in range(n_heads):
            hs = slice(h * d, (h + 1) * d)
            o = (_dot(p_scr[h, :, 0:BAND_PAST], vh_scr[h].astype(BF16))
                 + _dot(p_scr[h, :, BAND_PAST:band], vn_ref[qr, hs]))
            o_ref[qr, hs] = o.astype(o_ref.dtype)


def _attn_sample(q, kb, vb, cache_k, cache_v, ext, att_into, *, row_blk0, n_sample, n_heads, spb):
    width = n_heads * ATT_HEAD_DIM
    assert n_sample % spb == 0 and row_blk0 % spb == 0
    kern = functools.partial(_attn_sample_kernel, n_heads=n_heads)
    cache_spec = pl.BlockSpec((spb, BAND_PAST * n_heads, ATT_HEAD_DIM), lambda s: (s, 0, 0))
    row_spec = pl.BlockSpec((spb * CHUNK, width), lambda s: (row_blk0 // spb + s, 0))
    return pl.pallas_call(
        kern,
        grid=(n_sample // spb,),
        in_specs=[row_spec, row_spec, row_spec, cache_spec, cache_spec,
                  pl.BlockSpec((n_heads, ATT_WIN), lambda s: (0, 0)),
                  pl.BlockSpec(memory_space=pl.ANY)],
        out_specs=row_spec,
        out_shape=jax.ShapeDtypeStruct(att_into.shape, BF16),
        input_output_aliases={6: 0},
        scratch_shapes=[pltpu.VMEM((2, n_heads, CHUNK, ATT_WIN), F32),
                        pltpu.VMEM((n_heads, BAND_PAST, ATT_HEAD_DIM), F32),
                        pltpu.VMEM((n_heads, BAND_PAST, ATT_HEAD_DIM), F32),
                        pltpu.VMEM((n_heads, CHUNK, ATT_WIN), F32),
                        pltpu.VMEM((n_heads, CHUNK, ATT_WIN), BF16)],
        compiler_params=_cparams(1),
        name="attn_sample",
    )(q, kb, vb, cache_k, cache_v, ext, att_into)


def _mix_kernel(s_ref, a_ref, w1_ref, w2_ref, g1_ref, g2_ref, o_ref):
    a1 = _dot(s_ref[...], w1_ref[...])
    a2 = _dot(a_ref[...], w2_ref[...])
    g1 = _sigmoid(g1_ref[...].astype(F32))
    g2 = _sigmoid(g2_ref[...].astype(F32))
    o_ref[...] = (g1 * a1 + g2 * a2).astype(o_ref.dtype)


def _mix(ssd_y, att, w1, w2, gates_arr, *, gate_col0, tm, tn):
    m, k1 = ssd_y.shape
    k2 = att.shape[1]
    n = w1.shape[1]
    assert att.shape[0] == m and m % tm == 0
    g1_blk = gate_col0 // tn
    g2_blk = (gate_col0 + n) // tn
    return pl.pallas_call(
        _mix_kernel,
        grid=(n // tn, m // tm),
        in_specs=[
            pl.BlockSpec((tm, k1), lambda j, i: (i, 0)),
            pl.BlockSpec((tm, k2), lambda j, i: (i, 0)),
            pl.BlockSpec((k1, tn), lambda j, i: (0, j), pipeline_mode=pl.Buffered(1)),
            pl.BlockSpec((k2, tn), lambda j, i: (0, j), pipeline_mode=pl.Buffered(1)),
            pl.BlockSpec((tm, tn), lambda j, i: (i, g1_blk + j)),
            pl.BlockSpec((tm, tn), lambda j, i: (i, g2_blk + j)),
        ],
        out_specs=pl.BlockSpec((tm, tn), lambda j, i: (i, j)),
        out_shape=jax.ShapeDtypeStruct((m, n), BF16),
        compiler_params=_cparams(2),
        name="branch_mix",
    )(ssd_y, att, w1, w2, gates_arr, gates_arr)


def _layer_norm(t, g, b):
    mu = jnp.mean(t, axis=-1, keepdims=True)
    tc = t - mu
    var = jnp.mean(tc * tc, axis=-1, keepdims=True)
    return tc * lax.rsqrt(var + LN_EPS) * g + b


LN_SUB_ROWS = 256


def _proj_ln_kernel(*refs, alpha, n_k, n_prompt_tiles, split_resid, n_out):
    a_ref, w_ref = refs[0], refs[1]
    pos = 2
    r_refs = refs[pos:pos + (2 if split_resid else 1)]
    pos += len(r_refs)
    g_ref, b_ref = refs[pos], refs[pos + 1]
    pos += 2
    o_refs = refs[pos:pos + n_out]
    acc_ref = refs[pos + n_out] if n_k > 1 else None
    i = pl.program_id(0)
    kk = pl.program_id(1)
    tm = a_ref.shape[0]

    def resid_rows(rs):
        if split_resid:
            return jnp.where(i < n_prompt_tiles, r_refs[0][rs, :], r_refs[1][rs, :])
        return r_refs[0][rs, :]

    def finish(first):
        for r in range(tm // LN_SUB_ROWS):
            rs = slice(r * LN_SUB_ROWS, (r + 1) * LN_SUB_ROWS)
            base = alpha * resid_rows(rs) if first else acc_ref[rs, :]
            y = _layer_norm(base + _dot(a_ref[rs, :], w_ref[...]), g_ref[...], b_ref[...])
            for o_ref in o_refs:
                o_ref[rs, :] = y.astype(o_ref.dtype)

    if n_k == 1:
        finish(True)
        return

    @pl.when(kk == 0)
    def _():
        acc_ref[...] = alpha * resid_rows(slice(None)) + _dot(a_ref[...], w_ref[...])

    if n_k > 2:
        @pl.when(jnp.logical_and(kk > 0, kk < n_k - 1))
        def _():
            acc_ref[...] += _dot(a_ref[...], w_ref[...])

    @pl.when(kk == n_k - 1)
    def _():
        finish(False)


def _proj_ln(a, w, resid, gamma, beta, out_dtypes, *, alpha, tm, n_k, row0, n_rows, m_prompt=None, name):
    k = a.shape[1]
    n = w.shape[1]
    tk = k // n_k
    assert tk * n_k == k and n_rows % tm == 0 and row0 % tm == 0 and tm % LN_SUB_ROWS == 0
    blk0 = row0 // tm
    split_resid = isinstance(resid, tuple)
    row_map = lambda i, kk: (blk0 + i, 0)
    w_mode = dict(pipeline_mode=pl.Buffered(1)) if n_k == 1 else {}
    in_specs = [pl.BlockSpec((tm, tk), lambda i, kk: (blk0 + i, kk)),
                pl.BlockSpec((tk, n), lambda i, kk: (kk, 0), **w_mode)]
    args = [a, w]
    npt = 0
    if split_resid:
        assert row0 == 0 and m_prompt % tm == 0
        npt = m_prompt // tm
        in_specs += [pl.BlockSpec((tm, n), lambda i, kk: (jnp.minimum(i, npt - 1), 0)),
                     pl.BlockSpec((tm, n), lambda i, kk: (jnp.maximum(i - npt, 0), 0))]
        args += list(resid)
    else:
        in_specs.append(pl.BlockSpec((tm, n), row_map))
        args.append(resid)
    in_specs += [pl.BlockSpec((1, n), lambda i, kk: (0, 0))] * 2
    args += [gamma, beta]
    kern = functools.partial(_proj_ln_kernel, alpha=alpha, n_k=n_k, n_prompt_tiles=npt,
                             split_resid=split_resid, n_out=len(out_dtypes))
    return pl.pallas_call(
        kern,
        grid=(n_rows // tm, n_k),
        in_specs=in_specs,
        out_specs=[pl.BlockSpec((tm, n), lambda i, kk: (i, 0)) for _ in out_dtypes],
        out_shape=[jax.ShapeDtypeStruct((n_rows, n), dt) for dt in out_dtypes],
        scratch_shapes=[pltpu.VMEM((tm, n), F32)] if n_k > 1 else [],
        compiler_params=_cparams(2),
        name=name,
    )(*args)


def _glu_kernel(*refs, n_side):
    h_ref, wg_ref, wu_ref = refs[:3]
    side_in = refs[3:3 + n_side]
    o_ref = refs[3 + n_side]
    side_out = refs[4 + n_side:4 + 2 * n_side]
    wg_sc, wu_sc = refs[4 + 2 * n_side:]

    @pl.when(pl.program_id(1) == 0)
    def _():
        _cast_weight_tile(wg_ref, wg_sc)
        _cast_weight_tile(wu_ref, wu_sc)

    for s_in, s_out in zip(side_in, side_out):
        s_out[...] = s_in[...].astype(BF16)

    tm = h_ref.shape[0]
    sub = tm // GLU_SUB_BLOCKS
    for r in range(GLU_SUB_BLOCKS):
        rs = slice(r * sub, (r + 1) * sub)
        h = h_ref[rs, :]
        g = _dot(h, wg_sc[...])
        u = _dot(h, wu_sc[...])
        o_ref[rs, :] = (g * jax.nn.sigmoid(g) * u).astype(o_ref.dtype)


def _glu(h, w_gate_up, *, tm, tf, side=()):
    m, k = h.shape
    d_ff = w_gate_up.shape[1] // 2
    nf = d_ff // tf
    assert nf * tf == d_ff
    side_in, side_out, side_shape = _side_cast_specs(side, m // tm)
    assert all(w.shape[0] // rows <= nf * (m // tm) for w, rows in side)
    return pl.pallas_call(
        functools.partial(_glu_kernel, n_side=len(side)),
        grid=(nf, m // tm),
        in_specs=[
            pl.BlockSpec((tm, k), lambda j, i: (i, 0)),
            pl.BlockSpec((k, tf), lambda j, i: (0, j)),
            pl.BlockSpec((k, tf), lambda j, i: (0, nf + j)),
        ] + side_in,
        out_specs=[pl.BlockSpec((tm, tf), lambda j, i: (i, j))] + side_out,
        out_shape=[jax.ShapeDtypeStruct((m, d_ff), BF16)] + side_shape,
        scratch_shapes=[pltpu.VMEM((k, tf), BF16), pltpu.VMEM((k, tf), BF16)],
        compiler_params=_cparams(2),
        name="ffn_glu",
    )(h, w_gate_up, w_gate_up, *[w for w, _ in side])


def kernel(x_prompt, x_sample, cache_k, cache_v, state_conv, state_ssm, w_in, conv_w, conv_b, dt_bias,
           a_log, d_skip, ssd_norm_w, rel_bias, w_ssd_out, w_att_out, w_o, ln1_g, ln1_b, w_gate_up,
           w_down, ln2_g, ln2_b):
    depth = w_in.shape[0]
    assert depth == 1
    batch, seq, d_model = x_prompt.shape
    n_sample, dec_seq, _ = x_sample.shape
    assert batch == 1 and dec_seq == CHUNK and seq % ATT_TILE == 0
    n_ssd_heads = dt_bias.shape[1]
    inner = n_ssd_heads * SSD_HEADDIM
    conv_dim = conv_w.shape[2]
    n_groups = SSD_GROUPS
    d_state = SSD_STATE
    assert conv_dim == inner + 2 * n_groups * d_state
    n_heads = rel_bias.shape[1]
    width = n_heads * ATT_HEAD_DIM
    assert cache_k.shape[2] == BAND_PAST
    alpha = (2.0 * depth) ** 0.25
    keep = min(BAND_PAST, seq)

    m_p = batch * seq
    m_s = n_sample * dec_seq
    xp2 = x_prompt.reshape(m_p, d_model)
    xs2 = x_sample.reshape(m_s, d_model)

    wt = w_in[0].T
    tn = 1024
    o_xbc = inner
    o_dt = o_xbc + conv_dim
    o_q = o_dt + n_ssd_heads
    shift = o_q % tn
    assert o_dt % tn == 0 and inner % tn == 0 and conv_dim % tn == 0 and width % tn == 0
    q_blk = o_dt // tn
    xb, dtr = _xcast_dt(xp2, xs2, wt, dt_col=o_dt, n_dt=n_ssd_heads, tm=1024)
    zs, w_att_b, w_o_b = _matmul_w32(xb, wt, lambda j: j, inner // tn, [BF16], tm=1536, tn=tn, silu=True,
                                     side=[(w_att_out[0], SIDE_CAST_ROWS), (w_o[0], SIDE_CAST_ROWS)],
                                     name="in_proj_z")
    (xbc,) = _matmul_w32(xb, wt, lambda j: inner // tn + j, conv_dim // tn, [F32], tm=1536, tn=tn,
                         name="in_proj_xbc")
    wpt = width // tn
    qg, w_ssd_b = _matmul_w32(xb, wt, lambda j: jnp.where(j < wpt, q_blk + j, q_blk + 2 * wpt + j),
                              wpt + 2 * d_model // tn, [BF16], tm=1536, tn=tn, shift=shift,
                              side=[(w_ssd_out[0], SIDE_CAST_ROWS)], name="in_proj_qg")
    kb, k_p, k_s = _matmul_w32(xb, wt, lambda j: q_blk + wpt + j, wpt, [BF16], tm=1024, tn=tn, shift=shift,
                               head_out=(keep, m_s), name="in_proj_k")
    vb, v_p, v_s = _matmul_w32(xb, wt, lambda j: q_blk + 2 * wpt + j, wpt, [BF16], tm=1024, tn=tn,
                               shift=shift, head_out=(keep, m_s), name="in_proj_v")

    pad_h = (0, LANES - n_ssd_heads)
    dtb = jnp.pad(dt_bias[0], pad_h).reshape(1, LANES)
    alog = jnp.pad(a_log[0], pad_h).reshape(1, LANES)
    dskx = jnp.repeat(d_skip[0], SSD_HEADDIM).reshape(1, inner)
    emat = (jnp.arange(LANES)[:, None] == (jnp.arange(inner)[None, :] // SSD_HEADDIM)).astype(BF16)
    ssd_args = (zs, xbc, dtr, state_conv[0], state_ssm[0].reshape(n_sample, inner, d_state),
                conv_w[0], conv_b[0].reshape(1, conv_dim), dtb, alog, dskx, ssd_norm_w[0].reshape(1, inner), emat)
    ssd_dims = dict(inner=inner, n_groups=n_groups, d_state=d_state)
    ssd_y, conv_p, h_p = _ssd(*ssd_args, row0=0, n_streams=batch, chunks_per_stream=seq // CHUNK,
                              cps=SSD_PROMPT_CPS, per_stream=False, name="ssd_scan_prompt", **ssd_dims)
    ssd_y, conv_s, h_s = _ssd(*ssd_args, row0=m_p, n_streams=n_sample, chunks_per_stream=1, cps=1,
                              per_stream=True, name="ssd_scan_sample", y_into=ssd_y, **ssd_dims)

    rb = rel_bias[0]
    band = BAND_PAST + CHUNK
    n_edge = band - REL_CLIP
    n_rev = ATT_WIN - n_edge
    rev = rb[:, ::-1][:, 1:1 + n_rev]
    ext = jnp.concatenate([jnp.broadcast_to(rb[:, 2 * REL_CLIP:], (n_heads, n_edge)), rev], axis=1)
    ext = jnp.pad(ext, ((0, 0), (0, ATT_WIN - ext.shape[1])))
    att = _attn_prompt(qg, kb, vb, ext, n_tiles=m_p // ATT_TILE, n_heads=n_heads)
    cache_rows = (n_sample, BAND_PAST * n_heads, ATT_HEAD_DIM)
    att = _attn_sample(qg, kb, vb, cache_k[0].reshape(cache_rows), cache_v[0].reshape(cache_rows), ext, att,
                       row_blk0=m_p // CHUNK, n_sample=n_sample, n_heads=n_heads, spb=ATT_SAMPLE_SPB)

    mixed = _mix(ssd_y, att, w_ssd_b, w_att_b, qg, gate_col0=width, tm=768, tn=1024)
    g1 = ln1_g[0].reshape(1, d_model)
    b1 = ln1_b[0].reshape(1, d_model)
    g2 = ln2_g[0].reshape(1, d_model)
    b2 = ln2_b[0].reshape(1, d_model)
    h_f, h_b = _proj_ln(mixed, w_o_b, (xp2, xs2), g1, b1, [F32, BF16], alpha=alpha, tm=512,
                        n_k=1, row0=0, n_rows=m_p + m_s, m_prompt=m_p, name="wo_ln")
    act, w_dn = _glu(h_b, w_gate_up[0], tm=2304, tf=512, side=[(w_down[0], DOWN_CAST_ROWS)])
    (y_p,) = _proj_ln(act, w_dn, h_f, g2, b2, [F32], alpha=alpha, tm=512, n_k=2, row0=0, n_rows=m_p,
                      name="down_ln_prompt")
    (y_s,) = _proj_ln(act, w_dn, h_f, g2, b2, [F32], alpha=alpha, tm=512, n_k=2, row0=m_p, n_rows=m_s,
                      name="down_ln_sample")

    return (y_p.reshape(batch, seq, d_model),
            y_s.reshape(n_sample, dec_seq, d_model),
            conv_p.reshape(1, batch, CONV_W - 1, conv_dim),
            h_p.reshape(1, batch, n_ssd_heads, SSD_HEADDIM, d_state),
            k_p.reshape(1, batch, keep, n_heads, ATT_HEAD_DIM),
            v_p.reshape(1, batch, keep, n_heads, ATT_HEAD_DIM),
            conv_s.reshape(1, n_sample, CONV_W - 1, conv_dim),
            h_s.reshape(1, n_sample, n_ssd_heads, SSD_HEADDIM, d_state),
            k_s.reshape(1, n_sample, dec_seq, n_heads, ATT_HEAD_DIM),
            v_s.reshape(1, n_sample, dec_seq, n_heads, ATT_HEAD_DIM))
```

```python
import functools

import jax
import jax.numpy as jnp
from jax import lax
from jax.experimental import pallas as pl
from jax.experimental.pallas import tpu as pltpu

F32 = jnp.float32
BF16 = jnp.bfloat16

CHUNK = 64
SSD_HEADDIM = 64
SSD_GROUPS = 8
SSD_STATE = 128
CONV_W = 4
ATT_HEAD_DIM = 128
BAND_CHUNKS = 8
BAND_PAST = BAND_CHUNKS * CHUNK
REL_CLIP = 128
LN_EPS = 1e-5
RMS_EPS = 1e-5
MASK_NEG = -1e30

LANES = 128
SUBLANES = 8
VMEM_LIMIT_BYTES = 56 * 1024 * 1024

PROJ_TM, PROJ_TN = 1536, 1024
KV_TM = 1024
XCAST_TM = 1024
MIX_TM, MIX_TN = 768, 1024
LN_TM = 512
DOWN_K_SPLIT = 2
GLU_TM, GLU_TF = 2304, 512
SSD_PROMPT_CPS = 4
SIDE_CAST_ROWS = 128
DOWN_CAST_ROWS = 176
ATT_TILE = 512
ATT_SAMPLE_SPB = 2
GLU_SUB_BLOCKS = 2
SILU_SUB_BLOCKS = 2
ATT_WIN = 5 * LANES


def _cparams(n_axes):
    return pltpu.CompilerParams(
        dimension_semantics=("arbitrary",) * n_axes,
        vmem_limit_bytes=VMEM_LIMIT_BYTES,
    )


def _sigmoid(x):
    return 0.5 * jnp.tanh(0.5 * x) + 0.5


def _silu(x):
    h = 0.5 * x
    return h + h * jnp.tanh(h)


def _dot(a, b):
    return jnp.dot(a, b, preferred_element_type=F32)


def _dot_nt(a, b):
    return lax.dot_general(a, b, (((1,), (1,)), ((), ())), preferred_element_type=F32)


def _dot_tn(a, b):
    return lax.dot_general(a, b, (((0,), (0,)), ((), ())), preferred_element_type=F32)


def _xcast_dt_kernel(xp_ref, xs_ref, wd_ref, xb_ref, dt_ref, *, n_prompt_tiles, n_dt):
    i = pl.program_id(0)
    x = jnp.where(i < n_prompt_tiles, xp_ref[...], xs_ref[...]).astype(BF16)
    xb_ref[...] = x
    acc = _dot_nt(x, wd_ref[...].astype(BF16))
    lane = lax.broadcasted_iota(jnp.int32, acc.shape, 1)
    dt_ref[...] = jnp.where(lane < n_dt, acc, 0.0)


def _xcast_dt(xp, xs, wt, *, dt_col, n_dt, tm):
    m_p, k = xp.shape
    m_s = xs.shape[0]
    npt = m_p // tm
    n_tiles = npt + m_s // tm
    assert dt_col % LANES == 0 and n_dt <= LANES
    kern = functools.partial(_xcast_dt_kernel, n_prompt_tiles=npt, n_dt=n_dt)
    return pl.pallas_call(
        kern,
        grid=(n_tiles,),
        in_specs=[
            pl.BlockSpec((tm, k), lambda i: (jnp.minimum(i, npt - 1), 0)),
            pl.BlockSpec((tm, k), lambda i: (jnp.maximum(i - npt, 0), 0)),
            pl.BlockSpec((LANES, k), lambda i: (dt_col // LANES, 0)),
        ],
        out_specs=[pl.BlockSpec((tm, k), lambda i: (i, 0)),
                   pl.BlockSpec((tm, LANES), lambda i: (i, 0))],
        out_shape=[jax.ShapeDtypeStruct((m_p + m_s, k), BF16),
                   jax.ShapeDtypeStruct((m_p + m_s, LANES), F32)],
        compiler_params=_cparams(1),
        name="xcast_dt",
    )(xp, xs, wt)


CAST_ROWS = 256


def _cast_rows(src_ref, src_row0, dst_ref, dst_row0, n_rows):
    done = 0
    while done < n_rows:
        step = min(CAST_ROWS, n_rows - done)
        dst_ref[dst_row0 + done:dst_row0 + done + step, :] = (
            src_ref[src_row0 + done:src_row0 + done + step, :].astype(BF16))
        done += step


def _cast_weight_tile(w_ref, wsc):
    _cast_rows(w_ref, 0, wsc, 0, wsc.shape[0])


def _side_cast_specs(side, n_inner):
    in_specs, out_specs, out_shape = [], [], []
    for w, rows in side:
        n_blk = w.shape[0] // rows
        assert n_blk * rows == w.shape[0] and rows % 16 == 0
        idx = lambda j, i, n_blk=n_blk: (jnp.minimum(j * n_inner + i, n_blk - 1), 0)
        in_specs.append(pl.BlockSpec((rows, w.shape[1]), idx))
        out_specs.append(pl.BlockSpec((rows, w.shape[1]), idx))
        out_shape.append(jax.ShapeDtypeStruct(w.shape, BF16))
    return in_specs, out_specs, out_shape


def _mm_w32_kernel(*refs, shift, n_out, head_rows, silu, n_side):
    x_ref, wa_ref = refs[0], refs[1]
    pos = 2
    wb_ref = None
    if shift:
        wb_ref = refs[pos]
        pos += 1
    side_in = refs[pos:pos + n_side]
    pos += n_side
    o_refs = refs[pos:pos + n_out]
    pos += n_out
    hp_ref = hs_ref = None
    if head_rows is not None:
        hp_ref, hs_ref = refs[pos], refs[pos + 1]
        pos += 2
    side_out = refs[pos:pos + n_side]
    pos += n_side
    wsc = refs[pos]
    i = pl.program_id(1)

    @pl.when(i == 0)
    def _():
        tn = wsc.shape[0]
        _cast_rows(wa_ref, shift, wsc, 0, tn - shift)
        if shift:
            _cast_rows(wb_ref, 0, wsc, tn - shift, shift)

    for s_in, s_out in zip(side_in, side_out):
        s_out[...] = s_in[...].astype(BF16)

    if silu:
        sub = x_ref.shape[0] // SILU_SUB_BLOCKS
        for r in range(SILU_SUB_BLOCKS):
            rs = slice(r * sub, (r + 1) * sub)
            res = _silu(_dot_nt(x_ref[rs, :], wsc[...]))
            for o_ref in o_refs:
                o_ref[rs, :] = res.astype(o_ref.dtype)
        return

    acc = _dot_nt(x_ref[...], wsc[...])
    for o_ref in o_refs:
        o_ref[...] = acc.astype(o_ref.dtype)

    if head_rows is not None:
        (tile_p, row0_p), (tile_s, row0_s) = head_rows

        @pl.when(i == tile_p)
        def _():
            hp_ref[...] = acc[row0_p:row0_p + hp_ref.shape[0], :]

        @pl.when(i == tile_s)
        def _():
            hs_ref[...] = acc[row0_s:row0_s + hs_ref.shape[0], :]


def _matmul_w32(x, wt, col_blk, n_col_tiles, out_dtypes, *, tm, tn, shift=0, head_out=None, silu=False,
                side=(), name):
    m, k = x.shape
    assert m % tm == 0 and tn % LANES == 0 and shift % 16 == 0 and shift < tn
    assert not silu or (head_out is None and tm % (16 * SILU_SUB_BLOCKS) == 0)
    in_specs = [pl.BlockSpec((tm, k), lambda j, i: (i, 0)),
                pl.BlockSpec((tn, k), lambda j, i: (col_blk(j), 0))]
    args = [x, wt]
    if shift:
        assert tn % shift == 0
        per = tn // shift
        in_specs.append(pl.BlockSpec((shift, k), lambda j, i: ((col_blk(j) + 1) * per, 0)))
        args.append(wt)
    side_in, side_out, side_shape = _side_cast_specs(side, m // tm)
    assert all(w.shape[0] // rows <= n_col_tiles * (m // tm) for w, rows in side)
    in_specs += side_in
    args += [w for w, _ in side]
    n = n_col_tiles * tn
    out_specs = [pl.BlockSpec((tm, tn), lambda j, i: (i, j)) for _ in out_dtypes]
    out_shape = [jax.ShapeDtypeStruct((m, n), dt) for dt in out_dtypes]
    head_rows = None
    if head_out is not None:
        n_keep, m_s = head_out
        m_p = m - m_s
        head_rows = (divmod(m_p - n_keep, tm), divmod(m_p, tm))
        assert head_rows[0][1] + n_keep <= tm and head_rows[1][1] + m_s <= tm
        out_specs += [pl.BlockSpec((n_keep, tn), lambda j, i: (0, j)),
                      pl.BlockSpec((m_s, tn), lambda j, i: (0, j))]
        out_shape += [jax.ShapeDtypeStruct((n_keep, n), F32), jax.ShapeDtypeStruct((m_s, n), F32)]
    out_specs += side_out
    out_shape += side_shape
    kern = functools.partial(_mm_w32_kernel, shift=shift, n_out=len(out_dtypes), head_rows=head_rows,
                             silu=silu, n_side=len(side))
    return pl.pallas_call(
        kern,
        grid=(n_col_tiles, m // tm),
        in_specs=in_specs,
        out_specs=out_specs,
        out_shape=out_shape,
        scratch_shapes=[pltpu.VMEM((tn, k), BF16)],
        compiler_params=_cparams(2),
        name=name,
    )(*args)


def _softplus(x):
    return jnp.maximum(x, 0.0) + jnp.log1p(jnp.exp(-jnp.abs(x)))


def _ssd_chunk(u, zs_ref, dt_ref, cw_ref, cb_ref, dtb_ref, alog_ref, dsk_ref, nw_ref, e_ref, y_ref,
               xpad, act, ht, cst, ex, ysc, *, inner, n_groups, d_state):
    r0 = u * CHUNK
    rows = slice(r0, r0 + CHUNK)
    gw = inner // n_groups
    conv_dim = inner + 2 * n_groups * d_state

    for j in range(conv_dim // LANES):
        sl = slice(j * LANES, (j + 1) * LANES)
        blk = xpad[r0:r0 + SUBLANES + CHUNK, sl]
        prev = pltpu.roll(blk, 1, 0)
        near = blk * cw_ref[3:4, sl] + prev * cw_ref[2:3, sl]
        far = pltpu.roll(blk * cw_ref[1:2, sl] + prev * cw_ref[0:1, sl], 2, 0)
        a = cb_ref[:, sl] + (near + far)[SUBLANES:, :]
        act[rows, sl] = _silu(a)

    dt = _softplus(dt_ref[rows, :] + dtb_ref[...])
    a_neg = -jnp.exp(alog_ref[...])
    da = dt * a_neg
    ri = lax.broadcasted_iota(jnp.int32, (CHUNK, CHUNK), 0)
    ci = lax.broadcasted_iota(jnp.int32, (CHUNK, CHUNK), 1)
    tri = (ri >= ci).astype(F32)
    cs = jnp.dot(tri, da, precision=lax.Precision.HIGHEST, preferred_element_type=F32)
    ecs = jnp.exp(cs)
    cs_last = cs[CHUNK - 1:CHUNK, :]
    dout = jnp.exp(cs_last - cs) * dt
    cd = jnp.exp(cs_last)
    cd_hi = cd.astype(BF16)
    cd_lo = (cd - cd_hi.astype(F32)).astype(BF16)
    lhs = jnp.concatenate(
        [dt.astype(BF16), ecs.astype(BF16), dout.astype(BF16),
         jnp.broadcast_to(cd_hi, (16, LANES)), jnp.broadcast_to(cd_lo, (16, LANES))], axis=0)
    ex[u] = _dot(lhs, e_ref[...])

    cst[u] = jnp.concatenate([cs, jnp.zeros((LANES - CHUNK, LANES), F32)], axis=0).T
    lane = lax.broadcasted_iota(jnp.int32, (CHUNK, LANES), 1)
    lo_half = lane < CHUNK
    n_pairs = inner // LANES
    ev = cst[u, pl.ds(0, n_pairs, stride=2), :]
    od = cst[u, pl.ds(1, n_pairs, stride=2), :]
    lane_p = lax.broadcasted_iota(jnp.int32, (n_pairs, LANES), 1)
    cst2 = jnp.where(lane_p < CHUNK, ev, pltpu.roll(od, CHUNK, 1))
    row = lax.broadcasted_iota(jnp.int32, (CHUNK, LANES), 0)
    causal2 = row >= jnp.where(lo_half, lane, lane - CHUNK)

    pairs_per_group = gw // LANES
    for g in range(n_groups):
        gs = slice(g * gw, (g + 1) * gw)
        bm_g = act[rows, inner + g * d_state: inner + (g + 1) * d_state].astype(BF16)
        cm_g = act[rows, inner + (n_groups + g) * d_state: inner + (n_groups + g + 1) * d_state].astype(BF16)
        cb2 = _dot_nt(cm_g, jnp.concatenate([bm_g, bm_g], axis=0))
        h_g = ht[:, gs]
        yoff = _dot(cm_g, h_g.astype(BF16))
        for p in range(pairs_per_group):
            k = g * pairs_per_group + p
            ls = slice(k * LANES, (k + 1) * LANES)
            csc = jnp.where(lo_half,
                            jnp.broadcast_to(cs[:, 2 * k:2 * k + 1], (CHUNK, LANES)),
                            jnp.broadcast_to(cs[:, 2 * k + 1:2 * k + 2], (CHUNK, LANES)))
            diff = csc - cst2[k:k + 1, :]
            dec = jnp.exp(jnp.where(causal2, diff, MASK_NEG))
            m2 = (cb2 * dec).astype(BF16)
            xs_p = act[rows, ls]
            xdt = xs_p * ex[u, 0:CHUNK, ls]
            x2 = jnp.concatenate([jnp.where(lo_half, xdt, 0.0), jnp.where(lo_half, 0.0, xdt)],
                                 axis=0).astype(BF16)
            yd = _dot(m2, x2)
            ysc[rows, ls] = (yd + yoff[:, p * LANES:(p + 1) * LANES] * ex[u, CHUNK:2 * CHUNK, ls]
                             + dsk_ref[:, ls] * xs_p)
        xw = (act[rows, gs] * ex[u, 2 * CHUNK:3 * CHUNK, gs]).astype(BF16)
        st = _dot_tn(bm_g, xw)
        cdx = ex[u, 3 * CHUNK:3 * CHUNK + 1, gs] + ex[u, 3 * CHUNK + 16:3 * CHUNK + 17, gs]
        ht[:, gs] = h_g * cdx + st

    for g in range(n_groups):
        gs = slice(g * gw, (g + 1) * gw)
        v = ysc[rows, gs] * zs_ref[rows, gs].astype(F32)
        ms = jnp.mean(v * v, axis=-1, keepdims=True)
        y_ref[rows, gs] = (v * lax.rsqrt(ms + RMS_EPS) * nw_ref[:, gs]).astype(y_ref.dtype)


def _ssd_kernel(*refs, cps, per_stream, aliased, inner, n_groups, d_state):
    (zs_ref, xbc_ref, dt_ref, hist_ref, h0_ref, cw_ref, cb_ref, dtb_ref, alog_ref,
     dsk_ref, nw_ref, e_ref) = refs[:12]
    y_ref, conv_ref, ho_ref, xpad, act, ht, cst, ex, ysc = refs[12 + int(aliased):]
    c = pl.program_id(0)
    n_rows = cps * CHUNK
    conv_dim = inner + 2 * n_groups * d_state

    @pl.when(c == 0)
    def _():
        xpad[0:SUBLANES, :] = jnp.zeros((SUBLANES, conv_dim), F32)
        if not per_stream:
            ht[...] = jnp.zeros_like(ht)

    if per_stream:
        xpad[5:8, :] = hist_ref[0]
        ht[...] = h0_ref[0].T

    xpad[SUBLANES:SUBLANES + n_rows, :] = xbc_ref[...]
    conv_ref[0] = xbc_ref[n_rows - (CONV_W - 1):n_rows, :]

    for u in range(cps):
        _ssd_chunk(u, zs_ref, dt_ref, cw_ref, cb_ref, dtb_ref, alog_ref, dsk_ref, nw_ref, e_ref, y_ref,
                   xpad, act, ht, cst, ex, ysc, inner=inner, n_groups=n_groups, d_state=d_state)
    xpad[5:8, :] = xpad[5 + n_rows:8 + n_rows, :]

    if per_stream:
        ho_ref[0] = ht[...].T
    else:
        @pl.when(c == pl.num_programs(0) - 1)
        def _():
            ho_ref[0] = ht[...].T


def _ssd(zs, xbc, dtr, hist, h0, conv_w, conv_b, dtb, alog, dskx, normw, emat, *,
         row0, n_streams, chunks_per_stream, cps, per_stream, inner, n_groups, d_state, name, y_into=None):
    conv_dim = xbc.shape[1]
    hp = inner
    n_rows = cps * CHUNK
    assert chunks_per_stream % cps == 0 and row0 % n_rows == 0
    assert per_stream == (chunks_per_stream == cps) and (per_stream or n_streams == 1)
    steps_per_stream = chunks_per_stream // cps
    blk0 = row0 // n_rows
    stream = (lambda c: c) if per_stream else (lambda c: 0)
    aliased = y_into is not None

    kern = functools.partial(_ssd_kernel, cps=cps, per_stream=per_stream, aliased=aliased, inner=inner,
                             n_groups=n_groups, d_state=d_state)
    const = lambda c: (0, 0)
    row_map = lambda c: (blk0 + c, 0)
    extra_specs = [pl.BlockSpec(memory_space=pl.ANY)] if aliased else []
    extra_args = [y_into] if aliased else []
    return pl.pallas_call(
        kern,
        grid=(n_streams * steps_per_stream,),
        input_output_aliases={12: 0} if aliased else {},
        in_specs=[
            pl.BlockSpec((n_rows, inner), row_map),
            pl.BlockSpec((n_rows, conv_dim), row_map),
            pl.BlockSpec((n_rows, LANES), row_map),
            pl.BlockSpec((1, CONV_W - 1, conv_dim), lambda c: (stream(c), 0, 0)),
            pl.BlockSpec((1, hp, d_state), lambda c: (stream(c), 0, 0)),
            pl.BlockSpec((CONV_W, conv_dim), const),
            pl.BlockSpec((1, conv_dim), const),
            pl.BlockSpec((1, LANES), const),
            pl.BlockSpec((1, LANES), const),
            pl.BlockSpec((1, inner), const),
            pl.BlockSpec((1, inner), const),
            pl.BlockSpec((LANES, inner), const),
        ] + extra_specs,
        out_specs=[
            pl.BlockSpec((n_rows, inner), row_map),
            pl.BlockSpec((1, CONV_W - 1, conv_dim), lambda c: (stream(c), 0, 0)),
            pl.BlockSpec((1, hp, d_state), lambda c: (stream(c), 0, 0)),
        ],
        out_shape=[
            jax.ShapeDtypeStruct((zs.shape[0], inner), BF16),
            jax.ShapeDtypeStruct((n_streams, CONV_W - 1, conv_dim), F32),
            jax.ShapeDtypeStruct((n_streams, hp, d_state), F32),
        ],
        scratch_shapes=[
            pltpu.VMEM((SUBLANES + n_rows, conv_dim), F32),
            pltpu.VMEM((n_rows, conv_dim), F32),
            pltpu.VMEM((d_state, inner), F32),
            pltpu.VMEM((cps, LANES, LANES), F32),
            pltpu.VMEM((cps, 3 * CHUNK + 32, inner), F32),
            pltpu.VMEM((n_rows, inner), F32),
        ],
        compiler_params=_cparams(1),
        name=name,
    )(zs, xbc, dtr, hist, h0, conv_w, conv_b, dtb, alog, dskx, normw, emat, *extra_args)


def _band_bias_tables(ext_ref, bias_scr, n_heads):
    col = lax.broadcasted_iota(jnp.int32, (CHUNK, ATT_WIN), 1)
    band = BAND_PAST + CHUNK
    for h in range(n_heads):
        e = jnp.broadcast_to(ext_ref[h:h + 1, :], (CHUNK, ATT_WIN))
        even = pltpu.roll(e, ATT_WIN - (CHUNK - 1), 1, stride=1, stride_axis=0)
        odd = pltpu.roll(e, 1, 1, stride=1, stride_axis=0)
        bias_scr[0, h] = jnp.where(col < band, even, MASK_NEG)
        bias_scr[1, h] = jnp.where(col >= CHUNK, odd, MASK_NEG)


ATT_BLOCKS = BAND_PAST // ATT_TILE + 1


def _attn_prompt_kernel(*refs, n_heads):
    q_ref = refs[0]
    k_refs = refs[1:1 + ATT_BLOCKS]
    v_refs = refs[1 + ATT_BLOCKS:1 + 2 * ATT_BLOCKS]
    ext_ref, o_ref, bias_scr, s_scr, p_scr = refs[1 + 2 * ATT_BLOCKS:]
    i = pl.program_id(0)
    d = ATT_HEAD_DIM
    scale = d ** -0.5
    blk = ATT_TILE

    @pl.when(i == 0)
    def _():
        _band_bias_tables(ext_ref, bias_scr, n_heads)
        p_scr[...] = jnp.zeros_like(p_scr)

    past_mask = [jnp.where(i < ATT_BLOCKS - 1 - b, MASK_NEG, 0.0).astype(F32) for b in range(ATT_BLOCKS - 1)]

    for h in range(n_heads):
        hs = slice(h * d, (h + 1) * d)
        q_h = q_ref[:, hs]
        for b in range(ATT_BLOCKS):
            s = _dot_nt(q_h, k_refs[b][:, hs]) * scale
            s_scr[:, b * blk:(b + 1) * blk] = s + past_mask[b] if b < ATT_BLOCKS - 1 else s
        for t in range(ATT_TILE // CHUNK):
            rs = slice(t * CHUNK, (t + 1) * CHUNK)
            c0 = LANES * (t // 2)
            ws = slice(c0, c0 + ATT_WIN)
            s = s_scr[rs, ws] + bias_scr[t % 2, h]
            m = jnp.max(s, axis=-1, keepdims=True)
            p = jnp.exp(s - m)
            r = 1.0 / jnp.sum(p, axis=-1, keepdims=True)
            p_scr[rs, ws] = (p * r).astype(BF16)
        o = _dot(p_scr[:, 0:blk], v_refs[0][:, hs])
        for b in range(1, ATT_BLOCKS):
            o = o + _dot(p_scr[:, b * blk:(b + 1) * blk], v_refs[b][:, hs])
        o_ref[:, hs] = o.astype(o_ref.dtype)


def _attn_prompt(q, kb, vb, ext, *, n_tiles, n_heads):
    width = n_heads * ATT_HEAD_DIM
    assert BAND_PAST % ATT_TILE == 0 and ATT_TILE % (2 * CHUNK) == 0
    kern = functools.partial(_attn_prompt_kernel, n_heads=n_heads)
    window = [pl.BlockSpec((ATT_TILE, width), lambda i, back=back: (jnp.maximum(i - back, 0), 0))
              for back in range(ATT_BLOCKS - 1, -1, -1)]
    return pl.pallas_call(
        kern,
        grid=(n_tiles,),
        in_specs=[pl.BlockSpec((ATT_TILE, width), lambda i: (i, 0))] + window + window
                 + [pl.BlockSpec((n_heads, ATT_WIN), lambda i: (0, 0))],
        out_specs=pl.BlockSpec((ATT_TILE, width), lambda i: (i, 0)),
        out_shape=jax.ShapeDtypeStruct((q.shape[0], width), BF16),
        scratch_shapes=[
            pltpu.VMEM((2, n_heads, CHUNK, ATT_WIN), F32),
            pltpu.VMEM((ATT_TILE, ATT_BLOCKS * ATT_TILE), F32),
            pltpu.VMEM((ATT_TILE, ATT_BLOCKS * ATT_TILE), BF16),
        ],
        compiler_params=_cparams(1),
        name="attn_prompt",
    )(q, *([kb] * ATT_BLOCKS), *([vb] * ATT_BLOCKS), ext)


def _sublane_transpose8(t):
    sub = lax.broadcasted_iota(jnp.int32, (SUBLANES, LANES), 0)
    for k in (4, 2, 1):
        keep = (sub // k) % 2 == 0
        new = [None] * SUBLANES
        for i in range(SUBLANES):
            if (i // k) % 2 == 0:
                j = i + k
                new[i] = jnp.where(keep, t[i], pltpu.roll(t[j], k, 0))
                new[j] = jnp.where(keep, pltpu.roll(t[i], SUBLANES - k, 0), t[j])
        t = new
    return t


def _split_heads(c_ref, u, dst, n_heads):
    halves = n_heads // SUBLANES

    def body(g, carry):
        p0 = pl.multiple_of(g * SUBLANES, SUBLANES)
        for hh in range(halves):
            tiles = [c_ref[u, pl.ds(pl.multiple_of((p0 + p) * n_heads + hh * SUBLANES, SUBLANES), SUBLANES), :]
                     for p in range(SUBLANES)]
            for a, tile in enumerate(_sublane_transpose8(tiles)):
                dst[hh * SUBLANES + a, pl.ds(p0, SUBLANES), :] = tile
        return carry

    lax.fori_loop(0, BAND_PAST // SUBLANES, body, 0, unroll=2)


def _attn_sample_kernel(q_ref, kn_ref, vn_ref, ck_ref, cv_ref, ext_ref, att_in_ref, o_ref,
                        bias_scr, kh_scr, vh_scr, s_scr, p_scr, *, n_heads):
    del att_in_ref
    d = ATT_HEAD_DIM
    scale = d ** -0.5

    @pl.when(pl.program_id(0) == 0)
    def _():
        _band_bias_tables(ext_ref, bias_scr, n_heads)

    for u in range(ck_ref.shape[0]):
        qr = slice(u * CHUNK, (u + 1) * CHUNK)
        _split_heads(ck_ref, u, kh_scr, n_heads)
        _split_heads(cv_ref, u, vh_scr, n_heads)
        band = BAND_PAST + CHUNK
        for h in range(n_heads):
            hs = slice(h * d, (h + 1) * d)
            q_h = q_ref[qr, hs]
            s_scr[h, :, 0:BAND_PAST] = (_dot_nt(q_h, kh_scr[h].astype(BF16)) * scale
                                        + bias_scr[0, h, :, 0:BAND_PAST])
            s_scr[h, :, BAND_PAST:band] = (_dot_nt(q_h, kn_ref[qr, hs]) * scale
                                           + bias_scr[0, h, :, BAND_PAST:band])
        for h in range(n_heads):
            s = s_scr[h, :, 0:band]
            p = jnp.exp(s - jnp.max(s, axis=-1, keepdims=True))
            r = 1.0 / jnp.sum(p, axis=-1, keepdims=True)
            p_scr[h, :, 0:band] = (p * r).astype(BF16)
        for h in range(n_heads):
            hs = slice(h * d, (h + 1) * d)
            o = (_dot(p_scr[h, :, 0:BAND_PAST], vh_scr[h].astype(BF16))
                 + _dot(p_scr[h, :, BAND_PAST:band], vn_ref[qr, hs]))
            o_ref[qr, hs] = o.astype(o_ref.dtype)


def _attn_sample(q, kb, vb, cache_k, cache_v, ext, att_into, *, row_blk0, n_sample, n_heads, spb):
    width = n_heads * ATT_HEAD_DIM
    assert n_sample % spb == 0 and row_blk0 % spb == 0
    kern = functools.partial(_attn_sample_kernel, n_heads=n_heads)
    cache_spec = pl.BlockSpec((spb, BAND_PAST * n_heads, ATT_HEAD_DIM), lambda s: (s, 0, 0))
    row_spec = pl.BlockSpec((spb * CHUNK, width), lambda s: (row_blk0 // spb + s, 0))
    return pl.pallas_call(
        kern,
        grid=(n_sample // spb,),
        in_specs=[row_spec, row_spec, row_spec, cache_spec, cache_spec,
                  pl.BlockSpec((n_heads, ATT_WIN), lambda s: (0, 0)),
                  pl.BlockSpec(memory_space=pl.ANY)],
        out_specs=row_spec,
        out_shape=jax.ShapeDtypeStruct(att_into.shape, BF16),
        input_output_aliases={6: 0},
        scratch_shapes=[pltpu.VMEM((2, n_heads, CHUNK, ATT_WIN), F32),
                        pltpu.VMEM((n_heads, BAND_PAST, ATT_HEAD_DIM), F32),
                        pltpu.VMEM((n_heads, BAND_PAST, ATT_HEAD_DIM), F32),
                        pltpu.VMEM((n_heads, CHUNK, ATT_WIN), F32),
                        pltpu.VMEM((n_heads, CHUNK, ATT_WIN), BF16)],
        compiler_params=_cparams(1),
        name="attn_sample",
    )(q, kb, vb, cache_k, cache_v, ext, att_into)


def _mix_kernel(s_ref, a_ref, w1_ref, w2_ref, g1_ref, g2_ref, o_ref):
    a1 = _dot(s_ref[...], w1_ref[...])
    a2 = _dot(a_ref[...], w2_ref[...])
    g1 = _sigmoid(g1_ref[...].astype(F32))
    g2 = _sigmoid(g2_ref[...].astype(F32))
    o_ref[...] = (g1 * a1 + g2 * a2).astype(o_ref.dtype)


def _mix(ssd_y, att, w1, w2, gates_arr, *, gate_col0, tm, tn):
    m, k1 = ssd_y.shape
    k2 = att.shape[1]
    n = w1.shape[1]
    assert att.shape[0] == m and m % tm == 0
    g1_blk = gate_col0 // tn
    g2_blk = (gate_col0 + n) // tn
    return pl.pallas_call(
        _mix_kernel,
        grid=(n // tn, m // tm),
        in_specs=[
            pl.BlockSpec((tm, k1), lambda j, i: (i, 0)),
            pl.BlockSpec((tm, k2), lambda j, i: (i, 0)),
            pl.BlockSpec((k1, tn), lambda j, i: (0, j), pipeline_mode=pl.Buffered(1)),
            pl.BlockSpec((k2, tn), lambda j, i: (0, j), pipeline_mode=pl.Buffered(1)),
            pl.BlockSpec((tm, tn), lambda j, i: (i, g1_blk + j)),
            pl.BlockSpec((tm, tn), lambda j, i: (i, g2_blk + j)),
        ],
        out_specs=pl.BlockSpec((tm, tn), lambda j, i: (i, j)),
        out_shape=jax.ShapeDtypeStruct((m, n), BF16),
        compiler_params=_cparams(2),
        name="branch_mix",
    )(ssd_y, att, w1, w2, gates_arr, gates_arr)


def _layer_norm(t, g, b):
    mu = jnp.mean(t, axis=-1, keepdims=True)
    tc = t - mu
    var = jnp.mean(tc * tc, axis=-1, keepdims=True)
    return tc * lax.rsqrt(var + LN_EPS) * g + b


LN_SUB_ROWS = 256


def _proj_ln_kernel(*refs, alpha, n_k, n_prompt_tiles, split_resid, n_out):
    a_ref, w_ref = refs[0], refs[1]
    pos = 2
    r_refs = refs[pos:pos + (2 if split_resid else 1)]
    pos += len(r_refs)
    g_ref, b_ref = refs[pos], refs[pos + 1]
    pos += 2
    o_refs = refs[pos:pos + n_out]
    acc_ref = refs[pos + n_out] if n_k > 1 else None
    i = pl.program_id(0)
    kk = pl.program_id(1)
    tm = a_ref.shape[0]

    def resid_rows(rs):
        if split_resid:
            return jnp.where(i < n_prompt_tiles, r_refs[0][rs, :], r_refs[1][rs, :])
        return r_refs[0][rs, :]

    def finish(first):
        for r in range(tm // LN_SUB_ROWS):
            rs = slice(r * LN_SUB_ROWS, (r + 1) * LN_SUB_ROWS)
            base = alpha * resid_rows(rs) if first else acc_ref[rs, :]
            y = _layer_norm(base + _dot(a_ref[rs, :], w_ref[...]), g_ref[...], b_ref[...])
            for o_ref in o_refs:
                o_ref[rs, :] = y.astype(o_ref.dtype)

    if n_k == 1:
        finish(True)
        return

    @pl.when(kk == 0)
    def _():
        acc_ref[...] = alpha * resid_rows(slice(None)) + _dot(a_ref[...], w_ref[...])

    if n_k > 2:
        @pl.when(jnp.logical_and(kk > 0, kk < n_k - 1))
        def _():
            acc_ref[...] += _dot(a_ref[...], w_ref[...])

    @pl.when(kk == n_k - 1)
    def _():
        finish(False)


def _proj_ln(a, w, resid, gamma, beta, out_dtypes, *, alpha, tm, n_k, row0, n_rows, m_prompt=None, name):
    k = a.shape[1]
    n = w.shape[1]
    tk = k // n_k
    assert tk * n_k == k and n_rows % tm == 0 and row0 % tm == 0 and tm % LN_SUB_ROWS == 0
    blk0 = row0 // tm
    split_resid = isinstance(resid, tuple)
    row_map = lambda i, kk: (blk0 + i, 0)
    w_mode = dict(pipeline_mode=pl.Buffered(1)) if n_k == 1 else {}
    in_specs = [pl.BlockSpec((tm, tk), lambda i, kk: (blk0 + i, kk)),
                pl.BlockSpec((tk, n), lambda i, kk: (kk, 0), **w_mode)]
    args = [a, w]
    npt = 0
    if split_resid:
        assert row0 == 0 and m_prompt % tm == 0
        npt = m_prompt // tm
        in_specs += [pl.BlockSpec((tm, n), lambda i, kk: (jnp.minimum(i, npt - 1), 0)),
                     pl.BlockSpec((tm, n), lambda i, kk: (jnp.maximum(i - npt, 0), 0))]
        args += list(resid)
    else:
        in_specs.append(pl.BlockSpec((tm, n), row_map))
        args.append(resid)
    in_specs += [pl.BlockSpec((1, n), lambda i, kk: (0, 0))] * 2
    args += [gamma, beta]
    kern = functools.partial(_proj_ln_kernel, alpha=alpha, n_k=n_k, n_prompt_tiles=npt,
                             split_resid=split_resid, n_out=len(out_dtypes))
    return pl.pallas_call(
        kern,
        grid=(n_rows // tm, n_k),
        in_specs=in_specs,
        out_specs=[pl.BlockSpec((tm, n), lambda i, kk: (i, 0)) for _ in out_dtypes],
        out_shape=[jax.ShapeDtypeStruct((n_rows, n), dt) for dt in out_dtypes],
        scratch_shapes=[pltpu.VMEM((tm, n), F32)] if n_k > 1 else [],
        compiler_params=_cparams(2),
        name=name,
    )(*args)


def _glu_kernel(*refs, n_side):
    h_ref, wg_ref, wu_ref = refs[:3]
    side_in = refs[3:3 + n_side]
    o_ref = refs[3 + n_side]
    side_out = refs[4 + n_side:4 + 2 * n_side]
    wg_sc, wu_sc = refs[4 + 2 * n_side:]

    @pl.when(pl.program_id(1) == 0)
    def _():
        _cast_weight_tile(wg_ref, wg_sc)
        _cast_weight_tile(wu_ref, wu_sc)

    for s_in, s_out in zip(side_in, side_out):
        s_out[...] = s_in[...].astype(BF16)

    tm = h_ref.shape[0]
    sub = tm // GLU_SUB_BLOCKS
    for r in range(GLU_SUB_BLOCKS):
        rs = slice(r * sub, (r + 1) * sub)
        h = h_ref[rs, :]
        g = _dot(h, wg_sc[...])
        u = _dot(h, wu_sc[...])
        o_ref[rs, :] = (g * jax.nn.sigmoid(g) * u).astype(o_ref.dtype)


def _glu(h, w_gate_up, *, tm, tf, side=()):
    m, k = h.shape
    d_ff = w_gate_up.shape[1] // 2
    nf = d_ff // tf
    assert nf * tf == d_ff
    side_in, side_out, side_shape = _side_cast_specs(side, m // tm)
    assert all(w.shape[0] // rows <= nf * (m // tm) for w, rows in side)
    return pl.pallas_call(
        functools.partial(_glu_kernel, n_side=len(side)),
        grid=(nf, m // tm),
        in_specs=[
            pl.BlockSpec((tm, k), lambda j, i: (i, 0)),
            pl.BlockSpec((k, tf), lambda j, i: (0, j)),
            pl.BlockSpec((k, tf), lambda j, i: (0, nf + j)),
        ] + side_in,
        out_specs=[pl.BlockSpec((tm, tf), lambda j, i: (i, j))] + side_out,
        out_shape=[jax.ShapeDtypeStruct((m, d_ff), BF16)] + side_shape,
        scratch_shapes=[pltpu.VMEM((k, tf), BF16), pltpu.VMEM((k, tf), BF16)],
        compiler_params=_cparams(2),
        name="ffn_glu",
    )(h, w_gate_up, w_gate_up, *[w for w, _ in side])


def kernel(x_prompt, x_sample, cache_k, cache_v, state_conv, state_ssm, w_in, conv_w, conv_b, dt_bias,
           a_log, d_skip, ssd_norm_w, rel_bias, w_ssd_out, w_att_out, w_o, ln1_g, ln1_b, w_gate_up,
           w_down, ln2_g, ln2_b):
    depth = w_in.shape[0]
    assert depth == 1
    batch, seq, d_model = x_prompt.shape
    n_sample, dec_seq, _ = x_sample.shape
    assert batch == 1 and dec_seq == CHUNK and seq % ATT_TILE == 0
    n_ssd_heads = dt_bias.shape[1]
    inner = n_ssd_heads * SSD_HEADDIM
    conv_dim = conv_w.shape[2]
    n_groups = SSD_GROUPS
    d_state = SSD_STATE
    assert conv_dim == inner + 2 * n_groups * d_state
    n_heads = rel_bias.shape[1]
    width = n_heads * ATT_HEAD_DIM
    assert cache_k.shape[2] == BAND_PAST
    alpha = (2.0 * depth) ** 0.25
    keep = min(BAND_PAST, seq)

    m_p = batch * seq
    m_s = n_sample * dec_seq
    xp2 = x_prompt.reshape(m_p, d_model)
    xs2 = x_sample.reshape(m_s, d_model)

    wt = w_in[0].T
    tn = PROJ_TN
    o_xbc = inner
    o_dt = o_xbc + conv_dim
    o_q = o_dt + n_ssd_heads
    shift = o_q % tn
    assert o_dt % tn == 0 and inner % tn == 0 and conv_dim % tn == 0 and width % tn == 0
    q_blk = o_dt // tn
    xb, dtr = _xcast_dt(xp2, xs2, wt, dt_col=o_dt, n_dt=n_ssd_heads, tm=XCAST_TM)
    zs, w_att_b, w_o_b = _matmul_w32(xb, wt, lambda j: j, inner // tn, [BF16], tm=PROJ_TM, tn=tn, silu=True,
                                     side=[(w_att_out[0], SIDE_CAST_ROWS), (w_o[0], SIDE_CAST_ROWS)],
                                     name="in_proj_z")
    (xbc,) = _matmul_w32(xb, wt, lambda j: inner // tn + j, conv_dim // tn, [F32], tm=PROJ_TM, tn=tn,
                         name="in_proj_xbc")
    wpt = width // tn
    qg, w_ssd_b = _matmul_w32(xb, wt, lambda j: jnp.where(j < wpt, q_blk + j, q_blk + 2 * wpt + j),
                              wpt + 2 * d_model // tn, [BF16], tm=PROJ_TM, tn=tn, shift=shift,
                              side=[(w_ssd_out[0], SIDE_CAST_ROWS)], name="in_proj_qg")
    kb, k_p, k_s = _matmul_w32(xb, wt, lambda j: q_blk + wpt + j, wpt, [BF16], tm=KV_TM, tn=tn, shift=shift,
                               head_out=(keep, m_s), name="in_proj_k")
    vb, v_p, v_s = _matmul_w32(xb, wt, lambda j: q_blk + 2 * wpt + j, wpt, [BF16], tm=KV_TM, tn=tn,
                               shift=shift, head_out=(keep, m_s), name="in_proj_v")

    pad_h = (0, LANES - n_ssd_heads)
    dtb = jnp.pad(dt_bias[0], pad_h).reshape(1, LANES)
    alog = jnp.pad(a_log[0], pad_h).reshape(1, LANES)
    dskx = jnp.repeat(d_skip[0], SSD_HEADDIM).reshape(1, inner)
    emat = (jnp.arange(LANES)[:, None] == (jnp.arange(inner)[None, :] // SSD_HEADDIM)).astype(BF16)
    ssd_args = (zs, xbc, dtr, state_conv[0], state_ssm[0].reshape(n_sample, inner, d_state),
                conv_w[0], conv_b[0].reshape(1, conv_dim), dtb, alog, dskx, ssd_norm_w[0].reshape(1, inner), emat)
    ssd_dims = dict(inner=inner, n_groups=n_groups, d_state=d_state)
    ssd_y, conv_p, h_p = _ssd(*ssd_args, row0=0, n_streams=batch, chunks_per_stream=seq // CHUNK,
                              cps=SSD_PROMPT_CPS, per_stream=False, name="ssd_scan_prompt", **ssd_dims)
    ssd_y, conv_s, h_s = _ssd(*ssd_args, row0=m_p, n_streams=n_sample, chunks_per_stream=1, cps=1,
                              per_stream=True, name="ssd_scan_sample", y_into=ssd_y, **ssd_dims)

    rb = rel_bias[0]
    band = BAND_PAST + CHUNK
    n_edge = band - REL_CLIP
    n_rev = ATT_WIN - n_edge
    rev = rb[:, ::-1][:, 1:1 + n_rev]
    ext = jnp.concatenate([jnp.broadcast_to(rb[:, 2 * REL_CLIP:], (n_heads, n_edge)), rev], axis=1)
    ext = jnp.pad(ext, ((0, 0), (0, ATT_WIN - ext.shape[1])))
    att = _attn_prompt(qg, kb, vb, ext, n_tiles=m_p // ATT_TILE, n_heads=n_heads)
    cache_rows = (n_sample, BAND_PAST * n_heads, ATT_HEAD_DIM)
    att = _attn_sample(qg, kb, vb, cache_k[0].reshape(cache_rows), cache_v[0].reshape(cache_rows), ext, att,
                       row_blk0=m_p // CHUNK, n_sample=n_sample, n_heads=n_heads, spb=ATT_SAMPLE_SPB)

    mixed = _mix(ssd_y, att, w_ssd_b, w_att_b, qg, gate_col0=width, tm=MIX_TM, tn=MIX_TN)
    g1 = ln1_g[0].reshape(1, d_model)
    b1 = ln1_b[0].reshape(1, d_model)
    g2 = ln2_g[0].reshape(1, d_model)
    b2 = ln2_b[0].reshape(1, d_model)
    h_f, h_b = _proj_ln(mixed, w_o_b, (xp2, xs2), g1, b1, [F32, BF16], alpha=alpha, tm=LN_TM,
                        n_k=1, row0=0, n_rows=m_p + m_s, m_prompt=m_p, name="wo_ln")
    act, w_dn = _glu(h_b, w_gate_up[0], tm=GLU_TM, tf=GLU_TF, side=[(w_down[0], DOWN_CAST_ROWS)])
    (y_p,) = _proj_ln(act, w_dn, h_f, g2, b2, [F32], alpha=alpha, tm=LN_TM, n_k=DOWN_K_SPLIT, row0=0,
                      n_rows=m_p, name="down_ln_prompt")
    (y_s,) = _proj_ln(act, w_dn, h_f, g2, b2, [F32], alpha=alpha, tm=LN_TM, n_k=DOWN_K_SPLIT, row0=m_p,
                      n_rows=m_s, name="down_ln_sample")

    return (y_p.reshape(batch, seq, d_model),
            y_s.reshape(n_sample, dec_seq, d_model),
            conv_p.reshape(1, batch, CONV_W - 1, conv_dim),
            h_p.reshape(1, batch, n_ssd_heads, SSD_HEADDIM, d_state),
            k_p.reshape(1, batch, keep, n_heads, ATT_HEAD_DIM),
            v_p.reshape(1, batch, keep, n_heads, ATT_HEAD_DIM),
            conv_s.reshape(1, n_sample, CONV_W - 1, conv_dim),
            h_s.reshape(1, n_sample, n_ssd_heads, SSD_HEADDIM, d_state),
            k_s.reshape(1, n_sample, dec_seq, n_heads, ATT_HEAD_DIM),
            v_s.reshape(1, n_sample, dec_seq, n_heads, ATT_HEAD_DIM))
```

```python
import functools

import jax
import jax.numpy as jnp
from jax import lax
from jax.experimental import pallas as pl
from jax.experimental.pallas import tpu as pltpu

F32 = jnp.float32
BF16 = jnp.bfloat16

CHUNK = 64
SSD_HEADDIM = 64
SSD_GROUPS = 8
SSD_STATE = 128
CONV_W = 4
ATT_HEAD_DIM = 128
BAND_CHUNKS = 8
BAND_PAST = BAND_CHUNKS * CHUNK
REL_CLIP = 128
LN_EPS = 1e-5
RMS_EPS = 1e-5
MASK_NEG = -1e30

LANES = 128
SUBLANES = 8
VMEM_LIMIT_BYTES = 56 * 1024 * 1024

PROJ_TM, PROJ_TN = 1536, 1024
KV_TM = 1024
XCAST_TM = 1024
MIX_TM, MIX_TN = 768, 1024
LN_TM = 512
DOWN_K_SPLIT = 2
GLU_TM, GLU_TF = 2304, 512
SSD_PROMPT_CPS = 4
SIDE_CAST_ROWS = 128
DOWN_CAST_ROWS = 176
ATT_TILE = 512
ATT_SAMPLE_SPB = 2
GLU_SUB_BLOCKS = 2
SILU_SUB_BLOCKS = 2
ATT_WIN = 5 * LANES


def _cparams(n_axes):
    return pltpu.CompilerParams(
        dimension_semantics=("arbitrary",) * n_axes,
        vmem_limit_bytes=VMEM_LIMIT_BYTES,
    )


def _sigmoid(x):
    return 0.5 * jnp.tanh(0.5 * x) + 0.5


def _silu(x):
    h = 0.5 * x
    return h + h * jnp.tanh(h)


def _dot(a, b):
    return jnp.dot(a, b, preferred_element_type=F32)


def _dot_nt(a, b):
    return lax.dot_general(a, b, (((1,), (1,)), ((), ())), preferred_element_type=F32)


def _dot_tn(a, b):
    return lax.dot_general(a, b, (((0,), (0,)), ((), ())), preferred_element_type=F32)


def _xcast_dt_kernel(xp_ref, xs_ref, wd_ref, xb_ref, dt_ref, *, n_prompt_tiles, n_dt):
    i = pl.program_id(0)
    x = jnp.where(i < n_prompt_tiles, xp_ref[...], xs_ref[...]).astype(BF16)
    xb_ref[...] = x
    acc = _dot_nt(x, wd_ref[...].astype(BF16))
    lane = lax.broadcasted_iota(jnp.int32, acc.shape, 1)
    dt_ref[...] = jnp.where(lane < n_dt, acc, 0.0)


def _xcast_dt(xp, xs, wt, *, dt_col, n_dt, tm):
    m_p, k = xp.shape
    m_s = xs.shape[0]
    npt = m_p // tm
    n_tiles = npt + m_s // tm
    assert dt_col % LANES == 0 and n_dt <= LANES
    kern = functools.partial(_xcast_dt_kernel, n_prompt_tiles=npt, n_dt=n_dt)
    return pl.pallas_call(
        kern,
        grid=(n_tiles,),
        in_specs=[
            pl.BlockSpec((tm, k), lambda i: (jnp.minimum(i, npt - 1), 0)),
            pl.BlockSpec((tm, k), lambda i: (jnp.maximum(i - npt, 0), 0)),
            pl.BlockSpec((LANES, k), lambda i: (dt_col // LANES, 0)),
        ],
        out_specs=[pl.BlockSpec((tm, k), lambda i: (i, 0)),
                   pl.BlockSpec((tm, LANES), lambda i: (i, 0))],
        out_shape=[jax.ShapeDtypeStruct((m_p + m_s, k), BF16),
                   jax.ShapeDtypeStruct((m_p + m_s, LANES), F32)],
        compiler_params=_cparams(1),
        name="xcast_dt",
    )(xp, xs, wt)


CAST_ROWS = 256


def _cast_rows(src_ref, src_row0, dst_ref, dst_row0, n_rows):
    done = 0
    while done < n_rows:
        step = min(CAST_ROWS, n_rows - done)
        dst_ref[dst_row0 + done:dst_row0 + done + step, :] = (
            src_ref[src_row0 + done:src_row0 + done + step, :].astype(BF16))
        done += step


def _cast_weight_tile(w_ref, wsc):
    _cast_rows(w_ref, 0, wsc, 0, wsc.shape[0])


def _side_cast_specs(side, n_inner):
    in_specs, out_specs, out_shape = [], [], []
    for w, rows in side:
        n_blk = w.shape[0] // rows
        assert n_blk * rows == w.shape[0] and rows % 16 == 0
        idx = lambda j, i, n_blk=n_blk: (jnp.minimum(j * n_inner + i, n_blk - 1), 0)
        in_specs.append(pl.BlockSpec((rows, w.shape[1]), idx))
        out_specs.append(pl.BlockSpec((rows, w.shape[1]), idx))
        out_shape.append(jax.ShapeDtypeStruct(w.shape, BF16))
    return in_specs, out_specs, out_shape


def _mm_w32_kernel(*refs, shift, n_out, head_rows, silu, n_side):
    x_ref, wa_ref = refs[0], refs[1]
    pos = 2
    wb_ref = None
    if shift:
        wb_ref = refs[pos]
        pos += 1
    side_in = refs[pos:pos + n_side]
    pos += n_side
    o_refs = refs[pos:pos + n_out]
    pos += n_out
    hp_ref = hs_ref = None
    if head_rows is not None:
        hp_ref, hs_ref = refs[pos], refs[pos + 1]
        pos += 2
    side_out = refs[pos:pos + n_side]
    pos += n_side
    wsc = refs[pos]
    i = pl.program_id(1)

    @pl.when(i == 0)
    def _():
        tn = wsc.shape[0]
        _cast_rows(wa_ref, shift, wsc, 0, tn - shift)
        if shift:
            _cast_rows(wb_ref, 0, wsc, tn - shift, shift)

    for s_in, s_out in zip(side_in, side_out):
        s_out[...] = s_in[...].astype(BF16)

    if silu:
        sub = x_ref.shape[0] // SILU_SUB_BLOCKS
        for r in range(SILU_SUB_BLOCKS):
            rs = slice(r * sub, (r + 1) * sub)
            res = _silu(_dot_nt(x_ref[rs, :], wsc[...]))
            for o_ref in o_refs:
                o_ref[rs, :] = res.astype(o_ref.dtype)
        return

    acc = _dot_nt(x_ref[...], wsc[...])
    for o_ref in o_refs:
        o_ref[...] = acc.astype(o_ref.dtype)

    if head_rows is not None:
        (tile_p, row0_p), (tile_s, row0_s) = head_rows

        @pl.when(i == tile_p)
        def _():
            hp_ref[...] = acc[row0_p:row0_p + hp_ref.shape[0], :]

        @pl.when(i == tile_s)
        def _():
            hs_ref[...] = acc[row0_s:row0_s + hs_ref.shape[0], :]


def _matmul_w32(x, wt, col_blk, n_col_tiles, out_dtypes, *, tm, tn, shift=0, head_out=None, silu=False,
                side=(), name):
    m, k = x.shape
    assert m % tm == 0 and tn % LANES == 0 and shift % 16 == 0 and shift < tn
    assert not silu or (head_out is None and tm % (16 * SILU_SUB_BLOCKS) == 0)
    in_specs = [pl.BlockSpec((tm, k), lambda j, i: (i, 0)),
                pl.BlockSpec((tn, k), lambda j, i: (col_blk(j), 0))]
    args = [x, wt]
    if shift:
        assert tn % shift == 0
        per = tn // shift
        in_specs.append(pl.BlockSpec((shift, k), lambda j, i: ((col_blk(j) + 1) * per, 0)))
        args.append(wt)
    side_in, side_out, side_shape = _side_cast_specs(side, m // tm)
    assert all(w.shape[0] // rows <= n_col_tiles * (m // tm) for w, rows in side)
    in_specs += side_in
    args += [w for w, _ in side]
    n = n_col_tiles * tn
    out_specs = [pl.BlockSpec((tm, tn), lambda j, i: (i, j)) for _ in out_dtypes]
    out_shape = [jax.ShapeDtypeStruct((m, n), dt) for dt in out_dtypes]
    head_rows = None
    if head_out is not None:
        n_keep, m_s = head_out
        m_p = m - m_s
        head_rows = (divmod(m_p - n_keep, tm), divmod(m_p, tm))
        assert head_rows[0][1] + n_keep <= tm and head_rows[1][1] + m_s <= tm
        out_specs += [pl.BlockSpec((n_keep, tn), lambda j, i: (0, j)),
                      pl.BlockSpec((m_s, tn), lambda j, i: (0, j))]
        out_shape += [jax.ShapeDtypeStruct((n_keep, n), F32), jax.ShapeDtypeStruct((m_s, n), F32)]
    out_specs += side_out
    out_shape += side_shape
    kern = functools.partial(_mm_w32_kernel, shift=shift, n_out=len(out_dtypes), head_rows=head_rows,
                             silu=silu, n_side=len(side))
    return pl.pallas_call(
        kern,
        grid=(n_col_tiles, m // tm),
        in_specs=in_specs,
        out_specs=out_specs,
        out_shape=out_shape,
        scratch_shapes=[pltpu.VMEM((tn, k), BF16)],
        compiler_params=_cparams(2),
        name=name,
    )(*args)


def _softplus(x):
    return jnp.maximum(x, 0.0) + jnp.log1p(jnp.exp(-jnp.abs(x)))


def _ssd_chunk(u, zs_ref, xbc_ref, dt_ref, cw_ref, cb_ref, dtb_ref, alog_ref, dsk_ref, nw_ref, e_ref, y_ref,
               hist8, act, ht, cst, ex, ysc, *, inner, n_groups, d_state):
    r0 = u * CHUNK
    rows = slice(r0, r0 + CHUNK)
    gw = inner // n_groups
    conv_dim = inner + 2 * n_groups * d_state

    for j in range(conv_dim // LANES):
        sl = slice(j * LANES, (j + 1) * LANES)
        if u == 0:
            blk = jnp.concatenate([hist8[:, sl], xbc_ref[0:CHUNK, sl]], axis=0)
        else:
            blk = xbc_ref[r0 - SUBLANES:r0 + CHUNK, sl]
        prev = pltpu.roll(blk, 1, 0)
        near = blk * cw_ref[3:4, sl] + prev * cw_ref[2:3, sl]
        far = pltpu.roll(blk * cw_ref[1:2, sl] + prev * cw_ref[0:1, sl], 2, 0)
        a = cb_ref[:, sl] + (near + far)[SUBLANES:, :]
        act[rows, sl] = _silu(a)

    dt = _softplus(dt_ref[rows, :] + dtb_ref[...])
    a_neg = -jnp.exp(alog_ref[...])
    da = dt * a_neg
    ri = lax.broadcasted_iota(jnp.int32, (CHUNK, CHUNK), 0)
    ci = lax.broadcasted_iota(jnp.int32, (CHUNK, CHUNK), 1)
    tri = (ri >= ci).astype(F32)
    cs = jnp.dot(tri, da, precision=lax.Precision.HIGHEST, preferred_element_type=F32)
    ecs = jnp.exp(cs)
    cs_last = cs[CHUNK - 1:CHUNK, :]
    dout = jnp.exp(cs_last - cs) * dt
    cd = jnp.exp(cs_last)
    cd_hi = cd.astype(BF16)
    cd_lo = (cd - cd_hi.astype(F32)).astype(BF16)
    lhs = jnp.concatenate(
        [dt.astype(BF16), ecs.astype(BF16), dout.astype(BF16),
         jnp.broadcast_to(cd_hi, (16, LANES)), jnp.broadcast_to(cd_lo, (16, LANES))], axis=0)
    ex[u] = _dot(lhs, e_ref[...])

    cst[u] = jnp.concatenate([cs, jnp.zeros((LANES - CHUNK, LANES), F32)], axis=0).T
    lane = lax.broadcasted_iota(jnp.int32, (CHUNK, LANES), 1)
    lo_half = lane < CHUNK
    n_pairs = inner // LANES
    ev = cst[u, pl.ds(0, n_pairs, stride=2), :]
    od = cst[u, pl.ds(1, n_pairs, stride=2), :]
    lane_p = lax.broadcasted_iota(jnp.int32, (n_pairs, LANES), 1)
    cst2 = jnp.where(lane_p < CHUNK, ev, pltpu.roll(od, CHUNK, 1))
    row = lax.broadcasted_iota(jnp.int32, (CHUNK, LANES), 0)
    causal2 = row >= jnp.where(lo_half, lane, lane - CHUNK)

    pairs_per_group = gw // LANES
    for g in range(n_groups):
        gs = slice(g * gw, (g + 1) * gw)
        bm_g = act[rows, inner + g * d_state: inner + (g + 1) * d_state].astype(BF16)
        cm_g = act[rows, inner + (n_groups + g) * d_state: inner + (n_groups + g + 1) * d_state].astype(BF16)
        cb2 = _dot_nt(cm_g, jnp.concatenate([bm_g, bm_g], axis=0))
        h_g = ht[:, gs]
        yoff = _dot(cm_g, h_g.astype(BF16))
        for p in range(pairs_per_group):
            k = g * pairs_per_group + p
            ls = slice(k * LANES, (k + 1) * LANES)
            csc = jnp.where(lo_half,
                            jnp.broadcast_to(cs[:, 2 * k:2 * k + 1], (CHUNK, LANES)),
                            jnp.broadcast_to(cs[:, 2 * k + 1:2 * k + 2], (CHUNK, LANES)))
            diff = csc - cst2[k:k + 1, :]
            dec = jnp.exp(jnp.where(causal2, diff, MASK_NEG))
            m2 = (cb2 * dec).astype(BF16)
            xs_p = act[rows, ls]
            xdt = xs_p * ex[u, 0:CHUNK, ls]
            x2 = jnp.concatenate([jnp.where(lo_half, xdt, 0.0), jnp.where(lo_half, 0.0, xdt)],
                                 axis=0).astype(BF16)
            yd = _dot(m2, x2)
            ysc[rows, ls] = (yd + yoff[:, p * LANES:(p + 1) * LANES] * ex[u, CHUNK:2 * CHUNK, ls]
                             + dsk_ref[:, ls] * xs_p)
        xw = (act[rows, gs] * ex[u, 2 * CHUNK:3 * CHUNK, gs]).astype(BF16)
        st = _dot_tn(bm_g, xw)
        cdx = ex[u, 3 * CHUNK:3 * CHUNK + 1, gs] + ex[u, 3 * CHUNK + 16:3 * CHUNK + 17, gs]
        ht[:, gs] = h_g * cdx + st

    for g in range(n_groups):
        gs = slice(g * gw, (g + 1) * gw)
        v = ysc[rows, gs] * zs_ref[rows, gs].astype(F32)
        ms = jnp.mean(v * v, axis=-1, keepdims=True)
        y_ref[rows, gs] = (v * lax.rsqrt(ms + RMS_EPS) * nw_ref[:, gs]).astype(y_ref.dtype)


def _ssd_kernel(*refs, cps, per_stream, aliased, inner, n_groups, d_state):
    (zs_ref, xbc_ref, dt_ref, hist_ref, h0_ref, cw_ref, cb_ref, dtb_ref, alog_ref,
     dsk_ref, nw_ref, e_ref) = refs[:12]
    y_ref, conv_ref, ho_ref, hist8, act, ht, cst, ex, ysc = refs[12 + int(aliased):]
    c = pl.program_id(0)
    n_rows = cps * CHUNK
    tail = slice(n_rows - (CONV_W - 1), n_rows)
    last = slice(SUBLANES - (CONV_W - 1), SUBLANES)

    @pl.when(c == 0)
    def _():
        hist8[...] = jnp.zeros_like(hist8)
        if not per_stream:
            ht[...] = jnp.zeros_like(ht)

    if per_stream:
        hist8[last, :] = hist_ref[0]
        ht[...] = h0_ref[0].T

    conv_ref[0] = xbc_ref[tail, :]

    for u in range(cps):
        _ssd_chunk(u, zs_ref, xbc_ref, dt_ref, cw_ref, cb_ref, dtb_ref, alog_ref, dsk_ref, nw_ref, e_ref, y_ref,
                   hist8, act, ht, cst, ex, ysc, inner=inner, n_groups=n_groups, d_state=d_state)
    hist8[last, :] = xbc_ref[tail, :]

    if per_stream:
        ho_ref[0] = ht[...].T
    else:
        @pl.when(c == pl.num_programs(0) - 1)
        def _():
            ho_ref[0] = ht[...].T


def _ssd(zs, xbc, dtr, hist, h0, conv_w, conv_b, dtb, alog, dskx, normw, emat, *,
         row0, n_streams, chunks_per_stream, cps, per_stream, inner, n_groups, d_state, name, y_into=None):
    conv_dim = xbc.shape[1]
    hp = inner
    n_rows = cps * CHUNK
    assert chunks_per_stream % cps == 0 and row0 % n_rows == 0
    assert per_stream == (chunks_per_stream == cps) and (per_stream or n_streams == 1)
    steps_per_stream = chunks_per_stream // cps
    blk0 = row0 // n_rows
    stream = (lambda c: c) if per_stream else (lambda c: 0)
    aliased = y_into is not None

    kern = functools.partial(_ssd_kernel, cps=cps, per_stream=per_stream, aliased=aliased, inner=inner,
                             n_groups=n_groups, d_state=d_state)
    const = lambda c: (0, 0)
    row_map = lambda c: (blk0 + c, 0)
    extra_specs = [pl.BlockSpec(memory_space=pl.ANY)] if aliased else []
    extra_args = [y_into] if aliased else []
    return pl.pallas_call(
        kern,
        grid=(n_streams * steps_per_stream,),
        input_output_aliases={12: 0} if aliased else {},
        in_specs=[
            pl.BlockSpec((n_rows, inner), row_map),
            pl.BlockSpec((n_rows, conv_dim), row_map),
            pl.BlockSpec((n_rows, LANES), row_map),
            pl.BlockSpec((1, CONV_W - 1, conv_dim), lambda c: (stream(c), 0, 0)),
            pl.BlockSpec((1, hp, d_state), lambda c: (stream(c), 0, 0)),
            pl.BlockSpec((CONV_W, conv_dim), const),
            pl.BlockSpec((1, conv_dim), const),
            pl.BlockSpec((1, LANES), const),
            pl.BlockSpec((1, LANES), const),
            pl.BlockSpec((1, inner), const),
            pl.BlockSpec((1, inner), const),
            pl.BlockSpec((LANES, inner), const),
        ] + extra_specs,
        out_specs=[
            pl.BlockSpec((n_rows, inner), row_map),
            pl.BlockSpec((1, CONV_W - 1, conv_dim), lambda c: (stream(c), 0, 0)),
            pl.BlockSpec((1, hp, d_state), lambda c: (stream(c), 0, 0)),
        ],
        out_shape=[
            jax.ShapeDtypeStruct((zs.shape[0], inner), BF16),
            jax.ShapeDtypeStruct((n_streams, CONV_W - 1, conv_dim), F32),
            jax.ShapeDtypeStruct((n_streams, hp, d_state), F32),
        ],
        scratch_shapes=[
            pltpu.VMEM((SUBLANES, conv_dim), F32),
            pltpu.VMEM((n_rows, conv_dim), F32),
            pltpu.VMEM((d_state, inner), F32),
            pltpu.VMEM((cps, LANES, LANES), F32),
            pltpu.VMEM((cps, 3 * CHUNK + 32, inner), F32),
            pltpu.VMEM((n_rows, inner), F32),
        ],
        compiler_params=_cparams(1),
        name=name,
    )(zs, xbc, dtr, hist, h0, conv_w, conv_b, dtb, alog, dskx, normw, emat, *extra_args)


def _band_bias_tables(ext_ref, bias_scr, n_heads):
    col = lax.broadcasted_iota(jnp.int32, (CHUNK, ATT_WIN), 1)
    band = BAND_PAST + CHUNK
    for h in range(n_heads):
        e = jnp.broadcast_to(ext_ref[h:h + 1, :], (CHUNK, ATT_WIN))
        even = pltpu.roll(e, ATT_WIN - (CHUNK - 1), 1, stride=1, stride_axis=0)
        odd = pltpu.roll(e, 1, 1, stride=1, stride_axis=0)
        bias_scr[0, h] = jnp.where(col < band, even, MASK_NEG)
        bias_scr[1, h] = jnp.where(col >= CHUNK, odd, MASK_NEG)


ATT_BLOCKS = BAND_PAST // ATT_TILE + 1


def _attn_prompt_kernel(*refs, n_heads):
    q_ref = refs[0]
    k_refs = refs[1:1 + ATT_BLOCKS]
    v_refs = refs[1 + ATT_BLOCKS:1 + 2 * ATT_BLOCKS]
    ext_ref, o_ref, bias_scr, s_scr, p_scr = refs[1 + 2 * ATT_BLOCKS:]
    i = pl.program_id(0)
    d = ATT_HEAD_DIM
    scale = d ** -0.5
    blk = ATT_TILE

    @pl.when(i == 0)
    def _():
        _band_bias_tables(ext_ref, bias_scr, n_heads)
        p_scr[...] = jnp.zeros_like(p_scr)

    past_mask = [jnp.where(i < ATT_BLOCKS - 1 - b, MASK_NEG, 0.0).astype(F32) for b in range(ATT_BLOCKS - 1)]

    for h in range(n_heads):
        hs = slice(h * d, (h + 1) * d)
        q_h = q_ref[:, hs]
        for b in range(ATT_BLOCKS):
            s = _dot_nt(q_h, k_refs[b][:, hs]) * scale
            s_scr[:, b * blk:(b + 1) * blk] = s + past_mask[b] if b < ATT_BLOCKS - 1 else s
        for t in range(ATT_TILE // CHUNK):
            rs = slice(t * CHUNK, (t + 1) * CHUNK)
            c0 = LANES * (t // 2)
            ws = slice(c0, c0 + ATT_WIN)
            s = s_scr[rs, ws] + bias_scr[t % 2, h]
            m = jnp.max(s, axis=-1, keepdims=True)
            p = jnp.exp(s - m)
            r = 1.0 / jnp.sum(p, axis=-1, keepdims=True)
            p_scr[rs, ws] = (p * r).astype(BF16)
        o = _dot(p_scr[:, 0:blk], v_refs[0][:, hs])
        for b in range(1, ATT_BLOCKS):
            o = o + _dot(p_scr[:, b * blk:(b + 1) * blk], v_refs[b][:, hs])
        o_ref[:, hs] = o.astype(o_ref.dtype)


def _attn_prompt(q, kb, vb, ext, *, n_tiles, n_heads):
    width = n_heads * ATT_HEAD_DIM
    assert BAND_PAST % ATT_TILE == 0 and ATT_TILE % (2 * CHUNK) == 0
    kern = functools.partial(_attn_prompt_kernel, n_heads=n_heads)
    window = [pl.BlockSpec((ATT_TILE, width), lambda i, back=back: (jnp.maximum(i - back, 0), 0))
              for back in range(ATT_BLOCKS - 1, -1, -1)]
    return pl.pallas_call(
        kern,
        grid=(n_tiles,),
        in_specs=[pl.BlockSpec((ATT_TILE, width), lambda i: (i, 0))] + window + window
                 + [pl.BlockSpec((n_heads, ATT_WIN), lambda i: (0, 0))],
        out_specs=pl.BlockSpec((ATT_TILE, width), lambda i: (i, 0)),
        out_shape=jax.ShapeDtypeStruct((q.shape[0], width), BF16),
        scratch_shapes=[
            pltpu.VMEM((2, n_heads, CHUNK, ATT_WIN), F32),
            pltpu.VMEM((ATT_TILE, ATT_BLOCKS * ATT_TILE), F32),
            pltpu.VMEM((ATT_TILE, ATT_BLOCKS * ATT_TILE), BF16),
        ],
        compiler_params=_cparams(1),
        name="attn_prompt",
    )(q, *([kb] * ATT_BLOCKS), *([vb] * ATT_BLOCKS), ext)


def _sublane_transpose8(t):
    sub = lax.broadcasted_iota(jnp.int32, (SUBLANES, LANES), 0)
    for k in (4, 2, 1):
        keep = (sub // k) % 2 == 0
        new = [None] * SUBLANES
        for i in range(SUBLANES):
            if (i // k) % 2 == 0:
                j = i + k
                new[i] = jnp.where(keep, t[i], pltpu.roll(t[j], k, 0))
                new[j] = jnp.where(keep, pltpu.roll(t[i], SUBLANES - k, 0), t[j])
        t = new
    return t


def _split_heads(c_ref, u, dst, n_heads):
    halves = n_heads // SUBLANES

    def body(g, carry):
        p0 = pl.multiple_of(g * SUBLANES, SUBLANES)
        for hh in range(halves):
            tiles = [c_ref[u, pl.ds(pl.multiple_of((p0 + p) * n_heads + hh * SUBLANES, SUBLANES), SUBLANES), :]
                     for p in range(SUBLANES)]
            for a, tile in enumerate(_sublane_transpose8(tiles)):
                dst[hh * SUBLANES + a, pl.ds(p0, SUBLANES), :] = tile
        return carry

    lax.fori_loop(0, BAND_PAST // SUBLANES, body, 0, unroll=2)


def _attn_sample_kernel(q_ref, kn_ref, vn_ref, ck_ref, cv_ref, ext_ref, att_in_ref, o_ref,
                        bias_scr, kh_scr, vh_scr, s_scr, p_scr, *, n_heads):
    del att_in_ref
    d = ATT_HEAD_DIM
    scale = d ** -0.5

    @pl.when(pl.program_id(0) == 0)
    def _():
        _band_bias_tables(ext_ref, bias_scr, n_heads)

    for u in range(ck_ref.shape[0]):
        qr = slice(u * CHUNK, (u + 1) * CHUNK)
        _split_heads(ck_ref, u, kh_scr, n_heads)
        _split_heads(cv_ref, u, vh_scr, n_heads)
        band = BAND_PAST + CHUNK
        for h in range(n_heads):
            hs = slice(h * d, (h + 1) * d)
            q_h = q_ref[qr, hs]
            s_scr[h, :, 0:BAND_PAST] = (_dot_nt(q_h, kh_scr[h].astype(BF16)) * scale
                                        + bias_scr[0, h, :, 0:BAND_PAST])
            s_scr[h, :, BAND_PAST:band] = (_dot_nt(q_h, kn_ref[qr, hs]) * scale
                                           + bias_scr[0, h, :, BAND_PAST:band])
        for h in range(n_heads):
            s = s_scr[h, :, 0:band]
            p = jnp.exp(s - jnp.max(s, axis=-1, keepdims=True))
            r = 1.0 / jnp.sum(p, axis=-1, keepdims=True)
            p_scr[h, :, 0:band] = (p * r).astype(BF16)
        for h in range(n_heads):
            hs = slice(h * d, (h + 1) * d)
            o = (_dot(p_scr[h, :, 0:BAND_PAST], vh_scr[h].astype(BF16))
                 + _dot(p_scr[h, :, BAND_PAST:band], vn_ref[qr, hs]))
            o_ref[qr, hs] = o.astype(o_ref.dtype)


def _attn_sample(q, kb, vb, cache_k, cache_v, ext, att_into, *, row_blk0, n_sample, n_heads, spb):
    width = n_heads * ATT_HEAD_DIM
    assert n_sample % spb == 0 and row_blk0 % spb == 0
    kern = functools.partial(_attn_sample_kernel, n_heads=n_heads)
    cache_spec = pl.BlockSpec((spb, BAND_PAST * n_heads, ATT_HEAD_DIM), lambda s: (s, 0, 0))
    row_spec = pl.BlockSpec((spb * CHUNK, width), lambda s: (row_blk0 // spb + s, 0))
    return pl.pallas_call(
        kern,
        grid=(n_sample // spb,),
        in_specs=[row_spec, row_spec, row_spec, cache_spec, cache_spec,
                  pl.BlockSpec((n_heads, ATT_WIN), lambda s: (0, 0)),
                  pl.BlockSpec(memory_space=pl.ANY)],
        out_specs=row_spec,
        out_shape=jax.ShapeDtypeStruct(att_into.shape, BF16),
        input_output_aliases={6: 0},
        scratch_shapes=[pltpu.VMEM((2, n_heads, CHUNK, ATT_WIN), F32),
                        pltpu.VMEM((n_heads, BAND_PAST, ATT_HEAD_DIM), F32),
                        pltpu.VMEM((n_heads, BAND_PAST, ATT_HEAD_DIM), F32),
                        pltpu.VMEM((n_heads, CHUNK, ATT_WIN), F32),
                        pltpu.VMEM((n_heads, CHUNK, ATT_WIN), BF16)],
        compiler_params=_cparams(1),
        name="attn_sample",
    )(q, kb, vb, cache_k, cache_v, ext, att_into)


def _mix_kernel(s_ref, a_ref, w1_ref, w2_ref, g1_ref, g2_ref, o_ref):
    a1 = _dot(s_ref[...], w1_ref[...])
    a2 = _dot(a_ref[...], w2_ref[...])
    g1 = _sigmoid(g1_ref[...].astype(F32))
    g2 = _sigmoid(g2_ref[...].astype(F32))
    o_ref[...] = (g1 * a1 + g2 * a2).astype(o_ref.dtype)


def _mix(ssd_y, att, w1, w2, gates_arr, *, gate_col0, tm, tn):
    m, k1 = ssd_y.shape
    k2 = att.shape[1]
    n = w1.shape[1]
    assert att.shape[0] == m and m % tm == 0
    g1_blk = gate_col0 // tn
    g2_blk = (gate_col0 + n) // tn
    return pl.pallas_call(
        _mix_kernel,
        grid=(n // tn, m // tm),
        in_specs=[
            pl.BlockSpec((tm, k1), lambda j, i: (i, 0)),
            pl.BlockSpec((tm, k2), lambda j, i: (i, 0)),
            pl.BlockSpec((k1, tn), lambda j, i: (0, j), pipeline_mode=pl.Buffered(1)),
            pl.BlockSpec((k2, tn), lambda j, i: (0, j), pipeline_mode=pl.Buffered(1)),
            pl.BlockSpec((tm, tn), lambda j, i: (i, g1_blk + j)),
            pl.BlockSpec((tm, tn), lambda j, i: (i, g2_blk + j)),
        ],
        out_specs=pl.BlockSpec((tm, tn), lambda j, i: (i, j)),
        out_shape=jax.ShapeDtypeStruct((m, n), BF16),
        compiler_params=_cparams(2),
        name="branch_mix",
    )(ssd_y, att, w1, w2, gates_arr, gates_arr)


def _layer_norm(t, g, b):
    mu = jnp.mean(t, axis=-1, keepdims=True)
    tc = t - mu
    var = jnp.mean(tc * tc, axis=-1, keepdims=True)
    return tc * lax.rsqrt(var + LN_EPS) * g + b


LN_SUB_ROWS = 256


def _proj_ln_kernel(*refs, alpha, n_k, n_prompt_tiles, split_resid, n_out):
    a_ref, w_ref = refs[0], refs[1]
    pos = 2
    r_refs = refs[pos:pos + (2 if split_resid else 1)]
    pos += len(r_refs)
    g_ref, b_ref = refs[pos], refs[pos + 1]
    pos += 2
    o_refs = refs[pos:pos + n_out]
    acc_ref = refs[pos + n_out] if n_k > 1 else None
    i = pl.program_id(0)
    kk = pl.program_id(1)
    tm = a_ref.shape[0]

    def resid_rows(rs):
        if split_resid:
            return jnp.where(i < n_prompt_tiles, r_refs[0][rs, :], r_refs[1][rs, :])
        return r_refs[0][rs, :]

    def finish(first):
        for r in range(tm // LN_SUB_ROWS):
            rs = slice(r * LN_SUB_ROWS, (r + 1) * LN_SUB_ROWS)
            base = alpha * resid_rows(rs) if first else acc_ref[rs, :]
            y = _layer_norm(base + _dot(a_ref[rs, :], w_ref[...]), g_ref[...], b_ref[...])
            for o_ref in o_refs:
                o_ref[rs, :] = y.astype(o_ref.dtype)

    if n_k == 1:
        finish(True)
        return

    @pl.when(kk == 0)
    def _():
        acc_ref[...] = alpha * resid_rows(slice(None)) + _dot(a_ref[...], w_ref[...])

    if n_k > 2:
        @pl.when(jnp.logical_and(kk > 0, kk < n_k - 1))
        def _():
            acc_ref[...] += _dot(a_ref[...], w_ref[...])

    @pl.when(kk == n_k - 1)
    def _():
        finish(False)


def _proj_ln(a, w, resid, gamma, beta, out_dtypes, *, alpha, tm, n_k, row0, n_rows, m_prompt=None, name):
    k = a.shape[1]
    n = w.shape[1]
    tk = k // n_k
    assert tk * n_k == k and n_rows % tm == 0 and row0 % tm == 0 and tm % LN_SUB_ROWS == 0
    blk0 = row0 // tm
    split_resid = isinstance(resid, tuple)
    row_map = lambda i, kk: (blk0 + i, 0)
    w_mode = dict(pipeline_mode=pl.Buffered(1)) if n_k == 1 else {}
    in_specs = [pl.BlockSpec((tm, tk), lambda i, kk: (blk0 + i, kk)),
                pl.BlockSpec((tk, n), lambda i, kk: (kk, 0), **w_mode)]
    args = [a, w]
    npt = 0
    if split_resid:
        assert row0 == 0 and m_prompt % tm == 0
        npt = m_prompt // tm
        in_specs += [pl.BlockSpec((tm, n), lambda i, kk: (jnp.minimum(i, npt - 1), 0)),
                     pl.BlockSpec((tm, n), lambda i, kk: (jnp.maximum(i - npt, 0), 0))]
        args += list(resid)
    else:
        in_specs.append(pl.BlockSpec((tm, n), row_map))
        args.append(resid)
    in_specs += [pl.BlockSpec((1, n), lambda i, kk: (0, 0))] * 2
    args += [gamma, beta]
    kern = functools.partial(_proj_ln_kernel, alpha=alpha, n_k=n_k, n_prompt_tiles=npt,
                             split_resid=split_resid, n_out=len(out_dtypes))
    return pl.pallas_call(
        kern,
        grid=(n_rows // tm, n_k),
        in_specs=in_specs,
        out_specs=[pl.BlockSpec((tm, n), lambda i, kk: (i, 0)) for _ in out_dtypes],
        out_shape=[jax.ShapeDtypeStruct((n_rows, n), dt) for dt in out_dtypes],
        scratch_shapes=[pltpu.VMEM((tm, n), F32)] if n_k > 1 else [],
        compiler_params=_cparams(2),
        name=name,
    )(*args)


def _glu_kernel(*refs, n_side):
    h_ref, wg_ref, wu_ref = refs[:3]
    side_in = refs[3:3 + n_side]
    o_ref = refs[3 + n_side]
    side_out = refs[4 + n_side:4 + 2 * n_side]
    wg_sc, wu_sc = refs[4 + 2 * n_side:]

    @pl.when(pl.program_id(1) == 0)
    def _():
        _cast_weight_tile(wg_ref, wg_sc)
        _cast_weight_tile(wu_ref, wu_sc)

    for s_in, s_out in zip(side_in, side_out):
        s_out[...] = s_in[...].astype(BF16)

    tm = h_ref.shape[0]
    sub = tm // GLU_SUB_BLOCKS
    for r in range(GLU_SUB_BLOCKS):
        rs = slice(r * sub, (r + 1) * sub)
        h = h_ref[rs, :]
        g = _dot(h, wg_sc[...])
        u = _dot(h, wu_sc[...])
        o_ref[rs, :] = (g * jax.nn.sigmoid(g) * u).astype(o_ref.dtype)


def _glu(h, w_gate_up, *, tm, tf, side=()):
    m, k = h.shape
    d_ff = w_gate_up.shape[1] // 2
    nf = d_ff // tf
    assert nf * tf == d_ff
    side_in, side_out, side_shape = _side_cast_specs(side, m // tm)
    assert all(w.shape[0] // rows <= nf * (m // tm) for w, rows in side)
    return pl.pallas_call(
        functools.partial(_glu_kernel, n_side=len(side)),
        grid=(nf, m // tm),
        in_specs=[
            pl.BlockSpec((tm, k), lambda j, i: (i, 0)),
            pl.BlockSpec((k, tf), lambda j, i: (0, j)),
            pl.BlockSpec((k, tf), lambda j, i: (0, nf + j)),
        ] + side_in,
        out_specs=[pl.BlockSpec((tm, tf), lambda j, i: (i, j))] + side_out,
        out_shape=[jax.ShapeDtypeStruct((m, d_ff), BF16)] + side_shape,
        scratch_shapes=[pltpu.VMEM((k, tf), BF16), pltpu.VMEM((k, tf), BF16)],
        compiler_params=_cparams(2),
        name="ffn_glu",
    )(h, w_gate_up, w_gate_up, *[w for w, _ in side])


def kernel(x_prompt, x_sample, cache_k, cache_v, state_conv, state_ssm, w_in, conv_w, conv_b, dt_bias,
           a_log, d_skip, ssd_norm_w, rel_bias, w_ssd_out, w_att_out, w_o, ln1_g, ln1_b, w_gate_up,
           w_down, ln2_g, ln2_b):
    depth = w_in.shape[0]
    assert depth == 1
    batch, seq, d_model = x_prompt.shape
    n_sample, dec_seq, _ = x_sample.shape
    assert batch == 1 and dec_seq == CHUNK and seq % ATT_TILE == 0
    n_ssd_heads = dt_bias.shape[1]
    inner = n_ssd_heads * SSD_HEADDIM
    conv_dim = conv_w.shape[2]
    n_groups = SSD_GROUPS
    d_state = SSD_STATE
    assert conv_dim == inner + 2 * n_groups * d_state
    n_heads = rel_bias.shape[1]
    width = n_heads * ATT_HEAD_DIM
    assert cache_k.shape[2] == BAND_PAST
    alpha = (2.0 * depth) ** 0.25
    keep = min(BAND_PAST, seq)

    m_p = batch * seq
    m_s = n_sample * dec_seq
    xp2 = x_prompt.reshape(m_p, d_model)
    xs2 = x_sample.reshape(m_s, d_model)

    wt = w_in[0].T
    tn = PROJ_TN
    o_xbc = inner
    o_dt = o_xbc + conv_dim
    o_q = o_dt + n_ssd_heads
    shift = o_q % tn
    assert o_dt % tn == 0 and inner % tn == 0 and conv_dim % tn == 0 and width % tn == 0
    q_blk = o_dt // tn
    xb, dtr = _xcast_dt(xp2, xs2, wt, dt_col=o_dt, n_dt=n_ssd_heads, tm=XCAST_TM)
    zs, w_att_b, w_o_b = _matmul_w32(xb, wt, lambda j: j, inner // tn, [BF16], tm=PROJ_TM, tn=tn, silu=True,
                                     side=[(w_att_out[0], SIDE_CAST_ROWS), (w_o[0], SIDE_CAST_ROWS)],
                                     name="in_proj_z")
    (xbc,) = _matmul_w32(xb, wt, lambda j: inner // tn + j, conv_dim // tn, [F32], tm=PROJ_TM, tn=tn,
                         name="in_proj_xbc")
    wpt = width // tn
    qg, w_ssd_b = _matmul_w32(xb, wt, lambda j: jnp.where(j < wpt, q_blk + j, q_blk + 2 * wpt + j),
                              wpt + 2 * d_model // tn, [BF16], tm=PROJ_TM, tn=tn, shift=shift,
                              side=[(w_ssd_out[0], SIDE_CAST_ROWS)], name="in_proj_qg")
    kb, k_p, k_s = _matmul_w32(xb, wt, lambda j: q_blk + wpt + j, wpt, [BF16], tm=KV_TM, tn=tn, shift=shift,
                               head_out=(keep, m_s), name="in_proj_k")
    vb, v_p, v_s = _matmul_w32(xb, wt, lambda j: q_blk + 2 * wpt + j, wpt, [BF16], tm=KV_TM, tn=tn,
                               shift=shift, head_out=(keep, m_s), name="in_proj_v")

    pad_h = (0, LANES - n_ssd_heads)
    dtb = jnp.pad(dt_bias[0], pad_h).reshape(1, LANES)
    alog = jnp.pad(a_log[0], pad_h).reshape(1, LANES)
    dskx = jnp.repeat(d_skip[0], SSD_HEADDIM).reshape(1, inner)
    emat = (jnp.arange(LANES)[:, None] == (jnp.arange(inner)[None, :] // SSD_HEADDIM)).astype(BF16)
    ssd_args = (zs, xbc, dtr, state_conv[0], state_ssm[0].reshape(n_sample, inner, d_state),
                conv_w[0], conv_b[0].reshape(1, conv_dim), dtb, alog, dskx, ssd_norm_w[0].reshape(1, inner), emat)
    ssd_dims = dict(inner=inner, n_groups=n_groups, d_state=d_state)
    ssd_y, conv_p, h_p = _ssd(*ssd_args, row0=0, n_streams=batch, chunks_per_stream=seq // CHUNK,
                              cps=SSD_PROMPT_CPS, per_stream=False, name="ssd_scan_prompt", **ssd_dims)
    ssd_y, conv_s, h_s = _ssd(*ssd_args, row0=m_p, n_streams=n_sample, chunks_per_stream=1, cps=1,
                              per_stream=True, name="ssd_scan_sample", y_into=ssd_y, **ssd_dims)

    rb = rel_bias[0]
    band = BAND_PAST + CHUNK
    n_edge = band - REL_CLIP
    n_rev = ATT_WIN - n_edge
    rev = rb[:, ::-1][:, 1:1 + n_rev]
    ext = jnp.concatenate([jnp.broadcast_to(rb[:, 2 * REL_CLIP:], (n_heads, n_edge)), rev], axis=1)
    ext = jnp.pad(ext, ((0, 0), (0, ATT_WIN - ext.shape[1])))
    att = _attn_prompt(qg, kb, vb, ext, n_tiles=m_p // ATT_TILE, n_heads=n_heads)
    cache_rows = (n_sample, BAND_PAST * n_heads, ATT_HEAD_DIM)
    att = _attn_sample(qg, kb, vb, cache_k[0].reshape(cache_rows), cache_v[0].reshape(cache_rows), ext, att,
                       row_blk0=m_p // CHUNK, n_sample=n_sample, n_heads=n_heads, spb=ATT_SAMPLE_SPB)

    mixed = _mix(ssd_y, att, w_ssd_b, w_att_b, qg, gate_col0=width, tm=MIX_TM, tn=MIX_TN)
    g1 = ln1_g[0].reshape(1, d_model)
    b1 = ln1_b[0].reshape(1, d_model)
    g2 = ln2_g[0].reshape(1, d_model)
    b2 = ln2_b[0].reshape(1, d_model)
    h_f, h_b = _proj_ln(mixed, w_o_b, (xp2, xs2), g1, b1, [F32, BF16], alpha=alpha, tm=LN_TM,
                        n_k=1, row0=0, n_rows=m_p + m_s, m_prompt=m_p, name="wo_ln")
    act, w_dn = _glu(h_b, w_gate_up[0], tm=GLU_TM, tf=GLU_TF, side=[(w_down[0], DOWN_CAST_ROWS)])
    (y_p,) = _proj_ln(act, w_dn, h_f, g2, b2, [F32], alpha=alpha, tm=LN_TM, n_k=DOWN_K_SPLIT, row0=0,
                      n_rows=m_p, name="down_ln_prompt")
    (y_s,) = _proj_ln(act, w_dn, h_f, g2, b2, [F32], alpha=alpha, tm=LN_TM, n_k=DOWN_K_SPLIT, row0=m_p,
                      n_rows=m_s, name="down_ln_sample")

    return (y_p.reshape(batch, seq, d_model),
            y_s.reshape(n_sample, dec_seq, d_model),
            conv_p.reshape(1, batch, CONV_W - 1, conv_dim),
            h_p.reshape(1, batch, n_ssd_heads, SSD_HEADDIM, d_state),
            k_p.reshape(1, batch, keep, n_heads, ATT_HEAD_DIM),
            v_p.reshape(1, batch, keep, n_heads, ATT_HEAD_DIM),
            conv_s.reshape(1, n_sample, CONV_W - 1, conv_dim),
            h_s.reshape(1, n_sample, n_ssd_heads, SSD_HEADDIM, d_state),
            k_s.reshape(1, n_sample, dec_seq, n_heads, ATT_HEAD_DIM),
            v_s.reshape(1, n_sample, dec_seq, n_heads, ATT_HEAD_DIM))
```

```python
import functools

import jax
import jax.numpy as jnp
from jax import lax
from jax.experimental import pallas as pl
from jax.experimental.pallas import tpu as pltpu

F32 = jnp.float32
BF16 = jnp.bfloat16

CHUNK = 64
SSD_HEADDIM = 64
SSD_GROUPS = 8
SSD_STATE = 128
CONV_W = 4
ATT_HEAD_DIM = 128
BAND_CHUNKS = 8
BAND_PAST = BAND_CHUNKS * CHUNK
REL_CLIP = 128
LN_EPS = 1e-5
RMS_EPS = 1e-5
MASK_NEG = -1e30

LANES = 128
SUBLANES = 8
VMEM_LIMIT_BYTES = 56 * 1024 * 1024

PROJ_TM, PROJ_TN = 1536, 1024
KV_TM = 1024
XCAST_TM = 1024
MIX_TM, MIX_TN = 768, 1024
LN_TM = 512
DOWN_K_SPLIT = 2
GLU_TM, GLU_TF = 2304, 512
SSD_PROMPT_CPS = 4
SIDE_CAST_ROWS = 128
DOWN_CAST_ROWS = 176
ATT_TILE = 512
GLU_SUB_BLOCKS = 2
SILU_SUB_BLOCKS = 2
ATT_WIN = 5 * LANES


def _cparams(n_axes):
    return pltpu.CompilerParams(
        dimension_semantics=("arbitrary",) * n_axes,
        vmem_limit_bytes=VMEM_LIMIT_BYTES,
    )


def _sigmoid(x):
    return 0.5 * jnp.tanh(0.5 * x) + 0.5


def _silu(x):
    h = 0.5 * x
    return h + h * jnp.tanh(h)


def _dot(a, b):
    return jnp.dot(a, b, preferred_element_type=F32)


def _dot_nt(a, b):
    return lax.dot_general(a, b, (((1,), (1,)), ((), ())), preferred_element_type=F32)


def _dot_tn(a, b):
    return lax.dot_general(a, b, (((0,), (0,)), ((), ())), preferred_element_type=F32)


def _xcast_dt_kernel(xp_ref, xs_ref, wd_ref, xb_ref, dt_ref, *, n_prompt_tiles, n_dt):
    i = pl.program_id(0)
    x = jnp.where(i < n_prompt_tiles, xp_ref[...], xs_ref[...]).astype(BF16)
    xb_ref[...] = x
    acc = _dot_nt(x, wd_ref[...].astype(BF16))
    lane = lax.broadcasted_iota(jnp.int32, acc.shape, 1)
    dt_ref[...] = jnp.where(lane < n_dt, acc, 0.0)


def _xcast_dt(xp, xs, wt, *, dt_col, n_dt, tm):
    m_p, k = xp.shape
    m_s = xs.shape[0]
    npt = m_p // tm
    n_tiles = npt + m_s // tm
    assert dt_col % LANES == 0 and n_dt <= LANES
    kern = functools.partial(_xcast_dt_kernel, n_prompt_tiles=npt, n_dt=n_dt)
    return pl.pallas_call(
        kern,
        grid=(n_tiles,),
        in_specs=[
            pl.BlockSpec((tm, k), lambda i: (jnp.minimum(i, npt - 1), 0)),
            pl.BlockSpec((tm, k), lambda i: (jnp.maximum(i - npt, 0), 0)),
            pl.BlockSpec((LANES, k), lambda i: (dt_col // LANES, 0)),
        ],
        out_specs=[pl.BlockSpec((tm, k), lambda i: (i, 0)),
                   pl.BlockSpec((tm, LANES), lambda i: (i, 0))],
        out_shape=[jax.ShapeDtypeStruct((m_p + m_s, k), BF16),
                   jax.ShapeDtypeStruct((m_p + m_s, LANES), F32)],
        compiler_params=_cparams(1),
        name="xcast_dt",
    )(xp, xs, wt)


CAST_ROWS = 256


def _cast_rows(src_ref, src_row0, dst_ref, dst_row0, n_rows):
    done = 0
    while done < n_rows:
        step = min(CAST_ROWS, n_rows - done)
        dst_ref[dst_row0 + done:dst_row0 + done + step, :] = (
            src_ref[src_row0 + done:src_row0 + done + step, :].astype(BF16))
        done += step


def _cast_weight_tile(w_ref, wsc):
    _cast_rows(w_ref, 0, wsc, 0, wsc.shape[0])


def _side_cast_specs(side, n_inner):
    in_specs, out_specs, out_shape = [], [], []
    for w, rows in side:
        n_blk = w.shape[0] // rows
        assert n_blk * rows == w.shape[0] and rows % 16 == 0
        idx = lambda j, i, n_blk=n_blk: (jnp.minimum(j * n_inner + i, n_blk - 1), 0)
        in_specs.append(pl.BlockSpec((rows, w.shape[1]), idx))
        out_specs.append(pl.BlockSpec((rows, w.shape[1]), idx))
        out_shape.append(jax.ShapeDtypeStruct(w.shape, BF16))
    return in_specs, out_specs, out_shape


def _mm_w32_kernel(*refs, shift, n_out, head_rows, silu, n_side):
    x_ref, wa_ref = refs[0], refs[1]
    pos = 2
    wb_ref = None
    if shift:
        wb_ref = refs[pos]
        pos += 1
    side_in = refs[pos:pos + n_side]
    pos += n_side
    o_refs = refs[pos:pos + n_out]
    pos += n_out
    hp_ref = hs_ref = None
    if head_rows is not None:
        hp_ref, hs_ref = refs[pos], refs[pos + 1]
        pos += 2
    side_out = refs[pos:pos + n_side]
    pos += n_side
    wsc = refs[pos]
    i = pl.program_id(1)

    @pl.when(i == 0)
    def _():
        tn = wsc.shape[0]
        _cast_rows(wa_ref, shift, wsc, 0, tn - shift)
        if shift:
            _cast_rows(wb_ref, 0, wsc, tn - shift, shift)

    for s_in, s_out in zip(side_in, side_out):
        s_out[...] = s_in[...].astype(BF16)

    if silu:
        sub = x_ref.shape[0] // SILU_SUB_BLOCKS
        for r in range(SILU_SUB_BLOCKS):
            rs = slice(r * sub, (r + 1) * sub)
            res = _silu(_dot_nt(x_ref[rs, :], wsc[...]))
            for o_ref in o_refs:
                o_ref[rs, :] = res.astype(o_ref.dtype)
        return

    acc = _dot_nt(x_ref[...], wsc[...])
    for o_ref in o_refs:
        o_ref[...] = acc.astype(o_ref.dtype)

    if head_rows is not None:
        (tile_p, row0_p), (tile_s, row0_s) = head_rows

        @pl.when(i == tile_p)
        def _():
            hp_ref[...] = acc[row0_p:row0_p + hp_ref.shape[0], :]

        @pl.when(i == tile_s)
        def _():
            hs_ref[...] = acc[row0_s:row0_s + hs_ref.shape[0], :]


def _matmul_w32(x, wt, col_blk, n_col_tiles, out_dtypes, *, tm, tn, shift=0, head_out=None, silu=False,
                side=(), name):
    m, k = x.shape
    assert m % tm == 0 and tn % LANES == 0 and shift % 16 == 0 and shift < tn
    assert not silu or (head_out is None and tm % (16 * SILU_SUB_BLOCKS) == 0)
    in_specs = [pl.BlockSpec((tm, k), lambda j, i: (i, 0)),
                pl.BlockSpec((tn, k), lambda j, i: (col_blk(j), 0))]
    args = [x, wt]
    if shift:
        assert tn % shift == 0
        per = tn // shift
        in_specs.append(pl.BlockSpec((shift, k), lambda j, i: ((col_blk(j) + 1) * per, 0)))
        args.append(wt)
    side_in, side_out, side_shape = _side_cast_specs(side, m // tm)
    assert all(w.shape[0] // rows <= n_col_tiles * (m // tm) for w, rows in side)
    in_specs += side_in
    args += [w for w, _ in side]
    n = n_col_tiles * tn
    out_specs = [pl.BlockSpec((tm, tn), lambda j, i: (i, j)) for _ in out_dtypes]
    out_shape = [jax.ShapeDtypeStruct((m, n), dt) for dt in out_dtypes]
    head_rows = None
    if head_out is not None:
        n_keep, m_s = head_out
        m_p = m - m_s
        head_rows = (divmod(m_p - n_keep, tm), divmod(m_p, tm))
        assert head_rows[0][1] + n_keep <= tm and head_rows[1][1] + m_s <= tm
        out_specs += [pl.BlockSpec((n_keep, tn), lambda j, i: (0, j)),
                      pl.BlockSpec((m_s, tn), lambda j, i: (0, j))]
        out_shape += [jax.ShapeDtypeStruct((n_keep, n), F32), jax.ShapeDtypeStruct((m_s, n), F32)]
    out_specs += side_out
    out_shape += side_shape
    kern = functools.partial(_mm_w32_kernel, shift=shift, n_out=len(out_dtypes), head_rows=head_rows,
                             silu=silu, n_side=len(side))
    return pl.pallas_call(
        kern,
        grid=(n_col_tiles, m // tm),
        in_specs=in_specs,
        out_specs=out_specs,
        out_shape=out_shape,
        scratch_shapes=[pltpu.VMEM((tn, k), BF16)],
        compiler_params=_cparams(2),
        name=name,
    )(*args)


def _softplus(x):
    return jnp.maximum(x, 0.0) + jnp.log1p(jnp.exp(-jnp.abs(x)))


def _ssd_chunk(u, zs_ref, xbc_ref, dt_ref, cw_ref, cb_ref, dtb_ref, alog_ref, dsk_ref, nw_ref, e_ref, y_ref,
               hist8, act, ht, cst, ex, ysc, *, inner, n_groups, d_state):
    r0 = u * CHUNK
    rows = slice(r0, r0 + CHUNK)
    gw = inner // n_groups
    conv_dim = inner + 2 * n_groups * d_state

    for j in range(conv_dim // LANES):
        sl = slice(j * LANES, (j + 1) * LANES)
        if u == 0:
            blk = jnp.concatenate([hist8[:, sl], xbc_ref[0:CHUNK, sl]], axis=0)
        else:
            blk = xbc_ref[r0 - SUBLANES:r0 + CHUNK, sl]
        prev = pltpu.roll(blk, 1, 0)
        near = blk * cw_ref[3:4, sl] + prev * cw_ref[2:3, sl]
        far = pltpu.roll(blk * cw_ref[1:2, sl] + prev * cw_ref[0:1, sl], 2, 0)
        a = cb_ref[:, sl] + (near + far)[SUBLANES:, :]
        act[rows, sl] = _silu(a)

    dt = _softplus(dt_ref[rows, :] + dtb_ref[...])
    a_neg = -jnp.exp(alog_ref[...])
    da = dt * a_neg
    ri = lax.broadcasted_iota(jnp.int32, (CHUNK, CHUNK), 0)
    ci = lax.broadcasted_iota(jnp.int32, (CHUNK, CHUNK), 1)
    tri = (ri >= ci).astype(F32)
    cs = jnp.dot(tri, da, precision=lax.Precision.HIGHEST, preferred_element_type=F32)
    ecs = jnp.exp(cs)
    cs_last = cs[CHUNK - 1:CHUNK, :]
    dout = jnp.exp(cs_last - cs) * dt
    cd = jnp.exp(cs_last)
    cd_hi = cd.astype(BF16)
    cd_lo = (cd - cd_hi.astype(F32)).astype(BF16)
    lhs = jnp.concatenate(
        [dt.astype(BF16), ecs.astype(BF16), dout.astype(BF16),
         jnp.broadcast_to(cd_hi, (16, LANES)), jnp.broadcast_to(cd_lo, (16, LANES))], axis=0)
    ex[u] = _dot(lhs, e_ref[...])

    cst[u] = jnp.concatenate([cs, jnp.zeros((LANES - CHUNK, LANES), F32)], axis=0).T
    lane = lax.broadcasted_iota(jnp.int32, (CHUNK, LANES), 1)
    lo_half = lane < CHUNK
    n_pairs = inner // LANES
    ev = cst[u, pl.ds(0, n_pairs, stride=2), :]
    od = cst[u, pl.ds(1, n_pairs, stride=2), :]
    lane_p = lax.broadcasted_iota(jnp.int32, (n_pairs, LANES), 1)
    cst2 = jnp.where(lane_p < CHUNK, ev, pltpu.roll(od, CHUNK, 1))
    row = lax.broadcasted_iota(jnp.int32, (CHUNK, LANES), 0)
    causal2 = row >= jnp.where(lo_half, lane, lane - CHUNK)

    pairs_per_group = gw // LANES
    for g in range(n_groups):
        gs = slice(g * gw, (g + 1) * gw)
        bm_g = act[rows, inner + g * d_state: inner + (g + 1) * d_state].astype(BF16)
        cm_g = act[rows, inner + (n_groups + g) * d_state: inner + (n_groups + g + 1) * d_state].astype(BF16)
        cb2 = _dot_nt(cm_g, jnp.concatenate([bm_g, bm_g], axis=0))
        h_g = ht[:, gs]
        yoff = _dot(cm_g, h_g.astype(BF16))
        for p in range(pairs_per_group):
            k = g * pairs_per_group + p
            ls = slice(k * LANES, (k + 1) * LANES)
            csc = jnp.where(lo_half,
                            jnp.broadcast_to(cs[:, 2 * k:2 * k + 1], (CHUNK, LANES)),
                            jnp.broadcast_to(cs[:, 2 * k + 1:2 * k + 2], (CHUNK, LANES)))
            diff = csc - cst2[k:k + 1, :]
            dec = jnp.exp(jnp.where(causal2, diff, MASK_NEG))
            m2 = (cb2 * dec).astype(BF16)
            xs_p = act[rows, ls]
            xdt = xs_p * ex[u, 0:CHUNK, ls]
            x2 = jnp.concatenate([jnp.where(lo_half, xdt, 0.0), jnp.where(lo_half, 0.0, xdt)],
                                 axis=0).astype(BF16)
            yd = _dot(m2, x2)
            ysc[rows, ls] = (yd + yoff[:, p * LANES:(p + 1) * LANES] * ex[u, CHUNK:2 * CHUNK, ls]
                             + dsk_ref[:, ls] * xs_p)
        xw = (act[rows, gs] * ex[u, 2 * CHUNK:3 * CHUNK, gs]).astype(BF16)
        st = _dot_tn(bm_g, xw)
        cdx = ex[u, 3 * CHUNK:3 * CHUNK + 1, gs] + ex[u, 3 * CHUNK + 16:3 * CHUNK + 17, gs]
        ht[:, gs] = h_g * cdx + st

    for g in range(n_groups):
        gs = slice(g * gw, (g + 1) * gw)
        v = ysc[rows, gs] * zs_ref[rows, gs].astype(F32)
        ms = jnp.mean(v * v, axis=-1, keepdims=True)
        y_ref[rows, gs] = (v * lax.rsqrt(ms + RMS_EPS) * nw_ref[:, gs]).astype(y_ref.dtype)


def _ssd_kernel(*refs, cps, per_stream, aliased, inner, n_groups, d_state):
    (zs_ref, xbc_ref, dt_ref, hist_ref, h0_ref, cw_ref, cb_ref, dtb_ref, alog_ref,
     dsk_ref, nw_ref, e_ref) = refs[:12]
    y_ref, conv_ref, ho_ref, hist8, act, ht, cst, ex, ysc = refs[12 + int(aliased):]
    c = pl.program_id(0)
    n_rows = cps * CHUNK
    tail = slice(n_rows - (CONV_W - 1), n_rows)
    last = slice(SUBLANES - (CONV_W - 1), SUBLANES)

    @pl.when(c == 0)
    def _():
        hist8[...] = jnp.zeros_like(hist8)
        if not per_stream:
            ht[...] = jnp.zeros_like(ht)

    if per_stream:
        hist8[last, :] = hist_ref[0]
        ht[...] = h0_ref[0].T

    conv_ref[0] = xbc_ref[tail, :]

    for u in range(cps):
        _ssd_chunk(u, zs_ref, xbc_ref, dt_ref, cw_ref, cb_ref, dtb_ref, alog_ref, dsk_ref, nw_ref, e_ref, y_ref,
                   hist8, act, ht, cst, ex, ysc, inner=inner, n_groups=n_groups, d_state=d_state)
    hist8[last, :] = xbc_ref[tail, :]

    if per_stream:
        ho_ref[0] = ht[...].T
    else:
        @pl.when(c == pl.num_programs(0) - 1)
        def _():
            ho_ref[0] = ht[...].T


def _ssd(zs, xbc, dtr, hist, h0, conv_w, conv_b, dtb, alog, dskx, normw, emat, *,
         row0, n_streams, chunks_per_stream, cps, per_stream, inner, n_groups, d_state, name, y_into=None):
    conv_dim = xbc.shape[1]
    hp = inner
    n_rows = cps * CHUNK
    assert chunks_per_stream % cps == 0 and row0 % n_rows == 0
    assert per_stream == (chunks_per_stream == cps) and (per_stream or n_streams == 1)
    steps_per_stream = chunks_per_stream // cps
    blk0 = row0 // n_rows
    stream = (lambda c: c) if per_stream else (lambda c: 0)
    aliased = y_into is not None

    kern = functools.partial(_ssd_kernel, cps=cps, per_stream=per_stream, aliased=aliased, inner=inner,
                             n_groups=n_groups, d_state=d_state)
    const = lambda c: (0, 0)
    row_map = lambda c: (blk0 + c, 0)
    extra_specs = [pl.BlockSpec(memory_space=pl.ANY)] if aliased else []
    extra_args = [y_into] if aliased else []
    return pl.pallas_call(
        kern,
        grid=(n_streams * steps_per_stream,),
        input_output_aliases={12: 0} if aliased else {},
        in_specs=[
            pl.BlockSpec((n_rows, inner), row_map),
            pl.BlockSpec((n_rows, conv_dim), row_map),
            pl.BlockSpec((n_rows, LANES), row_map),
            pl.BlockSpec((1, CONV_W - 1, conv_dim), lambda c: (stream(c), 0, 0)),
            pl.BlockSpec((1, hp, d_state), lambda c: (stream(c), 0, 0)),
            pl.BlockSpec((CONV_W, conv_dim), const),
            pl.BlockSpec((1, conv_dim), const),
            pl.BlockSpec((1, LANES), const),
            pl.BlockSpec((1, LANES), const),
            pl.BlockSpec((1, inner), const),
            pl.BlockSpec((1, inner), const),
            pl.BlockSpec((LANES, inner), const),
        ] + extra_specs,
        out_specs=[
            pl.BlockSpec((n_rows, inner), row_map),
            pl.BlockSpec((1, CONV_W - 1, conv_dim), lambda c: (stream(c), 0, 0)),
            pl.BlockSpec((1, hp, d_state), lambda c: (stream(c), 0, 0)),
        ],
        out_shape=[
            jax.ShapeDtypeStruct((zs.shape[0], inner), BF16),
            jax.ShapeDtypeStruct((n_streams, CONV_W - 1, conv_dim), F32),
            jax.ShapeDtypeStruct((n_streams, hp, d_state), F32),
        ],
        scratch_shapes=[
            pltpu.VMEM((SUBLANES, conv_dim), F32),
            pltpu.VMEM((n_rows, conv_dim), F32),
            pltpu.VMEM((d_state, inner), F32),
            pltpu.VMEM((cps, LANES, LANES), F32),
            pltpu.VMEM((cps, 3 * CHUNK + 32, inner), F32),
            pltpu.VMEM((n_rows, inner), F32),
        ],
        compiler_params=_cparams(1),
        name=name,
    )(zs, xbc, dtr, hist, h0, conv_w, conv_b, dtb, alog, dskx, normw, emat, *extra_args)


def _band_bias_tables(ext_ref, bias_scr, n_heads):
    col = lax.broadcasted_iota(jnp.int32, (CHUNK, ATT_WIN), 1)
    band = BAND_PAST + CHUNK
    for h in range(n_heads):
        e = jnp.broadcast_to(ext_ref[h:h + 1, :], (CHUNK, ATT_WIN))
        even = pltpu.roll(e, ATT_WIN - (CHUNK - 1), 1, stride=1, stride_axis=0)
        odd = pltpu.roll(e, 1, 1, stride=1, stride_axis=0)
        bias_scr[0, h] = jnp.where(col < band, even, MASK_NEG)
        bias_scr[1, h] = jnp.where(col >= CHUNK, odd, MASK_NEG)


ATT_BLOCKS = BAND_PAST // ATT_TILE + 1


def _attn_prompt_kernel(*refs, n_heads):
    q_ref = refs[0]
    k_refs = refs[1:1 + ATT_BLOCKS]
    v_refs = refs[1 + ATT_BLOCKS:1 + 2 * ATT_BLOCKS]
    ext_ref, o_ref, bias_scr, s_scr, p_scr = refs[1 + 2 * ATT_BLOCKS:]
    i = pl.program_id(0)
    d = ATT_HEAD_DIM
    scale = d ** -0.5
    blk = ATT_TILE

    @pl.when(i == 0)
    def _():
        _band_bias_tables(ext_ref, bias_scr, n_heads)
        p_scr[...] = jnp.zeros_like(p_scr)

    past_mask = [jnp.where(i < ATT_BLOCKS - 1 - b, MASK_NEG, 0.0).astype(F32) for b in range(ATT_BLOCKS - 1)]

    for h in range(n_heads):
        hs = slice(h * d, (h + 1) * d)
        q_h = q_ref[:, hs]
        for b in range(ATT_BLOCKS):
            s = _dot_nt(q_h, k_refs[b][:, hs]) * scale
            s_scr[:, b * blk:(b + 1) * blk] = s + past_mask[b] if b < ATT_BLOCKS - 1 else s
        for t in range(ATT_TILE // CHUNK):
            rs = slice(t * CHUNK, (t + 1) * CHUNK)
            c0 = LANES * (t // 2)
            ws = slice(c0, c0 + ATT_WIN)
            s = s_scr[rs, ws] + bias_scr[t % 2, h]
            m = jnp.max(s, axis=-1, keepdims=True)
            p = jnp.exp(s - m)
            r = 1.0 / jnp.sum(p, axis=-1, keepdims=True)
            p_scr[rs, ws] = (p * r).astype(BF16)
        o = _dot(p_scr[:, 0:blk], v_refs[0][:, hs])
        for b in range(1, ATT_BLOCKS):
            o = o + _dot(p_scr[:, b * blk:(b + 1) * blk], v_refs[b][:, hs])
        o_ref[:, hs] = o.astype(o_ref.dtype)


def _attn_prompt(q, kb, vb, ext, *, n_tiles, n_heads):
    width = n_heads * ATT_HEAD_DIM
    assert BAND_PAST % ATT_TILE == 0 and ATT_TILE % (2 * CHUNK) == 0
    kern = functools.partial(_attn_prompt_kernel, n_heads=n_heads)
    window = [pl.BlockSpec((ATT_TILE, width), lambda i, back=back: (jnp.maximum(i - back, 0), 0))
              for back in range(ATT_BLOCKS - 1, -1, -1)]
    return pl.pallas_call(
        kern,
        grid=(n_tiles,),
        in_specs=[pl.BlockSpec((ATT_TILE, width), lambda i: (i, 0))] + window + window
                 + [pl.BlockSpec((n_heads, ATT_WIN), lambda i: (0, 0))],
        out_specs=pl.BlockSpec((ATT_TILE, width), lambda i: (i, 0)),
        out_shape=jax.ShapeDtypeStruct((q.shape[0], width), BF16),
        scratch_shapes=[
            pltpu.VMEM((2, n_heads, CHUNK, ATT_WIN), F32),
            pltpu.VMEM((ATT_TILE, ATT_BLOCKS * ATT_TILE), F32),
            pltpu.VMEM((ATT_TILE, ATT_BLOCKS * ATT_TILE), BF16),
        ],
        compiler_params=_cparams(1),
        name="attn_prompt",
    )(q, *([kb] * ATT_BLOCKS), *([vb] * ATT_BLOCKS), ext)


def _head_copy(cache_hbm, heads_vmem, sem, stream, slot, which, h):
    return pltpu.make_async_copy(cache_hbm.at[stream, :, h, :], heads_vmem.at[slot, h], sem.at[slot, which])


def _attn_sample_kernel(q_ref, kn_ref, vn_ref, ck_hbm, cv_hbm, ext_ref, att_in_ref, o_ref,
                        bias_scr, kh_scr, vh_scr, s_scr, p_scr, sem, *, n_heads):
    del att_in_ref
    d = ATT_HEAD_DIM
    scale = d ** -0.5
    s_id = pl.program_id(0)
    slot = lax.rem(s_id, 2)

    def copies(stream, slt):
        return ([_head_copy(ck_hbm, kh_scr, sem, stream, slt, 0, h) for h in range(n_heads)]
                + [_head_copy(cv_hbm, vh_scr, sem, stream, slt, 1, h) for h in range(n_heads)])

    @pl.when(s_id == 0)
    def _():
        for c in copies(0, 0):
            c.start()
        _band_bias_tables(ext_ref, bias_scr, n_heads)

    @pl.when(s_id + 1 < pl.num_programs(0))
    def _():
        for c in copies(s_id + 1, 1 - slot):
            c.start()

    for c in copies(s_id, slot):
        c.wait()

    band = BAND_PAST + CHUNK
    for h in range(n_heads):
        hs = slice(h * d, (h + 1) * d)
        q_h = q_ref[:, hs]
        s_scr[h, :, 0:BAND_PAST] = (_dot_nt(q_h, kh_scr[slot, h].astype(BF16)) * scale
                                    + bias_scr[0, h, :, 0:BAND_PAST])
        s_scr[h, :, BAND_PAST:band] = (_dot_nt(q_h, kn_ref[:, hs]) * scale
                                       + bias_scr[0, h, :, BAND_PAST:band])
    for h in range(n_heads):
        s = s_scr[h, :, 0:band]
        p = jnp.exp(s - jnp.max(s, axis=-1, keepdims=True))
        r = 1.0 / jnp.sum(p, axis=-1, keepdims=True)
        p_scr[h, :, 0:band] = (p * r).astype(BF16)
    for h in range(n_heads):
        hs = slice(h * d, (h + 1) * d)
        o = (_dot(p_scr[h, :, 0:BAND_PAST], vh_scr[slot, h].astype(BF16))
             + _dot(p_scr[h, :, BAND_PAST:band], vn_ref[:, hs]))
        o_ref[:, hs] = o.astype(o_ref.dtype)


def _attn_sample(q, kb, vb, cache_k, cache_v, ext, att_into, *, row_blk0, n_sample, n_heads):
    width = n_heads * ATT_HEAD_DIM
    kern = functools.partial(_attn_sample_kernel, n_heads=n_heads)
    row_spec = pl.BlockSpec((CHUNK, width), lambda s: (row_blk0 + s, 0))
    heads_buf = pltpu.VMEM((2, n_heads, BAND_PAST, ATT_HEAD_DIM), F32)
    return pl.pallas_call(
        kern,
        grid=(n_sample,),
        in_specs=[row_spec, row_spec, row_spec,
                  pl.BlockSpec(memory_space=pl.ANY), pl.BlockSpec(memory_space=pl.ANY),
                  pl.BlockSpec((n_heads, ATT_WIN), lambda s: (0, 0)),
                  pl.BlockSpec(memory_space=pl.ANY)],
        out_specs=row_spec,
        out_shape=jax.ShapeDtypeStruct(att_into.shape, BF16),
        input_output_aliases={6: 0},
        scratch_shapes=[pltpu.VMEM((2, n_heads, CHUNK, ATT_WIN), F32),
                        heads_buf, heads_buf,
                        pltpu.VMEM((n_heads, CHUNK, ATT_WIN), F32),
                        pltpu.VMEM((n_heads, CHUNK, ATT_WIN), BF16),
                        pltpu.SemaphoreType.DMA((2, 2))],
        compiler_params=_cparams(1),
        name="attn_sample",
    )(q, kb, vb, cache_k, cache_v, ext, att_into)


def _mix_kernel(s_ref, a_ref, w1_ref, w2_ref, g1_ref, g2_ref, o_ref):
    a1 = _dot(s_ref[...], w1_ref[...])
    a2 = _dot(a_ref[...], w2_ref[...])
    g1 = _sigmoid(g1_ref[...].astype(F32))
    g2 = _sigmoid(g2_ref[...].astype(F32))
    o_ref[...] = (g1 * a1 + g2 * a2).astype(o_ref.dtype)


def _mix(ssd_y, att, w1, w2, gates_arr, *, gate_col0, tm, tn):
    m, k1 = ssd_y.shape
    k2 = att.shape[1]
    n = w1.shape[1]
    assert att.shape[0] == m and m % tm == 0
    g1_blk = gate_col0 // tn
    g2_blk = (gate_col0 + n) // tn
    return pl.pallas_call(
        _mix_kernel,
        grid=(n // tn, m // tm),
        in_specs=[
            pl.BlockSpec((tm, k1), lambda j, i: (i, 0)),
            pl.BlockSpec((tm, k2), lambda j, i: (i, 0)),
            pl.BlockSpec((k1, tn), lambda j, i: (0, j), pipeline_mode=pl.Buffered(1)),
            pl.BlockSpec((k2, tn), lambda j, i: (0, j), pipeline_mode=pl.Buffered(1)),
            pl.BlockSpec((tm, tn), lambda j, i: (i, g1_blk + j)),
            pl.BlockSpec((tm, tn), lambda j, i: (i, g2_blk + j)),
        ],
        out_specs=pl.BlockSpec((tm, tn), lambda j, i: (i, j)),
        out_shape=jax.ShapeDtypeStruct((m, n), BF16),
        compiler_params=_cparams(2),
        name="branch_mix",
    )(ssd_y, att, w1, w2, gates_arr, gates_arr)


def _layer_norm(t, g, b):
    mu = jnp.mean(t, axis=-1, keepdims=True)
    tc = t - mu
    var = jnp.mean(tc * tc, axis=-1, keepdims=True)
    return tc * lax.rsqrt(var + LN_EPS) * g + b


LN_SUB_ROWS = 256


def _proj_ln_kernel(*refs, alpha, n_k, n_prompt_tiles, split_resid, n_out):
    a_ref, w_ref = refs[0], refs[1]
    pos = 2
    r_refs = refs[pos:pos + (2 if split_resid else 1)]
    pos += len(r_refs)
    g_ref, b_ref = refs[pos], refs[pos + 1]
    pos += 2
    o_refs = refs[pos:pos + n_out]
    acc_ref = refs[pos + n_out] if n_k > 1 else None
    i = pl.program_id(0)
    kk = pl.program_id(1)
    tm = a_ref.shape[0]

    def resid_rows(rs):
        if split_resid:
            return jnp.where(i < n_prompt_tiles, r_refs[0][rs, :], r_refs[1][rs, :])
        return r_refs[0][rs, :]

    def finish(first):
        for r in range(tm // LN_SUB_ROWS):
            rs = slice(r * LN_SUB_ROWS, (r + 1) * LN_SUB_ROWS)
            base = alpha * resid_rows(rs) if first else acc_ref[rs, :]
            y = _layer_norm(base + _dot(a_ref[rs, :], w_ref[...]), g_ref[...], b_ref[...])
            for o_ref in o_refs:
                o_ref[rs, :] = y.astype(o_ref.dtype)

    if n_k == 1:
        finish(True)
        return

    @pl.when(kk == 0)
    def _():
        acc_ref[...] = alpha * resid_rows(slice(None)) + _dot(a_ref[...], w_ref[...])

    if n_k > 2:
        @pl.when(jnp.logical_and(kk > 0, kk < n_k - 1))
        def _():
            acc_ref[...] += _dot(a_ref[...], w_ref[...])

    @pl.when(kk == n_k - 1)
    def _():
        finish(False)


def _proj_ln(a, w, resid, gamma, beta, out_dtypes, *, alpha, tm, n_k, row0, n_rows, m_prompt=None, name):
    k = a.shape[1]
    n = w.shape[1]
    tk = k // n_k
    assert tk * n_k == k and n_rows % tm == 0 and row0 % tm == 0 and tm % LN_SUB_ROWS == 0
    blk0 = row0 // tm
    split_resid = isinstance(resid, tuple)
    row_map = lambda i, kk: (blk0 + i, 0)
    w_mode = dict(pipeline_mode=pl.Buffered(1)) if n_k == 1 else {}
    in_specs = [pl.BlockSpec((tm, tk), lambda i, kk: (blk0 + i, kk)),
                pl.BlockSpec((tk, n), lambda i, kk: (kk, 0), **w_mode)]
    args = [a, w]
    npt = 0
    if split_resid:
        assert row0 == 0 and m_prompt % tm == 0
        npt = m_prompt // tm
        in_specs += [pl.BlockSpec((tm, n), lambda i, kk: (jnp.minimum(i, npt - 1), 0)),
                     pl.BlockSpec((tm, n), lambda i, kk: (jnp.maximum(i - npt, 0), 0))]
        args += list(resid)
    else:
        in_specs.append(pl.BlockSpec((tm, n), row_map))
        args.append(resid)
    in_specs += [pl.BlockSpec((1, n), lambda i, kk: (0, 0))] * 2
    args += [gamma, beta]
    kern = functools.partial(_proj_ln_kernel, alpha=alpha, n_k=n_k, n_prompt_tiles=npt,
                             split_resid=split_resid, n_out=len(out_dtypes))
    return pl.pallas_call(
        kern,
        grid=(n_rows // tm, n_k),
        in_specs=in_specs,
        out_specs=[pl.BlockSpec((tm, n), lambda i, kk: (i, 0)) for _ in out_dtypes],
        out_shape=[jax.ShapeDtypeStruct((n_rows, n), dt) for dt in out_dtypes],
        scratch_shapes=[pltpu.VMEM((tm, n), F32)] if n_k > 1 else [],
        compiler_params=_cparams(2),
        name=name,
    )(*args)


def _glu_kernel(*refs, n_side):
    h_ref, wg_ref, wu_ref = refs[:3]
    side_in = refs[3:3 + n_side]
    o_ref = refs[3 + n_side]
    side_out = refs[4 + n_side:4 + 2 * n_side]
    wg_sc, wu_sc = refs[4 + 2 * n_side:]

    @pl.when(pl.program_id(1) == 0)
    def _():
        _cast_weight_tile(wg_ref, wg_sc)
        _cast_weight_tile(wu_ref, wu_sc)

    for s_in, s_out in zip(side_in, side_out):
        s_out[...] = s_in[...].astype(BF16)

    tm = h_ref.shape[0]
    sub = tm // GLU_SUB_BLOCKS
    for r in range(GLU_SUB_BLOCKS):
        rs = slice(r * sub, (r + 1) * sub)
        h = h_ref[rs, :]
        g = _dot(h, wg_sc[...])
        u = _dot(h, wu_sc[...])
        o_ref[rs, :] = (g * jax.nn.sigmoid(g) * u).astype(o_ref.dtype)


def _glu(h, w_gate_up, *, tm, tf, side=()):
    m, k = h.shape
    d_ff = w_gate_up.shape[1] // 2
    nf = d_ff // tf
    assert nf * tf == d_ff
    side_in, side_out, side_shape = _side_cast_specs(side, m // tm)
    assert all(w.shape[0] // rows <= nf * (m // tm) for w, rows in side)
    return pl.pallas_call(
        functools.partial(_glu_kernel, n_side=len(side)),
        grid=(nf, m // tm),
        in_specs=[
            pl.BlockSpec((tm, k), lambda j, i: (i, 0)),
            pl.BlockSpec((k, tf), lambda j, i: (0, j)),
            pl.BlockSpec((k, tf), lambda j, i: (0, nf + j)),
        ] + side_in,
        out_specs=[pl.BlockSpec((tm, tf), lambda j, i: (i, j))] + side_out,
        out_shape=[jax.ShapeDtypeStruct((m, d_ff), BF16)] + side_shape,
        scratch_shapes=[pltpu.VMEM((k, tf), BF16), pltpu.VMEM((k, tf), BF16)],
        compiler_params=_cparams(2),
        name="ffn_glu",
    )(h, w_gate_up, w_gate_up, *[w for w, _ in side])


def kernel(x_prompt, x_sample, cache_k, cache_v, state_conv, state_ssm, w_in, conv_w, conv_b, dt_bias,
           a_log, d_skip, ssd_norm_w, rel_bias, w_ssd_out, w_att_out, w_o, ln1_g, ln1_b, w_gate_up,
           w_down, ln2_g, ln2_b):
    depth = w_in.shape[0]
    assert depth == 1
    batch, seq, d_model = x_prompt.shape
    n_sample, dec_seq, _ = x_sample.shape
    assert batch == 1 and dec_seq == CHUNK and seq % ATT_TILE == 0
    n_ssd_heads = dt_bias.shape[1]
    inner = n_ssd_heads * SSD_HEADDIM
    conv_dim = conv_w.shape[2]
    n_groups = SSD_GROUPS
    d_state = SSD_STATE
    assert conv_dim == inner + 2 * n_groups * d_state
    n_heads = rel_bias.shape[1]
    width = n_heads * ATT_HEAD_DIM
    assert cache_k.shape[2] == BAND_PAST
    alpha = (2.0 * depth) ** 0.25
    keep = min(BAND_PAST, seq)

    m_p = batch * seq
    m_s = n_sample * dec_seq
    xp2 = x_prompt.reshape(m_p, d_model)
    xs2 = x_sample.reshape(m_s, d_model)

    wt = w_in[0].T
    tn = PROJ_TN
    o_xbc = inner
    o_dt = o_xbc + conv_dim
    o_q = o_dt + n_ssd_heads
    shift = o_q % tn
    assert o_dt % tn == 0 and inner % tn == 0 and conv_dim % tn == 0 and width % tn == 0
    q_blk = o_dt // tn
    xb, dtr = _xcast_dt(xp2, xs2, wt, dt_col=o_dt, n_dt=n_ssd_heads, tm=XCAST_TM)
    zs, w_att_b, w_o_b = _matmul_w32(xb, wt, lambda j: j, inner // tn, [BF16], tm=PROJ_TM, tn=tn, silu=True,
                                     side=[(w_att_out[0], SIDE_CAST_ROWS), (w_o[0], SIDE_CAST_ROWS)],
                                     name="in_proj_z")
    (xbc,) = _matmul_w32(xb, wt, lambda j: inner // tn + j, conv_dim // tn, [F32], tm=PROJ_TM, tn=tn,
                         name="in_proj_xbc")
    wpt = width // tn
    qg, w_ssd_b = _matmul_w32(xb, wt, lambda j: jnp.where(j < wpt, q_blk + j, q_blk + 2 * wpt + j),
                              wpt + 2 * d_model // tn, [BF16], tm=PROJ_TM, tn=tn, shift=shift,
                              side=[(w_ssd_out[0], SIDE_CAST_ROWS)], name="in_proj_qg")
    kb, k_p, k_s = _matmul_w32(xb, wt, lambda j: q_blk + wpt + j, wpt, [BF16], tm=KV_TM, tn=tn, shift=shift,
                               head_out=(keep, m_s), name="in_proj_k")
    vb, v_p, v_s = _matmul_w32(xb, wt, lambda j: q_blk + 2 * wpt + j, wpt, [BF16], tm=KV_TM, tn=tn,
                               shift=shift, head_out=(keep, m_s), name="in_proj_v")

    pad_h = (0, LANES - n_ssd_heads)
    dtb = jnp.pad(dt_bias[0], pad_h).reshape(1, LANES)
    alog = jnp.pad(a_log[0], pad_h).reshape(1, LANES)
    dskx = jnp.repeat(d_skip[0], SSD_HEADDIM).reshape(1, inner)
    emat = (jnp.arange(LANES)[:, None] == (jnp.arange(inner)[None, :] // SSD_HEADDIM)).astype(BF16)
    ssd_args = (zs, xbc, dtr, state_conv[0], state_ssm[0].reshape(n_sample, inner, d_state),
                conv_w[0], conv_b[0].reshape(1, conv_dim), dtb, alog, dskx, ssd_norm_w[0].reshape(1, inner), emat)
    ssd_dims = dict(inner=inner, n_groups=n_groups, d_state=d_state)
    ssd_y, conv_p, h_p = _ssd(*ssd_args, row0=0, n_streams=batch, chunks_per_stream=seq // CHUNK,
                              cps=SSD_PROMPT_CPS, per_stream=False, name="ssd_scan_prompt", **ssd_dims)
    ssd_y, conv_s, h_s = _ssd(*ssd_args, row0=m_p, n_streams=n_sample, chunks_per_stream=1, cps=1,
                              per_stream=True, name="ssd_scan_sample", y_into=ssd_y, **ssd_dims)

    rb = rel_bias[0]
    band = BAND_PAST + CHUNK
    n_edge = band - REL_CLIP
    n_rev = ATT_WIN - n_edge
    rev = rb[:, ::-1][:, 1:1 + n_rev]
    ext = jnp.concatenate([jnp.broadcast_to(rb[:, 2 * REL_CLIP:], (n_heads, n_edge)), rev], axis=1)
    ext = jnp.pad(ext, ((0, 0), (0, ATT_WIN - ext.shape[1])))
    att = _attn_prompt(qg, kb, vb, ext, n_tiles=m_p // ATT_TILE, n_heads=n_heads)
    att = _attn_sample(qg, kb, vb, cache_k[0], cache_v[0], ext, att,
                       row_blk0=m_p // CHUNK, n_sample=n_sample, n_heads=n_heads)

    mixed = _mix(ssd_y, att, w_ssd_b, w_att_b, qg, gate_col0=width, tm=MIX_TM, tn=MIX_TN)
    g1 = ln1_g[0].reshape(1, d_model)
    b1 = ln1_b[0].reshape(1, d_model)
    g2 = ln2_g[0].reshape(1, d_model)
    b2 = ln2_b[0].reshape(1, d_model)
    h_f, h_b = _proj_ln(mixed, w_o_b, (xp2, xs2), g1, b1, [F32, BF16], alpha=alpha, tm=LN_TM,
                        n_k=1, row0=0, n_rows=m_p + m_s, m_prompt=m_p, name="wo_ln")
    act, w_dn = _glu(h_b, w_gate_up[0], tm=GLU_TM, tf=GLU_TF, side=[(w_down[0], DOWN_CAST_ROWS)])
    (y_p,) = _proj_ln(act, w_dn, h_f, g2, b2, [F32], alpha=alpha, tm=LN_TM, n_k=DOWN_K_SPLIT, row0=0,
                      n_rows=m_p, name="down_ln_prompt")
    (y_s,) = _proj_ln(act, w_dn, h_f, g2, b2, [F32], alpha=alpha, tm=LN_TM, n_k=DOWN_K_SPLIT, row0=m_p,
                      n_rows=m_s, name="down_ln_sample")

    return (y_p.reshape(batch, seq, d_model),
            y_s.reshape(n_sample, dec_seq, d_model),
            conv_p.reshape(1, batch, CONV_W - 1, conv_dim),
            h_p.reshape(1, batch, n_ssd_heads, SSD_HEADDIM, d_state),
            k_p.reshape(1, batch, keep, n_heads, ATT_HEAD_DIM),
            v_p.reshape(1, batch, keep, n_heads, ATT_HEAD_DIM),
            conv_s.reshape(1, n_sample, CONV_W - 1, conv_dim),
            h_s.reshape(1, n_sample, n_ssd_heads, SSD_HEADDIM, d_state),
            k_s.reshape(1, n_sample, dec_seq, n_heads, ATT_HEAD_DIM),
            v_s.reshape(1, n_sample, dec_seq, n_heads, ATT_HEAD_DIM))
```

```python
import functools

import jax
import jax.numpy as jnp
from jax import lax
from jax.experimental import pallas as pl
from jax.experimental.pallas import tpu as pltpu

F32 = jnp.float32
BF16 = jnp.bfloat16

CHUNK = 64
SSD_HEADDIM = 64
SSD_GROUPS = 8
SSD_STATE = 128
CONV_W = 4
ATT_HEAD_DIM = 128
BAND_CHUNKS = 8
BAND_PAST = BAND_CHUNKS * CHUNK
REL_CLIP = 128
LN_EPS = 1e-5
RMS_EPS = 1e-5
MASK_NEG = -1e30

LANES = 128
SUBLANES = 8
VMEM_LIMIT_BYTES = 56 * 1024 * 1024

PROJ_TM, PROJ_TN = 1536, 1024
KV_TM = 1024
XCAST_TM = 1024
MIX_TM, MIX_TN = 768, 1024
LN_TM = 512
DOWN_K_SPLIT = 2
GLU_TM, GLU_TF = 2304, 512
SSD_PROMPT_CPS = 4
SIDE_CAST_ROWS = 128
DOWN_CAST_ROWS = 176
ATT_TILE = 512
GLU_SUB_BLOCKS = 2
SILU_SUB_BLOCKS = 2
ATT_WIN = 5 * LANES


def _cparams(n_axes):
    return pltpu.CompilerParams(
        dimension_semantics=("arbitrary",) * n_axes,
        vmem_limit_bytes=VMEM_LIMIT_BYTES,
    )


def _sigmoid(x):
    return 0.5 * jnp.tanh(0.5 * x) + 0.5


def _silu(x):
    h = 0.5 * x
    return h + h * jnp.tanh(h)


def _dot(a, b):
    return jnp.dot(a, b, preferred_element_type=F32)


def _dot_nt(a, b):
    return lax.dot_general(a, b, (((1,), (1,)), ((), ())), preferred_element_type=F32)


def _dot_tn(a, b):
    return lax.dot_general(a, b, (((0,), (0,)), ((), ())), preferred_element_type=F32)


def _xcast_dt_kernel(xp_ref, xs_ref, wd_ref, xb_ref, dt_ref, *, n_prompt_tiles, n_dt):
    i = pl.program_id(0)
    x = jnp.where(i < n_prompt_tiles, xp_ref[...], xs_ref[...]).astype(BF16)
    xb_ref[...] = x
    acc = _dot_nt(x, wd_ref[...].astype(BF16))
    lane = lax.broadcasted_iota(jnp.int32, acc.shape, 1)
    dt_ref[...] = jnp.where(lane < n_dt, acc, 0.0)


def _xcast_dt(xp, xs, wt, *, dt_col, n_dt, tm):
    m_p, k = xp.shape
    m_s = xs.shape[0]
    npt = m_p // tm
    n_tiles = npt + m_s // tm
    assert dt_col % LANES == 0 and n_dt <= LANES
    kern = functools.partial(_xcast_dt_kernel, n_prompt_tiles=npt, n_dt=n_dt)
    return pl.pallas_call(
        kern,
        grid=(n_tiles,),
        in_specs=[
            pl.BlockSpec((tm, k), lambda i: (jnp.minimum(i, npt - 1), 0)),
            pl.BlockSpec((tm, k), lambda i: (jnp.maximum(i - npt, 0), 0)),
            pl.BlockSpec((LANES, k), lambda i: (dt_col // LANES, 0)),
        ],
        out_specs=[pl.BlockSpec((tm, k), lambda i: (i, 0)),
                   pl.BlockSpec((tm, LANES), lambda i: (i, 0))],
        out_shape=[jax.ShapeDtypeStruct((m_p + m_s, k), BF16),
                   jax.ShapeDtypeStruct((m_p + m_s, LANES), F32)],
        compiler_params=_cparams(1),
        name="xcast_dt",
    )(xp, xs, wt)


CAST_ROWS = 256


def _cast_rows(src_ref, src_row0, dst_ref, dst_row0, n_rows):
    done = 0
    while done < n_rows:
        step = min(CAST_ROWS, n_rows - done)
        dst_ref[dst_row0 + done:dst_row0 + done + step, :] = (
            src_ref[src_row0 + done:src_row0 + done + step, :].astype(BF16))
        done += step


def _cast_weight_tile(w_ref, wsc):
    _cast_rows(w_ref, 0, wsc, 0, wsc.shape[0])


def _side_cast_specs(side, n_inner):
    in_specs, out_specs, out_shape = [], [], []
    for w, rows in side:
        n_blk = w.shape[0] // rows
        assert n_blk * rows == w.shape[0] and rows % 16 == 0
        idx = lambda j, i, n_blk=n_blk: (jnp.minimum(j * n_inner + i, n_blk - 1), 0)
        in_specs.append(pl.BlockSpec((rows, w.shape[1]), idx))
        out_specs.append(pl.BlockSpec((rows, w.shape[1]), idx))
        out_shape.append(jax.ShapeDtypeStruct(w.shape, BF16))
    return in_specs, out_specs, out_shape


def _mm_w32_kernel(*refs, shift, n_out, head_rows, silu, n_side):
    x_ref, wa_ref = refs[0], refs[1]
    pos = 2
    wb_ref = None
    if shift:
        wb_ref = refs[pos]
        pos += 1
    side_in = refs[pos:pos + n_side]
    pos += n_side
    o_refs = refs[pos:pos + n_out]
    pos += n_out
    hp_ref = hs_ref = None
    if head_rows is not None:
        hp_ref, hs_ref = refs[pos], refs[pos + 1]
        pos += 2
    side_out = refs[pos:pos + n_side]
    pos += n_side
    wsc = refs[pos]
    i = pl.program_id(1)

    @pl.when(i == 0)
    def _():
        tn = wsc.shape[0]
        _cast_rows(wa_ref, shift, wsc, 0, tn - shift)
        if shift:
            _cast_rows(wb_ref, 0, wsc, tn - shift, shift)

    for s_in, s_out in zip(side_in, side_out):
        s_out[...] = s_in[...].astype(BF16)

    if silu:
        sub = x_ref.shape[0] // SILU_SUB_BLOCKS
        for r in range(SILU_SUB_BLOCKS):
            rs = slice(r * sub, (r + 1) * sub)
            res = _silu(_dot_nt(x_ref[rs, :], wsc[...]))
            for o_ref in o_refs:
                o_ref[rs, :] = res.astype(o_ref.dtype)
        return

    acc = _dot_nt(x_ref[...], wsc[...])
    for o_ref in o_refs:
        o_ref[...] = acc.astype(o_ref.dtype)

    if head_rows is not None:
        (tile_p, row0_p), (tile_s, row0_s) = head_rows

        @pl.when(i == tile_p)
        def _():
            hp_ref[...] = acc[row0_p:row0_p + hp_ref.shape[0], :]

        @pl.when(i == tile_s)
        def _():
            hs_ref[...] = acc[row0_s:row0_s + hs_ref.shape[0], :]


def _matmul_w32(x, wt, col_blk, n_col_tiles, out_dtypes, *, tm, tn, shift=0, head_out=None, silu=False,
                side=(), name):
    m, k = x.shape
    assert m % tm == 0 and tn % LANES == 0 and shift % 16 == 0 and shift < tn
    assert not silu or (head_out is None and tm % (16 * SILU_SUB_BLOCKS) == 0)
    in_specs = [pl.BlockSpec((tm, k), lambda j, i: (i, 0)),
                pl.BlockSpec((tn, k), lambda j, i: (col_blk(j), 0))]
    args = [x, wt]
    if shift:
        assert tn % shift == 0
        per = tn // shift
        in_specs.append(pl.BlockSpec((shift, k), lambda j, i: ((col_blk(j) + 1) * per, 0)))
        args.append(wt)
    side_in, side_out, side_shape = _side_cast_specs(side, m // tm)
    assert all(w.shape[0] // rows <= n_col_tiles * (m // tm) for w, rows in side)
    in_specs += side_in
    args += [w for w, _ in side]
    n = n_col_tiles * tn
    out_specs = [pl.BlockSpec((tm, tn), lambda j, i: (i, j)) for _ in out_dtypes]
    out_shape = [jax.ShapeDtypeStruct((m, n), dt) for dt in out_dtypes]
    head_rows = None
    if head_out is not None:
        n_keep, m_s = head_out
        m_p = m - m_s
        head_rows = (divmod(m_p - n_keep, tm), divmod(m_p, tm))
        assert head_rows[0][1] + n_keep <= tm and head_rows[1][1] + m_s <= tm
        out_specs += [pl.BlockSpec((n_keep, tn), lambda j, i: (0, j)),
                      pl.BlockSpec((m_s, tn), lambda j, i: (0, j))]
        out_shape += [jax.ShapeDtypeStruct((n_keep, n), F32), jax.ShapeDtypeStruct((m_s, n), F32)]
    out_specs += side_out
    out_shape += side_shape
    kern = functools.partial(_mm_w32_kernel, shift=shift, n_out=len(out_dtypes), head_rows=head_rows,
                             silu=silu, n_side=len(side))
    return pl.pallas_call(
        kern,
        grid=(n_col_tiles, m // tm),
        in_specs=in_specs,
        out_specs=out_specs,
        out_shape=out_shape,
        scratch_shapes=[pltpu.VMEM((tn, k), BF16)],
        compiler_params=_cparams(2),
        name=name,
    )(*args)


def _softplus(x):
    return jnp.maximum(x, 0.0) + jnp.log1p(jnp.exp(-jnp.abs(x)))


def _ssd_chunk(u, zs_ref, xbc_ref, dt_ref, cw_ref, cb_ref, dtb_ref, alog_ref, dsk_ref, nw_ref, e_ref, y_ref,
               hist8, act, ht, cst, ex, ysc, *, inner, n_groups, d_state):
    r0 = u * CHUNK
    rows = slice(r0, r0 + CHUNK)
    gw = inner // n_groups
    conv_dim = inner + 2 * n_groups * d_state

    for j in range(conv_dim // LANES):
        sl = slice(j * LANES, (j + 1) * LANES)
        if u == 0:
            blk = jnp.concatenate([hist8[:, sl], xbc_ref[0:CHUNK, sl]], axis=0)
        else:
            blk = xbc_ref[r0 - SUBLANES:r0 + CHUNK, sl]
        prev = pltpu.roll(blk, 1, 0)
        near = blk * cw_ref[3:4, sl] + prev * cw_ref[2:3, sl]
        far = pltpu.roll(blk * cw_ref[1:2, sl] + prev * cw_ref[0:1, sl], 2, 0)
        a = cb_ref[:, sl] + (near + far)[SUBLANES:, :]
        act[rows, sl] = _silu(a)

    dt = _softplus(dt_ref[rows, :] + dtb_ref[...])
    a_neg = -jnp.exp(alog_ref[...])
    da = dt * a_neg
    ri = lax.broadcasted_iota(jnp.int32, (CHUNK, CHUNK), 0)
    ci = lax.broadcasted_iota(jnp.int32, (CHUNK, CHUNK), 1)
    tri = (ri >= ci).astype(F32)
    cs = jnp.dot(tri, da, precision=lax.Precision.HIGHEST, preferred_element_type=F32)
    ecs = jnp.exp(cs)
    cs_last = cs[CHUNK - 1:CHUNK, :]
    dout = jnp.exp(cs_last - cs) * dt
    cd = jnp.exp(cs_last)
    cd_hi = cd.astype(BF16)
    cd_lo = (cd - cd_hi.astype(F32)).astype(BF16)
    lhs = jnp.concatenate(
        [dt.astype(BF16), ecs.astype(BF16), dout.astype(BF16),
         jnp.broadcast_to(cd_hi, (16, LANES)), jnp.broadcast_to(cd_lo, (16, LANES))], axis=0)
    ex[u] = _dot(lhs, e_ref[...])

    cst[u] = jnp.concatenate([cs, jnp.zeros((LANES - CHUNK, LANES), F32)], axis=0).T
    lane = lax.broadcasted_iota(jnp.int32, (CHUNK, LANES), 1)
    lo_half = lane < CHUNK
    n_pairs = inner // LANES
    ev = cst[u, pl.ds(0, n_pairs, stride=2), :]
    od = cst[u, pl.ds(1, n_pairs, stride=2), :]
    lane_p = lax.broadcasted_iota(jnp.int32, (n_pairs, LANES), 1)
    cst2 = jnp.where(lane_p < CHUNK, ev, pltpu.roll(od, CHUNK, 1))
    row = lax.broadcasted_iota(jnp.int32, (CHUNK, LANES), 0)
    causal2 = row >= jnp.where(lo_half, lane, lane - CHUNK)

    pairs_per_group = gw // LANES
    for g in range(n_groups):
        gs = slice(g * gw, (g + 1) * gw)
        bm_g = act[rows, inner + g * d_state: inner + (g + 1) * d_state].astype(BF16)
        cm_g = act[rows, inner + (n_groups + g) * d_state: inner + (n_groups + g + 1) * d_state].astype(BF16)
        cb2 = _dot_nt(cm_g, jnp.concatenate([bm_g, bm_g], axis=0))
        h_g = ht[:, gs]
        yoff = _dot(cm_g, h_g.astype(BF16))
        for p in range(pairs_per_group):
            k = g * pairs_per_group + p
            ls = slice(k * LANES, (k + 1) * LANES)
            csc = jnp.where(lo_half,
                            jnp.broadcast_to(cs[:, 2 * k:2 * k + 1], (CHUNK, LANES)),
                            jnp.broadcast_to(cs[:, 2 * k + 1:2 * k + 2], (CHUNK, LANES)))
            diff = csc - cst2[k:k + 1, :]
            dec = jnp.exp(jnp.where(causal2, diff, MASK_NEG))
            m2 = (cb2 * dec).astype(BF16)
            xs_p = act[rows, ls]
            xdt = xs_p * ex[u, 0:CHUNK, ls]
            x2 = jnp.concatenate([jnp.where(lo_half, xdt, 0.0), jnp.where(lo_half, 0.0, xdt)],
                                 axis=0).astype(BF16)
            yd = _dot(m2, x2)
            ysc[rows, ls] = (yd + yoff[:, p * LANES:(p + 1) * LANES] * ex[u, CHUNK:2 * CHUNK, ls]
                             + dsk_ref[:, ls] * xs_p)
        xw = (act[rows, gs] * ex[u, 2 * CHUNK:3 * CHUNK, gs]).astype(BF16)
        st = _dot_tn(bm_g, xw)
        cdx = ex[u, 3 * CHUNK:3 * CHUNK + 1, gs] + ex[u, 3 * CHUNK + 16:3 * CHUNK + 17, gs]
        ht[:, gs] = h_g * cdx + st

    for g in range(n_groups):
        gs = slice(g * gw, (g + 1) * gw)
        v = ysc[rows, gs] * zs_ref[rows, gs].astype(F32)
        ms = jnp.mean(v * v, axis=-1, keepdims=True)
        y_ref[rows, gs] = (v * lax.rsqrt(ms + RMS_EPS) * nw_ref[:, gs]).astype(y_ref.dtype)


def _ssd_kernel(*refs, cps, per_stream, aliased, inner, n_groups, d_state):
    (zs_ref, xbc_ref, dt_ref, hist_ref, h0_ref, cw_ref, cb_ref, dtb_ref, alog_ref,
     dsk_ref, nw_ref, e_ref) = refs[:12]
    y_ref, conv_ref, ho_ref, hist8, act, ht, cst, ex, ysc = refs[12 + int(aliased):]
    c = pl.program_id(0)
    n_rows = cps * CHUNK
    tail = slice(n_rows - (CONV_W - 1), n_rows)
    last = slice(SUBLANES - (CONV_W - 1), SUBLANES)

    @pl.when(c == 0)
    def _():
        hist8[...] = jnp.zeros_like(hist8)
        if not per_stream:
            ht[...] = jnp.zeros_like(ht)

    if per_stream:
        hist8[last, :] = hist_ref[0]
        ht[...] = h0_ref[0].T

    conv_ref[0] = xbc_ref[tail, :]

    for u in range(cps):
        _ssd_chunk(u, zs_ref, xbc_ref, dt_ref, cw_ref, cb_ref, dtb_ref, alog_ref, dsk_ref, nw_ref, e_ref, y_ref,
                   hist8, act, ht, cst, ex, ysc, inner=inner, n_groups=n_groups, d_state=d_state)
    hist8[last, :] = xbc_ref[tail, :]

    if per_stream:
        ho_ref[0] = ht[...].T
    else:
        @pl.when(c == pl.num_programs(0) - 1)
        def _():
            ho_ref[0] = ht[...].T


def _ssd(zs, xbc, dtr, hist, h0, conv_w, conv_b, dtb, alog, dskx, normw, emat, *,
         row0, n_streams, chunks_per_stream, cps, per_stream, inner, n_groups, d_state, name, y_into=None):
    conv_dim = xbc.shape[1]
    hp = inner
    n_rows = cps * CHUNK
    assert chunks_per_stream % cps == 0 and row0 % n_rows == 0
    assert per_stream == (chunks_per_stream == cps) and (per_stream or n_streams == 1)
    steps_per_stream = chunks_per_stream // cps
    blk0 = row0 // n_rows
    stream = (lambda c: c) if per_stream else (lambda c: 0)
    aliased = y_into is not None

    kern = functools.partial(_ssd_kernel, cps=cps, per_stream=per_stream, aliased=aliased, inner=inner,
                             n_groups=n_groups, d_state=d_state)
    const = lambda c: (0, 0)
    row_map = lambda c: (blk0 + c, 0)
    extra_specs = [pl.BlockSpec(memory_space=pl.ANY)] if aliased else []
    extra_args = [y_into] if aliased else []
    return pl.pallas_call(
        kern,
        grid=(n_streams * steps_per_stream,),
        input_output_aliases={12: 0} if aliased else {},
        in_specs=[
            pl.BlockSpec((n_rows, inner), row_map),
            pl.BlockSpec((n_rows, conv_dim), row_map),
            pl.BlockSpec((n_rows, LANES), row_map),
            pl.BlockSpec((1, CONV_W - 1, conv_dim), lambda c: (stream(c), 0, 0)),
            pl.BlockSpec((1, hp, d_state), lambda c: (stream(c), 0, 0)),
            pl.BlockSpec((CONV_W, conv_dim), const),
            pl.BlockSpec((1, conv_dim), const),
            pl.BlockSpec((1, LANES), const),
            pl.BlockSpec((1, LANES), const),
            pl.BlockSpec((1, inner), const),
            pl.BlockSpec((1, inner), const),
            pl.BlockSpec((LANES, inner), const),
        ] + extra_specs,
        out_specs=[
            pl.BlockSpec((n_rows, inner), row_map),
            pl.BlockSpec((1, CONV_W - 1, conv_dim), lambda c: (stream(c), 0, 0)),
            pl.BlockSpec((1, hp, d_state), lambda c: (stream(c), 0, 0)),
        ],
        out_shape=[
            jax.ShapeDtypeStruct((zs.shape[0], inner), BF16),
            jax.ShapeDtypeStruct((n_streams, CONV_W - 1, conv_dim), F32),
            jax.ShapeDtypeStruct((n_streams, hp, d_state), F32),
        ],
        scratch_shapes=[
            pltpu.VMEM((SUBLANES, conv_dim), F32),
            pltpu.VMEM((n_rows, conv_dim), F32),
            pltpu.VMEM((d_state, inner), F32),
            pltpu.VMEM((cps, LANES, LANES), F32),
            pltpu.VMEM((cps, 3 * CHUNK + 32, inner), F32),
            pltpu.VMEM((n_rows, inner), F32),
        ],
        compiler_params=_cparams(1),
        name=name,
    )(zs, xbc, dtr, hist, h0, conv_w, conv_b, dtb, alog, dskx, normw, emat, *extra_args)


def _band_bias_tables(ext_ref, bias_scr, n_heads):
    col = lax.broadcasted_iota(jnp.int32, (CHUNK, ATT_WIN), 1)
    band = BAND_PAST + CHUNK
    for h in range(n_heads):
        e = jnp.broadcast_to(ext_ref[h:h + 1, :], (CHUNK, ATT_WIN))
        even = pltpu.roll(e, ATT_WIN - (CHUNK - 1), 1, stride=1, stride_axis=0)
        odd = pltpu.roll(e, 1, 1, stride=1, stride_axis=0)
        bias_scr[0, h] = jnp.where(col < band, even, MASK_NEG)
        bias_scr[1, h] = jnp.where(col >= CHUNK, odd, MASK_NEG)


ATT_BLOCKS = BAND_PAST // ATT_TILE + 1


def _attn_prompt_kernel(*refs, n_heads):
    q_ref = refs[0]
    k_refs = refs[1:1 + ATT_BLOCKS]
    v_refs = refs[1 + ATT_BLOCKS:1 + 2 * ATT_BLOCKS]
    ext_ref, o_ref, bias_scr, s_scr, p_scr = refs[1 + 2 * ATT_BLOCKS:]
    i = pl.program_id(0)
    d = ATT_HEAD_DIM
    scale = d ** -0.5
    blk = ATT_TILE

    @pl.when(i == 0)
    def _():
        _band_bias_tables(ext_ref, bias_scr, n_heads)
        p_scr[...] = jnp.zeros_like(p_scr)

    past_mask = [jnp.where(i < ATT_BLOCKS - 1 - b, MASK_NEG, 0.0).astype(F32) for b in range(ATT_BLOCKS - 1)]

    for h in range(n_heads):
        hs = slice(h * d, (h + 1) * d)
        q_h = q_ref[:, hs]
        for b in range(ATT_BLOCKS):
            s = _dot_nt(q_h, k_refs[b][:, hs]) * scale
            s_scr[:, b * blk:(b + 1) * blk] = s + past_mask[b] if b < ATT_BLOCKS - 1 else s
        for t in range(ATT_TILE // CHUNK):
            rs = slice(t * CHUNK, (t + 1) * CHUNK)
            c0 = LANES * (t // 2)
            ws = slice(c0, c0 + ATT_WIN)
            s = s_scr[rs, ws] + bias_scr[t % 2, h]
            m = jnp.max(s, axis=-1, keepdims=True)
            p = jnp.exp(s - m)
            r = 1.0 / jnp.sum(p, axis=-1, keepdims=True)
            p_scr[rs, ws] = (p * r).astype(BF16)
        o = _dot(p_scr[:, 0:blk], v_refs[0][:, hs])
        for b in range(1, ATT_BLOCKS):
            o = o + _dot(p_scr[:, b * blk:(b + 1) * blk], v_refs[b][:, hs])
        o_ref[:, hs] = o.astype(o_ref.dtype)


def _attn_prompt(q, kb, vb, ext, *, n_tiles, n_heads):
    width = n_heads * ATT_HEAD_DIM
    assert BAND_PAST % ATT_TILE == 0 and ATT_TILE % (2 * CHUNK) == 0
    kern = functools.partial(_attn_prompt_kernel, n_heads=n_heads)
    window = [pl.BlockSpec((ATT_TILE, width), lambda i, back=back: (jnp.maximum(i - back, 0), 0))
              for back in range(ATT_BLOCKS - 1, -1, -1)]
    return pl.pallas_call(
        kern,
        grid=(n_tiles,),
        in_specs=[pl.BlockSpec((ATT_TILE, width), lambda i: (i, 0))] + window + window
                 + [pl.BlockSpec((n_heads, ATT_WIN), lambda i: (0, 0))],
        out_specs=pl.BlockSpec((ATT_TILE, width), lambda i: (i, 0)),
        out_shape=jax.ShapeDtypeStruct((q.shape[0], width), BF16),
        scratch_shapes=[
            pltpu.VMEM((2, n_heads, CHUNK, ATT_WIN), F32),
            pltpu.VMEM((ATT_TILE, ATT_BLOCKS * ATT_TILE), F32),
            pltpu.VMEM((ATT_TILE, ATT_BLOCKS * ATT_TILE), BF16),
        ],
        compiler_params=_cparams(1),
        name="attn_prompt",
    )(q, *([kb] * ATT_BLOCKS), *([vb] * ATT_BLOCKS), ext)


def _head_copy(cache_hbm, heads_vmem, sem, stream, slot, which, h):
    return pltpu.make_async_copy(cache_hbm.at[stream, :, h, :], heads_vmem.at[slot, h], sem.at[slot, which])


def _attn_sample_kernel(q_ref, kn_ref, vn_ref, ck_hbm, cv_hbm, ext_ref, att_in_ref, o_ref,
                        bias_scr, kh_scr, vh_scr, s_scr, p_scr, sem, *, n_heads):
    del att_in_ref
    d = ATT_HEAD_DIM
    scale = d ** -0.5
    s_id = pl.program_id(0)
    slot = lax.rem(s_id, 2)

    def copies(stream, slt):
        return ([_head_copy(ck_hbm, kh_scr, sem, stream, slt, 0, h) for h in range(n_heads)]
                + [_head_copy(cv_hbm, vh_scr, sem, stream, slt, 1, h) for h in range(n_heads)])

    def start_all(cs):
        for n, c in enumerate(cs):
            c.start(priority=n % 2)

    @pl.when(s_id == 0)
    def _():
        start_all(copies(0, 0))
        _band_bias_tables(ext_ref, bias_scr, n_heads)

    @pl.when(s_id + 1 < pl.num_programs(0))
    def _():
        start_all(copies(s_id + 1, 1 - slot))

    for c in copies(s_id, slot):
        c.wait()

    band = BAND_PAST + CHUNK
    for h in range(n_heads):
        hs = slice(h * d, (h + 1) * d)
        q_h = q_ref[:, hs]
        s_scr[h, :, 0:BAND_PAST] = (_dot_nt(q_h, kh_scr[slot, h].astype(BF16)) * scale
                                    + bias_scr[0, h, :, 0:BAND_PAST])
        s_scr[h, :, BAND_PAST:band] = (_dot_nt(q_h, kn_ref[:, hs]) * scale
                                       + bias_scr[0, h, :, BAND_PAST:band])
    for h in range(n_heads):
        s = s_scr[h, :, 0:band]
        p = jnp.exp(s - jnp.max(s, axis=-1, keepdims=True))
        r = 1.0 / jnp.sum(p, axis=-1, keepdims=True)
        p_scr[h, :, 0:band] = (p * r).astype(BF16)
    for h in range(n_heads):
        hs = slice(h * d, (h + 1) * d)
        o = (_dot(p_scr[h, :, 0:BAND_PAST], vh_scr[slot, h].astype(BF16))
             + _dot(p_scr[h, :, BAND_PAST:band], vn_ref[:, hs]))
        o_ref[:, hs] = o.astype(o_ref.dtype)


def _attn_sample(q, kb, vb, cache_k, cache_v, ext, att_into, *, row_blk0, n_sample, n_heads):
    width = n_heads * ATT_HEAD_DIM
    kern = functools.partial(_attn_sample_kernel, n_heads=n_heads)
    row_spec = pl.BlockSpec((CHUNK, width), lambda s: (row_blk0 + s, 0))
    heads_buf = pltpu.VMEM((2, n_heads, BAND_PAST, ATT_HEAD_DIM), F32)
    return pl.pallas_call(
        kern,
        grid=(n_sample,),
        in_specs=[row_spec, row_spec, row_spec,
                  pl.BlockSpec(memory_space=pl.ANY), pl.BlockSpec(memory_space=pl.ANY),
                  pl.BlockSpec((n_heads, ATT_WIN), lambda s: (0, 0)),
                  pl.BlockSpec(memory_space=pl.ANY)],
        out_specs=row_spec,
        out_shape=jax.ShapeDtypeStruct(att_into.shape, BF16),
        input_output_aliases={6: 0},
        scratch_shapes=[pltpu.VMEM((2, n_heads, CHUNK, ATT_WIN), F32),
                        heads_buf, heads_buf,
                        pltpu.VMEM((n_heads, CHUNK, ATT_WIN), F32),
                        pltpu.VMEM((n_heads, CHUNK, ATT_WIN), BF16),
                        pltpu.SemaphoreType.DMA((2, 2))],
        compiler_params=_cparams(1),
        name="attn_sample",
    )(q, kb, vb, cache_k, cache_v, ext, att_into)


def _mix_kernel(s_ref, a_ref, w1_ref, w2_ref, g1_ref, g2_ref, o_ref):
    a1 = _dot(s_ref[...], w1_ref[...])
    a2 = _dot(a_ref[...], w2_ref[...])
    g1 = _sigmoid(g1_ref[...].astype(F32))
    g2 = _sigmoid(g2_ref[...].astype(F32))
    o_ref[...] = (g1 * a1 + g2 * a2).astype(o_ref.dtype)


def _mix(ssd_y, att, w1, w2, gates_arr, *, gate_col0, tm, tn):
    m, k1 = ssd_y.shape
    k2 = att.shape[1]
    n = w1.shape[1]
    assert att.shape[0] == m and m % tm == 0
    g1_blk = gate_col0 // tn
    g2_blk = (gate_col0 + n) // tn
    return pl.pallas_call(
        _mix_kernel,
        grid=(n // tn, m // tm),
        in_specs=[
            pl.BlockSpec((tm, k1), lambda j, i: (i, 0)),
            pl.BlockSpec((tm, k2), lambda j, i: (i, 0)),
            pl.BlockSpec((k1, tn), lambda j, i: (0, j), pipeline_mode=pl.Buffered(1)),
            pl.BlockSpec((k2, tn), lambda j, i: (0, j), pipeline_mode=pl.Buffered(1)),
            pl.BlockSpec((tm, tn), lambda j, i: (i, g1_blk + j)),
            pl.BlockSpec((tm, tn), lambda j, i: (i, g2_blk + j)),
        ],
        out_specs=pl.BlockSpec((tm, tn), lambda j, i: (i, j)),
        out_shape=jax.ShapeDtypeStruct((m, n), BF16),
        compiler_params=_cparams(2),
        name="branch_mix",
    )(ssd_y, att, w1, w2, gates_arr, gates_arr)


def _layer_norm(t, g, b):
    mu = jnp.mean(t, axis=-1, keepdims=True)
    tc = t - mu
    var = jnp.mean(tc * tc, axis=-1, keepdims=True)
    return tc * lax.rsqrt(var + LN_EPS) * g + b


LN_SUB_ROWS = 256


def _proj_ln_kernel(*refs, alpha, n_k, n_prompt_tiles, split_resid, n_out):
    a_ref, w_ref = refs[0], refs[1]
    pos = 2
    r_refs = refs[pos:pos + (2 if split_resid else 1)]
    pos += len(r_refs)
    g_ref, b_ref = refs[pos], refs[pos + 1]
    pos += 2
    o_refs = refs[pos:pos + n_out]
    acc_ref = refs[pos + n_out] if n_k > 1 else None
    i = pl.program_id(0)
    kk = pl.program_id(1)
    tm = a_ref.shape[0]

    def resid_rows(rs):
        if split_resid:
            return jnp.where(i < n_prompt_tiles, r_refs[0][rs, :], r_refs[1][rs, :])
        return r_refs[0][rs, :]

    def finish(first):
        for r in range(tm // LN_SUB_ROWS):
            rs = slice(r * LN_SUB_ROWS, (r + 1) * LN_SUB_ROWS)
            base = alpha * resid_rows(rs) if first else acc_ref[rs, :]
            y = _layer_norm(base + _dot(a_ref[rs, :], w_ref[...]), g_ref[...], b_ref[...])
            for o_ref in o_refs:
                o_ref[rs, :] = y.astype(o_ref.dtype)

    if n_k == 1:
        finish(True)
        return

    @pl.when(kk == 0)
    def _():
        acc_ref[...] = alpha * resid_rows(slice(None)) + _dot(a_ref[...], w_ref[...])

    if n_k > 2:
        @pl.when(jnp.logical_and(kk > 0, kk < n_k - 1))
        def _():
            acc_ref[...] += _dot(a_ref[...], w_ref[...])

    @pl.when(kk == n_k - 1)
    def _():
        finish(False)


def _proj_ln(a, w, resid, gamma, beta, out_dtypes, *, alpha, tm, n_k, row0, n_rows, m_prompt=None, name):
    k = a.shape[1]
    n = w.shape[1]
    tk = k // n_k
    assert tk * n_k == k and n_rows % tm == 0 and row0 % tm == 0 and tm % LN_SUB_ROWS == 0
    blk0 = row0 // tm
    split_resid = isinstance(resid, tuple)
    row_map = lambda i, kk: (blk0 + i, 0)
    w_mode = dict(pipeline_mode=pl.Buffered(1)) if n_k == 1 else {}
    in_specs = [pl.BlockSpec((tm, tk), lambda i, kk: (blk0 + i, kk)),
                pl.BlockSpec((tk, n), lambda i, kk: (kk, 0), **w_mode)]
    args = [a, w]
    npt = 0
    if split_resid:
        assert row0 == 0 and m_prompt % tm == 0
        npt = m_prompt // tm
        in_specs += [pl.BlockSpec((tm, n), lambda i, kk: (jnp.minimum(i, npt - 1), 0)),
                     pl.BlockSpec((tm, n), lambda i, kk: (jnp.maximum(i - npt, 0), 0))]
        args += list(resid)
    else:
        in_specs.append(pl.BlockSpec((tm, n), row_map))
        args.append(resid)
    in_specs += [pl.BlockSpec((1, n), lambda i, kk: (0, 0))] * 2
    args += [gamma, beta]
    kern = functools.partial(_proj_ln_kernel, alpha=alpha, n_k=n_k, n_prompt_tiles=npt,
                             split_resid=split_resid, n_out=len(out_dtypes))
    return pl.pallas_call(
        kern,
        grid=(n_rows // tm, n_k),
        in_specs=in_specs,
        out_specs=[pl.BlockSpec((tm, n), lambda i, kk: (i, 0)) for _ in out_dtypes],
        out_shape=[jax.ShapeDtypeStruct((n_rows, n), dt) for dt in out_dtypes],
        scratch_shapes=[pltpu.VMEM((tm, n), F32)] if n_k > 1 else [],
        compiler_params=_cparams(2),
        name=name,
    )(*args)


def _glu_kernel(*refs, n_side):
    h_ref, wg_ref, wu_ref = refs[:3]
    side_in = refs[3:3 + n_side]
    o_ref = refs[3 + n_side]
    side_out = refs[4 + n_side:4 + 2 * n_side]
    wg_sc, wu_sc = refs[4 + 2 * n_side:]

    @pl.when(pl.program_id(1) == 0)
    def _():
        _cast_weight_tile(wg_ref, wg_sc)
        _cast_weight_tile(wu_ref, wu_sc)

    for s_in, s_out in zip(side_in, side_out):
        s_out[...] = s_in[...].astype(BF16)

    tm = h_ref.shape[0]
    sub = tm // GLU_SUB_BLOCKS
    for r in range(GLU_SUB_BLOCKS):
        rs = slice(r * sub, (r + 1) * sub)
        h = h_ref[rs, :]
        g = _dot(h, wg_sc[...])
        u = _dot(h, wu_sc[...])
        o_ref[rs, :] = (g * jax.nn.sigmoid(g) * u).astype(o_ref.dtype)


def _glu(h, w_gate_up, *, tm, tf, side=()):
    m, k = h.shape
    d_ff = w_gate_up.shape[1] // 2
    nf = d_ff // tf
    assert nf * tf == d_ff
    side_in, side_out, side_shape = _side_cast_specs(side, m // tm)
    assert all(w.shape[0] // rows <= nf * (m // tm) for w, rows in side)
    return pl.pallas_call(
        functools.partial(_glu_kernel, n_side=len(side)),
        grid=(nf, m // tm),
        in_specs=[
            pl.BlockSpec((tm, k), lambda j, i: (i, 0)),
            pl.BlockSpec((k, tf), lambda j, i: (0, j)),
            pl.BlockSpec((k, tf), lambda j, i: (0, nf + j)),
        ] + side_in,
        out_specs=[pl.BlockSpec((tm, tf), lambda j, i: (i, j))] + side_out,
        out_shape=[jax.ShapeDtypeStruct((m, d_ff), BF16)] + side_shape,
        scratch_shapes=[pltpu.VMEM((k, tf), BF16), pltpu.VMEM((k, tf), BF16)],
        compiler_params=_cparams(2),
        name="ffn_glu",
    )(h, w_gate_up, w_gate_up, *[w for w, _ in side])


def kernel(x_prompt, x_sample, cache_k, cache_v, state_conv, state_ssm, w_in, conv_w, conv_b, dt_bias,
           a_log, d_skip, ssd_norm_w, rel_bias, w_ssd_out, w_att_out, w_o, ln1_g, ln1_b, w_gate_up,
           w_down, ln2_g, ln2_b):
    depth = w_in.shape[0]
    assert depth == 1
    batch, seq, d_model = x_prompt.shape
    n_sample, dec_seq, _ = x_sample.shape
    assert batch == 1 and dec_seq == CHUNK and seq % ATT_TILE == 0
    n_ssd_heads = dt_bias.shape[1]
    inner = n_ssd_heads * SSD_HEADDIM
    conv_dim = conv_w.shape[2]
    n_groups = SSD_GROUPS
    d_state = SSD_STATE
    assert conv_dim == inner + 2 * n_groups * d_state
    n_heads = rel_bias.shape[1]
    width = n_heads * ATT_HEAD_DIM
    assert cache_k.shape[2] == BAND_PAST
    alpha = (2.0 * depth) ** 0.25
    keep = min(BAND_PAST, seq)

    m_p = batch * seq
    m_s = n_sample * dec_seq
    xp2 = x_prompt.reshape(m_p, d_model)
    xs2 = x_sample.reshape(m_s, d_model)

    wt = w_in[0].T
    tn = PROJ_TN
    o_xbc = inner
    o_dt = o_xbc + conv_dim
    o_q = o_dt + n_ssd_heads
    shift = o_q % tn
    assert o_dt % tn == 0 and inner % tn == 0 and conv_dim % tn == 0 and width % tn == 0
    q_blk = o_dt // tn
    xb, dtr = _xcast_dt(xp2, xs2, wt, dt_col=o_dt, n_dt=n_ssd_heads, tm=XCAST_TM)
    zs, w_att_b, w_o_b = _matmul_w32(xb, wt, lambda j: j, inner // tn, [BF16], tm=PROJ_TM, tn=tn, silu=True,
                                     side=[(w_att_out[0], SIDE_CAST_ROWS), (w_o[0], SIDE_CAST_ROWS)],
                                     name="in_proj_z")
    (xbc,) = _matmul_w32(xb, wt, lambda j: inner // tn + j, conv_dim // tn, [F32], tm=PROJ_TM, tn=tn,
                         name="in_proj_xbc")
    wpt = width // tn
    qg, w_ssd_b = _matmul_w32(xb, wt, lambda j: jnp.where(j < wpt, q_blk + j, q_blk + 2 * wpt + j),
                              wpt + 2 * d_model // tn, [BF16], tm=PROJ_TM, tn=tn, shift=shift,
                              side=[(w_ssd_out[0], SIDE_CAST_ROWS)], name="in_proj_qg")
    kb, k_p, k_s = _matmul_w32(xb, wt, lambda j: q_blk + wpt + j, wpt, [BF16], tm=KV_TM, tn=tn, shift=shift,
                               head_out=(keep, m_s), name="in_proj_k")
    vb, v_p, v_s = _matmul_w32(xb, wt, lambda j: q_blk + 2 * wpt + j, wpt, [BF16], tm=KV_TM, tn=tn,
                               shift=shift, head_out=(keep, m_s), name="in_proj_v")

    pad_h = (0, LANES - n_ssd_heads)
    dtb = jnp.pad(dt_bias[0], pad_h).reshape(1, LANES)
    alog = jnp.pad(a_log[0], pad_h).reshape(1, LANES)
    dskx = jnp.repeat(d_skip[0], SSD_HEADDIM).reshape(1, inner)
    emat = (jnp.arange(LANES)[:, None] == (jnp.arange(inner)[None, :] // SSD_HEADDIM)).astype(BF16)
    ssd_args = (zs, xbc, dtr, state_conv[0], state_ssm[0].reshape(n_sample, inner, d_state),
                conv_w[0], conv_b[0].reshape(1, conv_dim), dtb, alog, dskx, ssd_norm_w[0].reshape(1, inner), emat)
    ssd_dims = dict(inner=inner, n_groups=n_groups, d_state=d_state)
    ssd_y, conv_p, h_p = _ssd(*ssd_args, row0=0, n_streams=batch, chunks_per_stream=seq // CHUNK,
                              cps=SSD_PROMPT_CPS, per_stream=False, name="ssd_scan_prompt", **ssd_dims)
    ssd_y, conv_s, h_s = _ssd(*ssd_args, row0=m_p, n_streams=n_sample, chunks_per_stream=1, cps=1,
                              per_stream=True, name="ssd_scan_sample", y_into=ssd_y, **ssd_dims)

    rb = rel_bias[0]
    band = BAND_PAST + CHUNK
    n_edge = band - REL_CLIP
    n_rev = ATT_WIN - n_edge
    rev = rb[:, ::-1][:, 1:1 + n_rev]
    ext = jnp.concatenate([jnp.broadcast_to(rb[:, 2 * REL_CLIP:], (n_heads, n_edge)), rev], axis=1)
    ext = jnp.pad(ext, ((0, 0), (0, ATT_WIN - ext.shape[1])))
    att = _attn_prompt(qg, kb, vb, ext, n_tiles=m_p // ATT_TILE, n_heads=n_heads)
    att = _attn_sample(qg, kb, vb, cache_k[0], cache_v[0], ext, att,
                       row_blk0=m_p // CHUNK, n_sample=n_sample, n_heads=n_heads)

    mixed = _mix(ssd_y, att, w_ssd_b, w_att_b, qg, gate_col0=width, tm=MIX_TM, tn=MIX_TN)
    g1 = ln1_g[0].reshape(1, d_model)
    b1 = ln1_b[0].reshape(1, d_model)
    g2 = ln2_g[0].reshape(1, d_model)
    b2 = ln2_b[0].reshape(1, d_model)
    h_f, h_b = _proj_ln(mixed, w_o_b, (xp2, xs2), g1, b1, [F32, BF16], alpha=alpha, tm=LN_TM,
                        n_k=1, row0=0, n_rows=m_p + m_s, m_prompt=m_p, name="wo_ln")
    act, w_dn = _glu(h_b, w_gate_up[0], tm=GLU_TM, tf=GLU_TF, side=[(w_down[0], DOWN_CAST_ROWS)])
    (y_p,) = _proj_ln(act, w_dn, h_f, g2, b2, [F32], alpha=alpha, tm=LN_TM, n_k=DOWN_K_SPLIT, row0=0,
                      n_rows=m_p, name="down_ln_prompt")
    (y_s,) = _proj_ln(act, w_dn, h_f, g2, b2, [F32], alpha=alpha, tm=LN_TM, n_k=DOWN_K_SPLIT, row0=m_p,
                      n_rows=m_s, name="down_ln_sample")

    return (y_p.reshape(batch, seq, d_model),
            y_s.reshape(n_sample, dec_seq, d_model),
            conv_p.reshape(1, batch, CONV_W - 1, conv_dim),
            h_p.reshape(1, batch, n_ssd_heads, SSD_HEADDIM, d_state),
            k_p.reshape(1, batch, keep, n_heads, ATT_HEAD_DIM),
            v_p.reshape(1, batch, keep, n_heads, ATT_HEAD_DIM),
            conv_s.reshape(1, n_sample, CONV_W - 1, conv_dim),
            h_s.reshape(1, n_sample, n_ssd_heads, SSD_HEADDIM, d_state),
            k_s.reshape(1, n_sample, dec_seq, n_heads, ATT_HEAD_DIM),
            v_s.reshape(1, n_sample, dec_seq, n_heads, ATT_HEAD_DIM))
```

```python
import functools

import jax
import jax.numpy as jnp
from jax import lax
from jax.experimental import pallas as pl
from jax.experimental.pallas import tpu as pltpu

F32 = jnp.float32
BF16 = jnp.bfloat16

CHUNK = 64
SSD_HEADDIM = 64
SSD_GROUPS = 8
SSD_STATE = 128
CONV_W = 4
ATT_HEAD_DIM = 128
BAND_CHUNKS = 8
BAND_PAST = BAND_CHUNKS * CHUNK
REL_CLIP = 128
LN_EPS = 1e-5
RMS_EPS = 1e-5
MASK_NEG = -1e30

LANES = 128
SUBLANES = 8
VMEM_LIMIT_BYTES = 56 * 1024 * 1024

PROJ_TM, PROJ_TN = 1536, 1024
KV_TM = 1024
XCAST_TM = 1024
MIX_TM, MIX_TN = 768, 1024
LN_TM = 512
DOWN_K_SPLIT = 2
GLU_TM, GLU_TF = 2304, 512
SSD_PROMPT_CPS = 4
SIDE_CAST_ROWS = 128
DOWN_CAST_ROWS = 176
ATT_TILE = 512
GLU_SUB_BLOCKS = 2
SILU_SUB_BLOCKS = 2
ATT_WIN = 5 * LANES


def _cparams(n_axes):
    return pltpu.CompilerParams(
        dimension_semantics=("arbitrary",) * n_axes,
        vmem_limit_bytes=VMEM_LIMIT_BYTES,
    )


def _sigmoid(x):
    return 0.5 * jnp.tanh(0.5 * x) + 0.5


def _silu(x):
    h = 0.5 * x
    return h + h * jnp.tanh(h)


def _dot(a, b):
    return jnp.dot(a, b, preferred_element_type=F32)


def _dot_nt(a, b):
    return lax.dot_general(a, b, (((1,), (1,)), ((), ())), preferred_element_type=F32)


def _dot_tn(a, b):
    return lax.dot_general(a, b, (((0,), (0,)), ((), ())), preferred_element_type=F32)


def _xcast_dt_kernel(xp_ref, xs_ref, wd_ref, xb_ref, dt_ref, *, n_prompt_tiles, n_dt):
    i = pl.program_id(0)
    x = jnp.where(i < n_prompt_tiles, xp_ref[...], xs_ref[...]).astype(BF16)
    xb_ref[...] = x
    acc = _dot_nt(x, wd_ref[...].astype(BF16))
    lane = lax.broadcasted_iota(jnp.int32, acc.shape, 1)
    dt_ref[...] = jnp.where(lane < n_dt, acc, 0.0)


def _xcast_dt(xp, xs, wt, *, dt_col, n_dt, tm):
    m_p, k = xp.shape
    m_s = xs.shape[0]
    npt = m_p // tm
    n_tiles = npt + m_s // tm
    assert dt_col % LANES == 0 and n_dt <= LANES
    kern = functools.partial(_xcast_dt_kernel, n_prompt_tiles=npt, n_dt=n_dt)
    return pl.pallas_call(
        kern,
        grid=(n_tiles,),
        in_specs=[
            pl.BlockSpec((tm, k), lambda i: (jnp.minimum(i, npt - 1), 0)),
            pl.BlockSpec((tm, k), lambda i: (jnp.maximum(i - npt, 0), 0)),
            pl.BlockSpec((LANES, k), lambda i: (dt_col // LANES, 0)),
        ],
        out_specs=[pl.BlockSpec((tm, k), lambda i: (i, 0)),
                   pl.BlockSpec((tm, LANES), lambda i: (i, 0))],
        out_shape=[jax.ShapeDtypeStruct((m_p + m_s, k), BF16),
                   jax.ShapeDtypeStruct((m_p + m_s, LANES), F32)],
        compiler_params=_cparams(1),
        name="xcast_dt",
    )(xp, xs, wt)


CAST_ROWS = 256


def _cast_rows(src_ref, src_row0, dst_ref, dst_row0, n_rows):
    done = 0
    while done < n_rows:
        step = min(CAST_ROWS, n_rows - done)
        dst_ref[dst_row0 + done:dst_row0 + done + step, :] = (
            src_ref[src_row0 + done:src_row0 + done + step, :].astype(BF16))
        done += step


def _cast_weight_tile(w_ref, wsc):
    _cast_rows(w_ref, 0, wsc, 0, wsc.shape[0])


def _side_cast_specs(side, n_inner):
    in_specs, out_specs, out_shape = [], [], []
    for w, rows in side:
        n_blk = w.shape[0] // rows
        assert n_blk * rows == w.shape[0] and rows % 16 == 0
        idx = lambda j, i, n_blk=n_blk: (jnp.minimum(j * n_inner + i, n_blk - 1), 0)
        in_specs.append(pl.BlockSpec((rows, w.shape[1]), idx))
        out_specs.append(pl.BlockSpec((rows, w.shape[1]), idx))
        out_shape.append(jax.ShapeDtypeStruct(w.shape, BF16))
    return in_specs, out_specs, out_shape


def _mm_w32_kernel(*refs, shift, n_out, head_rows, silu, n_side):
    x_ref, wa_ref = refs[0], refs[1]
    pos = 2
    wb_ref = None
    if shift:
        wb_ref = refs[pos]
        pos += 1
    side_in = refs[pos:pos + n_side]
    pos += n_side
    o_refs = refs[pos:pos + n_out]
    pos += n_out
    hp_ref = hs_ref = None
    if head_rows is not None:
        hp_ref, hs_ref = refs[pos], refs[pos + 1]
        pos += 2
    side_out = refs[pos:pos + n_side]
    pos += n_side
    wsc = refs[pos]
    i = pl.program_id(1)

    @pl.when(i == 0)
    def _():
        tn = wsc.shape[0]
        _cast_rows(wa_ref, shift, wsc, 0, tn - shift)
        if shift:
            _cast_rows(wb_ref, 0, wsc, tn - shift, shift)

    for s_in, s_out in zip(side_in, side_out):
        s_out[...] = s_in[...].astype(BF16)

    if silu:
        sub = x_ref.shape[0] // SILU_SUB_BLOCKS
        for r in range(SILU_SUB_BLOCKS):
            rs = slice(r * sub, (r + 1) * sub)
            res = _silu(_dot_nt(x_ref[rs, :], wsc[...]))
            for o_ref in o_refs:
                o_ref[rs, :] = res.astype(o_ref.dtype)
        return

    acc = _dot_nt(x_ref[...], wsc[...])
    for o_ref in o_refs:
        o_ref[...] = acc.astype(o_ref.dtype)

    if head_rows is not None:
        (tile_p, row0_p), (tile_s, row0_s) = head_rows

        @pl.when(i == tile_p)
        def _():
            hp_ref[...] = acc[row0_p:row0_p + hp_ref.shape[0], :]

        @pl.when(i == tile_s)
        def _():
            hs_ref[...] = acc[row0_s:row0_s + hs_ref.shape[0], :]


def _matmul_w32(x, wt, col_blk, n_col_tiles, out_dtypes, *, tm, tn, shift=0, head_out=None, silu=False,
                side=(), name):
    m, k = x.shape
    assert m % tm == 0 and tn % LANES == 0 and shift % 16 == 0 and shift < tn
    assert not silu or (head_out is None and tm % (16 * SILU_SUB_BLOCKS) == 0)
    in_specs = [pl.BlockSpec((tm, k), lambda j, i: (i, 0)),
                pl.BlockSpec((tn, k), lambda j, i: (col_blk(j), 0))]
    args = [x, wt]
    if shift:
        assert tn % shift == 0
        per = tn // shift
        in_specs.append(pl.BlockSpec((shift, k), lambda j, i: ((col_blk(j) + 1) * per, 0)))
        args.append(wt)
    side_in, side_out, side_shape = _side_cast_specs(side, m // tm)
    assert all(w.shape[0] // rows <= n_col_tiles * (m // tm) for w, rows in side)
    in_specs += side_in
    args += [w for w, _ in side]
    n = n_col_tiles * tn
    out_specs = [pl.BlockSpec((tm, tn), lambda j, i: (i, j)) for _ in out_dtypes]
    out_shape = [jax.ShapeDtypeStruct((m, n), dt) for dt in out_dtypes]
    head_rows = None
    if head_out is not None:
        n_keep, m_s = head_out
        m_p = m - m_s
        head_rows = (divmod(m_p - n_keep, tm), divmod(m_p, tm))
        assert head_rows[0][1] + n_keep <= tm and head_rows[1][1] + m_s <= tm
        out_specs += [pl.BlockSpec((n_keep, tn), lambda j, i: (0, j)),
                      pl.BlockSpec((m_s, tn), lambda j, i: (0, j))]
        out_shape += [jax.ShapeDtypeStruct((n_keep, n), F32), jax.ShapeDtypeStruct((m_s, n), F32)]
    out_specs += side_out
    out_shape += side_shape
    kern = functools.partial(_mm_w32_kernel, shift=shift, n_out=len(out_dtypes), head_rows=head_rows,
                             silu=silu, n_side=len(side))
    return pl.pallas_call(
        kern,
        grid=(n_col_tiles, m // tm),
        in_specs=in_specs,
        out_specs=out_specs,
        out_shape=out_shape,
        scratch_shapes=[pltpu.VMEM((tn, k), BF16)],
        compiler_params=_cparams(2),
        name=name,
    )(*args)


def _softplus(x):
    return jnp.maximum(x, 0.0) + jnp.log1p(jnp.exp(-jnp.abs(x)))


def _ssd_chunk(u, zs_ref, xbc_ref, dt_ref, cw_ref, cb_ref, dtb_ref, alog_ref, dsk_ref, nw_ref, e_ref, y_ref,
               hist8, act, ht, cst, ex, ysc, *, inner, n_groups, d_state):
    r0 = u * CHUNK
    rows = slice(r0, r0 + CHUNK)
    gw = inner // n_groups
    conv_dim = inner + 2 * n_groups * d_state

    for j in range(conv_dim // LANES):
        sl = slice(j * LANES, (j + 1) * LANES)
        if u == 0:
            blk = jnp.concatenate([hist8[:, sl], xbc_ref[0:CHUNK, sl]], axis=0)
        else:
            blk = xbc_ref[r0 - SUBLANES:r0 + CHUNK, sl]
        prev = pltpu.roll(blk, 1, 0)
        near = blk * cw_ref[3:4, sl] + prev * cw_ref[2:3, sl]
        far = pltpu.roll(blk * cw_ref[1:2, sl] + prev * cw_ref[0:1, sl], 2, 0)
        a = cb_ref[:, sl] + (near + far)[SUBLANES:, :]
        act[rows, sl] = _silu(a)

    dt = _softplus(dt_ref[rows, :] + dtb_ref[...])
    a_neg = -jnp.exp(alog_ref[...])
    da = dt * a_neg
    ri = lax.broadcasted_iota(jnp.int32, (CHUNK, CHUNK), 0)
    ci = lax.broadcasted_iota(jnp.int32, (CHUNK, CHUNK), 1)
    tri = (ri >= ci).astype(F32)
    cs = jnp.dot(tri, da, precision=lax.Precision.HIGHEST, preferred_element_type=F32)
    ecs = jnp.exp(cs)
    cs_last = cs[CHUNK - 1:CHUNK, :]
    dout = jnp.exp(cs_last - cs) * dt
    cd = jnp.exp(cs_last)
    cd_hi = cd.astype(BF16)
    cd_lo = (cd - cd_hi.astype(F32)).astype(BF16)
    lhs = jnp.concatenate(
        [dt.astype(BF16), ecs.astype(BF16), dout.astype(BF16),
         jnp.broadcast_to(cd_hi, (16, LANES)), jnp.broadcast_to(cd_lo, (16, LANES))], axis=0)
    ex[u] = _dot(lhs, e_ref[...])

    cst[u] = jnp.concatenate([cs, jnp.zeros((LANES - CHUNK, LANES), F32)], axis=0).T
    lane = lax.broadcasted_iota(jnp.int32, (CHUNK, LANES), 1)
    lo_half = lane < CHUNK
    n_pairs = inner // LANES
    ev = cst[u, pl.ds(0, n_pairs, stride=2), :]
    od = cst[u, pl.ds(1, n_pairs, stride=2), :]
    lane_p = lax.broadcasted_iota(jnp.int32, (n_pairs, LANES), 1)
    cst2 = jnp.where(lane_p < CHUNK, ev, pltpu.roll(od, CHUNK, 1))
    row = lax.broadcasted_iota(jnp.int32, (CHUNK, LANES), 0)
    causal2 = row >= jnp.where(lo_half, lane, lane - CHUNK)

    pairs_per_group = gw // LANES
    for g in range(n_groups):
        gs = slice(g * gw, (g + 1) * gw)
        bm_g = act[rows, inner + g * d_state: inner + (g + 1) * d_state].astype(BF16)
        cm_g = act[rows, inner + (n_groups + g) * d_state: inner + (n_groups + g + 1) * d_state].astype(BF16)
        cb2 = _dot_nt(cm_g, jnp.concatenate([bm_g, bm_g], axis=0))
        h_g = ht[:, gs]
        yoff = _dot(cm_g, h_g.astype(BF16))
        for p in range(pairs_per_group):
            k = g * pairs_per_group + p
            ls = slice(k * LANES, (k + 1) * LANES)
            csc = jnp.where(lo_half,
                            jnp.broadcast_to(cs[:, 2 * k:2 * k + 1], (CHUNK, LANES)),
                            jnp.broadcast_to(cs[:, 2 * k + 1:2 * k + 2], (CHUNK, LANES)))
            diff = csc - cst2[k:k + 1, :]
            dec = jnp.exp(jnp.where(causal2, diff, MASK_NEG))
            m2 = (cb2 * dec).astype(BF16)
            xs_p = act[rows, ls]
            xdt = xs_p * ex[u, 0:CHUNK, ls]
            x2 = jnp.concatenate([jnp.where(lo_half, xdt, 0.0), jnp.where(lo_half, 0.0, xdt)],
                                 axis=0).astype(BF16)
            yd = _dot(m2, x2)
            ysc[rows, ls] = (yd + yoff[:, p * LANES:(p + 1) * LANES] * ex[u, CHUNK:2 * CHUNK, ls]
                             + dsk_ref[:, ls] * xs_p)
        xw = (act[rows, gs] * ex[u, 2 * CHUNK:3 * CHUNK, gs]).astype(BF16)
        st = _dot_tn(bm_g, xw)
        cdx = ex[u, 3 * CHUNK:3 * CHUNK + 1, gs] + ex[u, 3 * CHUNK + 16:3 * CHUNK + 17, gs]
        ht[:, gs] = h_g * cdx + st

    for g in range(n_groups):
        gs = slice(g * gw, (g + 1) * gw)
        v = ysc[rows, gs] * zs_ref[rows, gs].astype(F32)
        ms = jnp.mean(v * v, axis=-1, keepdims=True)
        y_ref[rows, gs] = (v * lax.rsqrt(ms + RMS_EPS) * nw_ref[:, gs]).astype(y_ref.dtype)


def _ssd_kernel(*refs, cps, per_stream, aliased, inner, n_groups, d_state):
    (zs_ref, xbc_ref, dt_ref, hist_ref, h0_ref, cw_ref, cb_ref, dtb_ref, alog_ref,
     dsk_ref, nw_ref, e_ref) = refs[:12]
    y_ref, conv_ref, ho_ref, hist8, act, ht, cst, ex, ysc = refs[12 + int(aliased):]
    c = pl.program_id(0)
    n_rows = cps * CHUNK
    tail = slice(n_rows - (CONV_W - 1), n_rows)
    last = slice(SUBLANES - (CONV_W - 1), SUBLANES)

    @pl.when(c == 0)
    def _():
        hist8[...] = jnp.zeros_like(hist8)
        if not per_stream:
            ht[...] = jnp.zeros_like(ht)

    if per_stream:
        hist8[last, :] = hist_ref[0]
        ht[...] = h0_ref[0].T

    conv_ref[0] = xbc_ref[tail, :]

    for u in range(cps):
        _ssd_chunk(u, zs_ref, xbc_ref, dt_ref, cw_ref, cb_ref, dtb_ref, alog_ref, dsk_ref, nw_ref, e_ref, y_ref,
                   hist8, act, ht, cst, ex, ysc, inner=inner, n_groups=n_groups, d_state=d_state)
    hist8[last, :] = xbc_ref[tail, :]

    if per_stream:
        ho_ref[0] = ht[...].T
    else:
        @pl.when(c == pl.num_programs(0) - 1)
        def _():
            ho_ref[0] = ht[...].T


def _ssd(zs, xbc, dtr, hist, h0, conv_w, conv_b, dtb, alog, dskx, normw, emat, *,
         row0, n_streams, chunks_per_stream, cps, per_stream, inner, n_groups, d_state, name, y_into=None):
    conv_dim = xbc.shape[1]
    hp = inner
    n_rows = cps * CHUNK
    assert chunks_per_stream % cps == 0 and row0 % n_rows == 0
    assert per_stream == (chunks_per_stream == cps) and (per_stream or n_streams == 1)
    steps_per_stream = chunks_per_stream // cps
    blk0 = row0 // n_rows
    stream = (lambda c: c) if per_stream else (lambda c: 0)
    aliased = y_into is not None

    kern = functools.partial(_ssd_kernel, cps=cps, per_stream=per_stream, aliased=aliased, inner=inner,
                             n_groups=n_groups, d_state=d_state)
    const = lambda c: (0, 0)
    row_map = lambda c: (blk0 + c, 0)
    extra_specs = [pl.BlockSpec(memory_space=pl.ANY)] if aliased else []
    extra_args = [y_into] if aliased else []
    return pl.pallas_call(
        kern,
        grid=(n_streams * steps_per_stream,),
        input_output_aliases={12: 0} if aliased else {},
        in_specs=[
            pl.BlockSpec((n_rows, inner), row_map),
            pl.BlockSpec((n_rows, conv_dim), row_map),
            pl.BlockSpec((n_rows, LANES), row_map),
            pl.BlockSpec((1, CONV_W - 1, conv_dim), lambda c: (stream(c), 0, 0)),
            pl.BlockSpec((1, hp, d_state), lambda c: (stream(c), 0, 0)),
            pl.BlockSpec((CONV_W, conv_dim), const),
            pl.BlockSpec((1, conv_dim), const),
            pl.BlockSpec((1, LANES), const),
            pl.BlockSpec((1, LANES), const),
            pl.BlockSpec((1, inner), const),
            pl.BlockSpec((1, inner), const),
            pl.BlockSpec((LANES, inner), const),
        ] + extra_specs,
        out_specs=[
            pl.BlockSpec((n_rows, inner), row_map),
            pl.BlockSpec((1, CONV_W - 1, conv_dim), lambda c: (stream(c), 0, 0)),
            pl.BlockSpec((1, hp, d_state), lambda c: (stream(c), 0, 0)),
        ],
        out_shape=[
            jax.ShapeDtypeStruct((zs.shape[0], inner), BF16),
            jax.ShapeDtypeStruct((n_streams, CONV_W - 1, conv_dim), F32),
            jax.ShapeDtypeStruct((n_streams, hp, d_state), F32),
        ],
        scratch_shapes=[
            pltpu.VMEM((SUBLANES, conv_dim), F32),
            pltpu.VMEM((n_rows, conv_dim), F32),
            pltpu.VMEM((d_state, inner), F32),
            pltpu.VMEM((cps, LANES, LANES), F32),
            pltpu.VMEM((cps, 3 * CHUNK + 32, inner), F32),
            pltpu.VMEM((n_rows, inner), F32),
        ],
        compiler_params=_cparams(1),
        name=name,
    )(zs, xbc, dtr, hist, h0, conv_w, conv_b, dtb, alog, dskx, normw, emat, *extra_args)


def _band_bias_tables(ext_ref, bias_scr, n_heads):
    col = lax.broadcasted_iota(jnp.int32, (CHUNK, ATT_WIN), 1)
    band = BAND_PAST + CHUNK
    for h in range(n_heads):
        e = jnp.broadcast_to(ext_ref[h:h + 1, :], (CHUNK, ATT_WIN))
        even = pltpu.roll(e, ATT_WIN - (CHUNK - 1), 1, stride=1, stride_axis=0)
        odd = pltpu.roll(e, 1, 1, stride=1, stride_axis=0)
        bias_scr[0, h] = jnp.where(col < band, even, MASK_NEG)
        bias_scr[1, h] = jnp.where(col >= CHUNK, odd, MASK_NEG)


ATT_BLOCKS = BAND_PAST // ATT_TILE + 1


def _attn_prompt_kernel(*refs, n_heads):
    q_ref = refs[0]
    k_refs = refs[1:1 + ATT_BLOCKS]
    v_refs = refs[1 + ATT_BLOCKS:1 + 2 * ATT_BLOCKS]
    ext_ref, o_ref, bias_scr, s_scr, p_scr = refs[1 + 2 * ATT_BLOCKS:]
    i = pl.program_id(0)
    d = ATT_HEAD_DIM
    scale = d ** -0.5
    blk = ATT_TILE

    @pl.when(i == 0)
    def _():
        _band_bias_tables(ext_ref, bias_scr, n_heads)
        p_scr[...] = jnp.zeros_like(p_scr)

    past_mask = [jnp.where(i < ATT_BLOCKS - 1 - b, MASK_NEG, 0.0).astype(F32) for b in range(ATT_BLOCKS - 1)]

    for h in range(n_heads):
        hs = slice(h * d, (h + 1) * d)
        q_h = q_ref[:, hs]
        for b in range(ATT_BLOCKS):
            s = _dot_nt(q_h, k_refs[b][:, hs]) * scale
            s_scr[:, b * blk:(b + 1) * blk] = s + past_mask[b] if b < ATT_BLOCKS - 1 else s
        for t in range(ATT_TILE // CHUNK):
            rs = slice(t * CHUNK, (t + 1) * CHUNK)
            c0 = LANES * (t // 2)
            ws = slice(c0, c0 + ATT_WIN)
            s = s_scr[rs, ws] + bias_scr[t % 2, h]
            m = jnp.max(s, axis=-1, keepdims=True)
            p = jnp.exp(s - m)
            r = 1.0 / jnp.sum(p, axis=-1, keepdims=True)
            p_scr[rs, ws] = (p * r).astype(BF16)
        o = _dot(p_scr[:, 0:blk], v_refs[0][:, hs])
        for b in range(1, ATT_BLOCKS):
            o = o + _dot(p_scr[:, b * blk:(b + 1) * blk], v_refs[b][:, hs])
        o_ref[:, hs] = o.astype(o_ref.dtype)


def _attn_prompt(q, kb, vb, ext, *, n_tiles, n_heads):
    width = n_heads * ATT_HEAD_DIM
    assert BAND_PAST % ATT_TILE == 0 and ATT_TILE % (2 * CHUNK) == 0
    kern = functools.partial(_attn_prompt_kernel, n_heads=n_heads)
    window = [pl.BlockSpec((ATT_TILE, width), lambda i, back=back: (jnp.maximum(i - back, 0), 0))
              for back in range(ATT_BLOCKS - 1, -1, -1)]
    return pl.pallas_call(
        kern,
        grid=(n_tiles,),
        in_specs=[pl.BlockSpec((ATT_TILE, width), lambda i: (i, 0))] + window + window
                 + [pl.BlockSpec((n_heads, ATT_WIN), lambda i: (0, 0))],
        out_specs=pl.BlockSpec((ATT_TILE, width), lambda i: (i, 0)),
        out_shape=jax.ShapeDtypeStruct((q.shape[0], width), BF16),
        scratch_shapes=[
            pltpu.VMEM((2, n_heads, CHUNK, ATT_WIN), F32),
            pltpu.VMEM((ATT_TILE, ATT_BLOCKS * ATT_TILE), F32),
            pltpu.VMEM((ATT_TILE, ATT_BLOCKS * ATT_TILE), BF16),
        ],
        compiler_params=_cparams(1),
        name="attn_prompt",
    )(q, *([kb] * ATT_BLOCKS), *([vb] * ATT_BLOCKS), ext)


def _head_copy(cache_hbm, heads_vmem, sem, stream, slot, which, h):
    return pltpu.make_async_copy(cache_hbm.at[stream, :, h, :], heads_vmem.at[slot, h], sem.at[slot, which])


def _attn_sample_kernel(q_ref, kn_ref, vn_ref, ck_hbm, cv_hbm, ext_ref, att_in_ref, o_ref,
                        bias_scr, kh_scr, vh_scr, s_scr, p_scr, sem, *, n_heads):
    del att_in_ref
    d = ATT_HEAD_DIM
    scale = d ** -0.5
    s_id = pl.program_id(0)
    slot = lax.rem(s_id, 2)

    def copies(stream, slt):
        return ([_head_copy(ck_hbm, kh_scr, sem, stream, slt, 0, h) for h in range(n_heads)]
                + [_head_copy(cv_hbm, vh_scr, sem, stream, slt, 1, h) for h in range(n_heads)])

    @pl.when(s_id == 0)
    def _():
        for c in copies(0, 0):
            c.start()
        _band_bias_tables(ext_ref, bias_scr, n_heads)

    @pl.when(s_id + 1 < pl.num_programs(0))
    def _():
        for c in copies(s_id + 1, 1 - slot):
            c.start()

    for c in copies(s_id, slot):
        c.wait()

    band = BAND_PAST + CHUNK
    for h in range(n_heads):
        hs = slice(h * d, (h + 1) * d)
        q_h = q_ref[:, hs]
        s_scr[h, :, 0:BAND_PAST] = (_dot_nt(q_h, kh_scr[slot, h].astype(BF16)) * scale
                                    + bias_scr[0, h, :, 0:BAND_PAST])
        s_scr[h, :, BAND_PAST:band] = (_dot_nt(q_h, kn_ref[:, hs]) * scale
                                       + bias_scr[0, h, :, BAND_PAST:band])
    for h in range(n_heads):
        s = s_scr[h, :, 0:band]
        p = jnp.exp(s - jnp.max(s, axis=-1, keepdims=True))
        r = 1.0 / jnp.sum(p, axis=-1, keepdims=True)
        p_scr[h, :, 0:band] = (p * r).astype(BF16)
    for h in range(n_heads):
        hs = slice(h * d, (h + 1) * d)
        o = (_dot(p_scr[h, :, 0:BAND_PAST], vh_scr[slot, h].astype(BF16))
             + _dot(p_scr[h, :, BAND_PAST:band], vn_ref[:, hs]))
        o_ref[:, hs] = o.astype(o_ref.dtype)


def _attn_sample(q, kb, vb, cache_k, cache_v, ext, att_into, *, row_blk0, n_sample, n_heads):
    width = n_heads * ATT_HEAD_DIM
    kern = functools.partial(_attn_sample_kernel, n_heads=n_heads)
    row_spec = pl.BlockSpec((CHUNK, width), lambda s: (row_blk0 + s, 0))
    heads_buf = pltpu.VMEM((2, n_heads, BAND_PAST, ATT_HEAD_DIM), F32)
    return pl.pallas_call(
        kern,
        grid=(n_sample,),
        in_specs=[row_spec, row_spec, row_spec,
                  pl.BlockSpec(memory_space=pl.ANY), pl.BlockSpec(memory_space=pl.ANY),
                  pl.BlockSpec((n_heads, ATT_WIN), lambda s: (0, 0)),
                  pl.BlockSpec(memory_space=pl.ANY)],
        out_specs=row_spec,
        out_shape=jax.ShapeDtypeStruct(att_into.shape, BF16),
        input_output_aliases={6: 0},
        scratch_shapes=[pltpu.VMEM((2, n_heads, CHUNK, ATT_WIN), F32),
                        heads_buf, heads_buf,
                        pltpu.VMEM((n_heads, CHUNK, ATT_WIN), F32),
                        pltpu.VMEM((n_heads, CHUNK, ATT_WIN), BF16),
                        pltpu.SemaphoreType.DMA((2, 2))],
        compiler_params=_cparams(1),
        name="attn_sample",
    )(q, kb, vb, cache_k, cache_v, ext, att_into)


def _sample_fused_kernel(*refs, n_heads, inner, n_groups, d_state):
    ssd_in, att_in, outs, ssd_scr, att_scr = refs[:13], refs[13:20], refs[20:24], refs[24:30], refs[30:]
    _ssd_kernel(*ssd_in, *outs[:3], *ssd_scr, cps=1, per_stream=True, aliased=True, inner=inner,
                n_groups=n_groups, d_state=d_state)
    _attn_sample_kernel(*att_in, outs[3], *att_scr, n_heads=n_heads)


def _sample_fused(ssd_args, y_into, q, kb, vb, cache_k, cache_v, ext, att_into, *,
                  row_blk0, n_sample, n_heads, inner, n_groups, d_state):
    xbc = ssd_args[1]
    conv_dim = xbc.shape[1]
    width = n_heads * ATT_HEAD_DIM
    kern = functools.partial(_sample_fused_kernel, n_heads=n_heads, inner=inner, n_groups=n_groups,
                             d_state=d_state)
    const = lambda c: (0, 0)
    row_map = lambda c: (row_blk0 + c, 0)
    per_stream = lambda c: (c, 0, 0)
    any_spec = pl.BlockSpec(memory_space=pl.ANY)
    att_row = pl.BlockSpec((CHUNK, width), row_map)
    heads_buf = pltpu.VMEM((2, n_heads, BAND_PAST, ATT_HEAD_DIM), F32)
    return pl.pallas_call(
        kern,
        grid=(n_sample,),
        input_output_aliases={12: 0, 19: 3},
        in_specs=[
            pl.BlockSpec((CHUNK, inner), row_map),
            pl.BlockSpec((CHUNK, conv_dim), row_map),
            pl.BlockSpec((CHUNK, LANES), row_map),
            pl.BlockSpec((1, CONV_W - 1, conv_dim), per_stream),
            pl.BlockSpec((1, inner, d_state), per_stream),
            pl.BlockSpec((CONV_W, conv_dim), const),
            pl.BlockSpec((1, conv_dim), const),
            pl.BlockSpec((1, LANES), const),
            pl.BlockSpec((1, LANES), const),
            pl.BlockSpec((1, inner), const),
            pl.BlockSpec((1, inner), const),
            pl.BlockSpec((LANES, inner), const),
            any_spec,
            att_row, att_row, att_row, any_spec, any_spec,
            pl.BlockSpec((n_heads, ATT_WIN), const),
            any_spec,
        ],
        out_specs=[
            pl.BlockSpec((CHUNK, inner), row_map),
            pl.BlockSpec((1, CONV_W - 1, conv_dim), per_stream),
            pl.BlockSpec((1, inner, d_state), per_stream),
            att_row,
        ],
        out_shape=[
            jax.ShapeDtypeStruct(y_into.shape, BF16),
            jax.ShapeDtypeStruct((n_sample, CONV_W - 1, conv_dim), F32),
            jax.ShapeDtypeStruct((n_sample, inner, d_state), F32),
            jax.ShapeDtypeStruct(att_into.shape, BF16),
        ],
        scratch_shapes=[
            pltpu.VMEM((SUBLANES, conv_dim), F32),
            pltpu.VMEM((CHUNK, conv_dim), F32),
            pltpu.VMEM((d_state, inner), F32),
            pltpu.VMEM((1, LANES, LANES), F32),
            pltpu.VMEM((1, 3 * CHUNK + 32, inner), F32),
            pltpu.VMEM((CHUNK, inner), F32),
            pltpu.VMEM((2, n_heads, CHUNK, ATT_WIN), F32),
            heads_buf, heads_buf,
            pltpu.VMEM((n_heads, CHUNK, ATT_WIN), F32),
            pltpu.VMEM((n_heads, CHUNK, ATT_WIN), BF16),
            pltpu.SemaphoreType.DMA((2, 2)),
        ],
        compiler_params=_cparams(1),
        name="sample_ssd_attn",
    )(*ssd_args, y_into, q, kb, vb, cache_k, cache_v, ext, att_into)


def _mix_kernel(s_ref, a_ref, w1_ref, w2_ref, g1_ref, g2_ref, o_ref):
    a1 = _dot(s_ref[...], w1_ref[...])
    a2 = _dot(a_ref[...], w2_ref[...])
    g1 = _sigmoid(g1_ref[...].astype(F32))
    g2 = _sigmoid(g2_ref[...].astype(F32))
    o_ref[...] = (g1 * a1 + g2 * a2).astype(o_ref.dtype)


def _mix(ssd_y, att, w1, w2, gates_arr, *, gate_col0, tm, tn):
    m, k1 = ssd_y.shape
    k2 = att.shape[1]
    n = w1.shape[1]
    assert att.shape[0] == m and m % tm == 0
    g1_blk = gate_col0 // tn
    g2_blk = (gate_col0 + n) // tn
    return pl.pallas_call(
        _mix_kernel,
        grid=(n // tn, m // tm),
        in_specs=[
            pl.BlockSpec((tm, k1), lambda j, i: (i, 0)),
            pl.BlockSpec((tm, k2), lambda j, i: (i, 0)),
            pl.BlockSpec((k1, tn), lambda j, i: (0, j), pipeline_mode=pl.Buffered(1)),
            pl.BlockSpec((k2, tn), lambda j, i: (0, j), pipeline_mode=pl.Buffered(1)),
            pl.BlockSpec((tm, tn), lambda j, i: (i, g1_blk + j)),
            pl.BlockSpec((tm, tn), lambda j, i: (i, g2_blk + j)),
        ],
        out_specs=pl.BlockSpec((tm, tn), lambda j, i: (i, j)),
        out_shape=jax.ShapeDtypeStruct((m, n), BF16),
        compiler_params=_cparams(2),
        name="branch_mix",
    )(ssd_y, att, w1, w2, gates_arr, gates_arr)


def _layer_norm(t, g, b):
    mu = jnp.mean(t, axis=-1, keepdims=True)
    tc = t - mu
    var = jnp.mean(tc * tc, axis=-1, keepdims=True)
    return tc * lax.rsqrt(var + LN_EPS) * g + b


LN_SUB_ROWS = 256


def _proj_ln_kernel(*refs, alpha, n_k, n_prompt_tiles, split_resid, n_out):
    a_ref, w_ref = refs[0], refs[1]
    pos = 2
    r_refs = refs[pos:pos + (2 if split_resid else 1)]
    pos += len(r_refs)
    g_ref, b_ref = refs[pos], refs[pos + 1]
    pos += 2
    o_refs = refs[pos:pos + n_out]
    acc_ref = refs[pos + n_out] if n_k > 1 else None
    i = pl.program_id(0)
    kk = pl.program_id(1)
    tm = a_ref.shape[0]

    def resid_rows(rs):
        if split_resid:
            return jnp.where(i < n_prompt_tiles, r_refs[0][rs, :], r_refs[1][rs, :])
        return r_refs[0][rs, :]

    def finish(first):
        for r in range(tm // LN_SUB_ROWS):
            rs = slice(r * LN_SUB_ROWS, (r + 1) * LN_SUB_ROWS)
            base = alpha * resid_rows(rs) if first else acc_ref[rs, :]
            y = _layer_norm(base + _dot(a_ref[rs, :], w_ref[...]), g_ref[...], b_ref[...])
            for o_ref in o_refs:
                o_ref[rs, :] = y.astype(o_ref.dtype)

    if n_k == 1:
        finish(True)
        return

    @pl.when(kk == 0)
    def _():
        acc_ref[...] = alpha * resid_rows(slice(None)) + _dot(a_ref[...], w_ref[...])

    if n_k > 2:
        @pl.when(jnp.logical_and(kk > 0, kk < n_k - 1))
        def _():
            acc_ref[...] += _dot(a_ref[...], w_ref[...])

    @pl.when(kk == n_k - 1)
    def _():
        finish(False)


def _proj_ln(a, w, resid, gamma, beta, out_dtypes, *, alpha, tm, n_k, row0, n_rows, m_prompt=None, name):
    k = a.shape[1]
    n = w.shape[1]
    tk = k // n_k
    assert tk * n_k == k and n_rows % tm == 0 and row0 % tm == 0 and tm % LN_SUB_ROWS == 0
    blk0 = row0 // tm
    split_resid = isinstance(resid, tuple)
    row_map = lambda i, kk: (blk0 + i, 0)
    w_mode = dict(pipeline_mode=pl.Buffered(1)) if n_k == 1 else {}
    in_specs = [pl.BlockSpec((tm, tk), lambda i, kk: (blk0 + i, kk)),
                pl.BlockSpec((tk, n), lambda i, kk: (kk, 0), **w_mode)]
    args = [a, w]
    npt = 0
    if split_resid:
        assert row0 == 0 and m_prompt % tm == 0
        npt = m_prompt // tm
        in_specs += [pl.BlockSpec((tm, n), lambda i, kk: (jnp.minimum(i, npt - 1), 0)),
                     pl.BlockSpec((tm, n), lambda i, kk: (jnp.maximum(i - npt, 0), 0))]
        args += list(resid)
    else:
        in_specs.append(pl.BlockSpec((tm, n), row_map))
        args.append(resid)
    in_specs += [pl.BlockSpec((1, n), lambda i, kk: (0, 0))] * 2
    args += [gamma, beta]
    kern = functools.partial(_proj_ln_kernel, alpha=alpha, n_k=n_k, n_prompt_tiles=npt,
                             split_resid=split_resid, n_out=len(out_dtypes))
    return pl.pallas_call(
        kern,
        grid=(n_rows // tm, n_k),
        in_specs=in_specs,
        out_specs=[pl.BlockSpec((tm, n), lambda i, kk: (i, 0)) for _ in out_dtypes],
        out_shape=[jax.ShapeDtypeStruct((n_rows, n), dt) for dt in out_dtypes],
        scratch_shapes=[pltpu.VMEM((tm, n), F32)] if n_k > 1 else [],
        compiler_params=_cparams(2),
        name=name,
    )(*args)


def _glu_kernel(*refs, n_side):
    h_ref, wg_ref, wu_ref = refs[:3]
    side_in = refs[3:3 + n_side]
    o_ref = refs[3 + n_side]
    side_out = refs[4 + n_side:4 + 2 * n_side]
    wg_sc, wu_sc = refs[4 + 2 * n_side:]

    @pl.when(pl.program_id(1) == 0)
    def _():
        _cast_weight_tile(wg_ref, wg_sc)
        _cast_weight_tile(wu_ref, wu_sc)

    for s_in, s_out in zip(side_in, side_out):
        s_out[...] = s_in[...].astype(BF16)

    tm = h_ref.shape[0]
    sub = tm // GLU_SUB_BLOCKS
    for r in range(GLU_SUB_BLOCKS):
        rs = slice(r * sub, (r + 1) * sub)
        h = h_ref[rs, :]
        g = _dot(h, wg_sc[...])
        u = _dot(h, wu_sc[...])
        o_ref[rs, :] = (g * jax.nn.sigmoid(g) * u).astype(o_ref.dtype)


def _glu(h, w_gate_up, *, tm, tf, side=()):
    m, k = h.shape
    d_ff = w_gate_up.shape[1] // 2
    nf = d_ff // tf
    assert nf * tf == d_ff
    side_in, side_out, side_shape = _side_cast_specs(side, m // tm)
    assert all(w.shape[0] // rows <= nf * (m // tm) for w, rows in side)
    return pl.pallas_call(
        functools.partial(_glu_kernel, n_side=len(side)),
        grid=(nf, m // tm),
        in_specs=[
            pl.BlockSpec((tm, k), lambda j, i: (i, 0)),
            pl.BlockSpec((k, tf), lambda j, i: (0, j)),
            pl.BlockSpec((k, tf), lambda j, i: (0, nf + j)),
        ] + side_in,
        out_specs=[pl.BlockSpec((tm, tf), lambda j, i: (i, j))] + side_out,
        out_shape=[jax.ShapeDtypeStruct((m, d_ff), BF16)] + side_shape,
        scratch_shapes=[pltpu.VMEM((k, tf), BF16), pltpu.VMEM((k, tf), BF16)],
        compiler_params=_cparams(2),
        name="ffn_glu",
    )(h, w_gate_up, w_gate_up, *[w for w, _ in side])


def kernel(x_prompt, x_sample, cache_k, cache_v, state_conv, state_ssm, w_in, conv_w, conv_b, dt_bias,
           a_log, d_skip, ssd_norm_w, rel_bias, w_ssd_out, w_att_out, w_o, ln1_g, ln1_b, w_gate_up,
           w_down, ln2_g, ln2_b):
    depth = w_in.shape[0]
    assert depth == 1
    batch, seq, d_model = x_prompt.shape
    n_sample, dec_seq, _ = x_sample.shape
    assert batch == 1 and dec_seq == CHUNK and seq % ATT_TILE == 0
    n_ssd_heads = dt_bias.shape[1]
    inner = n_ssd_heads * SSD_HEADDIM
    conv_dim = conv_w.shape[2]
    n_groups = SSD_GROUPS
    d_state = SSD_STATE
    assert conv_dim == inner + 2 * n_groups * d_state
    n_heads = rel_bias.shape[1]
    width = n_heads * ATT_HEAD_DIM
    assert cache_k.shape[2] == BAND_PAST
    alpha = (2.0 * depth) ** 0.25
    keep = min(BAND_PAST, seq)

    m_p = batch * seq
    m_s = n_sample * dec_seq
    xp2 = x_prompt.reshape(m_p, d_model)
    xs2 = x_sample.reshape(m_s, d_model)

    wt = w_in[0].T
    tn = PROJ_TN
    o_xbc = inner
    o_dt = o_xbc + conv_dim
    o_q = o_dt + n_ssd_heads
    shift = o_q % tn
    assert o_dt % tn == 0 and inner % tn == 0 and conv_dim % tn == 0 and width % tn == 0
    q_blk = o_dt // tn
    xb, dtr = _xcast_dt(xp2, xs2, wt, dt_col=o_dt, n_dt=n_ssd_heads, tm=XCAST_TM)
    zs, w_att_b, w_o_b = _matmul_w32(xb, wt, lambda j: j, inner // tn, [BF16], tm=PROJ_TM, tn=tn, silu=True,
                                     side=[(w_att_out[0], SIDE_CAST_ROWS), (w_o[0], SIDE_CAST_ROWS)],
                                     name="in_proj_z")
    (xbc,) = _matmul_w32(xb, wt, lambda j: inner // tn + j, conv_dim // tn, [F32], tm=PROJ_TM, tn=tn,
                         name="in_proj_xbc")
    wpt = width // tn
    qg, w_ssd_b = _matmul_w32(xb, wt, lambda j: jnp.where(j < wpt, q_blk + j, q_blk + 2 * wpt + j),
                              wpt + 2 * d_model // tn, [BF16], tm=PROJ_TM, tn=tn, shift=shift,
                              side=[(w_ssd_out[0], SIDE_CAST_ROWS)], name="in_proj_qg")
    kb, k_p, k_s = _matmul_w32(xb, wt, lambda j: q_blk + wpt + j, wpt, [BF16], tm=KV_TM, tn=tn, shift=shift,
                               head_out=(keep, m_s), name="in_proj_k")
    vb, v_p, v_s = _matmul_w32(xb, wt, lambda j: q_blk + 2 * wpt + j, wpt, [BF16], tm=KV_TM, tn=tn,
                               shift=shift, head_out=(keep, m_s), name="in_proj_v")

    pad_h = (0, LANES - n_ssd_heads)
    dtb = jnp.pad(dt_bias[0], pad_h).reshape(1, LANES)
    alog = jnp.pad(a_log[0], pad_h).reshape(1, LANES)
    dskx = jnp.repeat(d_skip[0], SSD_HEADDIM).reshape(1, inner)
    emat = (jnp.arange(LANES)[:, None] == (jnp.arange(inner)[None, :] // SSD_HEADDIM)).astype(BF16)
    ssd_args = (zs, xbc, dtr, state_conv[0], state_ssm[0].reshape(n_sample, inner, d_state),
                conv_w[0], conv_b[0].reshape(1, conv_dim), dtb, alog, dskx, ssd_norm_w[0].reshape(1, inner), emat)
    ssd_dims = dict(inner=inner, n_groups=n_groups, d_state=d_state)
    ssd_y, conv_p, h_p = _ssd(*ssd_args, row0=0, n_streams=batch, chunks_per_stream=seq // CHUNK,
                              cps=SSD_PROMPT_CPS, per_stream=False, name="ssd_scan_prompt", **ssd_dims)

    rb = rel_bias[0]
    band = BAND_PAST + CHUNK
    n_edge = band - REL_CLIP
    n_rev = ATT_WIN - n_edge
    rev = rb[:, ::-1][:, 1:1 + n_rev]
    ext = jnp.concatenate([jnp.broadcast_to(rb[:, 2 * REL_CLIP:], (n_heads, n_edge)), rev], axis=1)
    ext = jnp.pad(ext, ((0, 0), (0, ATT_WIN - ext.shape[1])))
    att = _attn_prompt(qg, kb, vb, ext, n_tiles=m_p // ATT_TILE, n_heads=n_heads)
    ssd_y, conv_s, h_s, att = _sample_fused(ssd_args, ssd_y, qg, kb, vb, cache_k[0], cache_v[0], ext, att,
                                            row_blk0=m_p // CHUNK, n_sample=n_sample, n_heads=n_heads, **ssd_dims)

    mixed = _mix(ssd_y, att, w_ssd_b, w_att_b, qg, gate_col0=width, tm=MIX_TM, tn=MIX_TN)
    g1 = ln1_g[0].reshape(1, d_model)
    b1 = ln1_b[0].reshape(1, d_model)
    g2 = ln2_g[0].reshape(1, d_model)
    b2 = ln2_b[0].reshape(1, d_model)
    h_f, h_b = _proj_ln(mixed, w_o_b, (xp2, xs2), g1, b1, [F32, BF16], alpha=alpha, tm=LN_TM,
                        n_k=1, row0=0, n_rows=m_p + m_s, m_prompt=m_p, name="wo_ln")
    act, w_dn = _glu(h_b, w_gate_up[0], tm=GLU_TM, tf=GLU_TF, side=[(w_down[0], DOWN_CAST_ROWS)])
    (y_p,) = _proj_ln(act, w_dn, h_f, g2, b2, [F32], alpha=alpha, tm=LN_TM, n_k=DOWN_K_SPLIT, row0=0,
                      n_rows=m_p, name="down_ln_prompt")
    (y_s,) = _proj_ln(act, w_dn, h_f, g2, b2, [F32], alpha=alpha, tm=LN_TM, n_k=DOWN_K_SPLIT, row0=m_p,
                      n_rows=m_s, name="down_ln_sample")

    return (y_p.reshape(batch, seq, d_model),
            y_s.reshape(n_sample, dec_seq, d_model),
            conv_p.reshape(1, batch, CONV_W - 1, conv_dim),
            h_p.reshape(1, batch, n_ssd_heads, SSD_HEADDIM, d_state),
            k_p.reshape(1, batch, keep, n_heads, ATT_HEAD_DIM),
            v_p.reshape(1, batch, keep, n_heads, ATT_HEAD_DIM),
            conv_s.reshape(1, n_sample, CONV_W - 1, conv_dim),
            h_s.reshape(1, n_sample, n_ssd_heads, SSD_HEADDIM, d_state),
            k_s.reshape(1, n_sample, dec_seq, n_heads, ATT_HEAD_DIM),
            v_s.reshape(1, n_sample, dec_seq, n_heads, ATT_HEAD_DIM))
```

```python
import functools

import jax
import jax.numpy as jnp
from jax import lax
from jax.experimental import pallas as pl
from jax.experimental.pallas import tpu as pltpu

F32 = jnp.float32
BF16 = jnp.bfloat16

CHUNK = 64
SSD_HEADDIM = 64
SSD_GROUPS = 8
SSD_STATE = 128
CONV_W = 4
ATT_HEAD_DIM = 128
BAND_CHUNKS = 8
BAND_PAST = BAND_CHUNKS * CHUNK
REL_CLIP = 128
LN_EPS = 1e-5
RMS_EPS = 1e-5
MASK_NEG = -1e30

LANES = 128
SUBLANES = 8
VMEM_LIMIT_BYTES = 56 * 1024 * 1024

PROJ_TM, PROJ_TN = 1536, 1024
KV_TM = 1024
XCAST_TM = 1024
MIX_TM, MIX_TN = 768, 1024
LN_TM = 512
DOWN_K_SPLIT = 2
GLU_TM, GLU_TF = 2304, 512
SSD_PROMPT_CPS = 4
SIDE_CAST_ROWS = 128
DOWN_CAST_ROWS = 176
ATT_TILE = 512
GLU_SUB_BLOCKS = 2
SILU_SUB_BLOCKS = 2
ATT_WIN = 5 * LANES


def _cparams(n_axes):
    return pltpu.CompilerParams(
        dimension_semantics=("arbitrary",) * n_axes,
        vmem_limit_bytes=VMEM_LIMIT_BYTES,
    )


def _sigmoid(x):
    return 0.5 * jnp.tanh(0.5 * x) + 0.5


def _silu(x):
    h = 0.5 * x
    return h + h * jnp.tanh(h)


def _dot(a, b):
    return jnp.dot(a, b, preferred_element_type=F32)


def _dot_nt(a, b):
    return lax.dot_general(a, b, (((1,), (1,)), ((), ())), preferred_element_type=F32)


def _dot_tn(a, b):
    return lax.dot_general(a, b, (((0,), (0,)), ((), ())), preferred_element_type=F32)


def _xcast_dt_kernel(xp_ref, xs_ref, wd_ref, xb_ref, dt_ref, *, n_prompt_tiles, n_dt):
    i = pl.program_id(0)
    x = jnp.where(i < n_prompt_tiles, xp_ref[...], xs_ref[...]).astype(BF16)
    xb_ref[...] = x
    acc = _dot_nt(x, wd_ref[...].astype(BF16))
    lane = lax.broadcasted_iota(jnp.int32, acc.shape, 1)
    dt_ref[...] = jnp.where(lane < n_dt, acc, 0.0)


def _xcast_dt(xp, xs, wt, *, dt_col, n_dt, tm):
    m_p, k = xp.shape
    m_s = xs.shape[0]
    npt = m_p // tm
    n_tiles = npt + m_s // tm
    assert dt_col % LANES == 0 and n_dt <= LANES
    kern = functools.partial(_xcast_dt_kernel, n_prompt_tiles=npt, n_dt=n_dt)
    return pl.pallas_call(
        kern,
        grid=(n_tiles,),
        in_specs=[
            pl.BlockSpec((tm, k), lambda i: (jnp.minimum(i, npt - 1), 0)),
            pl.BlockSpec((tm, k), lambda i: (jnp.maximum(i - npt, 0), 0)),
            pl.BlockSpec((LANES, k), lambda i: (dt_col // LANES, 0)),
        ],
        out_specs=[pl.BlockSpec((tm, k), lambda i: (i, 0)),
                   pl.BlockSpec((tm, LANES), lambda i: (i, 0))],
        out_shape=[jax.ShapeDtypeStruct((m_p + m_s, k), BF16),
                   jax.ShapeDtypeStruct((m_p + m_s, LANES), F32)],
        compiler_params=_cparams(1),
        name="xcast_dt",
    )(xp, xs, wt)


CAST_ROWS = 256


def _cast_rows(src_ref, src_row0, dst_ref, dst_row0, n_rows):
    done = 0
    while done < n_rows:
        step = min(CAST_ROWS, n_rows - done)
        dst_ref[dst_row0 + done:dst_row0 + done + step, :] = (
            src_ref[src_row0 + done:src_row0 + done + step, :].astype(BF16))
        done += step


def _cast_weight_tile(w_ref, wsc):
    _cast_rows(w_ref, 0, wsc, 0, wsc.shape[0])


def _side_cast_specs(side, n_inner):
    in_specs, out_specs, out_shape = [], [], []
    for w, rows in side:
        n_blk = w.shape[0] // rows
        assert n_blk * rows == w.shape[0] and rows % 16 == 0
        idx = lambda j, i, n_blk=n_blk: (jnp.minimum(j * n_inner + i, n_blk - 1), 0)
        in_specs.append(pl.BlockSpec((rows, w.shape[1]), idx))
        out_specs.append(pl.BlockSpec((rows, w.shape[1]), idx))
        out_shape.append(jax.ShapeDtypeStruct(w.shape, BF16))
    return in_specs, out_specs, out_shape


def _mm_w32_kernel(*refs, shift, n_out, head_rows, silu, n_side):
    x_ref, wa_ref = refs[0], refs[1]
    pos = 2
    wb_ref = None
    if shift:
        wb_ref = refs[pos]
        pos += 1
    side_in = refs[pos:pos + n_side]
    pos += n_side
    o_refs = refs[pos:pos + n_out]
    pos += n_out
    hp_ref = hs_ref = None
    if head_rows is not None:
        hp_ref, hs_ref = refs[pos], refs[pos + 1]
        pos += 2
    side_out = refs[pos:pos + n_side]
    pos += n_side
    wsc = refs[pos]
    i = pl.program_id(1)

    @pl.when(i == 0)
    def _():
        tn = wsc.shape[0]
        _cast_rows(wa_ref, shift, wsc, 0, tn - shift)
        if shift:
            _cast_rows(wb_ref, 0, wsc, tn - shift, shift)

    for s_in, s_out in zip(side_in, side_out):
        s_out[...] = s_in[...].astype(BF16)

    if silu:
        sub = x_ref.shape[0] // SILU_SUB_BLOCKS
        for r in range(SILU_SUB_BLOCKS):
            rs = slice(r * sub, (r + 1) * sub)
            res = _silu(_dot_nt(x_ref[rs, :], wsc[...]))
            for o_ref in o_refs:
                o_ref[rs, :] = res.astype(o_ref.dtype)
        return

    acc = _dot_nt(x_ref[...], wsc[...])
    for o_ref in o_refs:
        o_ref[...] = acc.astype(o_ref.dtype)

    if head_rows is not None:
        (tile_p, row0_p), (tile_s, row0_s) = head_rows

        @pl.when(i == tile_p)
        def _():
            hp_ref[...] = acc[row0_p:row0_p + hp_ref.shape[0], :]

        @pl.when(i == tile_s)
        def _():
            hs_ref[...] = acc[row0_s:row0_s + hs_ref.shape[0], :]


def _matmul_w32(x, wt, col_blk, n_col_tiles, out_dtypes, *, tm, tn, shift=0, head_out=None, silu=False,
                side=(), name):
    m, k = x.shape
    assert m % tm == 0 and tn % LANES == 0 and shift % 16 == 0 and shift < tn
    assert not silu or (head_out is None and tm % (16 * SILU_SUB_BLOCKS) == 0)
    in_specs = [pl.BlockSpec((tm, k), lambda j, i: (i, 0)),
                pl.BlockSpec((tn, k), lambda j, i: (col_blk(j), 0))]
    args = [x, wt]
    if shift:
        assert tn % shift == 0
        per = tn // shift
        in_specs.append(pl.BlockSpec((shift, k), lambda j, i: ((col_blk(j) + 1) * per, 0)))
        args.append(wt)
    side_in, side_out, side_shape = _side_cast_specs(side, m // tm)
    assert all(w.shape[0] // rows <= n_col_tiles * (m // tm) for w, rows in side)
    in_specs += side_in
    args += [w for w, _ in side]
    n = n_col_tiles * tn
    out_specs = [pl.BlockSpec((tm, tn), lambda j, i: (i, j)) for _ in out_dtypes]
    out_shape = [jax.ShapeDtypeStruct((m, n), dt) for dt in out_dtypes]
    head_rows = None
    if head_out is not None:
        n_keep, m_s = head_out
        m_p = m - m_s
        head_rows = (divmod(m_p - n_keep, tm), divmod(m_p, tm))
        assert head_rows[0][1] + n_keep <= tm and head_rows[1][1] + m_s <= tm
        out_specs += [pl.BlockSpec((n_keep, tn), lambda j, i: (0, j)),
                      pl.BlockSpec((m_s, tn), lambda j, i: (0, j))]
        out_shape += [jax.ShapeDtypeStruct((n_keep, n), F32), jax.ShapeDtypeStruct((m_s, n), F32)]
    out_specs += side_out
    out_shape += side_shape
    kern = functools.partial(_mm_w32_kernel, shift=shift, n_out=len(out_dtypes), head_rows=head_rows,
                             silu=silu, n_side=len(side))
    return pl.pallas_call(
        kern,
        grid=(n_col_tiles, m // tm),
        in_specs=in_specs,
        out_specs=out_specs,
        out_shape=out_shape,
        scratch_shapes=[pltpu.VMEM((tn, k), BF16)],
        compiler_params=_cparams(2),
        name=name,
    )(*args)


def _softplus(x):
    return jnp.maximum(x, 0.0) + jnp.log1p(jnp.exp(-jnp.abs(x)))


def _ssd_chunk(u, zs_ref, xbc_ref, dt_ref, cw_ref, cb_ref, dtb_ref, alog_ref, dsk_ref, nw_ref, e_ref, y_ref,
               hist8, act, ht, cst, ex, ysc, *, inner, n_groups, d_state):
    r0 = u * CHUNK
    rows = slice(r0, r0 + CHUNK)
    gw = inner // n_groups
    conv_dim = inner + 2 * n_groups * d_state

    for j in range(conv_dim // LANES):
        sl = slice(j * LANES, (j + 1) * LANES)
        if u == 0:
            blk = jnp.concatenate([hist8[:, sl], xbc_ref[0:CHUNK, sl]], axis=0)
        else:
            blk = xbc_ref[r0 - SUBLANES:r0 + CHUNK, sl]
        prev = pltpu.roll(blk, 1, 0)
        near = blk * cw_ref[3:4, sl] + prev * cw_ref[2:3, sl]
        far = pltpu.roll(blk * cw_ref[1:2, sl] + prev * cw_ref[0:1, sl], 2, 0)
        a = cb_ref[:, sl] + (near + far)[SUBLANES:, :]
        act[rows, sl] = _silu(a)

    dt = _softplus(dt_ref[rows, :] + dtb_ref[...])
    a_neg = -jnp.exp(alog_ref[...])
    da = dt * a_neg
    ri = lax.broadcasted_iota(jnp.int32, (CHUNK, CHUNK), 0)
    ci = lax.broadcasted_iota(jnp.int32, (CHUNK, CHUNK), 1)
    tri = (ri >= ci).astype(F32)
    cs = jnp.dot(tri, da, precision=lax.Precision.HIGHEST, preferred_element_type=F32)
    ecs = jnp.exp(cs)
    cs_last = cs[CHUNK - 1:CHUNK, :]
    dout = jnp.exp(cs_last - cs) * dt
    cd = jnp.exp(cs_last)
    cd_hi = cd.astype(BF16)
    cd_lo = (cd - cd_hi.astype(F32)).astype(BF16)
    lhs = jnp.concatenate(
        [dt.astype(BF16), ecs.astype(BF16), dout.astype(BF16),
         jnp.broadcast_to(cd_hi, (16, LANES)), jnp.broadcast_to(cd_lo, (16, LANES))], axis=0)
    ex[u] = _dot(lhs, e_ref[...])

    cst[u] = jnp.concatenate([cs, jnp.zeros((LANES - CHUNK, LANES), F32)], axis=0).T
    lane = lax.broadcasted_iota(jnp.int32, (CHUNK, LANES), 1)
    lo_half = lane < CHUNK
    n_pairs = inner // LANES
    ev = cst[u, pl.ds(0, n_pairs, stride=2), :]
    od = cst[u, pl.ds(1, n_pairs, stride=2), :]
    lane_p = lax.broadcasted_iota(jnp.int32, (n_pairs, LANES), 1)
    cst2 = jnp.where(lane_p < CHUNK, ev, pltpu.roll(od, CHUNK, 1))
    row = lax.broadcasted_iota(jnp.int32, (CHUNK, LANES), 0)
    causal2 = row >= jnp.where(lo_half, lane, lane - CHUNK)

    pairs_per_group = gw // LANES
    for g in range(n_groups):
        gs = slice(g * gw, (g + 1) * gw)
        bm_g = act[rows, inner + g * d_state: inner + (g + 1) * d_state].astype(BF16)
        cm_g = act[rows, inner + (n_groups + g) * d_state: inner + (n_groups + g + 1) * d_state].astype(BF16)
        cb2 = _dot_nt(cm_g, jnp.concatenate([bm_g, bm_g], axis=0))
        h_g = ht[:, gs]
        yoff = _dot(cm_g, h_g.astype(BF16))
        for p in range(pairs_per_group):
            k = g * pairs_per_group + p
            ls = slice(k * LANES, (k + 1) * LANES)
            csc = jnp.where(lo_half,
                            jnp.broadcast_to(cs[:, 2 * k:2 * k + 1], (CHUNK, LANES)),
                            jnp.broadcast_to(cs[:, 2 * k + 1:2 * k + 2], (CHUNK, LANES)))
            diff = csc - cst2[k:k + 1, :]
            dec = jnp.exp(jnp.where(causal2, diff, MASK_NEG))
            m2 = (cb2 * dec).astype(BF16)
            xs_p = act[rows, ls]
            xdt = xs_p * ex[u, 0:CHUNK, ls]
            x2 = jnp.concatenate([jnp.where(lo_half, xdt, 0.0), jnp.where(lo_half, 0.0, xdt)],
                                 axis=0).astype(BF16)
            yd = _dot(m2, x2)
            ysc[rows, ls] = (yd + yoff[:, p * LANES:(p + 1) * LANES] * ex[u, CHUNK:2 * CHUNK, ls]
                             + dsk_ref[:, ls] * xs_p)
        xw = (act[rows, gs] * ex[u, 2 * CHUNK:3 * CHUNK, gs]).astype(BF16)
        st = _dot_tn(bm_g, xw)
        cdx = ex[u, 3 * CHUNK:3 * CHUNK + 1, gs] + ex[u, 3 * CHUNK + 16:3 * CHUNK + 17, gs]
        ht[:, gs] = h_g * cdx + st

    for g in range(n_groups):
        gs = slice(g * gw, (g + 1) * gw)
        v = ysc[rows, gs] * zs_ref[rows, gs].astype(F32)
        ms = jnp.mean(v * v, axis=-1, keepdims=True)
        y_ref[rows, gs] = (v * lax.rsqrt(ms + RMS_EPS) * nw_ref[:, gs]).astype(y_ref.dtype)


def _ssd_kernel(*refs, cps, per_stream, aliased, inner, n_groups, d_state):
    (zs_ref, xbc_ref, dt_ref, hist_ref, h0_ref, cw_ref, cb_ref, dtb_ref, alog_ref,
     dsk_ref, nw_ref, e_ref) = refs[:12]
    y_ref, conv_ref, ho_ref, hist8, act, ht, cst, ex, ysc = refs[12 + int(aliased):]
    c = pl.program_id(0)
    n_rows = cps * CHUNK
    tail = slice(n_rows - (CONV_W - 1), n_rows)
    last = slice(SUBLANES - (CONV_W - 1), SUBLANES)

    @pl.when(c == 0)
    def _():
        hist8[...] = jnp.zeros_like(hist8)
        if not per_stream:
            ht[...] = jnp.zeros_like(ht)

    if per_stream:
        hist8[last, :] = hist_ref[0]
        ht[...] = h0_ref[0].T

    conv_ref[0] = xbc_ref[tail, :]

    for u in range(cps):
        _ssd_chunk(u, zs_ref, xbc_ref, dt_ref, cw_ref, cb_ref, dtb_ref, alog_ref, dsk_ref, nw_ref, e_ref, y_ref,
                   hist8, act, ht, cst, ex, ysc, inner=inner, n_groups=n_groups, d_state=d_state)
    hist8[last, :] = xbc_ref[tail, :]

    if per_stream:
        ho_ref[0] = ht[...].T
    else:
        @pl.when(c == pl.num_programs(0) - 1)
        def _():
            ho_ref[0] = ht[...].T


def _ssd(zs, xbc, dtr, hist, h0, conv_w, conv_b, dtb, alog, dskx, normw, emat, *,
         row0, n_streams, chunks_per_stream, cps, per_stream, inner, n_groups, d_state, name, y_into=None):
    conv_dim = xbc.shape[1]
    hp = inner
    n_rows = cps * CHUNK
    assert chunks_per_stream % cps == 0 and row0 % n_rows == 0
    assert per_stream == (chunks_per_stream == cps) and (per_stream or n_streams == 1)
    steps_per_stream = chunks_per_stream // cps
    blk0 = row0 // n_rows
    stream = (lambda c: c) if per_stream else (lambda c: 0)
    aliased = y_into is not None

    kern = functools.partial(_ssd_kernel, cps=cps, per_stream=per_stream, aliased=aliased, inner=inner,
                             n_groups=n_groups, d_state=d_state)
    const = lambda c: (0, 0)
    row_map = lambda c: (blk0 + c, 0)
    extra_specs = [pl.BlockSpec(memory_space=pl.ANY)] if aliased else []
    extra_args = [y_into] if aliased else []
    return pl.pallas_call(
        kern,
        grid=(n_streams * steps_per_stream,),
        input_output_aliases={12: 0} if aliased else {},
        in_specs=[
            pl.BlockSpec((n_rows, inner), row_map),
            pl.BlockSpec((n_rows, conv_dim), row_map),
            pl.BlockSpec((n_rows, LANES), row_map),
            pl.BlockSpec((1, CONV_W - 1, conv_dim), lambda c: (stream(c), 0, 0)),
            pl.BlockSpec((1, hp, d_state), lambda c: (stream(c), 0, 0)),
            pl.BlockSpec((CONV_W, conv_dim), const),
            pl.BlockSpec((1, conv_dim), const),
            pl.BlockSpec((1, LANES), const),
            pl.BlockSpec((1, LANES), const),
            pl.BlockSpec((1, inner), const),
            pl.BlockSpec((1, inner), const),
            pl.BlockSpec((LANES, inner), const),
        ] + extra_specs,
        out_specs=[
            pl.BlockSpec((n_rows, inner), row_map),
            pl.BlockSpec((1, CONV_W - 1, conv_dim), lambda c: (stream(c), 0, 0)),
            pl.BlockSpec((1, hp, d_state), lambda c: (stream(c), 0, 0)),
        ],
        out_shape=[
            jax.ShapeDtypeStruct((zs.shape[0], inner), BF16),
            jax.ShapeDtypeStruct((n_streams, CONV_W - 1, conv_dim), F32),
            jax.ShapeDtypeStruct((n_streams, hp, d_state), F32),
        ],
        scratch_shapes=[
            pltpu.VMEM((SUBLANES, conv_dim), F32),
            pltpu.VMEM((n_rows, conv_dim), F32),
            pltpu.VMEM((d_state, inner), F32),
            pltpu.VMEM((cps, LANES, LANES), F32),
            pltpu.VMEM((cps, 3 * CHUNK + 32, inner), F32),
            pltpu.VMEM((n_rows, inner), F32),
        ],
        compiler_params=_cparams(1),
        name=name,
    )(zs, xbc, dtr, hist, h0, conv_w, conv_b, dtb, alog, dskx, normw, emat, *extra_args)


def _band_bias_tables(ext_ref, bias_scr, n_heads):
    col = lax.broadcasted_iota(jnp.int32, (CHUNK, ATT_WIN), 1)
    band = BAND_PAST + CHUNK
    for h in range(n_heads):
        e = jnp.broadcast_to(ext_ref[h:h + 1, :], (CHUNK, ATT_WIN))
        even = pltpu.roll(e, ATT_WIN - (CHUNK - 1), 1, stride=1, stride_axis=0)
        odd = pltpu.roll(e, 1, 1, stride=1, stride_axis=0)
        bias_scr[0, h] = jnp.where(col < band, even, MASK_NEG)
        bias_scr[1, h] = jnp.where(col >= CHUNK, odd, MASK_NEG)


ATT_BLOCKS = BAND_PAST // ATT_TILE + 1


def _attn_prompt_kernel(*refs, n_heads):
    q_ref = refs[0]
    k_refs = refs[1:1 + ATT_BLOCKS]
    v_refs = refs[1 + ATT_BLOCKS:1 + 2 * ATT_BLOCKS]
    ext_ref, o_ref, bias_scr, s_scr, p_scr = refs[1 + 2 * ATT_BLOCKS:]
    i = pl.program_id(0)
    d = ATT_HEAD_DIM
    scale = d ** -0.5
    blk = ATT_TILE

    @pl.when(i == 0)
    def _():
        _band_bias_tables(ext_ref, bias_scr, n_heads)
        p_scr[...] = jnp.zeros_like(p_scr)

    past_mask = [jnp.where(i < ATT_BLOCKS - 1 - b, MASK_NEG, 0.0).astype(F32) for b in range(ATT_BLOCKS - 1)]

    for h in range(n_heads):
        hs = slice(h * d, (h + 1) * d)
        q_h = q_ref[:, hs]
        for b in range(ATT_BLOCKS):
            s = _dot_nt(q_h, k_refs[b][:, hs]) * scale
            s_scr[:, b * blk:(b + 1) * blk] = s + past_mask[b] if b < ATT_BLOCKS - 1 else s
        for t in range(ATT_TILE // CHUNK):
            rs = slice(t * CHUNK, (t + 1) * CHUNK)
            c0 = LANES * (t // 2)
            ws = slice(c0, c0 + ATT_WIN)
            s = s_scr[rs, ws] + bias_scr[t % 2, h]
            m = jnp.max(s, axis=-1, keepdims=True)
            p = jnp.exp(s - m)
            r = 1.0 / jnp.sum(p, axis=-1, keepdims=True)
            p_scr[rs, ws] = (p * r).astype(BF16)
        o = _dot(p_scr[:, 0:blk], v_refs[0][:, hs])
        for b in range(1, ATT_BLOCKS):
            o = o + _dot(p_scr[:, b * blk:(b + 1) * blk], v_refs[b][:, hs])
        o_ref[:, hs] = o.astype(o_ref.dtype)


def _attn_prompt(q, kb, vb, ext, *, n_tiles, n_heads):
    width = n_heads * ATT_HEAD_DIM
    assert BAND_PAST % ATT_TILE == 0 and ATT_TILE % (2 * CHUNK) == 0
    kern = functools.partial(_attn_prompt_kernel, n_heads=n_heads)
    window = [pl.BlockSpec((ATT_TILE, width), lambda i, back=back: (jnp.maximum(i - back, 0), 0))
              for back in range(ATT_BLOCKS - 1, -1, -1)]
    return pl.pallas_call(
        kern,
        grid=(n_tiles,),
        in_specs=[pl.BlockSpec((ATT_TILE, width), lambda i: (i, 0))] + window + window
                 + [pl.BlockSpec((n_heads, ATT_WIN), lambda i: (0, 0))],
        out_specs=pl.BlockSpec((ATT_TILE, width), lambda i: (i, 0)),
        out_shape=jax.ShapeDtypeStruct((q.shape[0], width), BF16),
        scratch_shapes=[
            pltpu.VMEM((2, n_heads, CHUNK, ATT_WIN), F32),
            pltpu.VMEM((ATT_TILE, ATT_BLOCKS * ATT_TILE), F32),
            pltpu.VMEM((ATT_TILE, ATT_BLOCKS * ATT_TILE), BF16),
        ],
        compiler_params=_cparams(1),
        name="attn_prompt",
    )(q, *([kb] * ATT_BLOCKS), *([vb] * ATT_BLOCKS), ext)


def _head_copy(cache_hbm, heads_vmem, sem, stream, slot, which, h):
    return pltpu.make_async_copy(cache_hbm.at[stream, :, h, :], heads_vmem.at[slot, h], sem.at[slot, which])


def _attn_sample_kernel(q_ref, kn_ref, vn_ref, ck_hbm, cv_hbm, ext_ref, att_in_ref, o_ref,
                        bias_scr, kh_scr, vh_scr, s_scr, p_scr, sem, *, n_heads, first_started=False):
    del att_in_ref
    d = ATT_HEAD_DIM
    scale = d ** -0.5
    s_id = pl.program_id(0)
    slot = lax.rem(s_id, 2)

    def copies(stream, slt):
        return ([_head_copy(ck_hbm, kh_scr, sem, stream, slt, 0, h) for h in range(n_heads)]
                + [_head_copy(cv_hbm, vh_scr, sem, stream, slt, 1, h) for h in range(n_heads)])

    @pl.when(s_id == 0)
    def _():
        if not first_started:
            for c in copies(0, 0):
                c.start()
        _band_bias_tables(ext_ref, bias_scr, n_heads)

    @pl.when(s_id + 1 < pl.num_programs(0))
    def _():
        for c in copies(s_id + 1, 1 - slot):
            c.start()

    for c in copies(s_id, slot):
        c.wait()

    band = BAND_PAST + CHUNK
    for h in range(n_heads):
        hs = slice(h * d, (h + 1) * d)
        q_h = q_ref[:, hs]
        s_scr[h, :, 0:BAND_PAST] = (_dot_nt(q_h, kh_scr[slot, h].astype(BF16)) * scale
                                    + bias_scr[0, h, :, 0:BAND_PAST])
        s_scr[h, :, BAND_PAST:band] = (_dot_nt(q_h, kn_ref[:, hs]) * scale
                                       + bias_scr[0, h, :, BAND_PAST:band])
    for h in range(n_heads):
        s = s_scr[h, :, 0:band]
        p = jnp.exp(s - jnp.max(s, axis=-1, keepdims=True))
        r = 1.0 / jnp.sum(p, axis=-1, keepdims=True)
        p_scr[h, :, 0:band] = (p * r).astype(BF16)
    for h in range(n_heads):
        hs = slice(h * d, (h + 1) * d)
        o = (_dot(p_scr[h, :, 0:BAND_PAST], vh_scr[slot, h].astype(BF16))
             + _dot(p_scr[h, :, BAND_PAST:band], vn_ref[:, hs]))
        o_ref[:, hs] = o.astype(o_ref.dtype)


def _attn_sample(q, kb, vb, cache_k, cache_v, ext, att_into, *, row_blk0, n_sample, n_heads):
    width = n_heads * ATT_HEAD_DIM
    kern = functools.partial(_attn_sample_kernel, n_heads=n_heads)
    row_spec = pl.BlockSpec((CHUNK, width), lambda s: (row_blk0 + s, 0))
    heads_buf = pltpu.VMEM((2, n_heads, BAND_PAST, ATT_HEAD_DIM), F32)
    return pl.pallas_call(
        kern,
        grid=(n_sample,),
        in_specs=[row_spec, row_spec, row_spec,
                  pl.BlockSpec(memory_space=pl.ANY), pl.BlockSpec(memory_space=pl.ANY),
                  pl.BlockSpec((n_heads, ATT_WIN), lambda s: (0, 0)),
                  pl.BlockSpec(memory_space=pl.ANY)],
        out_specs=row_spec,
        out_shape=jax.ShapeDtypeStruct(att_into.shape, BF16),
        input_output_aliases={6: 0},
        scratch_shapes=[pltpu.VMEM((2, n_heads, CHUNK, ATT_WIN), F32),
                        heads_buf, heads_buf,
                        pltpu.VMEM((n_heads, CHUNK, ATT_WIN), F32),
                        pltpu.VMEM((n_heads, CHUNK, ATT_WIN), BF16),
                        pltpu.SemaphoreType.DMA((2, 2))],
        compiler_params=_cparams(1),
        name="attn_sample",
    )(q, kb, vb, cache_k, cache_v, ext, att_into)


def _sample_fused_kernel(*refs, n_heads, inner, n_groups, d_state):
    ssd_in, att_in, outs, ssd_scr, att_scr = refs[:13], refs[13:20], refs[20:24], refs[24:30], refs[30:]

    @pl.when(pl.program_id(0) == 0)
    def _():
        for h in range(n_heads):
            _head_copy(att_in[3], att_scr[1], att_scr[5], 0, 0, 0, h).start()
            _head_copy(att_in[4], att_scr[2], att_scr[5], 0, 0, 1, h).start()

    _ssd_kernel(*ssd_in, *outs[:3], *ssd_scr, cps=1, per_stream=True, aliased=True, inner=inner,
                n_groups=n_groups, d_state=d_state)
    _attn_sample_kernel(*att_in, outs[3], *att_scr, n_heads=n_heads, first_started=True)


def _sample_fused(ssd_args, y_into, q, kb, vb, cache_k, cache_v, ext, att_into, *,
                  row_blk0, n_sample, n_heads, inner, n_groups, d_state):
    xbc = ssd_args[1]
    conv_dim = xbc.shape[1]
    width = n_heads * ATT_HEAD_DIM
    kern = functools.partial(_sample_fused_kernel, n_heads=n_heads, inner=inner, n_groups=n_groups,
                             d_state=d_state)
    const = lambda c: (0, 0)
    row_map = lambda c: (row_blk0 + c, 0)
    per_stream = lambda c: (c, 0, 0)
    any_spec = pl.BlockSpec(memory_space=pl.ANY)
    att_row = pl.BlockSpec((CHUNK, width), row_map)
    heads_buf = pltpu.VMEM((2, n_heads, BAND_PAST, ATT_HEAD_DIM), F32)
    return pl.pallas_call(
        kern,
        grid=(n_sample,),
        input_output_aliases={12: 0, 19: 3},
        in_specs=[
            pl.BlockSpec((CHUNK, inner), row_map),
            pl.BlockSpec((CHUNK, conv_dim), row_map),
            pl.BlockSpec((CHUNK, LANES), row_map),
            pl.BlockSpec((1, CONV_W - 1, conv_dim), per_stream),
            pl.BlockSpec((1, inner, d_state), per_stream),
            pl.BlockSpec((CONV_W, conv_dim), const),
            pl.BlockSpec((1, conv_dim), const),
            pl.BlockSpec((1, LANES), const),
            pl.BlockSpec((1, LANES), const),
            pl.BlockSpec((1, inner), const),
            pl.BlockSpec((1, inner), const),
            pl.BlockSpec((LANES, inner), const),
            any_spec,
            att_row, att_row, att_row, any_spec, any_spec,
            pl.BlockSpec((n_heads, ATT_WIN), const),
            any_spec,
        ],
        out_specs=[
            pl.BlockSpec((CHUNK, inner), row_map),
            pl.BlockSpec((1, CONV_W - 1, conv_dim), per_stream),
            pl.BlockSpec((1, inner, d_state), per_stream),
            att_row,
        ],
        out_shape=[
            jax.ShapeDtypeStruct(y_into.shape, BF16),
            jax.ShapeDtypeStruct((n_sample, CONV_W - 1, conv_dim), F32),
            jax.ShapeDtypeStruct((n_sample, inner, d_state), F32),
            jax.ShapeDtypeStruct(att_into.shape, BF16),
        ],
        scratch_shapes=[
            pltpu.VMEM((SUBLANES, conv_dim), F32),
            pltpu.VMEM((CHUNK, conv_dim), F32),
            pltpu.VMEM((d_state, inner), F32),
            pltpu.VMEM((1, LANES, LANES), F32),
            pltpu.VMEM((1, 3 * CHUNK + 32, inner), F32),
            pltpu.VMEM((CHUNK, inner), F32),
            pltpu.VMEM((2, n_heads, CHUNK, ATT_WIN), F32),
            heads_buf, heads_buf,
            pltpu.VMEM((n_heads, CHUNK, ATT_WIN), F32),
            pltpu.VMEM((n_heads, CHUNK, ATT_WIN), BF16),
            pltpu.SemaphoreType.DMA((2, 2)),
        ],
        compiler_params=_cparams(1),
        name="sample_ssd_attn",
    )(*ssd_args, y_into, q, kb, vb, cache_k, cache_v, ext, att_into)


def _mix_kernel(s_ref, a_ref, w1_ref, w2_ref, g1_ref, g2_ref, o_ref):
    a1 = _dot(s_ref[...], w1_ref[...])
    a2 = _dot(a_ref[...], w2_ref[...])
    g1 = _sigmoid(g1_ref[...].astype(F32))
    g2 = _sigmoid(g2_ref[...].astype(F32))
    o_ref[...] = (g1 * a1 + g2 * a2).astype(o_ref.dtype)


def _mix(ssd_y, att, w1, w2, gates_arr, *, gate_col0, tm, tn):
    m, k1 = ssd_y.shape
    k2 = att.shape[1]
    n = w1.shape[1]
    assert att.shape[0] == m and m % tm == 0
    g1_blk = gate_col0 // tn
    g2_blk = (gate_col0 + n) // tn
    return pl.pallas_call(
        _mix_kernel,
        grid=(n // tn, m // tm),
        in_specs=[
            pl.BlockSpec((tm, k1), lambda j, i: (i, 0)),
            pl.BlockSpec((tm, k2), lambda j, i: (i, 0)),
            pl.BlockSpec((k1, tn), lambda j, i: (0, j), pipeline_mode=pl.Buffered(1)),
            pl.BlockSpec((k2, tn), lambda j, i: (0, j), pipeline_mode=pl.Buffered(1)),
            pl.BlockSpec((tm, tn), lambda j, i: (i, g1_blk + j)),
            pl.BlockSpec((tm, tn), lambda j, i: (i, g2_blk + j)),
        ],
        out_specs=pl.BlockSpec((tm, tn), lambda j, i: (i, j)),
        out_shape=jax.ShapeDtypeStruct((m, n), BF16),
        compiler_params=_cparams(2),
        name="branch_mix",
    )(ssd_y, att, w1, w2, gates_arr, gates_arr)


def _layer_norm(t, g, b):
    mu = jnp.mean(t, axis=-1, keepdims=True)
    tc = t - mu
    var = jnp.mean(tc * tc, axis=-1, keepdims=True)
    return tc * lax.rsqrt(var + LN_EPS) * g + b


LN_SUB_ROWS = 256


def _proj_ln_kernel(*refs, alpha, n_k, n_prompt_tiles, split_resid, n_out):
    a_ref, w_ref = refs[0], refs[1]
    pos = 2
    r_refs = refs[pos:pos + (2 if split_resid else 1)]
    pos += len(r_refs)
    g_ref, b_ref = refs[pos], refs[pos + 1]
    pos += 2
    o_refs = refs[pos:pos + n_out]
    acc_ref = refs[pos + n_out] if n_k > 1 else None
    i = pl.program_id(0)
    kk = pl.program_id(1)
    tm = a_ref.shape[0]

    def resid_rows(rs):
        if split_resid:
            return jnp.where(i < n_prompt_tiles, r_refs[0][rs, :], r_refs[1][rs, :])
        return r_refs[0][rs, :]

    def finish(first):
        for r in range(tm // LN_SUB_ROWS):
            rs = slice(r * LN_SUB_ROWS, (r + 1) * LN_SUB_ROWS)
            base = alpha * resid_rows(rs) if first else acc_ref[rs, :]
            y = _layer_norm(base + _dot(a_ref[rs, :], w_ref[...]), g_ref[...], b_ref[...])
            for o_ref in o_refs:
                o_ref[rs, :] = y.astype(o_ref.dtype)

    if n_k == 1:
        finish(True)
        return

    @pl.when(kk == 0)
    def _():
        acc_ref[...] = alpha * resid_rows(slice(None)) + _dot(a_ref[...], w_ref[...])

    if n_k > 2:
        @pl.when(jnp.logical_and(kk > 0, kk < n_k - 1))
        def _():
            acc_ref[...] += _dot(a_ref[...], w_ref[...])

    @pl.when(kk == n_k - 1)
    def _():
        finish(False)


def _proj_ln(a, w, resid, gamma, beta, out_dtypes, *, alpha, tm, n_k, row0, n_rows, m_prompt=None, name):
    k = a.shape[1]
    n = w.shape[1]
    tk = k // n_k
    assert tk * n_k == k and n_rows % tm == 0 and row0 % tm == 0 and tm % LN_SUB_ROWS == 0
    blk0 = row0 // tm
    split_resid = isinstance(resid, tuple)
    row_map = lambda i, kk: (blk0 + i, 0)
    w_mode = dict(pipeline_mode=pl.Buffered(1)) if n_k == 1 else {}
    in_specs = [pl.BlockSpec((tm, tk), lambda i, kk: (blk0 + i, kk)),
                pl.BlockSpec((tk, n), lambda i, kk: (kk, 0), **w_mode)]
    args = [a, w]
    npt = 0
    if split_resid:
        assert row0 == 0 and m_prompt % tm == 0
        npt = m_prompt // tm
        in_specs += [pl.BlockSpec((tm, n), lambda i, kk: (jnp.minimum(i, npt - 1), 0)),
                     pl.BlockSpec((tm, n), lambda i, kk: (jnp.maximum(i - npt, 0), 0))]
        args += list(resid)
    else:
        in_specs.append(pl.BlockSpec((tm, n), row_map))
        args.append(resid)
    in_specs += [pl.BlockSpec((1, n), lambda i, kk: (0, 0))] * 2
    args += [gamma, beta]
    kern = functools.partial(_proj_ln_kernel, alpha=alpha, n_k=n_k, n_prompt_tiles=npt,
                             split_resid=split_resid, n_out=len(out_dtypes))
    return pl.pallas_call(
        kern,
        grid=(n_rows // tm, n_k),
        in_specs=in_specs,
        out_specs=[pl.BlockSpec((tm, n), lambda i, kk: (i, 0)) for _ in out_dtypes],
        out_shape=[jax.ShapeDtypeStruct((n_rows, n), dt) for dt in out_dtypes],
        scratch_shapes=[pltpu.VMEM((tm, n), F32)] if n_k > 1 else [],
        compiler_params=_cparams(2),
        name=name,
    )(*args)


def _glu_kernel(*refs, n_side):
    h_ref, wg_ref, wu_ref = refs[:3]
    side_in = refs[3:3 + n_side]
    o_ref = refs[3 + n_side]
    side_out = refs[4 + n_side:4 + 2 * n_side]
    wg_sc, wu_sc = refs[4 + 2 * n_side:]

    @pl.when(pl.program_id(1) == 0)
    def _():
        _cast_weight_tile(wg_ref, wg_sc)
        _cast_weight_tile(wu_ref, wu_sc)

    for s_in, s_out in zip(side_in, side_out):
        s_out[...] = s_in[...].astype(BF16)

    tm = h_ref.shape[0]
    sub = tm // GLU_SUB_BLOCKS
    for r in range(GLU_SUB_BLOCKS):
        rs = slice(r * sub, (r + 1) * sub)
        h = h_ref[rs, :]
        g = _dot(h, wg_sc[...])
        u = _dot(h, wu_sc[...])
        o_ref[rs, :] = (g * jax.nn.sigmoid(g) * u).astype(o_ref.dtype)


def _glu(h, w_gate_up, *, tm, tf, side=()):
    m, k = h.shape
    d_ff = w_gate_up.shape[1] // 2
    nf = d_ff // tf
    assert nf * tf == d_ff
    side_in, side_out, side_shape = _side_cast_specs(side, m // tm)
    assert all(w.shape[0] // rows <= nf * (m // tm) for w, rows in side)
    return pl.pallas_call(
        functools.partial(_glu_kernel, n_side=len(side)),
        grid=(nf, m // tm),
        in_specs=[
            pl.BlockSpec((tm, k), lambda j, i: (i, 0)),
            pl.BlockSpec((k, tf), lambda j, i: (0, j)),
            pl.BlockSpec((k, tf), lambda j, i: (0, nf + j)),
        ] + side_in,
        out_specs=[pl.BlockSpec((tm, tf), lambda j, i: (i, j))] + side_out,
        out_shape=[jax.ShapeDtypeStruct((m, d_ff), BF16)] + side_shape,
        scratch_shapes=[pltpu.VMEM((k, tf), BF16), pltpu.VMEM((k, tf), BF16)],
        compiler_params=_cparams(2),
        name="ffn_glu",
    )(h, w_gate_up, w_gate_up, *[w for w, _ in side])


def kernel(x_prompt, x_sample, cache_k, cache_v, state_conv, state_ssm, w_in, conv_w, conv_b, dt_bias,
           a_log, d_skip, ssd_norm_w, rel_bias, w_ssd_out, w_att_out, w_o, ln1_g, ln1_b, w_gate_up,
           w_down, ln2_g, ln2_b):
    depth = w_in.shape[0]
    assert depth == 1
    batch, seq, d_model = x_prompt.shape
    n_sample, dec_seq, _ = x_sample.shape
    assert batch == 1 and dec_seq == CHUNK and seq % ATT_TILE == 0
    n_ssd_heads = dt_bias.shape[1]
    inner = n_ssd_heads * SSD_HEADDIM
    conv_dim = conv_w.shape[2]
    n_groups = SSD_GROUPS
    d_state = SSD_STATE
    assert conv_dim == inner + 2 * n_groups * d_state
    n_heads = rel_bias.shape[1]
    width = n_heads * ATT_HEAD_DIM
    assert cache_k.shape[2] == BAND_PAST
    alpha = (2.0 * depth) ** 0.25
    keep = min(BAND_PAST, seq)

    m_p = batch * seq
    m_s = n_sample * dec_seq
    xp2 = x_prompt.reshape(m_p, d_model)
    xs2 = x_sample.reshape(m_s, d_model)

    wt = w_in[0].T
    tn = PROJ_TN
    o_xbc = inner
    o_dt = o_xbc + conv_dim
    o_q = o_dt + n_ssd_heads
    shift = o_q % tn
    assert o_dt % tn == 0 and inner % tn == 0 and conv_dim % tn == 0 and width % tn == 0
    q_blk = o_dt // tn
    xb, dtr = _xcast_dt(xp2, xs2, wt, dt_col=o_dt, n_dt=n_ssd_heads, tm=XCAST_TM)
    zs, w_att_b, w_o_b = _matmul_w32(xb, wt, lambda j: j, inner // tn, [BF16], tm=PROJ_TM, tn=tn, silu=True,
                                     side=[(w_att_out[0], SIDE_CAST_ROWS), (w_o[0], SIDE_CAST_ROWS)],
                                     name="in_proj_z")
    (xbc,) = _matmul_w32(xb, wt, lambda j: inner // tn + j, conv_dim // tn, [F32], tm=PROJ_TM, tn=tn,
                         name="in_proj_xbc")
    wpt = width // tn
    qg, w_ssd_b = _matmul_w32(xb, wt, lambda j: jnp.where(j < wpt, q_blk + j, q_blk + 2 * wpt + j),
                              wpt + 2 * d_model // tn, [BF16], tm=PROJ_TM, tn=tn, shift=shift,
                              side=[(w_ssd_out[0], SIDE_CAST_ROWS)], name="in_proj_qg")
    kb, k_p, k_s = _matmul_w32(xb, wt, lambda j: q_blk + wpt + j, wpt, [BF16], tm=KV_TM, tn=tn, shift=shift,
                               head_out=(keep, m_s), name="in_proj_k")
    vb, v_p, v_s = _matmul_w32(xb, wt, lambda j: q_blk + 2 * wpt + j, wpt, [BF16], tm=KV_TM, tn=tn,
                               shift=shift, head_out=(keep, m_s), name="in_proj_v")

    pad_h = (0, LANES - n_ssd_heads)
    dtb = jnp.pad(dt_bias[0], pad_h).reshape(1, LANES)
    alog = jnp.pad(a_log[0], pad_h).reshape(1, LANES)
    dskx = jnp.repeat(d_skip[0], SSD_HEADDIM).reshape(1, inner)
    emat = (jnp.arange(LANES)[:, None] == (jnp.arange(inner)[None, :] // SSD_HEADDIM)).astype(BF16)
    ssd_args = (zs, xbc, dtr, state_conv[0], state_ssm[0].reshape(n_sample, inner, d_state),
                conv_w[0], conv_b[0].reshape(1, conv_dim), dtb, alog, dskx, ssd_norm_w[0].reshape(1, inner), emat)
    ssd_dims = dict(inner=inner, n_groups=n_groups, d_state=d_state)
    ssd_y, conv_p, h_p = _ssd(*ssd_args, row0=0, n_streams=batch, chunks_per_stream=seq // CHUNK,
                              cps=SSD_PROMPT_CPS, per_stream=False, name="ssd_scan_prompt", **ssd_dims)

    rb = rel_bias[0]
    band = BAND_PAST + CHUNK
    n_edge = band - REL_CLIP
    n_rev = ATT_WIN - n_edge
    rev = rb[:, ::-1][:, 1:1 + n_rev]
    ext = jnp.concatenate([jnp.broadcast_to(rb[:, 2 * REL_CLIP:], (n_heads, n_edge)), rev], axis=1)
    ext = jnp.pad(ext, ((0, 0), (0, ATT_WIN - ext.shape[1])))
    att = _attn_prompt(qg, kb, vb, ext, n_tiles=m_p // ATT_TILE, n_heads=n_heads)
    ssd_y, conv_s, h_s, att = _sample_fused(ssd_args, ssd_y, qg, kb, vb, cache_k[0], cache_v[0], ext, att,
                                            row_blk0=m_p // CHUNK, n_sample=n_sample, n_heads=n_heads, **ssd_dims)

    mixed = _mix(ssd_y, att, w_ssd_b, w_att_b, qg, gate_col0=width, tm=MIX_TM, tn=MIX_TN)
    g1 = ln1_g[0].reshape(1, d_model)
    b1 = ln1_b[0].reshape(1, d_model)
    g2 = ln2_g[0].reshape(1, d_model)
    b2 = ln2_b[0].reshape(1, d_model)
    h_f, h_b = _proj_ln(mixed, w_o_b, (xp2, xs2), g1, b1, [F32, BF16], alpha=alpha, tm=LN_TM,
                        n_k=1, row0=0, n_rows=m_p + m_s, m_prompt=m_p, name="wo_ln")
    act, w_dn = _glu(h_b, w_gate_up[0], tm=GLU_TM, tf=GLU_TF, side=[(w_down[0], DOWN_CAST_ROWS)])
    (y_p,) = _proj_ln(act, w_dn, h_f, g2, b2, [F32], alpha=alpha, tm=LN_TM, n_k=DOWN_K_SPLIT, row0=0,
                      n_rows=m_p, name="down_ln_prompt")
    (y_s,) = _proj_ln(act, w_dn, h_f, g2, b2, [F32], alpha=alpha, tm=LN_TM, n_k=DOWN_K_SPLIT, row0=m_p,
                      n_rows=m_s, name="down_ln_sample")

    return (y_p.reshape(batch, seq, d_model),
            y_s.reshape(n_sample, dec_seq, d_model),
            conv_p.reshape(1, batch, CONV_W - 1, conv_dim),
            h_p.reshape(1, batch, n_ssd_heads, SSD_HEADDIM, d_state),
            k_p.reshape(1, batch, keep, n_heads, ATT_HEAD_DIM),
            v_p.reshape(1, batch, keep, n_heads, ATT_HEAD_DIM),
            conv_s.reshape(1, n_sample, CONV_W - 1, conv_dim),
            h_s.reshape(1, n_sample, n_ssd_heads, SSD_HEADDIM, d_state),
            k_s.reshape(1, n_sample, dec_seq, n_heads, ATT_HEAD_DIM),
            v_s.reshape(1, n_sample, dec_seq, n_heads, ATT_HEAD_DIM))
```

```python
import functools

import jax
import jax.numpy as jnp
from jax import lax
from jax.experimental import pallas as pl
from jax.experimental.pallas import tpu as pltpu

F32 = jnp.float32
BF16 = jnp.bfloat16

CHUNK = 64
SSD_HEADDIM = 64
SSD_GROUPS = 8
SSD_STATE = 128
CONV_W = 4
ATT_HEAD_DIM = 128
BAND_CHUNKS = 8
BAND_PAST = BAND_CHUNKS * CHUNK
REL_CLIP = 128
LN_EPS = 1e-5
RMS_EPS = 1e-5
MASK_NEG = -1e30

LANES = 128
SUBLANES = 8
VMEM_LIMIT_BYTES = 56 * 1024 * 1024

PROJ_TM, PROJ_TN = 1536, 1024
KV_TM = 1024
XCAST_TM = 1024
MIX_TM, MIX_TN = 768, 1024
LN_TM = 512
DOWN_K_SPLIT = 2
GLU_TM, GLU_TF = 2304, 512
SSD_PROMPT_CPS = 4
SIDE_CAST_ROWS = 128
DOWN_CAST_ROWS = 176
ATT_TILE = 512
GLU_SUB_BLOCKS = 2
SILU_SUB_BLOCKS = 2
ATT_WIN = 5 * LANES


def _cparams(n_axes):
    return pltpu.CompilerParams(
        dimension_semantics=("arbitrary",) * n_axes,
        vmem_limit_bytes=VMEM_LIMIT_BYTES,
    )


def _sigmoid(x):
    return 0.5 * jnp.tanh(0.5 * x) + 0.5


def _silu(x):
    h = 0.5 * x
    return h + h * jnp.tanh(h)


def _dot(a, b):
    return jnp.dot(a, b, preferred_element_type=F32)


def _dot_nt(a, b):
    return lax.dot_general(a, b, (((1,), (1,)), ((), ())), preferred_element_type=F32)


def _dot_tn(a, b):
    return lax.dot_general(a, b, (((0,), (0,)), ((), ())), preferred_element_type=F32)


def _xcast_dt_kernel(xp_ref, xs_ref, wd_ref, xb_ref, dt_ref, *, n_prompt_tiles, n_dt):
    i = pl.program_id(0)
    x = jnp.where(i < n_prompt_tiles, xp_ref[...], xs_ref[...]).astype(BF16)
    xb_ref[...] = x
    acc = _dot_nt(x, wd_ref[...].astype(BF16))
    lane = lax.broadcasted_iota(jnp.int32, acc.shape, 1)
    dt_ref[...] = jnp.where(lane < n_dt, acc, 0.0)


def _xcast_dt(xp, xs, wt, *, dt_col, n_dt, tm):
    m_p, k = xp.shape
    m_s = xs.shape[0]
    npt = m_p // tm
    n_tiles = npt + m_s // tm
    assert dt_col % LANES == 0 and n_dt <= LANES
    kern = functools.partial(_xcast_dt_kernel, n_prompt_tiles=npt, n_dt=n_dt)
    return pl.pallas_call(
        kern,
        grid=(n_tiles,),
        in_specs=[
            pl.BlockSpec((tm, k), lambda i: (jnp.minimum(i, npt - 1), 0)),
            pl.BlockSpec((tm, k), lambda i: (jnp.maximum(i - npt, 0), 0)),
            pl.BlockSpec((LANES, k), lambda i: (dt_col // LANES, 0)),
        ],
        out_specs=[pl.BlockSpec((tm, k), lambda i: (i, 0)),
                   pl.BlockSpec((tm, LANES), lambda i: (i, 0))],
        out_shape=[jax.ShapeDtypeStruct((m_p + m_s, k), BF16),
                   jax.ShapeDtypeStruct((m_p + m_s, LANES), F32)],
        compiler_params=_cparams(1),
        name="xcast_dt",
    )(xp, xs, wt)


CAST_ROWS = 256


def _cast_rows(src_ref, src_row0, dst_ref, dst_row0, n_rows):
    done = 0
    while done < n_rows:
        step = min(CAST_ROWS, n_rows - done)
        dst_ref[dst_row0 + done:dst_row0 + done + step, :] = (
            src_ref[src_row0 + done:src_row0 + done + step, :].astype(BF16))
        done += step


def _cast_weight_tile(w_ref, wsc):
    _cast_rows(w_ref, 0, wsc, 0, wsc.shape[0])


def _side_cast_specs(side, n_inner):
    in_specs, out_specs, out_shape = [], [], []
    for w, rows in side:
        n_blk = w.shape[0] // rows
        assert n_blk * rows == w.shape[0] and rows % 16 == 0
        idx = lambda j, i, n_blk=n_blk: (jnp.minimum(j * n_inner + i, n_blk - 1), 0)
        in_specs.append(pl.BlockSpec((rows, w.shape[1]), idx))
        out_specs.append(pl.BlockSpec((rows, w.shape[1]), idx))
        out_shape.append(jax.ShapeDtypeStruct(w.shape, BF16))
    return in_specs, out_specs, out_shape


def _mm_w32_kernel(*refs, shift, n_out, head_rows, silu, n_side):
    x_ref, wa_ref = refs[0], refs[1]
    pos = 2
    wb_ref = None
    if shift:
        wb_ref = refs[pos]
        pos += 1
    side_in = refs[pos:pos + n_side]
    pos += n_side
    o_refs = refs[pos:pos + n_out]
    pos += n_out
    hp_ref = hs_ref = None
    if head_rows is not None:
        hp_ref, hs_ref = refs[pos], refs[pos + 1]
        pos += 2
    side_out = refs[pos:pos + n_side]
    pos += n_side
    wsc = refs[pos]
    i = pl.program_id(1)

    @pl.when(i == 0)
    def _():
        tn = wsc.shape[0]
        _cast_rows(wa_ref, shift, wsc, 0, tn - shift)
        if shift:
            _cast_rows(wb_ref, 0, wsc, tn - shift, shift)

    for s_in, s_out in zip(side_in, side_out):
        s_out[...] = s_in[...].astype(BF16)

    if silu:
        sub = x_ref.shape[0] // SILU_SUB_BLOCKS
        for r in range(SILU_SUB_BLOCKS):
            rs = slice(r * sub, (r + 1) * sub)
            res = _silu(_dot_nt(x_ref[rs, :], wsc[...]))
            for o_ref in o_refs:
                o_ref[rs, :] = res.astype(o_ref.dtype)
        return

    acc = _dot_nt(x_ref[...], wsc[...])
    for o_ref in o_refs:
        o_ref[...] = acc.astype(o_ref.dtype)

    if head_rows is not None:
        (tile_p, row0_p), (tile_s, row0_s) = head_rows

        @pl.when(i == tile_p)
        def _():
            hp_ref[...] = acc[row0_p:row0_p + hp_ref.shape[0], :]

        @pl.when(i == tile_s)
        def _():
            hs_ref[...] = acc[row0_s:row0_s + hs_ref.shape[0], :]


def _matmul_w32(x, wt, col_blk, n_col_tiles, out_dtypes, *, tm, tn, shift=0, head_out=None, silu=False,
                side=(), name):
    m, k = x.shape
    assert m % tm == 0 and tn % LANES == 0 and shift % 16 == 0 and shift < tn
    assert not silu or (head_out is None and tm % (16 * SILU_SUB_BLOCKS) == 0)
    in_specs = [pl.BlockSpec((tm, k), lambda j, i: (i, 0)),
                pl.BlockSpec((tn, k), lambda j, i: (col_blk(j), 0))]
    args = [x, wt]
    if shift:
        assert tn % shift == 0
        per = tn // shift
        in_specs.append(pl.BlockSpec((shift, k), lambda j, i: ((col_blk(j) + 1) * per, 0)))
        args.append(wt)
    side_in, side_out, side_shape = _side_cast_specs(side, m // tm)
    assert all(w.shape[0] // rows <= n_col_tiles * (m // tm) for w, rows in side)
    in_specs += side_in
    args += [w for w, _ in side]
    n = n_col_tiles * tn
    out_specs = [pl.BlockSpec((tm, tn), lambda j, i: (i, j)) for _ in out_dtypes]
    out_shape = [jax.ShapeDtypeStruct((m, n), dt) for dt in out_dtypes]
    head_rows = None
    if head_out is not None:
        n_keep, m_s = head_out
        m_p = m - m_s
        head_rows = (divmod(m_p - n_keep, tm), divmod(m_p, tm))
        assert head_rows[0][1] + n_keep <= tm and head_rows[1][1] + m_s <= tm
        out_specs += [pl.BlockSpec((n_keep, tn), lambda j, i: (0, j)),
                      pl.BlockSpec((m_s, tn), lambda j, i: (0, j))]
        out_shape += [jax.ShapeDtypeStruct((n_keep, n), F32), jax.ShapeDtypeStruct((m_s, n), F32)]
    out_specs += side_out
    out_shape += side_shape
    kern = functools.partial(_mm_w32_kernel, shift=shift, n_out=len(out_dtypes), head_rows=head_rows,
                             silu=silu, n_side=len(side))
    return pl.pallas_call(
        kern,
        grid=(n_col_tiles, m // tm),
        in_specs=in_specs,
        out_specs=out_specs,
        out_shape=out_shape,
        scratch_shapes=[pltpu.VMEM((tn, k), BF16)],
        compiler_params=_cparams(2),
        name=name,
    )(*args)


def _softplus(x):
    return jnp.maximum(x, 0.0) + jnp.log1p(jnp.exp(-jnp.abs(x)))


def _ssd_chunk(u, zs_ref, xbc_ref, dt_ref, cw_ref, cb_ref, dtb_ref, alog_ref, dsk_ref, nw_ref, e_ref, y_ref,
               hist8, act, ht, cst, ex, ysc, *, inner, n_groups, d_state):
    r0 = u * CHUNK
    rows = slice(r0, r0 + CHUNK)
    gw = inner // n_groups
    conv_dim = inner + 2 * n_groups * d_state

    for j in range(conv_dim // LANES):
        sl = slice(j * LANES, (j + 1) * LANES)
        if u == 0:
            blk = jnp.concatenate([hist8[:, sl], xbc_ref[0:CHUNK, sl]], axis=0)
        else:
            blk = xbc_ref[r0 - SUBLANES:r0 + CHUNK, sl]
        prev = pltpu.roll(blk, 1, 0)
        near = blk * cw_ref[3:4, sl] + prev * cw_ref[2:3, sl]
        far = pltpu.roll(blk * cw_ref[1:2, sl] + prev * cw_ref[0:1, sl], 2, 0)
        a = cb_ref[:, sl] + (near + far)[SUBLANES:, :]
        act[rows, sl] = _silu(a)

    dt = _softplus(dt_ref[rows, :] + dtb_ref[...])
    a_neg = -jnp.exp(alog_ref[...])
    da = dt * a_neg
    ri = lax.broadcasted_iota(jnp.int32, (CHUNK, CHUNK), 0)
    ci = lax.broadcasted_iota(jnp.int32, (CHUNK, CHUNK), 1)
    tri = (ri >= ci).astype(F32)
    cs = jnp.dot(tri, da, precision=lax.Precision.HIGHEST, preferred_element_type=F32)
    ecs = jnp.exp(cs)
    cs_last = cs[CHUNK - 1:CHUNK, :]
    dout = jnp.exp(cs_last - cs) * dt
    cd = jnp.exp(cs_last)
    cd_hi = cd.astype(BF16)
    cd_lo = (cd - cd_hi.astype(F32)).astype(BF16)
    lhs = jnp.concatenate(
        [dt.astype(BF16), ecs.astype(BF16), dout.astype(BF16),
         jnp.broadcast_to(cd_hi, (16, LANES)), jnp.broadcast_to(cd_lo, (16, LANES))], axis=0)
    ex[u] = _dot(lhs, e_ref[...])

    cst[u] = jnp.concatenate([cs, jnp.zeros((LANES - CHUNK, LANES), F32)], axis=0).T
    lane = lax.broadcasted_iota(jnp.int32, (CHUNK, LANES), 1)
    lo_half = lane < CHUNK
    n_pairs = inner // LANES
    ev = cst[u, pl.ds(0, n_pairs, stride=2), :]
    od = cst[u, pl.ds(1, n_pairs, stride=2), :]
    lane_p = lax.broadcasted_iota(jnp.int32, (n_pairs, LANES), 1)
    cst2 = jnp.where(lane_p < CHUNK, ev, pltpu.roll(od, CHUNK, 1))
    row = lax.broadcasted_iota(jnp.int32, (CHUNK, LANES), 0)
    causal2 = row >= jnp.where(lo_half, lane, lane - CHUNK)

    pairs_per_group = gw // LANES
    for g in range(n_groups):
        gs = slice(g * gw, (g + 1) * gw)
        bm_g = act[rows, inner + g * d_state: inner + (g + 1) * d_state].astype(BF16)
        cm_g = act[rows, inner + (n_groups + g) * d_state: inner + (n_groups + g + 1) * d_state].astype(BF16)
        cb2 = _dot_nt(cm_g, jnp.concatenate([bm_g, bm_g], axis=0))
        h_g = ht[:, gs]
        yoff = _dot(cm_g, h_g.astype(BF16))
        for p in range(pairs_per_group):
            k = g * pairs_per_group + p
            ls = slice(k * LANES, (k + 1) * LANES)
            csc = jnp.where(lo_half,
                            jnp.broadcast_to(cs[:, 2 * k:2 * k + 1], (CHUNK, LANES)),
                            jnp.broadcast_to(cs[:, 2 * k + 1:2 * k + 2], (CHUNK, LANES)))
            diff = csc - cst2[k:k + 1, :]
            dec = jnp.exp(jnp.where(causal2, diff, MASK_NEG))
            m2 = (cb2 * dec).astype(BF16)
            xs_p = act[rows, ls]
            xdt = xs_p * ex[u, 0:CHUNK, ls]
            x2 = jnp.concatenate([jnp.where(lo_half, xdt, 0.0), jnp.where(lo_half, 0.0, xdt)],
                                 axis=0).astype(BF16)
            yd = _dot(m2, x2)
            ysc[rows, ls] = (yd + yoff[:, p * LANES:(p + 1) * LANES] * ex[u, CHUNK:2 * CHUNK, ls]
                             + dsk_ref[:, ls] * xs_p)
        xw = (act[rows, gs] * ex[u, 2 * CHUNK:3 * CHUNK, gs]).astype(BF16)
        st = _dot_tn(bm_g, xw)
        cdx = ex[u, 3 * CHUNK:3 * CHUNK + 1, gs] + ex[u, 3 * CHUNK + 16:3 * CHUNK + 17, gs]
        ht[:, gs] = h_g * cdx + st

    for g in range(n_groups):
        gs = slice(g * gw, (g + 1) * gw)
        v = ysc[rows, gs] * zs_ref[rows, gs].astype(F32)
        ms = jnp.mean(v * v, axis=-1, keepdims=True)
        y_ref[rows, gs] = (v * lax.rsqrt(ms + RMS_EPS) * nw_ref[:, gs]).astype(y_ref.dtype)


def _ssd_kernel(*refs, cps, per_stream, aliased, inner, n_groups, d_state):
    (zs_ref, xbc_ref, dt_ref, hist_ref, h0_ref, cw_ref, cb_ref, dtb_ref, alog_ref,
     dsk_ref, nw_ref, e_ref) = refs[:12]
    y_ref, conv_ref, ho_ref, hist8, act, ht, cst, ex, ysc = refs[12 + int(aliased):]
    c = pl.program_id(0)
    n_rows = cps * CHUNK
    tail = slice(n_rows - (CONV_W - 1), n_rows)
    last = slice(SUBLANES - (CONV_W - 1), SUBLANES)

    @pl.when(c == 0)
    def _():
        hist8[...] = jnp.zeros_like(hist8)
        if not per_stream:
            ht[...] = jnp.zeros_like(ht)

    if per_stream:
        hist8[last, :] = hist_ref[0]
        ht[...] = h0_ref[0].T

    conv_ref[0] = xbc_ref[tail, :]

    for u in range(cps):
        _ssd_chunk(u, zs_ref, xbc_ref, dt_ref, cw_ref, cb_ref, dtb_ref, alog_ref, dsk_ref, nw_ref, e_ref, y_ref,
                   hist8, act, ht, cst, ex, ysc, inner=inner, n_groups=n_groups, d_state=d_state)
    hist8[last, :] = xbc_ref[tail, :]

    if per_stream:
        ho_ref[0] = ht[...].T
    else:
        @pl.when(c == pl.num_programs(0) - 1)
        def _():
            ho_ref[0] = ht[...].T


def _ssd(zs, xbc, dtr, hist, h0, conv_w, conv_b, dtb, alog, dskx, normw, emat, *,
         row0, n_streams, chunks_per_stream, cps, per_stream, inner, n_groups, d_state, name, y_into=None):
    conv_dim = xbc.shape[1]
    hp = inner
    n_rows = cps * CHUNK
    assert chunks_per_stream % cps == 0 and row0 % n_rows == 0
    assert per_stream == (chunks_per_stream == cps) and (per_stream or n_streams == 1)
    steps_per_stream = chunks_per_stream // cps
    blk0 = row0 // n_rows
    stream = (lambda c: c) if per_stream else (lambda c: 0)
    aliased = y_into is not None

    kern = functools.partial(_ssd_kernel, cps=cps, per_stream=per_stream, aliased=aliased, inner=inner,
                             n_groups=n_groups, d_state=d_state)
    const = lambda c: (0, 0)
    row_map = lambda c: (blk0 + c, 0)
    extra_specs = [pl.BlockSpec(memory_space=pl.ANY)] if aliased else []
    extra_args = [y_into] if aliased else []
    return pl.pallas_call(
        kern,
        grid=(n_streams * steps_per_stream,),
        input_output_aliases={12: 0} if aliased else {},
        in_specs=[
            pl.BlockSpec((n_rows, inner), row_map),
            pl.BlockSpec((n_rows, conv_dim), row_map),
            pl.BlockSpec((n_rows, LANES), row_map),
            pl.BlockSpec((1, CONV_W - 1, conv_dim), lambda c: (stream(c), 0, 0)),
            pl.BlockSpec((1, hp, d_state), lambda c: (stream(c), 0, 0)),
            pl.BlockSpec((CONV_W, conv_dim), const),
            pl.BlockSpec((1, conv_dim), const),
            pl.BlockSpec((1, LANES), const),
            pl.BlockSpec((1, LANES), const),
            pl.BlockSpec((1, inner), const),
            pl.BlockSpec((1, inner), const),
            pl.BlockSpec((LANES, inner), const),
        ] + extra_specs,
        out_specs=[
            pl.BlockSpec((n_rows, inner), row_map),
            pl.BlockSpec((1, CONV_W - 1, conv_dim), lambda c: (stream(c), 0, 0)),
            pl.BlockSpec((1, hp, d_state), lambda c: (stream(c), 0, 0)),
        ],
        out_shape=[
            jax.ShapeDtypeStruct((zs.shape[0], inner), BF16),
            jax.ShapeDtypeStruct((n_streams, CONV_W - 1, conv_dim), F32),
            jax.ShapeDtypeStruct((n_streams, hp, d_state), F32),
        ],
        scratch_shapes=[
            pltpu.VMEM((SUBLANES, conv_dim), F32),
            pltpu.VMEM((n_rows, conv_dim), F32),
            pltpu.VMEM((d_state, inner), F32),
            pltpu.VMEM((cps, LANES, LANES), F32),
            pltpu.VMEM((cps, 3 * CHUNK + 32, inner), F32),
            pltpu.VMEM((n_rows, inner), F32),
        ],
        compiler_params=_cparams(1),
        name=name,
    )(zs, xbc, dtr, hist, h0, conv_w, conv_b, dtb, alog, dskx, normw, emat, *extra_args)


def _band_bias_tables(ext_ref, bias_scr, n_heads):
    col = lax.broadcasted_iota(jnp.int32, (CHUNK, ATT_WIN), 1)
    band = BAND_PAST + CHUNK
    for h in range(n_heads):
        e = jnp.broadcast_to(ext_ref[h:h + 1, :], (CHUNK, ATT_WIN))
        even = pltpu.roll(e, ATT_WIN - (CHUNK - 1), 1, stride=1, stride_axis=0)
        odd = pltpu.roll(e, 1, 1, stride=1, stride_axis=0)
        bias_scr[0, h] = jnp.where(col < band, even, MASK_NEG)
        bias_scr[1, h] = jnp.where(col >= CHUNK, odd, MASK_NEG)


ATT_BLOCKS = BAND_PAST // ATT_TILE + 1


def _attn_prompt_kernel(*refs, n_heads):
    q_ref = refs[0]
    k_refs = refs[1:1 + ATT_BLOCKS]
    v_refs = refs[1 + ATT_BLOCKS:1 + 2 * ATT_BLOCKS]
    ext_ref, o_ref, bias_scr, s_scr, p_scr = refs[1 + 2 * ATT_BLOCKS:]
    i = pl.program_id(0)
    d = ATT_HEAD_DIM
    scale = d ** -0.5
    blk = ATT_TILE

    @pl.when(i == 0)
    def _():
        _band_bias_tables(ext_ref, bias_scr, n_heads)
        p_scr[...] = jnp.zeros_like(p_scr)

    past_mask = [jnp.where(i < ATT_BLOCKS - 1 - b, MASK_NEG, 0.0).astype(F32) for b in range(ATT_BLOCKS - 1)]

    for h in range(n_heads):
        hs = slice(h * d, (h + 1) * d)
        q_h = q_ref[:, hs]
        for b in range(ATT_BLOCKS):
            s = _dot_nt(q_h, k_refs[b][:, hs]) * scale
            s_scr[:, b * blk:(b + 1) * blk] = s + past_mask[b] if b < ATT_BLOCKS - 1 else s
        for t in range(ATT_TILE // CHUNK):
            rs = slice(t * CHUNK, (t + 1) * CHUNK)
            c0 = LANES * (t // 2)
            ws = slice(c0, c0 + ATT_WIN)
            s = s_scr[rs, ws] + bias_scr[t % 2, h]
            m = jnp.max(s, axis=-1, keepdims=True)
            p = jnp.exp(s - m)
            r = 1.0 / jnp.sum(p, axis=-1, keepdims=True)
            p_scr[rs, ws] = (p * r).astype(BF16)
        o = _dot(p_scr[:, 0:blk], v_refs[0][:, hs])
        for b in range(1, ATT_BLOCKS):
            o = o + _dot(p_scr[:, b * blk:(b + 1) * blk], v_refs[b][:, hs])
        o_ref[:, hs] = o.astype(o_ref.dtype)


def _attn_prompt(q, kb, vb, ext, *, n_tiles, n_heads):
    width = n_heads * ATT_HEAD_DIM
    assert BAND_PAST % ATT_TILE == 0 and ATT_TILE % (2 * CHUNK) == 0
    kern = functools.partial(_attn_prompt_kernel, n_heads=n_heads)
    window = [pl.BlockSpec((ATT_TILE, width), lambda i, back=back: (jnp.maximum(i - back, 0), 0))
              for back in range(ATT_BLOCKS - 1, -1, -1)]
    return pl.pallas_call(
        kern,
        grid=(n_tiles,),
        in_specs=[pl.BlockSpec((ATT_TILE, width), lambda i: (i, 0))] + window + window
                 + [pl.BlockSpec((n_heads, ATT_WIN), lambda i: (0, 0))],
        out_specs=pl.BlockSpec((ATT_TILE, width), lambda i: (i, 0)),
        out_shape=jax.ShapeDtypeStruct((q.shape[0], width), BF16),
        scratch_shapes=[
            pltpu.VMEM((2, n_heads, CHUNK, ATT_WIN), F32),
            pltpu.VMEM((ATT_TILE, ATT_BLOCKS * ATT_TILE), F32),
            pltpu.VMEM((ATT_TILE, ATT_BLOCKS * ATT_TILE), BF16),
        ],
        compiler_params=_cparams(1),
        name="attn_prompt",
    )(q, *([kb] * ATT_BLOCKS), *([vb] * ATT_BLOCKS), ext)


def _head_copy(cache_hbm, heads_vmem, sem, stream, slot, which, h):
    return pltpu.make_async_copy(cache_hbm.at[stream, :, h, :], heads_vmem.at[slot, h], sem.at[slot, which])


def _attn_sample_kernel(q_ref, kn_ref, vn_ref, ck_hbm, cv_hbm, ext_ref, att_in_ref, o_ref,
                        bias_scr, kh_scr, vh_scr, s_scr, p_scr, sem, *, n_heads, prologue_done=False):
    del att_in_ref
    d = ATT_HEAD_DIM
    scale = d ** -0.5
    s_id = pl.program_id(0)
    slot = lax.rem(s_id, 2)

    def copies(stream, slt):
        return ([_head_copy(ck_hbm, kh_scr, sem, stream, slt, 0, h) for h in range(n_heads)]
                + [_head_copy(cv_hbm, vh_scr, sem, stream, slt, 1, h) for h in range(n_heads)])

    if not prologue_done:
        @pl.when(s_id == 0)
        def _():
            for c in copies(0, 0):
                c.start()
            _band_bias_tables(ext_ref, bias_scr, n_heads)

    @pl.when(s_id + 1 < pl.num_programs(0))
    def _():
        for c in copies(s_id + 1, 1 - slot):
            c.start()

    for c in copies(s_id, slot):
        c.wait()

    band = BAND_PAST + CHUNK
    for h in range(n_heads):
        hs = slice(h * d, (h + 1) * d)
        q_h = q_ref[:, hs]
        s_scr[h, :, 0:BAND_PAST] = (_dot_nt(q_h, kh_scr[slot, h].astype(BF16)) * scale
                                    + bias_scr[0, h, :, 0:BAND_PAST])
        s_scr[h, :, BAND_PAST:band] = (_dot_nt(q_h, kn_ref[:, hs]) * scale
                                       + bias_scr[0, h, :, BAND_PAST:band])
    for h in range(n_heads):
        s = s_scr[h, :, 0:band]
        p = jnp.exp(s - jnp.max(s, axis=-1, keepdims=True))
        r = 1.0 / jnp.sum(p, axis=-1, keepdims=True)
        p_scr[h, :, 0:band] = (p * r).astype(BF16)
    for h in range(n_heads):
        hs = slice(h * d, (h + 1) * d)
        o = (_dot(p_scr[h, :, 0:BAND_PAST], vh_scr[slot, h].astype(BF16))
             + _dot(p_scr[h, :, BAND_PAST:band], vn_ref[:, hs]))
        o_ref[:, hs] = o.astype(o_ref.dtype)


def _attn_sample(q, kb, vb, cache_k, cache_v, ext, att_into, *, row_blk0, n_sample, n_heads):
    width = n_heads * ATT_HEAD_DIM
    kern = functools.partial(_attn_sample_kernel, n_heads=n_heads)
    row_spec = pl.BlockSpec((CHUNK, width), lambda s: (row_blk0 + s, 0))
    heads_buf = pltpu.VMEM((2, n_heads, BAND_PAST, ATT_HEAD_DIM), F32)
    return pl.pallas_call(
        kern,
        grid=(n_sample,),
        in_specs=[row_spec, row_spec, row_spec,
                  pl.BlockSpec(memory_space=pl.ANY), pl.BlockSpec(memory_space=pl.ANY),
                  pl.BlockSpec((n_heads, ATT_WIN), lambda s: (0, 0)),
                  pl.BlockSpec(memory_space=pl.ANY)],
        out_specs=row_spec,
        out_shape=jax.ShapeDtypeStruct(att_into.shape, BF16),
        input_output_aliases={6: 0},
        scratch_shapes=[pltpu.VMEM((2, n_heads, CHUNK, ATT_WIN), F32),
                        heads_buf, heads_buf,
                        pltpu.VMEM((n_heads, CHUNK, ATT_WIN), F32),
                        pltpu.VMEM((n_heads, CHUNK, ATT_WIN), BF16),
                        pltpu.SemaphoreType.DMA((2, 2))],
        compiler_params=_cparams(1),
        name="attn_sample",
    )(q, kb, vb, cache_k, cache_v, ext, att_into)


def _sample_fused_kernel(*refs, n_heads, inner, n_groups, d_state):
    ssd_in, att_in, outs, ssd_scr, att_scr = refs[:13], refs[13:20], refs[20:24], refs[24:30], refs[30:]

    @pl.when(pl.program_id(0) == 0)
    def _():
        for h in range(n_heads):
            _head_copy(att_in[3], att_scr[1], att_scr[5], 0, 0, 0, h).start()
            _head_copy(att_in[4], att_scr[2], att_scr[5], 0, 0, 1, h).start()
        _band_bias_tables(att_in[5], att_scr[0], n_heads)

    _ssd_kernel(*ssd_in, *outs[:3], *ssd_scr, cps=1, per_stream=True, aliased=True, inner=inner,
                n_groups=n_groups, d_state=d_state)
    _attn_sample_kernel(*att_in, outs[3], *att_scr, n_heads=n_heads, prologue_done=True)


def _sample_fused(ssd_args, y_into, q, kb, vb, cache_k, cache_v, ext, att_into, *,
                  row_blk0, n_sample, n_heads, inner, n_groups, d_state):
    xbc = ssd_args[1]
    conv_dim = xbc.shape[1]
    width = n_heads * ATT_HEAD_DIM
    kern = functools.partial(_sample_fused_kernel, n_heads=n_heads, inner=inner, n_groups=n_groups,
                             d_state=d_state)
    const = lambda c: (0, 0)
    row_map = lambda c: (row_blk0 + c, 0)
    per_stream = lambda c: (c, 0, 0)
    any_spec = pl.BlockSpec(memory_space=pl.ANY)
    att_row = pl.BlockSpec((CHUNK, width), row_map)
    heads_buf = pltpu.VMEM((2, n_heads, BAND_PAST, ATT_HEAD_DIM), F32)
    return pl.pallas_call(
        kern,
        grid=(n_sample,),
        input_output_aliases={12: 0, 19: 3},
        in_specs=[
            pl.BlockSpec((CHUNK, inner), row_map),
            pl.BlockSpec((CHUNK, conv_dim), row_map),
            pl.BlockSpec((CHUNK, LANES), row_map),
            pl.BlockSpec((1, CONV_W - 1, conv_dim), per_stream),
            pl.BlockSpec((1, inner, d_state), per_stream),
            pl.BlockSpec((CONV_W, conv_dim), const),
            pl.BlockSpec((1, conv_dim), const),
            pl.BlockSpec((1, LANES), const),
            pl.BlockSpec((1, LANES), const),
            pl.BlockSpec((1, inner), const),
            pl.BlockSpec((1, inner), const),
            pl.BlockSpec((LANES, inner), const),
            any_spec,
            att_row, att_row, att_row, any_spec, any_spec,
            pl.BlockSpec((n_heads, ATT_WIN), const),
            any_spec,
        ],
        out_specs=[
            pl.BlockSpec((CHUNK, inner), row_map),
            pl.BlockSpec((1, CONV_W - 1, conv_dim), per_stream),
            pl.BlockSpec((1, inner, d_state), per_stream),
            att_row,
        ],
        out_shape=[
            jax.ShapeDtypeStruct(y_into.shape, BF16),
            jax.ShapeDtypeStruct((n_sample, CONV_W - 1, conv_dim), F32),
            jax.ShapeDtypeStruct((n_sample, inner, d_state), F32),
            jax.ShapeDtypeStruct(att_into.shape, BF16),
        ],
        scratch_shapes=[
            pltpu.VMEM((SUBLANES, conv_dim), F32),
            pltpu.VMEM((CHUNK, conv_dim), F32),
            pltpu.VMEM((d_state, inner), F32),
            pltpu.VMEM((1, LANES, LANES), F32),
            pltpu.VMEM((1, 3 * CHUNK + 32, inner), F32),
            pltpu.VMEM((CHUNK, inner), F32),
            pltpu.VMEM((2, n_heads, CHUNK, ATT_WIN), F32),
            heads_buf, heads_buf,
            pltpu.VMEM((n_heads, CHUNK, ATT_WIN), F32),
            pltpu.VMEM((n_heads, CHUNK, ATT_WIN), BF16),
            pltpu.SemaphoreType.DMA((2, 2)),
        ],
        compiler_params=_cparams(1),
        name="sample_ssd_attn",
    )(*ssd_args, y_into, q, kb, vb, cache_k, cache_v, ext, att_into)


def _mix_kernel(s_ref, a_ref, w1_ref, w2_ref, g1_ref, g2_ref, o_ref):
    a1 = _dot(s_ref[...], w1_ref[...])
    a2 = _dot(a_ref[...], w2_ref[...])
    g1 = _sigmoid(g1_ref[...].astype(F32))
    g2 = _sigmoid(g2_ref[...].astype(F32))
    o_ref[...] = (g1 * a1 + g2 * a2).astype(o_ref.dtype)


def _mix(ssd_y, att, w1, w2, gates_arr, *, gate_col0, tm, tn):
    m, k1 = ssd_y.shape
    k2 = att.shape[1]
    n = w1.shape[1]
    assert att.shape[0] == m and m % tm == 0
    g1_blk = gate_col0 // tn
    g2_blk = (gate_col0 + n) // tn
    return pl.pallas_call(
        _mix_kernel,
        grid=(n // tn, m // tm),
        in_specs=[
            pl.BlockSpec((tm, k1), lambda j, i: (i, 0)),
            pl.BlockSpec((tm, k2), lambda j, i: (i, 0)),
            pl.BlockSpec((k1, tn), lambda j, i: (0, j), pipeline_mode=pl.Buffered(1)),
            pl.BlockSpec((k2, tn), lambda j, i: (0, j), pipeline_mode=pl.Buffered(1)),
            pl.BlockSpec((tm, tn), lambda j, i: (i, g1_blk + j)),
            pl.BlockSpec((tm, tn), lambda j, i: (i, g2_blk + j)),
        ],
        out_specs=pl.BlockSpec((tm, tn), lambda j, i: (i, j)),
        out_shape=jax.ShapeDtypeStruct((m, n), BF16),
        compiler_params=_cparams(2),
        name="branch_mix",
    )(ssd_y, att, w1, w2, gates_arr, gates_arr)


def _layer_norm(t, g, b):
    mu = jnp.mean(t, axis=-1, keepdims=True)
    tc = t - mu
    var = jnp.mean(tc * tc, axis=-1, keepdims=True)
    return tc * lax.rsqrt(var + LN_EPS) * g + b


LN_SUB_ROWS = 256


def _proj_ln_kernel(*refs, alpha, n_k, n_prompt_tiles, split_resid, n_out):
    a_ref, w_ref = refs[0], refs[1]
    pos = 2
    r_refs = refs[pos:pos + (2 if split_resid else 1)]
    pos += len(r_refs)
    g_ref, b_ref = refs[pos], refs[pos + 1]
    pos += 2
    o_refs = refs[pos:pos + n_out]
    acc_ref = refs[pos + n_out] if n_k > 1 else None
    i = pl.program_id(0)
    kk = pl.program_id(1)
    tm = a_ref.shape[0]

    def resid_rows(rs):
        if split_resid:
            return jnp.where(i < n_prompt_tiles, r_refs[0][rs, :], r_refs[1][rs, :])
        return r_refs[0][rs, :]

    def finish(first):
        for r in range(tm // LN_SUB_ROWS):
            rs = slice(r * LN_SUB_ROWS, (r + 1) * LN_SUB_ROWS)
            base = alpha * resid_rows(rs) if first else acc_ref[rs, :]
            y = _layer_norm(base + _dot(a_ref[rs, :], w_ref[...]), g_ref[...], b_ref[...])
            for o_ref in o_refs:
                o_ref[rs, :] = y.astype(o_ref.dtype)

    if n_k == 1:
        finish(True)
        return

    @pl.when(kk == 0)
    def _():
        acc_ref[...] = alpha * resid_rows(slice(None)) + _dot(a_ref[...], w_ref[...])

    if n_k > 2:
        @pl.when(jnp.logical_and(kk > 0, kk < n_k - 1))
        def _():
            acc_ref[...] += _dot(a_ref[...], w_ref[...])

    @pl.when(kk == n_k - 1)
    def _():
        finish(False)


def _proj_ln(a, w, resid, gamma, beta, out_dtypes, *, alpha, tm, n_k, row0, n_rows, m_prompt=None, name):
    k = a.shape[1]
    n = w.shape[1]
    tk = k // n_k
    assert tk * n_k == k and n_rows % tm == 0 and row0 % tm == 0 and tm % LN_SUB_ROWS == 0
    blk0 = row0 // tm
    split_resid = isinstance(resid, tuple)
    row_map = lambda i, kk: (blk0 + i, 0)
    w_mode = dict(pipeline_mode=pl.Buffered(1)) if n_k == 1 else {}
    in_specs = [pl.BlockSpec((tm, tk), lambda i, kk: (blk0 + i, kk)),
                pl.BlockSpec((tk, n), lambda i, kk: (kk, 0), **w_mode)]
    args = [a, w]
    npt = 0
    if split_resid:
        assert row0 == 0 and m_prompt % tm == 0
        npt = m_prompt // tm
        in_specs += [pl.BlockSpec((tm, n), lambda i, kk: (jnp.minimum(i, npt - 1), 0)),
                     pl.BlockSpec((tm, n), lambda i, kk: (jnp.maximum(i - npt, 0), 0))]
        args += list(resid)
    else:
        in_specs.append(pl.BlockSpec((tm, n), row_map))
        args.append(resid)
    in_specs += [pl.BlockSpec((1, n), lambda i, kk: (0, 0))] * 2
    args += [gamma, beta]
    kern = functools.partial(_proj_ln_kernel, alpha=alpha, n_k=n_k, n_prompt_tiles=npt,
                             split_resid=split_resid, n_out=len(out_dtypes))
    return pl.pallas_call(
        kern,
        grid=(n_rows // tm, n_k),
        in_specs=in_specs,
        out_specs=[pl.BlockSpec((tm, n), lambda i, kk: (i, 0)) for _ in out_dtypes],
        out_shape=[jax.ShapeDtypeStruct((n_rows, n), dt) for dt in out_dtypes],
        scratch_shapes=[pltpu.VMEM((tm, n), F32)] if n_k > 1 else [],
        compiler_params=_cparams(2),
        name=name,
    )(*args)


def _glu_kernel(*refs, n_side):
    h_ref, wg_ref, wu_ref = refs[:3]
    side_in = refs[3:3 + n_side]
    o_ref = refs[3 + n_side]
    side_out = refs[4 + n_side:4 + 2 * n_side]
    wg_sc, wu_sc = refs[4 + 2 * n_side:]

    @pl.when(pl.program_id(1) == 0)
    def _():
        _cast_weight_tile(wg_ref, wg_sc)
        _cast_weight_tile(wu_ref, wu_sc)

    for s_in, s_out in zip(side_in, side_out):
        s_out[...] = s_in[...].astype(BF16)

    tm = h_ref.shape[0]
    sub = tm // GLU_SUB_BLOCKS
    for r in range(GLU_SUB_BLOCKS):
        rs = slice(r * sub, (r + 1) * sub)
        h = h_ref[rs, :]
        g = _dot(h, wg_sc[...])
        u = _dot(h, wu_sc[...])
        o_ref[rs, :] = (g * jax.nn.sigmoid(g) * u).astype(o_ref.dtype)


def _glu(h, w_gate_up, *, tm, tf, side=()):
    m, k = h.shape
    d_ff = w_gate_up.shape[1] // 2
    nf = d_ff // tf
    assert nf * tf == d_ff
    side_in, side_out, side_shape = _side_cast_specs(side, m // tm)
    assert all(w.shape[0] // rows <= nf * (m // tm) for w, rows in side)
    return pl.pallas_call(
        functools.partial(_glu_kernel, n_side=len(side)),
        grid=(nf, m // tm),
        in_specs=[
            pl.BlockSpec((tm, k), lambda j, i: (i, 0)),
            pl.BlockSpec((k, tf), lambda j, i: (0, j)),
            pl.BlockSpec((k, tf), lambda j, i: (0, nf + j)),
        ] + side_in,
        out_specs=[pl.BlockSpec((tm, tf), lambda j, i: (i, j))] + side_out,
        out_shape=[jax.ShapeDtypeStruct((m, d_ff), BF16)] + side_shape,
        scratch_shapes=[pltpu.VMEM((k, tf), BF16), pltpu.VMEM((k, tf), BF16)],
        compiler_params=_cparams(2),
        name="ffn_glu",
    )(h, w_gate_up, w_gate_up, *[w for w, _ in side])


def kernel(x_prompt, x_sample, cache_k, cache_v, state_conv, state_ssm, w_in, conv_w, conv_b, dt_bias,
           a_log, d_skip, ssd_norm_w, rel_bias, w_ssd_out, w_att_out, w_o, ln1_g, ln1_b, w_gate_up,
           w_down, ln2_g, ln2_b):
    depth = w_in.shape[0]
    assert depth == 1
    batch, seq, d_model = x_prompt.shape
    n_sample, dec_seq, _ = x_sample.shape
    assert batch == 1 and dec_seq == CHUNK and seq % ATT_TILE == 0
    n_ssd_heads = dt_bias.shape[1]
    inner = n_ssd_heads * SSD_HEADDIM
    conv_dim = conv_w.shape[2]
    n_groups = SSD_GROUPS
    d_state = SSD_STATE
    assert conv_dim == inner + 2 * n_groups * d_state
    n_heads = rel_bias.shape[1]
    width = n_heads * ATT_HEAD_DIM
    assert cache_k.shape[2] == BAND_PAST
    alpha = (2.0 * depth) ** 0.25
    keep = min(BAND_PAST, seq)

    m_p = batch * seq
    m_s = n_sample * dec_seq
    xp2 = x_prompt.reshape(m_p, d_model)
    xs2 = x_sample.reshape(m_s, d_model)

    wt = w_in[0].T
    tn = PROJ_TN
    o_xbc = inner
    o_dt = o_xbc + conv_dim
    o_q = o_dt + n_ssd_heads
    shift = o_q % tn
    assert o_dt % tn == 0 and inner % tn == 0 and conv_dim % tn == 0 and width % tn == 0
    q_blk = o_dt // tn
    xb, dtr = _xcast_dt(xp2, xs2, wt, dt_col=o_dt, n_dt=n_ssd_heads, tm=XCAST_TM)
    zs, w_att_b, w_o_b = _matmul_w32(xb, wt, lambda j: j, inner // tn, [BF16], tm=PROJ_TM, tn=tn, silu=True,
                                     side=[(w_att_out[0], SIDE_CAST_ROWS), (w_o[0], SIDE_CAST_ROWS)],
                                     name="in_proj_z")
    (xbc,) = _matmul_w32(xb, wt, lambda j: inner // tn + j, conv_dim // tn, [F32], tm=PROJ_TM, tn=tn,
                         name="in_proj_xbc")
    wpt = width // tn
    qg, w_ssd_b = _matmul_w32(xb, wt, lambda j: jnp.where(j < wpt, q_blk + j, q_blk + 2 * wpt + j),
                              wpt + 2 * d_model // tn, [BF16], tm=PROJ_TM, tn=tn, shift=shift,
                              side=[(w_ssd_out[0], SIDE_CAST_ROWS)], name="in_proj_qg")
    kb, k_p, k_s = _matmul_w32(xb, wt, lambda j: q_blk + wpt + j, wpt, [BF16], tm=KV_TM, tn=tn, shift=shift,
                               head_out=(keep, m_s), name="in_proj_k")
    vb, v_p, v_s = _matmul_w32(xb, wt, lambda j: q_blk + 2 * wpt + j, wpt, [BF16], tm=KV_TM, tn=tn,
                               shift=shift, head_out=(keep, m_s), name="in_proj_v")

    pad_h = (0, LANES - n_ssd_heads)
    dtb = jnp.pad(dt_bias[0], pad_h).reshape(1, LANES)
    alog = jnp.pad(a_log[0], pad_h).reshape(1, LANES)
    dskx = jnp.repeat(d_skip[0], SSD_HEADDIM).reshape(1, inner)
    emat = (jnp.arange(LANES)[:, None] == (jnp.arange(inner)[None, :] // SSD_HEADDIM)).astype(BF16)
    ssd_args = (zs, xbc, dtr, state_conv[0], state_ssm[0].reshape(n_sample, inner, d_state),
                conv_w[0], conv_b[0].reshape(1, conv_dim), dtb, alog, dskx, ssd_norm_w[0].reshape(1, inner), emat)
    ssd_dims = dict(inner=inner, n_groups=n_groups, d_state=d_state)
    ssd_y, conv_p, h_p = _ssd(*ssd_args, row0=0, n_streams=batch, chunks_per_stream=seq // CHUNK,
                              cps=SSD_PROMPT_CPS, per_stream=False, name="ssd_scan_prompt", **ssd_dims)

    rb = rel_bias[0]
    band = BAND_PAST + CHUNK
    n_edge = band - REL_CLIP
    n_rev = ATT_WIN - n_edge
    rev = rb[:, ::-1][:, 1:1 + n_rev]
    ext = jnp.concatenate([jnp.broadcast_to(rb[:, 2 * REL_CLIP:], (n_heads, n_edge)), rev], axis=1)
    ext = jnp.pad(ext, ((0, 0), (0, ATT_WIN - ext.shape[1])))
    att = _attn_prompt(qg, kb, vb, ext, n_tiles=m_p // ATT_TILE, n_heads=n_heads)
    ssd_y, conv_s, h_s, att = _sample_fused(ssd_args, ssd_y, qg, kb, vb, cache_k[0], cache_v[0], ext, att,
                                            row_blk0=m_p // CHUNK, n_sample=n_sample, n_heads=n_heads, **ssd_dims)

    mixed = _mix(ssd_y, att, w_ssd_b, w_att_b, qg, gate_col0=width, tm=MIX_TM, tn=MIX_TN)
    g1 = ln1_g[0].reshape(1, d_model)
    b1 = ln1_b[0].reshape(1, d_model)
    g2 = ln2_g[0].reshape(1, d_model)
    b2 = ln2_b[0].reshape(1, d_model)
    h_f, h_b = _proj_ln(mixed, w_o_b, (xp2, xs2), g1, b1, [F32, BF16], alpha=alpha, tm=LN_TM,
                        n_k=1, row0=0, n_rows=m_p + m_s, m_prompt=m_p, name="wo_ln")
    act, w_dn = _glu(h_b, w_gate_up[0], tm=GLU_TM, tf=GLU_TF, side=[(w_down[0], DOWN_CAST_ROWS)])
    (y_p,) = _proj_ln(act, w_dn, h_f, g2, b2, [F32], alpha=alpha, tm=LN_TM, n_k=DOWN_K_SPLIT, row0=0,
                      n_rows=m_p, name="down_ln_prompt")
    (y_s,) = _proj_ln(act, w_dn, h_f, g2, b2, [F32], alpha=alpha, tm=LN_TM, n_k=DOWN_K_SPLIT, row0=m_p,
                      n_rows=m_s, name="down_ln_sample")

    return (y_p.reshape(batch, seq, d_model),
            y_s.reshape(n_sample, dec_seq, d_model),
            conv_p.reshape(1, batch, CONV_W - 1, conv_dim),
            h_p.reshape(1, batch, n_ssd_heads, SSD_HEADDIM, d_state),
            k_p.reshape(1, batch, keep, n_heads, ATT_HEAD_DIM),
            v_p.reshape(1, batch, keep, n_heads, ATT_HEAD_DIM),
            conv_s.reshape(1, n_sample, CONV_W - 1, conv_dim),
            h_s.reshape(1, n_sample, n_ssd_heads, SSD_HEADDIM, d_state),
            k_s.reshape(1, n_sample, dec_seq, n_heads, ATT_HEAD_DIM),
            v_s.reshape(1, n_sample, dec_seq, n_heads, ATT_HEAD_DIM))
```
